```python
import math
import jax, jax.numpy as jnp
from jax import lax
import numpy as np

D_MODEL = 1024
BATCH = 2
SEQ = 8192
DEPTH = 1

EPS = 1e-6
D_MIX = D_MODEL
RET_HEADS = 4
RET_DK = 128
RET_DV = 128
RET_WIDTH = RET_HEADS * RET_DV
RET_CHUNK = 128
RET_ROPE_THETA = 10000.0
DIFF_HEADS = 4
DIFF_DH = 64
DIFF_DV = 2 * DIFF_DH
DIFF_WIDTH = DIFF_HEADS * DIFF_DV
Q_BLOCK = 128
ROPE_THETA = 500000.0
ROT_DIM = DIFF_DH // 4
RET_Q = RET_HEADS * RET_DK
RET_K = RET_HEADS * RET_DK
RET_V = RET_HEADS * RET_DV
RET_G = RET_WIDTH
DIFF_Q = DIFF_HEADS * 2 * DIFF_DH
DIFF_K = DIFF_HEADS * 2 * DIFF_DH
DIFF_V = DIFF_HEADS * DIFF_DV
SPLITS = [RET_Q, RET_K, RET_V, RET_G, DIFF_Q, DIFF_K, DIFF_V]
D_IN_PROJ = sum(SPLITS)
N_EXPERTS = 32
TOP_K = 4
D_FF = D_MODEL
SWIGLU_LIMIT = 7.0
SWIGLU_ALPHA = 1.702
MOE_BLOCK = 128

kernel_name = "hybrid_retention_diffattn_moe_encoder"


def rms_norm(x, w):
    xf = x.astype(jnp.float32)
    y = xf * lax.rsqrt(jnp.mean(xf * xf, axis=-1, keepdims=True) + EPS)
    return (y * w.astype(jnp.float32)).astype(x.dtype)


def rotary_cos_sin(positions, inv_freq):
    ang = positions.astype(jnp.float32)[..., None] * inv_freq
    return jnp.cos(ang), jnp.sin(ang)


def apply_rotary(x, cos, sin):
    xf = x.astype(jnp.float32)
    half = xf.shape[-1] // 2
    x1, x2 = xf[..., :half], xf[..., half:]
    out = jnp.concatenate([x1 * cos - x2 * sin, x2 * cos + x1 * sin], axis=-1)
    return out.astype(x.dtype)


def retention_one_direction(q, k, v, log_decay, include_diag):
    B, H, S, dk = q.shape
    dv = v.shape[-1]
    C = RET_CHUNK
    NC = S // C
    qc = q.reshape(B, H, NC, C, dk)
    kc = k.reshape(B, H, NC, C, dk)
    vc = v.reshape(B, H, NC, C, dv)
    idx = jnp.arange(C, dtype=jnp.float32)
    dist = idx[:, None] - idx[None, :]
    mask = (dist >= 0) if include_diag else (dist > 0)
    ld = log_decay.astype(jnp.float32)
    decay_mat = jnp.where(mask[None], jnp.exp(ld[:, None, None] * jnp.where(mask, dist, 0.0)[None]), 0.0)
    scores = jnp.einsum("bhcnd,bhcmd->bhcnm", qc, kc) * decay_mat[None, :, None]
    intra = jnp.einsum("bhcnm,bhcme->bhcne", scores, vc)
    k_decay = jnp.exp(ld[:, None] * (C - 1 - idx)[None])
    kv = jnp.einsum("bhcmd,hm,bhcme->bhcde", kc, k_decay, vc)
    chunk_decay = jnp.exp(ld * C)[None, :, None, None]

    def step(state, kv_i):
        return state * chunk_decay + kv_i, state

    _, prev = lax.scan(step, jnp.zeros((B, H, dk, dv), jnp.float32), jnp.moveaxis(kv, 2, 0))
    prev = jnp.moveaxis(prev, 0, 2)
    q_decay = jnp.exp(ld[:, None] * (idx + 1.0)[None])
    cross = jnp.einsum("bhcnd,bhcde->bhcne", qc, prev) * q_decay[None, :, None, :, None]
    return (intra + cross).reshape(B, H, S, dv)


def retention_group(q_raw, k_raw, v_raw, g_raw, positions, log_decay_fwd, log_decay_bwd, norm_w):
    B, S, _ = q_raw.shape
    inv_freq = RET_ROPE_THETA ** (-jnp.linspace(0.0, 1.0, RET_DK // 2, dtype=jnp.float32))
    cos, sin = rotary_cos_sin(positions, inv_freq)
    cos, sin = cos[:, :, None, :], sin[:, :, None, :]
    q = apply_rotary(q_raw.reshape(B, S, RET_HEADS, RET_DK), cos, sin)
    k = apply_rotary(k_raw.reshape(B, S, RET_HEADS, RET_DK), cos, sin) * (RET_DK ** -0.5)
    v = v_raw.reshape(B, S, RET_HEADS, RET_DV)
    q = jnp.transpose(q, (0, 2, 1, 3)).astype(jnp.float32)
    k = jnp.transpose(k, (0, 2, 1, 3)).astype(jnp.float32)
    v = jnp.transpose(v, (0, 2, 1, 3)).astype(jnp.float32)
    y_fwd = retention_one_direction(q, k, v, log_decay_fwd, True)
    y_bwd = jnp.flip(retention_one_direction(jnp.flip(q, 2), jnp.flip(k, 2), jnp.flip(v, 2), log_decay_bwd, False), 2)
    y = jnp.transpose(y_fwd + y_bwd, (0, 2, 1, 3))
    y = rms_norm(y, norm_w.reshape(RET_HEADS, RET_DV))
    y = y.reshape(B, S, RET_WIDTH) * jax.nn.silu(g_raw.astype(jnp.float32))
    return y.astype(q_raw.dtype)


def diff_attention_group(q_raw, k_raw, v_raw, positions, q_norm_w, k_norm_w,
                         lq1, lk1, lq2, lk2, sub_norm_w, lambda_init):
    B, S, _ = q_raw.shape
    q = rms_norm(q_raw.reshape(B, S, DIFF_HEADS, 2, DIFF_DH), q_norm_w)
    k = rms_norm(k_raw.reshape(B, S, DIFF_HEADS, 2, DIFF_DH), k_norm_w)
    inv_freq = ROPE_THETA ** (-jnp.arange(0, ROT_DIM, 2, dtype=jnp.float32) / ROT_DIM)
    cos, sin = rotary_cos_sin(positions, inv_freq)
    cos, sin = cos[:, :, None, None, :], sin[:, :, None, None, :]
    q = jnp.concatenate([apply_rotary(q[..., :ROT_DIM], cos, sin), q[..., ROT_DIM:]], axis=-1)
    k = jnp.concatenate([apply_rotary(k[..., :ROT_DIM], cos, sin), k[..., ROT_DIM:]], axis=-1)
    q = jnp.transpose(q, (0, 2, 3, 1, 4))
    k = jnp.transpose(k, (0, 2, 3, 1, 4))
    v = jnp.transpose(v_raw.reshape(B, S, DIFF_HEADS, DIFF_DV), (0, 2, 1, 3)).astype(jnp.float32)
    lam = (jnp.exp(jnp.sum(lq1.astype(jnp.float32) * lk1.astype(jnp.float32)))
           - jnp.exp(jnp.sum(lq2.astype(jnp.float32) * lk2.astype(jnp.float32))) + lambda_init)
    scale = DIFF_DH ** -0.5
    n_blocks = S // Q_BLOCK
    q_blocks = jnp.moveaxis(q.reshape(B, DIFF_HEADS, 2, n_blocks, Q_BLOCK, DIFF_DH), 3, 0)

    def attend(q_blk):
        s = jnp.einsum("bhcqd,bhckd->bhcqk", q_blk, k).astype(jnp.float32) * scale
        p = jax.nn.softmax(s, axis=-1)
        a = p[:, :, 0] - lam * p[:, :, 1]
        return jnp.einsum("bhqk,bhkd->bhqd", a, v)

    o = lax.map(attend, q_blocks)
    o = jnp.transpose(o, (1, 0, 3, 2, 4)).reshape(B, S, DIFF_HEADS, DIFF_DV)
    o = rms_norm(o, sub_norm_w) * (1.0 - lambda_init)
    return o.reshape(B, S, DIFF_WIDTH).astype(q_raw.dtype)


def moe_ffn(h, w_router, b_router, w1, b1, w2, b2):
    B, S, D = h.shape
    T = B * S
    ht = h.reshape(T, D)
    logits = (ht @ w_router + b_router).astype(jnp.float32)
    top_val, top_idx = lax.top_k(logits, TOP_K)
    gates = jax.nn.softmax(top_val, axis=-1)
    A = T * TOP_K
    e_flat = top_idx.reshape(A).astype(jnp.int32)
    tok_flat = jnp.arange(A, dtype=jnp.int32) // TOP_K
    g_flat = gates.reshape(A)
    order = jnp.argsort(e_flat, stable=True)
    e_sorted, tok_sorted, g_sorted = e_flat[order], tok_flat[order], g_flat[order]
    counts = jnp.zeros((N_EXPERTS,), jnp.int32).at[e_flat].add(1)
    starts = jnp.cumsum(counts) - counts
    padded = ((counts + MOE_BLOCK - 1) // MOE_BLOCK) * MOE_BLOCK
    pad_end = jnp.cumsum(padded)
    pad_start = pad_end - padded
    dest = pad_start[e_sorted] + (jnp.arange(A, dtype=jnp.int32) - starts[e_sorted])
    P = A + N_EXPERTS * MOE_BLOCK
    NB = P // MOE_BLOCK
    tok_buf = jnp.full((P,), T, jnp.int32).at[dest].set(tok_sorted)
    gate_buf = jnp.zeros((P,), jnp.float32).at[dest].set(g_sorted)
    block_e = jnp.minimum(jnp.searchsorted(pad_end, jnp.arange(NB, dtype=jnp.int32) * MOE_BLOCK, side="right"),
                          N_EXPERTS - 1).astype(jnp.int32)
    h_pad = jnp.concatenate([ht, jnp.zeros((1, D), ht.dtype)], axis=0)
    x_blocks = h_pad[tok_buf].reshape(NB, MOE_BLOCK, D)

    def expert_block(args):
        xb, e = args
        u = xb @ w1[e] + b1[e]
        glu = jnp.minimum(u[:, :D_FF], SWIGLU_LIMIT)
        lin = jnp.clip(u[:, D_FF:], -SWIGLU_LIMIT, SWIGLU_LIMIT)
        act = glu * jax.nn.sigmoid(SWIGLU_ALPHA * glu) * (lin + 1.0)
        return act @ w2[e] + b2[e]

    y_blocks = lax.map(expert_block, (x_blocks, block_e))
    y = y_blocks.reshape(P, D).astype(jnp.float32) * gate_buf[:, None]
    out = jax.ops.segment_sum(y, tok_buf, num_segments=T + 1)[:T]
    return out.reshape(B, S, D).astype(h.dtype)


def setup_inputs(seed: int = 0) -> dict:
    key = jax.random.key(seed)
    ks = jax.random.split(key, 24)
    f32 = jnp.float32
    nrm = lambda k, shape, s: jax.random.normal(k, shape, f32) * s
    x = jax.random.normal(ks[0], (BATCH, SEQ, D_MODEL), f32)
    offsets = jax.random.randint(ks[1], (BATCH, 1), 0, 1024, dtype=jnp.int32)
    positions = (jnp.arange(SEQ, dtype=jnp.int32)[None, :] + offsets).astype(jnp.int32)
    base_decay = jnp.log(1.0 - 2.0 ** (-5.0 - jnp.arange(RET_HEADS, dtype=f32)))
    ret_log_decay_fwd = base_decay[None] * (1.0 + 0.05 * jax.random.normal(ks[2], (DEPTH, RET_HEADS), f32))
    ret_log_decay_bwd = base_decay[None] * (1.0 + 0.05 * jax.random.normal(ks[3], (DEPTH, RET_HEADS), f32))
    return {
        "x": x,
        "positions": positions,
        "norm1_w": 1.0 + nrm(ks[4], (DEPTH, D_MODEL), 0.02),
        "w_in": nrm(ks[5], (DEPTH, D_MODEL, D_IN_PROJ), D_MODEL ** -0.5),
        "ret_log_decay_fwd": ret_log_decay_fwd,
        "ret_log_decay_bwd": ret_log_decay_bwd,
        "ret_norm_w": 1.0 + nrm(ks[6], (DEPTH, RET_WIDTH), 0.02),
        "q_norm_w": 1.0 + nrm(ks[7], (DEPTH, DIFF_DH), 0.02),
        "k_norm_w": 1.0 + nrm(ks[8], (DEPTH, DIFF_DH), 0.02),
        "lambda_q1": nrm(ks[9], (DEPTH, DIFF_DH), 0.1),
        "lambda_k1": nrm(ks[10], (DEPTH, DIFF_DH), 0.1),
        "lambda_q2": nrm(ks[11], (DEPTH, DIFF_DH), 0.1),
        "lambda_k2": nrm(ks[12], (DEPTH, DIFF_DH), 0.1),
        "diff_norm_w": 1.0 + nrm(ks[13], (DEPTH, DIFF_DV), 0.02),
        "w_out": nrm(ks[14], (DEPTH, D_MIX, D_MODEL), D_MIX ** -0.5),
        "norm2_w": 1.0 + nrm(ks[15], (DEPTH, D_MODEL), 0.02),
        "w_router": nrm(ks[16], (DEPTH, D_MODEL, N_EXPERTS), D_MODEL ** -0.5),
        "b_router": nrm(ks[17], (DEPTH, N_EXPERTS), 0.01),
        "w1": nrm(ks[18], (DEPTH, N_EXPERTS, D_MODEL, 2 * D_FF), D_MODEL ** -0.5),
        "b1": nrm(ks[19], (DEPTH, N_EXPERTS, 2 * D_FF), 0.01),
        "w2": nrm(ks[20], (DEPTH, N_EXPERTS, D_FF, D_MODEL), D_FF ** -0.5),
        "b2": nrm(ks[21], (DEPTH, N_EXPERTS, D_MODEL), 0.01),
    }


def reference(x, positions, norm1_w, w_in, ret_log_decay_fwd, ret_log_decay_bwd, ret_norm_w,
              q_norm_w, k_norm_w, lambda_q1, lambda_k1, lambda_q2, lambda_k2, diff_norm_w,
              w_out, norm2_w, w_router, b_router, w1, b1, w2, b2):
    split_points = list(np.cumsum(SPLITS)[:-1])
    for l in range(DEPTH):
        lambda_init = 0.8 - 0.6 * math.exp(-0.3 * l)
        h = rms_norm(x, norm1_w[l])
        proj = h @ w_in[l]
        rq, rk, rv, rg, dq, dk, dv = jnp.split(proj, split_points, axis=-1)
        y_ret = retention_group(rq, rk, rv, rg, positions, ret_log_decay_fwd[l], ret_log_decay_bwd[l], ret_norm_w[l])
        y_diff = diff_attention_group(dq, dk, dv, positions, q_norm_w[l], k_norm_w[l],
                                      lambda_q1[l], lambda_k1[l], lambda_q2[l], lambda_k2[l],
                                      diff_norm_w[l], lambda_init)
        mixed = jnp.concatenate([y_ret, y_diff], axis=-1)
        x = x + (mixed @ w_out[l]).astype(x.dtype)
        h2 = rms_norm(x, norm2_w[l])
        x = x + moe_ffn(h2, w_router[l], b_router[l], w1[l], b1[l], w2[l], b2[l])
    return x
```

```python
import functools

import jax
import jax.numpy as jnp
from jax import lax
from jax.experimental import pallas as pl
from jax.experimental.pallas import tpu as pltpu

EPS = 1e-6
D_MODEL = 1024
RET_HEADS = 4
RET_DK = 128
RET_WIDTH = 512
RET_ROPE_THETA = 10000.0
DIFF_HEADS = 4
DIFF_DH = 64
DIFF_DV = 128
DIFF_WIDTH = 512
ROPE_THETA = 500000.0
ROT_DIM = DIFF_DH // 4
D_IN_PROJ = 3584
N_EXPERTS = 32
TOP_K = 4
D_FF = 1024
SWIGLU_LIMIT = 7.0
SWIGLU_ALPHA = 1.702
LAMBDA_INIT = 0.8 - 0.6 * 1.0

LANES = 128
VMEM_LIMIT = 56 * 1024 * 1024

COL_RQ, COL_RK, COL_RV, COL_RG, COL_DQ, COL_DK, COL_DV = 0, 4, 8, 12, 16, 20, 24

TM_PROJ = 512
TS_PREP = 512
RET_CHUNK = 128
TQ_ATTN = 256
TK_ATTN = 512
TM_ROUTE = 512
MOE_BLOCK = 256
TG_DISPATCH = 512
TH_COMBINE = 256


def _params(sem, **kw):
    return pltpu.CompilerParams(dimension_semantics=sem, vmem_limit_bytes=VMEM_LIMIT, **kw)


def _in_proj_kernel(x_ref, nw_ref, w_ref, o_ref):
    x = x_ref[...]
    ms = jnp.mean(x * x, axis=-1, keepdims=True)
    h = (x * lax.rsqrt(ms + EPS) * nw_ref[...]).astype(jnp.bfloat16)
    n_col = o_ref.shape[1]
    for c in range(0, n_col, 512):
        o_ref[:, c:c + 512] = jnp.dot(h, w_ref[:, c:c + 512],
                                      preferred_element_type=jnp.float32).astype(o_ref.dtype)


def _in_proj(x2, nw, w_bf16):
    T = x2.shape[0]
    return pl.pallas_call(
        _in_proj_kernel,
        out_shape=jax.ShapeDtypeStruct((T, D_IN_PROJ), jnp.bfloat16),
        grid=(T // TM_PROJ,),
        in_specs=[pl.BlockSpec((TM_PROJ, D_MODEL), lambda i: (i, 0)),
                  pl.BlockSpec((1, D_MODEL), lambda i: (0, 0)),
                  pl.BlockSpec((D_MODEL, D_IN_PROJ), lambda i: (0, 0))],
        out_specs=pl.BlockSpec((TM_PROJ, D_IN_PROJ), lambda i: (i, 0)),
        compiler_params=_params(("arbitrary",)),
        name="in_proj",
    )(x2, nw, w_bf16)


def _prep_kernel(rq_ref, rk_ref, dq_ref, dk_ref, c2_ref, s2_ref, ra_ref, rp_ref, rn_ref,
                 qw_ref, kw_ref, rqo_ref, rko_ref, qs_ref, ks_ref):
    ts = rq_ref.shape[0]
    c2 = c2_ref[...]
    s2 = s2_ref[...]
    ra = ra_ref[...]
    rp = rp_ref[...]
    rn = rn_ref[...]
    lane = lax.broadcasted_iota(jnp.int32, (ts, LANES), 1)
    lo = lane < DIFF_DH

    def qk_norm_rot(x, w):
        x2 = x * x
        s_lo = jnp.sum(jnp.where(lo, x2, 0.0), axis=-1, keepdims=True)
        s_hi = jnp.sum(jnp.where(lo, 0.0, x2), axis=-1, keepdims=True)
        ms = jnp.where(lo, s_lo, s_hi) * (1.0 / DIFF_DH)
        xn = x * lax.rsqrt(ms + EPS) * w
        return xn * ra + pltpu.roll(xn, ROT_DIM // 2, 1) * rp + pltpu.roll(xn, LANES - ROT_DIM // 2, 1) * rn

    for h in range(RET_HEADS):
        sl = slice(h * LANES, (h + 1) * LANES)
        q = rq_ref[:, sl].astype(jnp.float32)
        k = rk_ref[:, sl].astype(jnp.float32)
        rqo_ref[:, sl] = (q * c2 + pltpu.roll(q, RET_DK // 2, 1) * s2).astype(rqo_ref.dtype)
        rko_ref[:, sl] = ((k * c2 + pltpu.roll(k, RET_DK // 2, 1) * s2) * (RET_DK ** -0.5)).astype(rko_ref.dtype)
    for h in range(DIFF_HEADS):
        sl = slice(h * LANES, (h + 1) * LANES)
        q = qk_norm_rot(dq_ref[:, sl].astype(jnp.float32), qw_ref[...]) * (DIFF_DH ** -0.5)
        k = qk_norm_rot(dk_ref[:, sl].astype(jnp.float32), kw_ref[...])
        qs_ref[h, 0] = jnp.where(lo, q, 0.0).astype(qs_ref.dtype)
        qs_ref[h, 1] = jnp.where(lo, 0.0, q).astype(qs_ref.dtype)
        ks_ref[:, sl] = k.astype(ks_ref.dtype)


def _prep(proj, tabs, qw2, kw2, B, S):
    T = B * S
    n_s = S // TS_PREP
    col = lambda cb: pl.BlockSpec((TS_PREP, 512), lambda b, i: (b * n_s + i, cb))
    tab = pl.BlockSpec((None, TS_PREP, LANES), lambda b, i: (b, i, 0))
    vec = pl.BlockSpec((1, LANES), lambda b, i: (0, 0))
    out_tok = pl.BlockSpec((TS_PREP, 512), lambda b, i: (b * n_s + i, 0))
    return pl.pallas_call(
        _prep_kernel,
        out_shape=(jax.ShapeDtypeStruct((T, 512), jnp.bfloat16),
                   jax.ShapeDtypeStruct((T, 512), jnp.bfloat16),
                   jax.ShapeDtypeStruct((B, DIFF_HEADS, 2, S, LANES), jnp.bfloat16),
                   jax.ShapeDtypeStruct((T, 512), jnp.bfloat16)),
        grid=(B, n_s),
        in_specs=[col(COL_RQ // 4), col(COL_RK // 4), col(COL_DQ // 4), col(COL_DK // 4),
                  tab, tab, tab, tab, tab, vec, vec],
        out_specs=(out_tok, out_tok,
                   pl.BlockSpec((None, DIFF_HEADS, 2, TS_PREP, LANES), lambda b, i: (b, 0, 0, i, 0)),
                   out_tok),
        compiler_params=_params(("arbitrary", "arbitrary")),
        name="prep",
    )(proj, proj, proj, proj, *tabs, qw2, kw2)


def _retention_kernel(ldf_ref, ldb_ref, q_ref, k_ref, v_ref, g_ref, nw_ref, o_ref, sb_ref):
    C = RET_CHUNK
    S = q_ref.shape[0]
    n_chunks = S // C
    h = pl.program_id(1)
    ldf = ldf_ref[h]
    ldb = ldb_ref[h]
    row = lax.broadcasted_iota(jnp.int32, (C, C), 0).astype(jnp.float32)
    colm = lax.broadcasted_iota(jnp.int32, (C, C), 1).astype(jnp.float32)
    dist = row - colm
    decay = jnp.where(dist >= 0, jnp.exp(ldf * jnp.maximum(dist, 0.0)), jnp.exp(ldb * jnp.maximum(-dist, 0.0)))
    idx = lax.broadcasted_iota(jnp.int32, (C, 1), 0).astype(jnp.float32)
    q_dec_f = jnp.exp(ldf * (idx + 1.0))
    k_dec_f = jnp.exp(ldf * (C - 1.0 - idx))
    q_dec_b = jnp.exp(ldb * (C - idx))
    k_dec_b = jnp.exp(ldb * idx)
    chunk_dec_f = jnp.exp(ldf * C)
    chunk_dec_b = jnp.exp(ldb * C)
    f32 = jnp.float32
    bf16 = jnp.bfloat16

    def kv_state(k, v, k_dec):
        kd = (k.astype(f32) * k_dec).astype(bf16)
        return lax.dot_general(kd, v, (((0,), (0,)), ((), ())), preferred_element_type=f32)

    def bwd_step(i, state):
        c = n_chunks - 1 - i
        r0 = pl.multiple_of(c * C, C)
        sb_ref[c] = state
        return state * chunk_dec_b + kv_state(k_ref[pl.ds(r0, C), :], v_ref[pl.ds(r0, C), :], k_dec_b)

    lax.fori_loop(0, n_chunks, bwd_step, jnp.zeros((RET_DK, LANES), f32))

    def fwd_step(c, state):
        r0 = pl.multiple_of(c * C, C)
        q = q_ref[pl.ds(r0, C), :]
        k = k_ref[pl.ds(r0, C), :]
        v = v_ref[pl.ds(r0, C), :]
        scores = lax.dot_general(q, k, (((1,), (1,)), ((), ())), preferred_element_type=f32) * decay
        y = jnp.dot(scores.astype(bf16), v, preferred_element_type=f32)
        qf = q.astype(f32)
        y += jnp.dot((qf * q_dec_f).astype(bf16), state.astype(bf16), preferred_element_type=f32)
        y += jnp.dot((qf * q_dec_b).astype(bf16), sb_ref[c].astype(bf16), preferred_element_type=f32)
        yn = y * lax.rsqrt(jnp.mean(y * y, axis=-1, keepdims=True) + EPS) * nw_ref[...]
        g = g_ref[pl.ds(r0, C), :].astype(f32)
        o_ref[pl.ds(r0, C), :] = (yn * (g * jax.nn.sigmoid(g))).astype(o_ref.dtype)
        return state * chunk_dec_f + kv_state(k, v, k_dec_f)

    lax.fori_loop(0, n_chunks, fwd_step, jnp.zeros((RET_DK, LANES), f32))


def _retention(ldf, ldb, rq_r, rk_r, proj, nw, B, S):
    T = B * S
    smem = pl.BlockSpec(memory_space=pltpu.SMEM)
    seq = lambda cb: pl.BlockSpec((S, LANES), lambda b, h: (b, cb + h))
    return pl.pallas_call(
        _retention_kernel,
        out_shape=jax.ShapeDtypeStruct((T, RET_WIDTH), jnp.bfloat16),
        grid=(B, RET_HEADS),
        in_specs=[smem, smem, seq(0), seq(0), seq(COL_RV), seq(COL_RG),
                  pl.BlockSpec((1, LANES), lambda b, h: (0, h))],
        out_specs=seq(0),
        scratch_shapes=[pltpu.VMEM((S // RET_CHUNK, RET_DK, LANES), jnp.float32)],
        compiler_params=_params(("arbitrary", "arbitrary")),
        name="retention",
    )(ldf, ldb, rq_r, rk_r, proj, proj, nw)


def _diff_attn_kernel(q_ref, k_ref, v_ref, lam_ref, nw_ref, o_ref, m_ref, l_ref, acc_ref):
    tq = q_ref.shape[1]
    S = k_ref.shape[0]
    q = q_ref[...].reshape(2 * tq, LANES)
    m_ref[...] = jnp.full(m_ref.shape, -jnp.inf, jnp.float32)
    l_ref[...] = jnp.zeros(l_ref.shape, jnp.float32)
    acc_ref[...] = jnp.zeros(acc_ref.shape, jnp.float32)

    def kv_step(j, carry):
        r0 = pl.multiple_of(j * TK_ATTN, TK_ATTN)
        k = k_ref[pl.ds(r0, TK_ATTN), :]
        v = v_ref[pl.ds(r0, TK_ATTN), :]
        s = lax.dot_general(q, k, (((1,), (1,)), ((), ())), preferred_element_type=jnp.float32)
        m_prev = m_ref[...]
        m_new = jnp.maximum(m_prev, jnp.max(s, axis=-1, keepdims=True))
        alpha = jnp.exp(m_prev - m_new)
        p = jnp.exp(s - m_new)
        l_ref[...] = alpha * l_ref[...] + jnp.sum(p, axis=-1, keepdims=True)
        acc_ref[...] = alpha * acc_ref[...] + jnp.dot(p.astype(jnp.bfloat16), v,
                                                      preferred_element_type=jnp.float32)
        m_ref[...] = m_new
        return carry

    lax.fori_loop(0, S // TK_ATTN, kv_step, 0)
    o = acc_ref[...] / l_ref[...]
    d = o[:tq] - lam_ref[...] * o[tq:]
    dn = d * lax.rsqrt(jnp.mean(d * d, axis=-1, keepdims=True) + EPS) * nw_ref[...]
    o_ref[...] = (dn * (1.0 - LAMBDA_INIT)).astype(o_ref.dtype)


def _diff_attn(qs, ks, proj, lam, nw, B, S):
    T = B * S
    n_q = S // TQ_ATTN
    one = pl.BlockSpec((1, LANES), lambda b, h, i: (0, 0))
    return pl.pallas_call(
        _diff_attn_kernel,
        out_shape=jax.ShapeDtypeStruct((T, DIFF_WIDTH), jnp.bfloat16),
        grid=(B, DIFF_HEADS, n_q),
        in_specs=[pl.BlockSpec((None, None, 2, TQ_ATTN, LANES), lambda b, h, i: (b, h, 0, i, 0)),
                  pl.BlockSpec((S, LANES), lambda b, h, i: (b, h)),
                  pl.BlockSpec((S, LANES), lambda b, h, i: (b, COL_DV + h)),
                  one, one],
        out_specs=pl.BlockSpec((TQ_ATTN, LANES), lambda b, h, i: (b * n_q + i, h)),
        scratch_shapes=[pltpu.VMEM((2 * TQ_ATTN, 1), jnp.float32),
                        pltpu.VMEM((2 * TQ_ATTN, 1), jnp.float32),
                        pltpu.VMEM((2 * TQ_ATTN, LANES), jnp.float32)],
        compiler_params=_params(("arbitrary", "arbitrary", "arbitrary")),
        name="diff_attn",
    )(qs, ks, proj, lam, nw)


def _out_router_kernel(x_ref, yr_ref, yd_ref, wo_ref, n2_ref, wrt_ref, br_ref,
                       x1_ref, h2_ref, idx_ref, gate_ref, gate_t_ref, rank_ref, cnt_ref):
    tm = x_ref.shape[0]
    f32 = jnp.float32

    @pl.when(pl.program_id(0) == 0)
    def _():
        cnt_ref[...] = jnp.zeros(cnt_ref.shape, f32)

    att = jnp.dot(yr_ref[...], wo_ref[:RET_WIDTH, :], preferred_element_type=f32)
    att += jnp.dot(yd_ref[...], wo_ref[RET_WIDTH:, :], preferred_element_type=f32)
    x1 = x_ref[...] + att
    x1_ref[...] = x1
    h2 = x1 * lax.rsqrt(jnp.mean(x1 * x1, axis=-1, keepdims=True) + EPS) * n2_ref[...]
    h2_ref[...] = h2
    logits = lax.dot_general(wrt_ref[...], h2, (((1,), (1,)), ((), ())),
                             precision=lax.Precision.HIGHEST, preferred_element_type=f32) + br_ref[...]
    e_iota = lax.broadcasted_iota(jnp.int32, (N_EXPERTS, tm), 0)
    work = logits
    vals, idxs, hots = [], [], []
    for _ in range(TOP_K):
        mx = jnp.max(work, axis=0, keepdims=True)
        ix = jnp.min(jnp.where(work == mx, e_iota, N_EXPERTS), axis=0, keepdims=True)
        hot = e_iota == ix
        vals.append(mx)
        idxs.append(ix)
        hots.append(hot)
        work = jnp.where(hot, -jnp.inf, work)
    exps = [jnp.exp(v - vals[0]) for v in vals]
    denom = exps[0] + exps[1] + exps[2] + exps[3]
    gates = [e / denom for e in exps]
    sel = jnp.zeros((N_EXPERTS, tm), f32)
    for hot in hots:
        sel = jnp.where(hot, 1.0, sel)
    t_row = lax.broadcasted_iota(jnp.int32, (tm, tm), 0)
    t_col = lax.broadcasted_iota(jnp.int32, (tm, tm), 1)
    upper = jnp.where(t_row < t_col, 1.0, 0.0).astype(jnp.bfloat16)
    carry = cnt_ref[:, 0:1]
    rank_full = jnp.dot(sel.astype(jnp.bfloat16), upper, preferred_element_type=f32) + carry
    for k in range(TOP_K):
        idx_ref[k:k + 1, :] = idxs[k]
        gate_ref[k:k + 1, :] = gates[k]
        rank_ref[k:k + 1, :] = jnp.sum(jnp.where(hots[k], rank_full, 0.0), axis=0,
                                       keepdims=True).astype(jnp.int32)
    cnt_ref[...] = cnt_ref[...] + jnp.sum(sel, axis=1, keepdims=True)
    gate_rows = jnp.concatenate(gates + [jnp.zeros((LANES - TOP_K, tm), f32)], axis=0)
    gate_t_ref[...] = gate_rows.T


def _out_router(x2, y_ret, y_diff, wo_bf16, n2w, wrt, br):
    T = x2.shape[0]
    tok = lambda w: pl.BlockSpec((TM_ROUTE, w), lambda i: (i, 0))
    lanes_tok = pl.BlockSpec((TOP_K, TM_ROUTE), lambda i: (0, i))
    const = lambda s: pl.BlockSpec(s, lambda i: (0, 0))
    return pl.pallas_call(
        _out_router_kernel,
        out_shape=(jax.ShapeDtypeStruct((T, D_MODEL), jnp.float32),
                   jax.ShapeDtypeStruct((T, D_MODEL), jnp.float32),
                   jax.ShapeDtypeStruct((TOP_K, T), jnp.int32),
                   jax.ShapeDtypeStruct((TOP_K, T), jnp.float32),
                   jax.ShapeDtypeStruct((T, LANES), jnp.float32),
                   jax.ShapeDtypeStruct((TOP_K, T), jnp.int32),
                   jax.ShapeDtypeStruct((N_EXPERTS, LANES), jnp.float32)),
        grid=(T // TM_ROUTE,),
        in_specs=[tok(D_MODEL), tok(RET_WIDTH), tok(DIFF_WIDTH), const((D_MODEL, D_MODEL)),
                  const((1, D_MODEL)), const((N_EXPERTS, D_MODEL)), const((N_EXPERTS, 1))],
        out_specs=(tok(D_MODEL), tok(D_MODEL), lanes_tok, lanes_tok, tok(LANES), lanes_tok,
                   const((N_EXPERTS, LANES))),
        compiler_params=_params(("arbitrary",)),
        name="out_router",
    )(x2, y_ret, y_diff, wo_bf16, n2w, wrt, br)


def _row_copy(src_hbm, src_row, dst_ref, dst_row, sem):
    return pltpu.make_async_copy(src_hbm.at[pl.ds(src_row, 1)], dst_ref.at[pl.ds(dst_row, 1)], sem)


def _dispatch_kernel(dest_ref, h2_hbm, xs_in_hbm, xs_hbm, sem):
    del xs_in_hbm
    tg = dest_ref.shape[1]
    base = pl.program_id(0) * tg

    def issue(r, c):
        for k in range(TOP_K):
            _row_copy(h2_hbm, base + r, xs_hbm, dest_ref[k, r], sem).start()
        return c

    lax.fori_loop(0, tg, issue, 0)

    def drain(r, c):
        for k in range(TOP_K):
            _row_copy(h2_hbm, 0, xs_hbm, 0, sem).wait()
        return c

    lax.fori_loop(0, tg, drain, 0)


def _dispatch(dest, h2, xs_zero):
    T = h2.shape[0]
    return pl.pallas_call(
        _dispatch_kernel,
        out_shape=jax.ShapeDtypeStruct(xs_zero.shape, xs_zero.dtype),
        grid=(T // TG_DISPATCH,),
        in_specs=[pl.BlockSpec((TOP_K, TG_DISPATCH), lambda i: (0, i), memory_space=pltpu.SMEM),
                  pl.BlockSpec(memory_space=pl.ANY),
                  pl.BlockSpec(memory_space=pl.ANY)],
        out_specs=pl.BlockSpec(memory_space=pl.ANY),
        scratch_shapes=[pltpu.SemaphoreType.DMA(())],
        input_output_aliases={2: 0},
        compiler_params=_params(("arbitrary",), has_side_effects=True),
        name="dispatch",
    )(dest, h2, xs_zero)


def _experts_kernel(be_ref, nu_ref, xs_ref, w1_ref, b1_ref, w2_ref, b2_ref, ys_ref):
    j = pl.program_id(0)

    @pl.when(j < nu_ref[0])
    def _():
        x = xs_ref[...].astype(jnp.bfloat16)
        u = jnp.dot(x, w1_ref[...], preferred_element_type=jnp.float32) + b1_ref[...]
        glu = jnp.minimum(u[:, :D_FF], SWIGLU_LIMIT)
        lin = jnp.clip(u[:, D_FF:], -SWIGLU_LIMIT, SWIGLU_LIMIT)
        act = glu * jax.nn.sigmoid(SWIGLU_ALPHA * glu) * (lin + 1.0)
        ys_ref[...] = jnp.dot(act.astype(jnp.bfloat16), w2_ref[...],
                              preferred_element_type=jnp.float32) + b2_ref[...]

    @pl.when(j >= nu_ref[0])
    def _():
        ys_ref[...] = jnp.zeros(ys_ref.shape, ys_ref.dtype)


def _experts(block_e, n_used, xs, w1b, b1, w2b, b2):
    P = xs.shape[0]
    n_blocks = P // MOE_BLOCK
    grid_spec = pltpu.PrefetchScalarGridSpec(
        num_scalar_prefetch=2,
        grid=(n_blocks,),
        in_specs=[pl.BlockSpec((MOE_BLOCK, D_MODEL), lambda j, be, nu: (j, 0)),
                  pl.BlockSpec((None, D_MODEL, 2 * D_FF), lambda j, be, nu: (be[j], 0, 0)),
                  pl.BlockSpec((None, 1, 2 * D_FF), lambda j, be, nu: (be[j], 0, 0)),
                  pl.BlockSpec((None, D_FF, D_MODEL), lambda j, be, nu: (be[j], 0, 0)),
                  pl.BlockSpec((None, 1, D_MODEL), lambda j, be, nu: (be[j], 0, 0))],
        out_specs=pl.BlockSpec((MOE_BLOCK, D_MODEL), lambda j, be, nu: (j, 0)),
    )
    return pl.pallas_call(
        _experts_kernel,
        out_shape=jax.ShapeDtypeStruct((P, D_MODEL), jnp.float32),
        grid_spec=grid_spec,
        compiler_params=_params(("arbitrary",)),
        name="experts",
    )(block_e, n_used, xs, w1b, b1, w2b, b2)


def _combine_kernel(dest_ref, gate_t_ref, x1_ref, ys_hbm, o_ref, buf_ref, sem):
    th = x1_ref.shape[0]

    def issue(r, c):
        for k in range(TOP_K):
            _row_copy(ys_hbm, dest_ref[k, r], buf_ref.at[k], r, sem).start()
        return c

    lax.fori_loop(0, th, issue, 0)

    def drain(r, c):
        for k in range(TOP_K):
            _row_copy(ys_hbm, 0, buf_ref.at[k], 0, sem).wait()
        return c

    lax.fori_loop(0, th, drain, 0)
    g = gate_t_ref[...]
    acc = x1_ref[...]
    for k in range(TOP_K):
        acc = acc + g[:, k:k + 1] * buf_ref[k]
    o_ref[...] = acc


def _combine(dest, gate_t, x1, ys):
    T = x1.shape[0]
    tok = lambda w: pl.BlockSpec((TH_COMBINE, w), lambda i: (i, 0))
    return pl.pallas_call(
        _combine_kernel,
        out_shape=jax.ShapeDtypeStruct((T, D_MODEL), jnp.float32),
        grid=(T // TH_COMBINE,),
        in_specs=[pl.BlockSpec((TOP_K, TH_COMBINE), lambda i: (0, i), memory_space=pltpu.SMEM),
                  tok(LANES), tok(D_MODEL), pl.BlockSpec(memory_space=pl.ANY)],
        out_specs=tok(D_MODEL),
        scratch_shapes=[pltpu.VMEM((TOP_K, TH_COMBINE, D_MODEL), jnp.float32),
                        pltpu.SemaphoreType.DMA(())],
        compiler_params=_params(("arbitrary",)),
        name="combine",
    )(dest, gate_t, x1, ys)


def _rotary_tables(positions):
    pos = positions.astype(jnp.float32)[..., None]
    inv_r = RET_ROPE_THETA ** (-jnp.linspace(0.0, 1.0, RET_DK // 2, dtype=jnp.float32))
    ang = pos * inv_r
    cos, sin = jnp.cos(ang), jnp.sin(ang)
    c2 = jnp.concatenate([cos, cos], axis=-1)
    s2 = jnp.concatenate([-sin, sin], axis=-1)
    inv_d = ROPE_THETA ** (-jnp.arange(0, ROT_DIM, 2, dtype=jnp.float32) / ROT_DIM)
    ang_d = pos * inv_d
    cd, sd = jnp.cos(ang_d), jnp.sin(ang_d)
    half = ROT_DIM // 2
    ones = jnp.ones(cd.shape[:-1] + (DIFF_DH - ROT_DIM,), jnp.float32)
    zeros_h = jnp.zeros(cd.shape[:-1] + (half,), jnp.float32)
    zeros_r = jnp.zeros(cd.shape[:-1] + (DIFF_DH - ROT_DIM,), jnp.float32)
    ra = jnp.concatenate([cd, cd, ones], axis=-1)
    rp = jnp.concatenate([zeros_h, sd, zeros_r], axis=-1)
    rn = jnp.concatenate([-sd, zeros_h, zeros_r], axis=-1)
    dup = lambda t: jnp.concatenate([t, t], axis=-1)
    return c2, s2, dup(ra), dup(rp), dup(rn)


def kernel(x, positions, norm1_w, w_in, ret_log_decay_fwd, ret_log_decay_bwd, ret_norm_w, q_norm_w, k_norm_w, lambda_q1, lambda_k1, lambda_q2, lambda_k2, diff_norm_w, w_out, norm2_w, w_router, b_router, w1, b1, w2, b2):
    B, S, D = x.shape
    T = B * S
    f32 = jnp.float32
    bf16 = jnp.bfloat16
    x2 = x.reshape(T, D)

    proj = _in_proj(x2, norm1_w[0].reshape(1, D), w_in[0].astype(bf16))
    tabs = _rotary_tables(positions)
    dup = lambda w: jnp.concatenate([w, w]).reshape(1, LANES).astype(f32)
    rq_r, rk_r, qs, ks = _prep(proj, tabs, dup(q_norm_w[0]), dup(k_norm_w[0]), B, S)

    y_ret = _retention(ret_log_decay_fwd[0].astype(f32), ret_log_decay_bwd[0].astype(f32),
                       rq_r, rk_r, proj, ret_norm_w[0].reshape(1, RET_WIDTH).astype(f32), B, S)

    lam = (jnp.exp(jnp.sum(lambda_q1[0].astype(f32) * lambda_k1[0].astype(f32)))
           - jnp.exp(jnp.sum(lambda_q2[0].astype(f32) * lambda_k2[0].astype(f32))) + LAMBDA_INIT)
    lam_row = jnp.full((1, LANES), lam, f32)
    y_diff = _diff_attn(qs, ks, proj, lam_row, diff_norm_w[0].reshape(1, DIFF_DV).astype(f32), B, S)

    x1, h2, top_idx, gates, gate_t, rank, counts = _out_router(
        x2, y_ret, y_diff, w_out[0].astype(bf16), norm2_w[0].reshape(1, D),
        w_router[0].T.astype(f32), b_router[0].reshape(N_EXPERTS, 1).astype(f32))
    del gates

    cnt = counts[:, 0].astype(jnp.int32)
    padded = ((cnt + MOE_BLOCK - 1) // MOE_BLOCK) * MOE_BLOCK
    pad_end = jnp.cumsum(padded)
    pad_start = pad_end - padded
    P = T * TOP_K + N_EXPERTS * MOE_BLOCK
    n_blocks = P // MOE_BLOCK
    dest = pad_start[top_idx] + rank
    block_e = jnp.minimum(jnp.searchsorted(pad_end, jnp.arange(n_blocks, dtype=jnp.int32) * MOE_BLOCK,
                                           side="right"), N_EXPERTS - 1).astype(jnp.int32)
    n_used = (pad_end[-1:] // MOE_BLOCK).astype(jnp.int32)

    xs = _dispatch(dest, h2, jnp.zeros((P, D), f32))
    ys = _experts(block_e, n_used, xs, w1[0].astype(bf16), b1[0].reshape(N_EXPERTS, 1, 2 * D_FF),
                  w2[0].astype(bf16), b2[0].reshape(N_EXPERTS, 1, D))
    out = _combine(dest, gate_t, x1, ys)
    return out.reshape(B, S, D)
```

```python
import functools

import jax
import jax.numpy as jnp
from jax import lax
from jax.experimental import pallas as pl
from jax.experimental.pallas import tpu as pltpu

EPS = 1e-6
D_MODEL = 1024
RET_HEADS = 4
RET_DK = 128
RET_WIDTH = 512
RET_ROPE_THETA = 10000.0
DIFF_HEADS = 4
DIFF_DH = 64
DIFF_DV = 128
DIFF_WIDTH = 512
ROPE_THETA = 500000.0
ROT_DIM = DIFF_DH // 4
D_IN_PROJ = 3584
N_EXPERTS = 32
TOP_K = 4
D_FF = 1024
SWIGLU_LIMIT = 7.0
SWIGLU_ALPHA = 1.702
LAMBDA_INIT = 0.8 - 0.6 * 1.0

LOG2_E = 1.4426950408889634
LANES = 128
VMEM_LIMIT = 56 * 1024 * 1024

COL_RQ, COL_RK, COL_RV, COL_RG, COL_DQ, COL_DK, COL_DV = 0, 4, 8, 12, 16, 20, 24

TM_PROJ = 512
TS_PREP = 512
RET_CHUNK = 128
TQ_ATTN = 512
TK_ATTN = 2048
TM_ROUTE = 512
MOE_BLOCK = 256
TG_DISPATCH = 512
TH_COMBINE = 256


def _params(sem, **kw):
    return pltpu.CompilerParams(dimension_semantics=sem, vmem_limit_bytes=VMEM_LIMIT, **kw)


def _in_proj_kernel(x_ref, nw_ref, w_ref, o_ref):
    x = x_ref[...]
    ms = jnp.mean(x * x, axis=-1, keepdims=True)
    h = (x * lax.rsqrt(ms + EPS) * nw_ref[...]).astype(jnp.bfloat16)
    n_col = o_ref.shape[1]
    for c in range(0, n_col, 512):
        o_ref[:, c:c + 512] = jnp.dot(h, w_ref[:, c:c + 512],
                                      preferred_element_type=jnp.float32).astype(o_ref.dtype)


def _in_proj(x2, nw, w_bf16):
    T = x2.shape[0]
    return pl.pallas_call(
        _in_proj_kernel,
        out_shape=jax.ShapeDtypeStruct((T, D_IN_PROJ), jnp.bfloat16),
        grid=(T // TM_PROJ,),
        in_specs=[pl.BlockSpec((TM_PROJ, D_MODEL), lambda i: (i, 0)),
                  pl.BlockSpec((1, D_MODEL), lambda i: (0, 0)),
                  pl.BlockSpec((D_MODEL, D_IN_PROJ), lambda i: (0, 0))],
        out_specs=pl.BlockSpec((TM_PROJ, D_IN_PROJ), lambda i: (i, 0)),
        compiler_params=_params(("arbitrary",)),
        name="in_proj",
    )(x2, nw, w_bf16)


def _prep_kernel(rq_ref, rk_ref, dq_ref, dk_ref, c2_ref, s2_ref, ra_ref, rp_ref, rn_ref,
                 qw_ref, kw_ref, rqo_ref, rko_ref, qs_ref, ks_ref):
    ts = rq_ref.shape[0]
    c2 = c2_ref[...]
    s2 = s2_ref[...]
    ra = ra_ref[...]
    rp = rp_ref[...]
    rn = rn_ref[...]
    lane = lax.broadcasted_iota(jnp.int32, (ts, LANES), 1)
    lo = lane < DIFF_DH

    def qk_norm_rot(x, w):
        x2 = x * x
        s_lo = jnp.sum(jnp.where(lo, x2, 0.0), axis=-1, keepdims=True)
        s_hi = jnp.sum(jnp.where(lo, 0.0, x2), axis=-1, keepdims=True)
        ms = jnp.where(lo, s_lo, s_hi) * (1.0 / DIFF_DH)
        xn = x * lax.rsqrt(ms + EPS) * w
        return xn * ra + pltpu.roll(xn, ROT_DIM // 2, 1) * rp + pltpu.roll(xn, LANES - ROT_DIM // 2, 1) * rn

    for h in range(RET_HEADS):
        sl = slice(h * LANES, (h + 1) * LANES)
        q = rq_ref[:, sl].astype(jnp.float32)
        k = rk_ref[:, sl].astype(jnp.float32)
        rqo_ref[:, sl] = (q * c2 + pltpu.roll(q, RET_DK // 2, 1) * s2).astype(rqo_ref.dtype)
        rko_ref[:, sl] = ((k * c2 + pltpu.roll(k, RET_DK // 2, 1) * s2) * (RET_DK ** -0.5)).astype(rko_ref.dtype)
    for h in range(DIFF_HEADS):
        sl = slice(h * LANES, (h + 1) * LANES)
        q = qk_norm_rot(dq_ref[:, sl].astype(jnp.float32), qw_ref[...]) * (DIFF_DH ** -0.5 * LOG2_E)
        k = qk_norm_rot(dk_ref[:, sl].astype(jnp.float32), kw_ref[...])
        qs_ref[h, 0] = jnp.where(lo, q, 0.0).astype(qs_ref.dtype)
        qs_ref[h, 1] = jnp.where(lo, 0.0, q).astype(qs_ref.dtype)
        ks_ref[:, sl] = k.astype(ks_ref.dtype)


def _prep(proj, tabs, qw2, kw2, B, S):
    T = B * S
    n_s = S // TS_PREP
    col = lambda cb: pl.BlockSpec((TS_PREP, 512), lambda b, i: (b * n_s + i, cb))
    tab = pl.BlockSpec((None, TS_PREP, LANES), lambda b, i: (b, i, 0))
    vec = pl.BlockSpec((1, LANES), lambda b, i: (0, 0))
    out_tok = pl.BlockSpec((TS_PREP, 512), lambda b, i: (b * n_s + i, 0))
    return pl.pallas_call(
        _prep_kernel,
        out_shape=(jax.ShapeDtypeStruct((T, 512), jnp.bfloat16),
                   jax.ShapeDtypeStruct((T, 512), jnp.bfloat16),
                   jax.ShapeDtypeStruct((B, DIFF_HEADS, 2, S, LANES), jnp.bfloat16),
                   jax.ShapeDtypeStruct((T, 512), jnp.bfloat16)),
        grid=(B, n_s),
        in_specs=[col(COL_RQ // 4), col(COL_RK // 4), col(COL_DQ // 4), col(COL_DK // 4),
                  tab, tab, tab, tab, tab, vec, vec],
        out_specs=(out_tok, out_tok,
                   pl.BlockSpec((None, DIFF_HEADS, 2, TS_PREP, LANES), lambda b, i: (b, 0, 0, i, 0)),
                   out_tok),
        compiler_params=_params(("arbitrary", "arbitrary")),
        name="prep",
    )(proj, proj, proj, proj, *tabs, qw2, kw2)


def _retention_kernel(ldf_ref, ldb_ref, q_ref, k_ref, v_ref, g_ref, nw_ref, o_ref, sb_ref):
    C = RET_CHUNK
    S = q_ref.shape[0]
    n_chunks = S // C
    h = pl.program_id(1)
    ldf = ldf_ref[h]
    ldb = ldb_ref[h]
    row = lax.broadcasted_iota(jnp.int32, (C, C), 0).astype(jnp.float32)
    colm = lax.broadcasted_iota(jnp.int32, (C, C), 1).astype(jnp.float32)
    dist = row - colm
    decay = jnp.where(dist >= 0, jnp.exp(ldf * jnp.maximum(dist, 0.0)), jnp.exp(ldb * jnp.maximum(-dist, 0.0)))
    idx = lax.broadcasted_iota(jnp.int32, (C, 1), 0).astype(jnp.float32)
    q_dec_f = jnp.exp(ldf * (idx + 1.0))
    k_dec_f = jnp.exp(ldf * (C - 1.0 - idx))
    q_dec_b = jnp.exp(ldb * (C - idx))
    k_dec_b = jnp.exp(ldb * idx)
    chunk_dec_f = jnp.exp(ldf * C)
    chunk_dec_b = jnp.exp(ldb * C)
    f32 = jnp.float32
    bf16 = jnp.bfloat16

    def kv_state(k, v, k_dec):
        kd = (k.astype(f32) * k_dec).astype(bf16)
        return lax.dot_general(kd, v, (((0,), (0,)), ((), ())), preferred_element_type=f32)

    def bwd_step(i, state):
        c = n_chunks - 1 - i
        r0 = pl.multiple_of(c * C, C)
        sb_ref[c] = state
        return state * chunk_dec_b + kv_state(k_ref[pl.ds(r0, C), :], v_ref[pl.ds(r0, C), :], k_dec_b)

    lax.fori_loop(0, n_chunks, bwd_step, jnp.zeros((RET_DK, LANES), f32))

    def fwd_step(c, state):
        r0 = pl.multiple_of(c * C, C)
        q = q_ref[pl.ds(r0, C), :]
        k = k_ref[pl.ds(r0, C), :]
        v = v_ref[pl.ds(r0, C), :]
        scores = lax.dot_general(q, k, (((1,), (1,)), ((), ())), preferred_element_type=f32) * decay
        y = jnp.dot(scores.astype(bf16), v, preferred_element_type=f32)
        qf = q.astype(f32)
        y += jnp.dot((qf * q_dec_f).astype(bf16), state.astype(bf16), preferred_element_type=f32)
        y += jnp.dot((qf * q_dec_b).astype(bf16), sb_ref[c].astype(bf16), preferred_element_type=f32)
        yn = y * lax.rsqrt(jnp.mean(y * y, axis=-1, keepdims=True) + EPS) * nw_ref[...]
        g = g_ref[pl.ds(r0, C), :].astype(f32)
        o_ref[pl.ds(r0, C), :] = (yn * (g * jax.nn.sigmoid(g))).astype(o_ref.dtype)
        return state * chunk_dec_f + kv_state(k, v, k_dec_f)

    lax.fori_loop(0, n_chunks, fwd_step, jnp.zeros((RET_DK, LANES), f32))


def _retention(ldf, ldb, rq_r, rk_r, proj, nw, B, S):
    T = B * S
    smem = pl.BlockSpec(memory_space=pltpu.SMEM)
    seq = lambda cb: pl.BlockSpec((S, LANES), lambda b, h: (b, cb + h))
    return pl.pallas_call(
        _retention_kernel,
        out_shape=jax.ShapeDtypeStruct((T, RET_WIDTH), jnp.bfloat16),
        grid=(B, RET_HEADS),
        in_specs=[smem, smem, seq(0), seq(0), seq(COL_RV), seq(COL_RG),
                  pl.BlockSpec((1, LANES), lambda b, h: (0, h))],
        out_specs=seq(0),
        scratch_shapes=[pltpu.VMEM((S // RET_CHUNK, RET_DK, LANES), jnp.float32)],
        compiler_params=_params(("arbitrary", "arbitrary")),
        name="retention",
    )(ldf, ldb, rq_r, rk_r, proj, proj, nw)


def _diff_attn_kernel(q_ref, k_ref, v_ref, lam_ref, nw_ref, o_ref, m_ref, l_ref, acc_ref):
    tq = q_ref.shape[1]
    S = k_ref.shape[0]
    f32 = jnp.float32
    q = q_ref[...].reshape(2 * tq, LANES)
    m_ref[...] = jnp.full(m_ref.shape, -jnp.inf, f32)
    l_ref[...] = jnp.zeros(l_ref.shape, f32)
    acc_ref[...] = jnp.zeros(acc_ref.shape, f32)
    n_tiles = TK_ATTN // LANES

    def kv_step(j, carry):
        r0 = pl.multiple_of(j * TK_ATTN, TK_ATTN)
        k = k_ref[pl.ds(r0, TK_ATTN), :]
        v = v_ref[pl.ds(r0, TK_ATTN), :]
        s = lax.dot_general(q, k, (((1,), (1,)), ((), ())), preferred_element_type=f32)
        tiles = [s[:, c * LANES:(c + 1) * LANES] for c in range(n_tiles)]
        part = tiles[0]
        for t in tiles[1:]:
            part = jnp.maximum(part, t)
        m_prev = m_ref[...]
        m_new = jnp.maximum(m_prev, jnp.max(part, axis=-1, keepdims=True))
        alpha = jnp.exp2(m_prev - m_new)
        probs = [jnp.exp2(t - m_new) for t in tiles]
        psum = probs[0]
        for p in probs[1:]:
            psum = psum + p
        l_ref[...] = alpha * l_ref[...] + psum
        p_bf = jnp.concatenate([p.astype(jnp.bfloat16) for p in probs], axis=1)
        acc_ref[...] = alpha * acc_ref[...] + jnp.dot(p_bf, v, preferred_element_type=f32)
        m_ref[...] = m_new
        return carry

    lax.fori_loop(0, S // TK_ATTN, kv_step, 0)
    o = acc_ref[...] / jnp.sum(l_ref[...], axis=-1, keepdims=True)
    d = o[:tq] - lam_ref[...] * o[tq:]
    dn = d * lax.rsqrt(jnp.mean(d * d, axis=-1, keepdims=True) + EPS) * nw_ref[...]
    o_ref[...] = (dn * (1.0 - LAMBDA_INIT)).astype(o_ref.dtype)


def _diff_attn(qs, ks, proj, lam, nw, B, S):
    T = B * S
    n_q = S // TQ_ATTN
    one = pl.BlockSpec((1, LANES), lambda b, h, i: (0, 0))
    return pl.pallas_call(
        _diff_attn_kernel,
        out_shape=jax.ShapeDtypeStruct((T, DIFF_WIDTH), jnp.bfloat16),
        grid=(B, DIFF_HEADS, n_q),
        in_specs=[pl.BlockSpec((None, None, 2, TQ_ATTN, LANES), lambda b, h, i: (b, h, 0, i, 0)),
                  pl.BlockSpec((S, LANES), lambda b, h, i: (b, h)),
                  pl.BlockSpec((S, LANES), lambda b, h, i: (b, COL_DV + h)),
                  one, one],
        out_specs=pl.BlockSpec((TQ_ATTN, LANES), lambda b, h, i: (b * n_q + i, h)),
        scratch_shapes=[pltpu.VMEM((2 * TQ_ATTN, LANES), jnp.float32)] * 3,
        compiler_params=_params(("arbitrary", "arbitrary", "arbitrary")),
        name="diff_attn",
    )(qs, ks, proj, lam, nw)


def _out_router_kernel(x_ref, yr_ref, yd_ref, wo_ref, n2_ref, wrt_ref, br_ref,
                       x1_ref, h2_ref, idx_ref, gate_ref, gate_t_ref, rank_ref, cnt_ref):
    tm = x_ref.shape[0]
    f32 = jnp.float32

    @pl.when(pl.program_id(0) == 0)
    def _():
        cnt_ref[...] = jnp.zeros(cnt_ref.shape, f32)

    att = jnp.dot(yr_ref[...], wo_ref[:RET_WIDTH, :], preferred_element_type=f32)
    att += jnp.dot(yd_ref[...], wo_ref[RET_WIDTH:, :], preferred_element_type=f32)
    x1 = x_ref[...] + att
    x1_ref[...] = x1
    h2 = x1 * lax.rsqrt(jnp.mean(x1 * x1, axis=-1, keepdims=True) + EPS) * n2_ref[...]
    h2_ref[...] = h2
    logits = lax.dot_general(wrt_ref[...], h2, (((1,), (1,)), ((), ())),
                             precision=lax.Precision.HIGHEST, preferred_element_type=f32) + br_ref[...]
    e_iota = lax.broadcasted_iota(jnp.int32, (N_EXPERTS, tm), 0)
    work = logits
    vals, idxs, hots = [], [], []
    for _ in range(TOP_K):
        mx = jnp.max(work, axis=0, keepdims=True)
        ix = jnp.min(jnp.where(work == mx, e_iota, N_EXPERTS), axis=0, keepdims=True)
        hot = e_iota == ix
        vals.append(mx)
        idxs.append(ix)
        hots.append(hot)
        work = jnp.where(hot, -jnp.inf, work)
    exps = [jnp.exp(v - vals[0]) for v in vals]
    denom = exps[0] + exps[1] + exps[2] + exps[3]
    gates = [e / denom for e in exps]
    sel = jnp.zeros((N_EXPERTS, tm), f32)
    for hot in hots:
        sel = jnp.where(hot, 1.0, sel)
    t_row = lax.broadcasted_iota(jnp.int32, (tm, tm), 0)
    t_col = lax.broadcasted_iota(jnp.int32, (tm, tm), 1)
    upper = jnp.where(t_row < t_col, 1.0, 0.0).astype(jnp.bfloat16)
    carry = cnt_ref[:, 0:1]
    rank_full = jnp.dot(sel.astype(jnp.bfloat16), upper, preferred_element_type=f32) + carry
    for k in range(TOP_K):
        idx_ref[k:k + 1, :] = idxs[k]
        gate_ref[k:k + 1, :] = gates[k]
        rank_ref[k:k + 1, :] = jnp.sum(jnp.where(hots[k], rank_full, 0.0), axis=0,
                                       keepdims=True).astype(jnp.int32)
    cnt_ref[...] = cnt_ref[...] + jnp.sum(sel, axis=1, keepdims=True)
    gate_rows = jnp.concatenate(gates + [jnp.zeros((LANES - TOP_K, tm), f32)], axis=0)
    gate_t_ref[...] = gate_rows.T


def _out_router(x2, y_ret, y_diff, wo_bf16, n2w, wrt, br):
    T = x2.shape[0]
    tok = lambda w: pl.BlockSpec((TM_ROUTE, w), lambda i: (i, 0))
    lanes_tok = pl.BlockSpec((TOP_K, TM_ROUTE), lambda i: (0, i))
    const = lambda s: pl.BlockSpec(s, lambda i: (0, 0))
    return pl.pallas_call(
        _out_router_kernel,
        out_shape=(jax.ShapeDtypeStruct((T, D_MODEL), jnp.float32),
                   jax.ShapeDtypeStruct((T, D_MODEL), jnp.float32),
                   jax.ShapeDtypeStruct((TOP_K, T), jnp.int32),
                   jax.ShapeDtypeStruct((TOP_K, T), jnp.float32),
                   jax.ShapeDtypeStruct((T, LANES), jnp.float32),
                   jax.ShapeDtypeStruct((TOP_K, T), jnp.int32),
                   jax.ShapeDtypeStruct((N_EXPERTS, LANES), jnp.float32)),
        grid=(T // TM_ROUTE,),
        in_specs=[tok(D_MODEL), tok(RET_WIDTH), tok(DIFF_WIDTH), const((D_MODEL, D_MODEL)),
                  const((1, D_MODEL)), const((N_EXPERTS, D_MODEL)), const((N_EXPERTS, 1))],
        out_specs=(tok(D_MODEL), tok(D_MODEL), lanes_tok, lanes_tok, tok(LANES), lanes_tok,
                   const((N_EXPERTS, LANES))),
        compiler_params=_params(("arbitrary",)),
        name="out_router",
    )(x2, y_ret, y_diff, wo_bf16, n2w, wrt, br)


def _row_copy(src_hbm, src_row, dst_ref, dst_row, sem):
    return pltpu.make_async_copy(src_hbm.at[pl.ds(src_row, 1)], dst_ref.at[pl.ds(dst_row, 1)], sem)


def _dispatch_kernel(lo_ref, hi_ref, dest_ref, h2_ref, xs_hbm, zero_ref, sem, zero_sem):
    tg = h2_ref.shape[0]

    @pl.when(pl.program_id(0) == 0)
    def _():
        zero_ref[...] = jnp.zeros(zero_ref.shape, zero_ref.dtype)

        def per_expert(e, c):
            def issue(r, c2):
                _row_copy(zero_ref, 0, xs_hbm, r, zero_sem).start()
                return c2

            def drain(r, c2):
                _row_copy(zero_ref, 0, xs_hbm, 0, zero_sem).wait()
                return c2

            lax.fori_loop(lo_ref[e], hi_ref[e], issue, 0)
            lax.fori_loop(lo_ref[e], hi_ref[e], drain, 0)
            return c

        lax.fori_loop(0, N_EXPERTS, per_expert, 0)

        def tail_copy(j):
            return pltpu.make_async_copy(zero_ref, xs_hbm.at[pl.ds(j * MOE_BLOCK, MOE_BLOCK)], zero_sem)

        n_blocks = xs_hbm.shape[0] // MOE_BLOCK
        first_unused = hi_ref[N_EXPERTS - 1] // MOE_BLOCK
        lax.fori_loop(first_unused, n_blocks, lambda j, c: (tail_copy(j).start(), c)[1], 0)
        lax.fori_loop(first_unused, n_blocks, lambda j, c: (tail_copy(j).wait(), c)[1], 0)

    def issue(r, c):
        for k in range(TOP_K):
            _row_copy(h2_ref, r, xs_hbm, dest_ref[k, r], sem).start()
        return c

    lax.fori_loop(0, tg, issue, 0)

    def drain(r, c):
        for k in range(TOP_K):
            _row_copy(h2_ref, 0, xs_hbm, 0, sem).wait()
        return c

    lax.fori_loop(0, tg, drain, 0)


def _dispatch(pad_lo, pad_hi, dest, h2, P):
    T = h2.shape[0]
    grid_spec = pltpu.PrefetchScalarGridSpec(
        num_scalar_prefetch=2,
        grid=(T // TG_DISPATCH,),
        in_specs=[pl.BlockSpec((TOP_K, TG_DISPATCH), lambda i, lo, hi: (0, i), memory_space=pltpu.SMEM),
                  pl.BlockSpec((TG_DISPATCH, D_MODEL), lambda i, lo, hi: (i, 0))],
        out_specs=pl.BlockSpec(memory_space=pl.ANY),
        scratch_shapes=[pltpu.VMEM((MOE_BLOCK, D_MODEL), h2.dtype),
                        pltpu.SemaphoreType.DMA(()),
                        pltpu.SemaphoreType.DMA(())],
    )
    return pl.pallas_call(
        _dispatch_kernel,
        out_shape=jax.ShapeDtypeStruct((P, D_MODEL), h2.dtype),
        grid_spec=grid_spec,
        compiler_params=_params(("arbitrary",), has_side_effects=True),
        name="dispatch",
    )(pad_lo, pad_hi, dest, h2)


def _experts_kernel(be_ref, nu_ref, xs_ref, w1_ref, b1_ref, w2_ref, b2_ref, ys_ref):
    j = pl.program_id(0)

    @pl.when(j < nu_ref[0])
    def _():
        x = xs_ref[...].astype(jnp.bfloat16)
        u = jnp.dot(x, w1_ref[...], preferred_element_type=jnp.float32) + b1_ref[...]
        glu = jnp.minimum(u[:, :D_FF], SWIGLU_LIMIT)
        lin = jnp.clip(u[:, D_FF:], -SWIGLU_LIMIT, SWIGLU_LIMIT)
        act = glu * jax.nn.sigmoid(SWIGLU_ALPHA * glu) * (lin + 1.0)
        ys_ref[...] = jnp.dot(act.astype(jnp.bfloat16), w2_ref[...],
                              preferred_element_type=jnp.float32) + b2_ref[...]

    @pl.when(j >= nu_ref[0])
    def _():
        ys_ref[...] = jnp.zeros(ys_ref.shape, ys_ref.dtype)


def _experts(block_e, n_used, xs, w1b, b1, w2b, b2):
    P = xs.shape[0]
    n_blocks = P // MOE_BLOCK
    row_block = lambda j, be, nu: (j, 0)
    grid_spec = pltpu.PrefetchScalarGridSpec(
        num_scalar_prefetch=2,
        grid=(n_blocks,),
        in_specs=[pl.BlockSpec((MOE_BLOCK, D_MODEL), row_block),
                  pl.BlockSpec((None, D_MODEL, 2 * D_FF), lambda j, be, nu: (be[j], 0, 0)),
                  pl.BlockSpec((None, 1, 2 * D_FF), lambda j, be, nu: (be[j], 0, 0)),
                  pl.BlockSpec((None, D_FF, D_MODEL), lambda j, be, nu: (be[j], 0, 0)),
                  pl.BlockSpec((None, 1, D_MODEL), lambda j, be, nu: (be[j], 0, 0))],
        out_specs=pl.BlockSpec((MOE_BLOCK, D_MODEL), row_block),
    )
    return pl.pallas_call(
        _experts_kernel,
        out_shape=jax.ShapeDtypeStruct((P, D_MODEL), jnp.float32),
        grid_spec=grid_spec,
        compiler_params=_params(("arbitrary",)),
        name="experts",
    )(block_e, n_used, xs, w1b, b1, w2b, b2)


def _combine_kernel(dest_ref, gate_t_ref, x1_ref, ys_hbm, o_ref, buf_ref, sem):
    th = x1_ref.shape[0]

    def issue(r, c):
        for k in range(TOP_K):
            _row_copy(ys_hbm, dest_ref[k, r], buf_ref.at[k], r, sem).start()
        return c

    lax.fori_loop(0, th, issue, 0)

    def drain(r, c):
        for k in range(TOP_K):
            _row_copy(ys_hbm, 0, buf_ref.at[k], 0, sem).wait()
        return c

    lax.fori_loop(0, th, drain, 0)
    g = gate_t_ref[...]
    acc = x1_ref[...]
    for k in range(TOP_K):
        acc = acc + g[:, k:k + 1] * buf_ref[k]
    o_ref[...] = acc


def _combine(dest, gate_t, x1, ys):
    T = x1.shape[0]
    tok = lambda w: pl.BlockSpec((TH_COMBINE, w), lambda i: (i, 0))
    return pl.pallas_call(
        _combine_kernel,
        out_shape=jax.ShapeDtypeStruct((T, D_MODEL), jnp.float32),
        grid=(T // TH_COMBINE,),
        in_specs=[pl.BlockSpec((TOP_K, TH_COMBINE), lambda i: (0, i), memory_space=pltpu.SMEM),
                  tok(LANES), tok(D_MODEL), pl.BlockSpec(memory_space=pl.ANY)],
        out_specs=tok(D_MODEL),
        scratch_shapes=[pltpu.VMEM((TOP_K, TH_COMBINE, D_MODEL), jnp.float32),
                        pltpu.SemaphoreType.DMA(())],
        compiler_params=_params(("arbitrary",)),
        name="combine",
    )(dest, gate_t, x1, ys)


def _rotary_tables(positions):
    pos = positions.astype(jnp.float32)[..., None]
    lane = jnp.arange(LANES)
    half_r = RET_DK // 2
    inv_r = RET_ROPE_THETA ** (-jnp.linspace(0.0, 1.0, half_r, dtype=jnp.float32))
    ang = pos * inv_r[lane % half_r]
    c2 = jnp.cos(ang)
    s2 = jnp.sin(ang) * jnp.where(lane < half_r, -1.0, 1.0)
    half_d = ROT_DIM // 2
    inv_d = ROPE_THETA ** (-jnp.arange(0, ROT_DIM, 2, dtype=jnp.float32) / ROT_DIM)
    sub = lane % DIFF_DH
    ang_d = pos * inv_d[sub % half_d]
    cd, sd = jnp.cos(ang_d), jnp.sin(ang_d)
    ra = jnp.where(sub < ROT_DIM, cd, 1.0)
    rp = jnp.where((sub >= half_d) & (sub < ROT_DIM), sd, 0.0)
    rn = jnp.where(sub < half_d, -sd, 0.0)
    return c2, s2, ra, rp, rn


def kernel(x, positions, norm1_w, w_in, ret_log_decay_fwd, ret_log_decay_bwd, ret_norm_w, q_norm_w, k_norm_w, lambda_q1, lambda_k1, lambda_q2, lambda_k2, diff_norm_w, w_out, norm2_w, w_router, b_router, w1, b1, w2, b2):
    B, S, D = x.shape
    T = B * S
    f32 = jnp.float32
    bf16 = jnp.bfloat16
    x2 = x.reshape(T, D)

    proj = _in_proj(x2, norm1_w[0].reshape(1, D), w_in[0].astype(bf16))
    tabs = _rotary_tables(positions)
    dup = lambda w: jnp.concatenate([w, w]).reshape(1, LANES).astype(f32)
    rq_r, rk_r, qs, ks = _prep(proj, tabs, dup(q_norm_w[0]), dup(k_norm_w[0]), B, S)

    y_ret = _retention(ret_log_decay_fwd[0].astype(f32), ret_log_decay_bwd[0].astype(f32),
                       rq_r, rk_r, proj, ret_norm_w[0].reshape(1, RET_WIDTH).astype(f32), B, S)

    lam = (jnp.exp(jnp.sum(lambda_q1[0].astype(f32) * lambda_k1[0].astype(f32)))
           - jnp.exp(jnp.sum(lambda_q2[0].astype(f32) * lambda_k2[0].astype(f32))) + LAMBDA_INIT)
    lam_row = jnp.full((1, LANES), lam, f32)
    y_diff = _diff_attn(qs, ks, proj, lam_row, diff_norm_w[0].reshape(1, DIFF_DV).astype(f32), B, S)

    x1, h2, top_idx, gates, gate_t, rank, counts = _out_router(
        x2, y_ret, y_diff, w_out[0].astype(bf16), norm2_w[0].reshape(1, D),
        w_router[0].T.astype(f32), b_router[0].reshape(N_EXPERTS, 1).astype(f32))
    del gates

    cnt = counts[:, 0].astype(jnp.int32)
    padded = ((cnt + MOE_BLOCK - 1) // MOE_BLOCK) * MOE_BLOCK
    pad_end = jnp.cumsum(padded)
    pad_start = pad_end - padded
    P = T * TOP_K + N_EXPERTS * MOE_BLOCK
    n_blocks = P // MOE_BLOCK
    e_ids = jnp.arange(N_EXPERTS, dtype=jnp.int32)
    dest = rank + jnp.sum(jnp.where(top_idx[None] == e_ids[:, None, None], pad_start[:, None, None], 0),
                          axis=0)
    block_row = jnp.arange(n_blocks, dtype=jnp.int32) * MOE_BLOCK
    block_e = jnp.minimum(jnp.sum((pad_end[None, :] <= block_row[:, None]).astype(jnp.int32), axis=1),
                          N_EXPERTS - 1)
    n_used = (pad_end[-1:] // MOE_BLOCK).astype(jnp.int32)

    xs = _dispatch(pad_start + cnt, pad_end, dest, h2, P)
    ys = _experts(block_e, n_used, xs, w1[0].astype(bf16), b1[0].reshape(N_EXPERTS, 1, 2 * D_FF),
                  w2[0].astype(bf16), b2[0].reshape(N_EXPERTS, 1, D))
    out = _combine(dest, gate_t, x1, ys)
    return out.reshape(B, S, D)
```

```python
import jax
import jax.numpy as jnp
from jax import lax
from jax.experimental import pallas as pl
from jax.experimental.pallas import tpu as pltpu

EPS = 1e-6
D_MODEL = 1024
RET_HEADS = 4
RET_DK = 128
RET_WIDTH = 512
RET_ROPE_THETA = 10000.0
DIFF_HEADS = 4
DIFF_DH = 64
DIFF_DV = 128
DIFF_WIDTH = 512
ROPE_THETA = 500000.0
ROT_DIM = DIFF_DH // 4
D_IN_PROJ = 3584
N_EXPERTS = 32
TOP_K = 4
D_FF = 1024
SWIGLU_LIMIT = 7.0
SWIGLU_ALPHA = 1.702
LAMBDA_INIT = 0.8 - 0.6 * 1.0

LOG2_E = 1.4426950408889634
LANES = 128
SUBLANES = 8
VMEM_LIMIT = 56 * 1024 * 1024

COL_RQ, COL_RK, COL_RV, COL_RG, COL_DQ, COL_DK, COL_DV = 0, 4, 8, 12, 16, 20, 24

TM_PROJ = 512
TS_PREP = 512
RET_CHUNK = 128
RET_UNROLL = 4
TQ_ATTN = 512
TK_ATTN = 2048
TILE = 512
MOE_BLOCK = 256
RUN_ALIGN = SUBLANES
TILE_ROWS = TOP_K * TILE + N_EXPERTS * RUN_ALIGN


def _params(sem, **kw):
    return pltpu.CompilerParams(dimension_semantics=sem, vmem_limit_bytes=VMEM_LIMIT, **kw)


def _in_proj_kernel(x_ref, nw_ref, w_ref, o_ref):
    x = x_ref[...]
    ms = jnp.mean(x * x, axis=-1, keepdims=True)
    h = (x * lax.rsqrt(ms + EPS) * nw_ref[...]).astype(jnp.bfloat16)
    n_col = o_ref.shape[1]
    for c in range(0, n_col, 512):
        o_ref[:, c:c + 512] = jnp.dot(h, w_ref[:, c:c + 512],
                                      preferred_element_type=jnp.float32).astype(o_ref.dtype)


def _in_proj(x2, nw, w_bf16):
    T = x2.shape[0]
    return pl.pallas_call(
        _in_proj_kernel,
        out_shape=jax.ShapeDtypeStruct((T, D_IN_PROJ), jnp.bfloat16),
        grid=(T // TM_PROJ,),
        in_specs=[pl.BlockSpec((TM_PROJ, D_MODEL), lambda i: (i, 0)),
                  pl.BlockSpec((1, D_MODEL), lambda i: (0, 0)),
                  pl.BlockSpec((D_MODEL, D_IN_PROJ), lambda i: (0, 0))],
        out_specs=pl.BlockSpec((TM_PROJ, D_IN_PROJ), lambda i: (i, 0)),
        compiler_params=_params(("arbitrary",)),
        name="in_proj",
    )(x2, nw, w_bf16)


def _prep_kernel(rq_ref, rk_ref, dq_ref, dk_ref, c2_ref, s2_ref, ra_ref, rp_ref, rn_ref,
                 qw_ref, kw_ref, rqo_ref, rko_ref, qs_ref, ks_ref):
    ts = rq_ref.shape[0]
    c2 = c2_ref[...]
    s2 = s2_ref[...]
    ra = ra_ref[...]
    rp = rp_ref[...]
    rn = rn_ref[...]
    lane = lax.broadcasted_iota(jnp.int32, (ts, LANES), 1)
    lo = lane < DIFF_DH

    def qk_norm_rot(x, w):
        x2 = x * x
        s_lo = jnp.sum(jnp.where(lo, x2, 0.0), axis=-1, keepdims=True)
        s_hi = jnp.sum(jnp.where(lo, 0.0, x2), axis=-1, keepdims=True)
        ms = jnp.where(lo, s_lo, s_hi) * (1.0 / DIFF_DH)
        xn = x * lax.rsqrt(ms + EPS) * w
        return xn * ra + pltpu.roll(xn, ROT_DIM // 2, 1) * rp + pltpu.roll(xn, LANES - ROT_DIM // 2, 1) * rn

    for h in range(RET_HEADS):
        sl = slice(h * LANES, (h + 1) * LANES)
        q = rq_ref[:, sl].astype(jnp.float32)
        k = rk_ref[:, sl].astype(jnp.float32)
        rqo_ref[:, sl] = (q * c2 + pltpu.roll(q, RET_DK // 2, 1) * s2).astype(rqo_ref.dtype)
        rko_ref[:, sl] = ((k * c2 + pltpu.roll(k, RET_DK // 2, 1) * s2) * (RET_DK ** -0.5)).astype(rko_ref.dtype)
    for h in range(DIFF_HEADS):
        sl = slice(h * LANES, (h + 1) * LANES)
        q = qk_norm_rot(dq_ref[:, sl].astype(jnp.float32), qw_ref[...]) * (DIFF_DH ** -0.5 * LOG2_E)
        k = qk_norm_rot(dk_ref[:, sl].astype(jnp.float32), kw_ref[...])
        qs_ref[h, 0] = jnp.where(lo, q, 0.0).astype(qs_ref.dtype)
        qs_ref[h, 1] = jnp.where(lo, 0.0, q).astype(qs_ref.dtype)
        ks_ref[:, sl] = k.astype(ks_ref.dtype)


def _prep(proj, tabs, qw2, kw2, B, S):
    T = B * S
    n_s = S // TS_PREP
    col = lambda cb: pl.BlockSpec((TS_PREP, 512), lambda b, i: (b * n_s + i, cb))
    tab = pl.BlockSpec((None, TS_PREP, LANES), lambda b, i: (b, i, 0))
    vec = pl.BlockSpec((1, LANES), lambda b, i: (0, 0))
    out_tok = pl.BlockSpec((TS_PREP, 512), lambda b, i: (b * n_s + i, 0))
    return pl.pallas_call(
        _prep_kernel,
        out_shape=(jax.ShapeDtypeStruct((T, 512), jnp.bfloat16),
                   jax.ShapeDtypeStruct((T, 512), jnp.bfloat16),
                   jax.ShapeDtypeStruct((B, DIFF_HEADS, 2, S, LANES), jnp.bfloat16),
                   jax.ShapeDtypeStruct((T, 512), jnp.bfloat16)),
        grid=(B, n_s),
        in_specs=[col(COL_RQ // 4), col(COL_RK // 4), col(COL_DQ // 4), col(COL_DK // 4),
                  tab, tab, tab, tab, tab, vec, vec],
        out_specs=(out_tok, out_tok,
                   pl.BlockSpec((None, DIFF_HEADS, 2, TS_PREP, LANES), lambda b, i: (b, 0, 0, i, 0)),
                   out_tok),
        compiler_params=_params(("arbitrary", "arbitrary")),
        name="prep",
    )(proj, proj, proj, proj, *tabs, qw2, kw2)


def _retention_kernel(ldf_ref, ldb_ref, q_ref, k_ref, v_ref, g_ref, nw_ref, o_ref, sb_ref):
    C = RET_CHUNK
    S = q_ref.shape[0]
    n_chunks = S // C
    h = pl.program_id(1)
    ldf = ldf_ref[h]
    ldb = ldb_ref[h]
    row = lax.broadcasted_iota(jnp.int32, (C, C), 0).astype(jnp.float32)
    colm = lax.broadcasted_iota(jnp.int32, (C, C), 1).astype(jnp.float32)
    dist = row - colm
    decay = jnp.where(dist >= 0, jnp.exp(ldf * jnp.maximum(dist, 0.0)), jnp.exp(ldb * jnp.maximum(-dist, 0.0)))
    idx = lax.broadcasted_iota(jnp.int32, (C, 1), 0).astype(jnp.float32)
    q_dec_f = jnp.exp(ldf * (idx + 1.0))
    k_dec_f = jnp.exp(ldf * (C - 1.0 - idx))
    q_dec_b = jnp.exp(ldb * (C - idx))
    k_dec_b = jnp.exp(ldb * idx)
    chunk_dec_f = jnp.exp(ldf * C)
    chunk_dec_b = jnp.exp(ldb * C)
    f32 = jnp.float32
    bf16 = jnp.bfloat16

    def kv_state(k, v, k_dec):
        kd = (k.astype(f32) * k_dec).astype(bf16)
        return lax.dot_general(kd, v, (((0,), (0,)), ((), ())), preferred_element_type=f32)

    def bwd_step(i, state):
        c = n_chunks - 1 - i
        r0 = pl.multiple_of(c * C, C)
        sb_ref[c] = state
        return state * chunk_dec_b + kv_state(k_ref[pl.ds(r0, C), :], v_ref[pl.ds(r0, C), :], k_dec_b)

    lax.fori_loop(0, n_chunks, bwd_step, jnp.zeros((RET_DK, LANES), f32), unroll=RET_UNROLL)

    def fwd_step(c, state):
        r0 = pl.multiple_of(c * C, C)
        q = q_ref[pl.ds(r0, C), :]
        k = k_ref[pl.ds(r0, C), :]
        v = v_ref[pl.ds(r0, C), :]
        scores = lax.dot_general(q, k, (((1,), (1,)), ((), ())), preferred_element_type=f32) * decay
        y = jnp.dot(scores.astype(bf16), v, preferred_element_type=f32)
        qf = q.astype(f32)
        y += jnp.dot((qf * q_dec_f).astype(bf16), state.astype(bf16), preferred_element_type=f32)
        y += jnp.dot((qf * q_dec_b).astype(bf16), sb_ref[c].astype(bf16), preferred_element_type=f32)
        yn = y * lax.rsqrt(jnp.mean(y * y, axis=-1, keepdims=True) + EPS) * nw_ref[...]
        g = g_ref[pl.ds(r0, C), :].astype(f32)
        o_ref[pl.ds(r0, C), :] = (yn * (g * jax.nn.sigmoid(g))).astype(o_ref.dtype)
        return state * chunk_dec_f + kv_state(k, v, k_dec_f)

    lax.fori_loop(0, n_chunks, fwd_step, jnp.zeros((RET_DK, LANES), f32), unroll=RET_UNROLL)


def _retention(ldf, ldb, rq_r, rk_r, proj, nw, B, S):
    T = B * S
    smem = pl.BlockSpec(memory_space=pltpu.SMEM)
    seq = lambda cb: pl.BlockSpec((S, LANES), lambda b, h: (b, cb + h))
    return pl.pallas_call(
        _retention_kernel,
        out_shape=jax.ShapeDtypeStruct((T, RET_WIDTH), jnp.bfloat16),
        grid=(B, RET_HEADS),
        in_specs=[smem, smem, seq(0), seq(0), seq(COL_RV), seq(COL_RG),
                  pl.BlockSpec((1, LANES), lambda b, h: (0, h))],
        out_specs=seq(0),
        scratch_shapes=[pltpu.VMEM((S // RET_CHUNK, RET_DK, LANES), jnp.float32)],
        compiler_params=_params(("arbitrary", "arbitrary")),
        name="retention",
    )(ldf, ldb, rq_r, rk_r, proj, proj, nw)


def _diff_attn_kernel(q_ref, k_ref, v_ref, lam_ref, nw_ref, o_ref, m_ref, l_ref, acc_ref):
    tq = q_ref.shape[1]
    S = k_ref.shape[0]
    f32 = jnp.float32
    q = q_ref[...].reshape(2 * tq, LANES)
    m_ref[...] = jnp.full(m_ref.shape, -jnp.inf, f32)
    l_ref[...] = jnp.zeros(l_ref.shape, f32)
    acc_ref[...] = jnp.zeros(acc_ref.shape, f32)
    n_tiles = TK_ATTN // LANES

    def kv_step(j, carry):
        r0 = pl.multiple_of(j * TK_ATTN, TK_ATTN)
        k = k_ref[pl.ds(r0, TK_ATTN), :]
        v = v_ref[pl.ds(r0, TK_ATTN), :]
        s = lax.dot_general(q, k, (((1,), (1,)), ((), ())), preferred_element_type=f32)
        tiles = [s[:, c * LANES:(c + 1) * LANES] for c in range(n_tiles)]
        part = tiles[0]
        for t in tiles[1:]:
            part = jnp.maximum(part, t)
        m_prev = m_ref[...]
        m_new = jnp.maximum(m_prev, jnp.max(part, axis=-1, keepdims=True))
        alpha = jnp.exp2(m_prev - m_new)
        probs = [jnp.exp2(t - m_new) for t in tiles]
        psum = probs[0]
        for p in probs[1:]:
            psum = psum + p
        l_ref[...] = alpha * l_ref[...] + psum
        p_bf = jnp.concatenate([p.astype(jnp.bfloat16) for p in probs], axis=1)
        acc_ref[...] = alpha * acc_ref[...] + jnp.dot(p_bf, v, preferred_element_type=f32)
        m_ref[...] = m_new
        return carry

    lax.fori_loop(0, S // TK_ATTN, kv_step, 0)
    o = acc_ref[...] / jnp.sum(l_ref[...], axis=-1, keepdims=True)
    d = o[:tq] - lam_ref[...] * o[tq:]
    dn = d * lax.rsqrt(jnp.mean(d * d, axis=-1, keepdims=True) + EPS) * nw_ref[...]
    o_ref[...] = (dn * (1.0 - LAMBDA_INIT)).astype(o_ref.dtype)


def _diff_attn(qs, ks, proj, lam, nw, B, S):
    T = B * S
    n_q = S // TQ_ATTN
    one = pl.BlockSpec((1, LANES), lambda b, h, i: (0, 0))
    return pl.pallas_call(
        _diff_attn_kernel,
        out_shape=jax.ShapeDtypeStruct((T, DIFF_WIDTH), jnp.bfloat16),
        grid=(B, DIFF_HEADS, n_q),
        in_specs=[pl.BlockSpec((None, None, 2, TQ_ATTN, LANES), lambda b, h, i: (b, h, 0, i, 0)),
                  pl.BlockSpec((S, LANES), lambda b, h, i: (b, h)),
                  pl.BlockSpec((S, LANES), lambda b, h, i: (b, COL_DV + h)),
                  one, one],
        out_specs=pl.BlockSpec((TQ_ATTN, LANES), lambda b, h, i: (b * n_q + i, h)),
        scratch_shapes=[pltpu.VMEM((2 * TQ_ATTN, LANES), jnp.float32)] * 3,
        compiler_params=_params(("arbitrary", "arbitrary", "arbitrary")),
        name="diff_attn",
    )(qs, ks, proj, lam, nw)


def _out_router_kernel(x_ref, yr_ref, yd_ref, wo_ref, n2_ref, wrt_ref, br_ref,
                       x1_ref, h2_ref, pos_ref, gate_t_ref, len_ref, off_ref, tot_ref):
    tm = x_ref.shape[0]
    f32 = jnp.float32
    bf16 = jnp.bfloat16

    @pl.when(pl.program_id(0) == 0)
    def _():
        tot_ref[...] = jnp.zeros(tot_ref.shape, f32)

    att = jnp.dot(yr_ref[...], wo_ref[:RET_WIDTH, :], preferred_element_type=f32)
    att += jnp.dot(yd_ref[...], wo_ref[RET_WIDTH:, :], preferred_element_type=f32)
    x1 = x_ref[...] + att
    x1_ref[...] = x1
    h2 = x1 * lax.rsqrt(jnp.mean(x1 * x1, axis=-1, keepdims=True) + EPS) * n2_ref[...]
    h2_ref[...] = h2.astype(h2_ref.dtype)
    logits = lax.dot_general(wrt_ref[...], h2, (((1,), (1,)), ((), ())),
                             precision=lax.Precision.HIGHEST, preferred_element_type=f32) + br_ref[...]
    e_iota = lax.broadcasted_iota(jnp.int32, (N_EXPERTS, tm), 0)
    work = logits
    vals, hots = [], []
    for _ in range(TOP_K):
        mx = jnp.max(work, axis=0, keepdims=True)
        ix = jnp.min(jnp.where(work == mx, e_iota, N_EXPERTS), axis=0, keepdims=True)
        hot = e_iota == ix
        vals.append(mx)
        hots.append(hot)
        work = jnp.where(hot, -jnp.inf, work)
    exps = [jnp.exp(v - vals[0]) for v in vals]
    denom = exps[0] + exps[1] + exps[2] + exps[3]
    gates = [e / denom for e in exps]
    sel = jnp.zeros((N_EXPERTS, tm), f32)
    for hot in hots:
        sel = jnp.where(hot, 1.0, sel)
    t_row = lax.broadcasted_iota(jnp.int32, (tm, tm), 0)
    t_col = lax.broadcasted_iota(jnp.int32, (tm, tm), 1)
    upper = jnp.where(t_row < t_col, 1.0, 0.0).astype(bf16)
    rank = jnp.dot(sel.astype(bf16), upper, preferred_element_type=f32)
    cnt = jnp.sum(sel, axis=1, keepdims=True)
    run_units = jnp.floor((cnt + (RUN_ALIGN - 1.0)) * (1.0 / RUN_ALIGN))
    run_len = jnp.broadcast_to(run_units * RUN_ALIGN, (N_EXPERTS, LANES))
    e_row = lax.broadcasted_iota(jnp.int32, (N_EXPERTS, N_EXPERTS), 0)
    e_col = lax.broadcasted_iota(jnp.int32, (N_EXPERTS, N_EXPERTS), 1)
    lower = jnp.where(e_col < e_row, 1.0, 0.0).astype(bf16)
    run_start = jnp.dot(lower, jnp.broadcast_to(run_units, (N_EXPERTS, LANES)).astype(bf16),
                        preferred_element_type=f32) * RUN_ALIGN
    pos_full = rank + run_start[:, 0:1]
    pos = [jnp.sum(jnp.where(hot, pos_full, 0.0), axis=0, keepdims=True) for hot in hots]
    for k in range(TOP_K):
        pos_ref[k:k + 1, :] = pos[k].astype(jnp.int32)
    rows = jnp.concatenate(gates + pos + [jnp.zeros((LANES - 2 * TOP_K, tm), f32)], axis=0)
    gate_t_ref[...] = rows.T
    len_ref[0] = run_len
    off_ref[0] = tot_ref[...]
    tot_ref[...] = tot_ref[...] + run_len


def _out_router(x2, y_ret, y_diff, wo_bf16, n2w, wrt, br):
    T = x2.shape[0]
    n_tiles = T // TILE
    tok = lambda w: pl.BlockSpec((TILE, w), lambda i: (i, 0))
    const = lambda s: pl.BlockSpec(s, lambda i: (0, 0))
    per_tile = pl.BlockSpec((1, N_EXPERTS, LANES), lambda i: (i, 0, 0))
    return pl.pallas_call(
        _out_router_kernel,
        out_shape=(jax.ShapeDtypeStruct((T, D_MODEL), jnp.float32),
                   jax.ShapeDtypeStruct((T, D_MODEL), jnp.bfloat16),
                   jax.ShapeDtypeStruct((TOP_K, T), jnp.int32),
                   jax.ShapeDtypeStruct((T, LANES), jnp.float32),
                   jax.ShapeDtypeStruct((n_tiles, N_EXPERTS, LANES), jnp.float32),
                   jax.ShapeDtypeStruct((n_tiles, N_EXPERTS, LANES), jnp.float32),
                   jax.ShapeDtypeStruct((N_EXPERTS, LANES), jnp.float32)),
        grid=(n_tiles,),
        in_specs=[tok(D_MODEL), tok(RET_WIDTH), tok(DIFF_WIDTH), const((D_MODEL, D_MODEL)),
                  const((1, D_MODEL)), const((N_EXPERTS, D_MODEL)), const((N_EXPERTS, 1))],
        out_specs=(tok(D_MODEL), tok(D_MODEL), pl.BlockSpec((TOP_K, TILE), lambda i: (0, i)), tok(LANES),
                   per_tile, per_tile, const((N_EXPERTS, LANES))),
        compiler_params=_params(("arbitrary",)),
        name="out_router",
    )(x2, y_ret, y_diff, wo_bf16, n2w, wrt, br)


def _run_copies(src_ref, len_ref, dst_ref, tile, make_copy):
    for e in range(N_EXPERTS):
        n = pl.multiple_of(len_ref[tile * N_EXPERTS + e], RUN_ALIGN)
        s = pl.multiple_of(src_ref[tile * N_EXPERTS + e], RUN_ALIGN)
        d = pl.multiple_of(dst_ref[tile * N_EXPERTS + e], RUN_ALIGN)

        @pl.when(n > 0)
        def _():
            make_copy(s, d, n).start()


def _dispatch_kernel(src_ref, len_ref, dst_ref, rows_ref, zlo_ref, zlen_ref, nu_ref,
                     pos_ref, h2_ref, xs_hbm, xbuf_ref, zero_ref, sems, zero_sem):
    i = pl.program_id(0)
    n_tiles = pl.num_programs(0)
    cur = i % 2
    n_rows, tm = xbuf_ref.shape[1], h2_ref.shape[0]

    @pl.when(i == 0)
    def _():
        zero_ref[...] = jnp.zeros(zero_ref.shape, zero_ref.dtype)

        def pad_copy(e):
            n = pl.multiple_of(zlen_ref[e], RUN_ALIGN)
            lo = pl.multiple_of(zlo_ref[e], RUN_ALIGN)
            return pltpu.make_async_copy(zero_ref.at[pl.ds(0, n)], xs_hbm.at[pl.ds(lo, n)], zero_sem)

        def tail_copy(j):
            return pltpu.make_async_copy(zero_ref, xs_hbm.at[pl.ds(j * MOE_BLOCK, MOE_BLOCK)], zero_sem)

        def guarded(copy, op):
            def body(e, c):
                @pl.when(zlen_ref[e] > 0)
                def _():
                    op(copy(e))
                return c
            return body

        lax.fori_loop(0, N_EXPERTS, guarded(pad_copy, lambda cp: cp.start()), 0)
        lax.fori_loop(0, N_EXPERTS, guarded(pad_copy, lambda cp: cp.wait()), 0)
        n_blocks = xs_hbm.shape[0] // MOE_BLOCK
        lax.fori_loop(nu_ref[0], n_blocks, lambda j, c: (tail_copy(j).start(), c)[1], 0)
        lax.fori_loop(nu_ref[0], n_blocks, lambda j, c: (tail_copy(j).wait(), c)[1], 0)

    p_iota = lax.broadcasted_iota(jnp.int32, (n_rows, tm), 0)
    onehot = jnp.zeros((n_rows, tm), jnp.float32)
    for k in range(TOP_K):
        onehot = jnp.where(p_iota == pos_ref[k:k + 1, :], 1.0, onehot)
    xbuf_ref[cur] = jnp.dot(onehot.astype(jnp.bfloat16), h2_ref[...], preferred_element_type=jnp.float32)

    _run_copies(src_ref, len_ref, dst_ref, i,
                lambda s, d, n: pltpu.make_async_copy(xbuf_ref.at[cur, pl.ds(s, n)], xs_hbm.at[pl.ds(d, n)],
                                                      sems.at[cur]))

    def wait_tile(tile, slot):
        rows = pl.multiple_of(rows_ref[tile], RUN_ALIGN)
        pltpu.make_async_copy(xbuf_ref.at[slot, pl.ds(0, rows)], xs_hbm.at[pl.ds(0, rows)], sems.at[slot]).wait()

    @pl.when(i > 0)
    def _():
        wait_tile(i - 1, 1 - cur)

    @pl.when(i == n_tiles - 1)
    def _():
        wait_tile(i, cur)


def _dispatch(run_src, run_len, run_dst, tile_rows, zero_lo, zero_len, n_used, pos, h2, P):
    T = h2.shape[0]
    n_pre = 7
    grid_spec = pltpu.PrefetchScalarGridSpec(
        num_scalar_prefetch=n_pre,
        grid=(T // TILE,),
        in_specs=[pl.BlockSpec((TOP_K, TILE), lambda i, *_: (0, i)),
                  pl.BlockSpec((TILE, D_MODEL), lambda i, *_: (i, 0))],
        out_specs=pl.BlockSpec(memory_space=pl.ANY),
        scratch_shapes=[pltpu.VMEM((2, TILE_ROWS, D_MODEL), jnp.float32),
                        pltpu.VMEM((MOE_BLOCK, D_MODEL), jnp.float32),
                        pltpu.SemaphoreType.DMA((2,)),
                        pltpu.SemaphoreType.DMA(())],
    )
    return pl.pallas_call(
        _dispatch_kernel,
        out_shape=jax.ShapeDtypeStruct((P, D_MODEL), jnp.float32),
        grid_spec=grid_spec,
        compiler_params=_params(("arbitrary",), has_side_effects=True),
        name="dispatch",
    )(run_src, run_len, run_dst, tile_rows, zero_lo, zero_len, n_used, pos, h2)


def _experts_kernel(be_ref, nu_ref, xs_ref, w1_ref, b1_ref, w2_ref, b2_ref, ys_ref):
    j = pl.program_id(0)

    @pl.when(j < nu_ref[0])
    def _():
        x = xs_ref[...].astype(jnp.bfloat16)
        u = jnp.dot(x, w1_ref[...], preferred_element_type=jnp.float32) + b1_ref[...]
        glu = jnp.minimum(u[:, :D_FF], SWIGLU_LIMIT)
        lin = jnp.clip(u[:, D_FF:], -SWIGLU_LIMIT, SWIGLU_LIMIT)
        act = glu * jax.nn.sigmoid(SWIGLU_ALPHA * glu) * (lin + 1.0)
        ys_ref[...] = jnp.dot(act.astype(jnp.bfloat16), w2_ref[...],
                              preferred_element_type=jnp.float32) + b2_ref[...]

    @pl.when(j >= nu_ref[0])
    def _():
        ys_ref[...] = jnp.zeros(ys_ref.shape, ys_ref.dtype)


def _experts(block_e, n_used, xs, w1b, b1, w2b, b2):
    P = xs.shape[0]
    row_block = lambda j, be, nu: (j, 0)
    expert = lambda j, be, nu: (be[j], 0, 0)
    grid_spec = pltpu.PrefetchScalarGridSpec(
        num_scalar_prefetch=2,
        grid=(P // MOE_BLOCK,),
        in_specs=[pl.BlockSpec((MOE_BLOCK, D_MODEL), row_block),
                  pl.BlockSpec((None, D_MODEL, 2 * D_FF), expert),
                  pl.BlockSpec((None, 1, 2 * D_FF), expert),
                  pl.BlockSpec((None, D_FF, D_MODEL), expert),
                  pl.BlockSpec((None, 1, D_MODEL), expert)],
        out_specs=pl.BlockSpec((MOE_BLOCK, D_MODEL), row_block),
    )
    return pl.pallas_call(
        _experts_kernel,
        out_shape=jax.ShapeDtypeStruct((P, D_MODEL), jnp.float32),
        grid_spec=grid_spec,
        compiler_params=_params(("arbitrary",)),
        name="experts",
    )(block_e, n_used, xs, w1b, b1, w2b, b2)


def _combine_kernel(src_ref, len_ref, dst_ref, rows_ref, gate_t_ref, x1_ref, ys_hbm, o_ref, ybuf_ref, sems):
    i = pl.program_id(0)
    n_tiles = pl.num_programs(0)
    cur = i % 2
    n_rows, tm = ybuf_ref.shape[1], x1_ref.shape[0]

    def fetch(tile, slot):
        _run_copies(src_ref, len_ref, dst_ref, tile,
                    lambda s, d, n: pltpu.make_async_copy(ys_hbm.at[pl.ds(d, n)], ybuf_ref.at[slot, pl.ds(s, n)],
                                                          sems.at[slot]))

    @pl.when(i == 0)
    def _():
        ybuf_ref[...] = jnp.zeros(ybuf_ref.shape, ybuf_ref.dtype)
        fetch(0, 0)

    @pl.when(i + 1 < n_tiles)
    def _():
        fetch(i + 1, 1 - cur)

    rows = pl.multiple_of(rows_ref[i], RUN_ALIGN)
    pltpu.make_async_copy(ys_hbm.at[pl.ds(0, rows)], ybuf_ref.at[cur, pl.ds(0, rows)], sems.at[cur]).wait()

    g = gate_t_ref[...]
    p_iota = lax.broadcasted_iota(jnp.int32, (tm, n_rows), 1)
    weights = jnp.zeros((tm, n_rows), jnp.float32)
    for k in range(TOP_K):
        pos_k = g[:, TOP_K + k:TOP_K + k + 1].astype(jnp.int32)
        weights = jnp.where(p_iota == pos_k, g[:, k:k + 1], weights)
    o_ref[...] = x1_ref[...] + jnp.dot(weights.astype(jnp.bfloat16), ybuf_ref[cur].astype(jnp.bfloat16),
                                       preferred_element_type=jnp.float32)


def _combine(run_src, run_len, run_dst, tile_rows, gate_t, x1, ys):
    T = x1.shape[0]
    tok = lambda w: pl.BlockSpec((TILE, w), lambda i, *_: (i, 0))
    grid_spec = pltpu.PrefetchScalarGridSpec(
        num_scalar_prefetch=4,
        grid=(T // TILE,),
        in_specs=[tok(LANES), tok(D_MODEL), pl.BlockSpec(memory_space=pl.ANY)],
        out_specs=tok(D_MODEL),
        scratch_shapes=[pltpu.VMEM((2, TILE_ROWS, D_MODEL), jnp.float32),
                        pltpu.SemaphoreType.DMA((2,))],
    )
    return pl.pallas_call(
        _combine_kernel,
        out_shape=jax.ShapeDtypeStruct((T, D_MODEL), jnp.float32),
        grid_spec=grid_spec,
        compiler_params=_params(("arbitrary",)),
        name="combine",
    )(run_src, run_len, run_dst, tile_rows, gate_t, x1, ys)


def _rotary_tables(positions):
    pos = positions.astype(jnp.float32)[..., None]
    lane = jnp.arange(LANES)
    half_r = RET_DK // 2
    inv_r = RET_ROPE_THETA ** (-jnp.linspace(0.0, 1.0, half_r, dtype=jnp.float32))
    ang = pos * inv_r[lane % half_r]
    c2 = jnp.cos(ang)
    s2 = jnp.sin(ang) * jnp.where(lane < half_r, -1.0, 1.0)
    half_d = ROT_DIM // 2
    inv_d = ROPE_THETA ** (-jnp.arange(0, ROT_DIM, 2, dtype=jnp.float32) / ROT_DIM)
    sub = lane % DIFF_DH
    ang_d = pos * inv_d[sub % half_d]
    cd, sd = jnp.cos(ang_d), jnp.sin(ang_d)
    ra = jnp.where(sub < ROT_DIM, cd, 1.0)
    rp = jnp.where((sub >= half_d) & (sub < ROT_DIM), sd, 0.0)
    rn = jnp.where(sub < half_d, -sd, 0.0)
    return c2, s2, ra, rp, rn


def kernel(x, positions, norm1_w, w_in, ret_log_decay_fwd, ret_log_decay_bwd, ret_norm_w, q_norm_w, k_norm_w, lambda_q1, lambda_k1, lambda_q2, lambda_k2, diff_norm_w, w_out, norm2_w, w_router, b_router, w1, b1, w2, b2):
    B, S, D = x.shape
    T = B * S
    f32 = jnp.float32
    bf16 = jnp.bfloat16
    x2 = x.reshape(T, D)

    proj = _in_proj(x2, norm1_w[0].reshape(1, D), w_in[0].astype(bf16))
    tabs = _rotary_tables(positions)
    dup = lambda w: jnp.concatenate([w, w]).reshape(1, LANES).astype(f32)
    rq_r, rk_r, qs, ks = _prep(proj, tabs, dup(q_norm_w[0]), dup(k_norm_w[0]), B, S)

    y_ret = _retention(ret_log_decay_fwd[0].astype(f32), ret_log_decay_bwd[0].astype(f32),
                       rq_r, rk_r, proj, ret_norm_w[0].reshape(1, RET_WIDTH).astype(f32), B, S)

    lam = (jnp.exp(jnp.sum(lambda_q1[0].astype(f32) * lambda_k1[0].astype(f32)))
           - jnp.exp(jnp.sum(lambda_q2[0].astype(f32) * lambda_k2[0].astype(f32))) + LAMBDA_INIT)
    lam_row = jnp.full((1, LANES), lam, f32)
    y_diff = _diff_attn(qs, ks, proj, lam_row, diff_norm_w[0].reshape(1, DIFF_DV).astype(f32), B, S)

    x1, h2, pos, gate_t, len_t, off_t, tot_t = _out_router(
        x2, y_ret, y_diff, w_out[0].astype(bf16), norm2_w[0].reshape(1, D),
        w_router[0].T.astype(f32), b_router[0].reshape(N_EXPERTS, 1).astype(f32))

    n_tiles = T // TILE
    run_len = len_t[:, :, 0].astype(jnp.int32)
    total = tot_t[:, 0].astype(jnp.int32)
    padded = ((total + MOE_BLOCK - 1) // MOE_BLOCK) * MOE_BLOCK
    pad_end = jnp.cumsum(padded)
    pad_start = pad_end - padded
    run_dst = pad_start[None, :] + off_t[:, :, 0].astype(jnp.int32)
    run_src = jnp.cumsum(run_len, axis=1) - run_len
    tile_rows = jnp.sum(run_len, axis=1)
    P = T * TOP_K + n_tiles * N_EXPERTS * RUN_ALIGN + N_EXPERTS * MOE_BLOCK
    n_blocks = P // MOE_BLOCK
    block_row = jnp.arange(n_blocks, dtype=jnp.int32) * MOE_BLOCK
    block_e = jnp.minimum(jnp.sum((pad_end[None, :] <= block_row[:, None]).astype(jnp.int32), axis=1),
                          N_EXPERTS - 1)
    n_used = (pad_end[-1:] // MOE_BLOCK).astype(jnp.int32)
    runs = (run_src.reshape(-1), run_len.reshape(-1), run_dst.reshape(-1), tile_rows)

    xs = _dispatch(*runs, pad_start + total, padded - total, n_used, pos, h2, P)
    ys = _experts(block_e, n_used, xs, w1[0].astype(bf16), b1[0].reshape(N_EXPERTS, 1, 2 * D_FF),
                  w2[0].astype(bf16), b2[0].reshape(N_EXPERTS, 1, D))
    out = _combine(*runs, gate_t, x1, ys)
    return out.reshape(B, S, D)
```

```python
import jax
import jax.numpy as jnp
from jax import lax
from jax.experimental import pallas as pl
from jax.experimental.pallas import tpu as pltpu

EPS = 1e-6
D_MODEL = 1024
RET_HEADS = 4
RET_DK = 128
RET_WIDTH = 512
RET_ROPE_THETA = 10000.0
DIFF_HEADS = 4
DIFF_DH = 64
DIFF_DV = 128
DIFF_WIDTH = 512
ROPE_THETA = 500000.0
ROT_DIM = DIFF_DH // 4
D_IN_PROJ = 3584
N_EXPERTS = 32
TOP_K = 4
D_FF = 1024
SWIGLU_LIMIT = 7.0
SWIGLU_ALPHA = 1.702
LAMBDA_INIT = 0.8 - 0.6 * 1.0

LOG2_E = 1.4426950408889634
LANES = 128
SUBLANES = 8
VMEM_LIMIT = 56 * 1024 * 1024

COL_RQ, COL_RK, COL_RV, COL_RG, COL_DQ, COL_DK, COL_DV = 0, 4, 8, 12, 16, 20, 24

TM_PROJ = 512
TS_PREP = 512
RET_CHUNK = 128
RET_UNROLL = 4
TQ_ATTN = 512
TK_ATTN = 2048
TILE = 512
MOE_BLOCK = 512
RUN_ALIGN = SUBLANES
TILE_ROWS = TOP_K * TILE + N_EXPERTS * RUN_ALIGN


def _params(sem, **kw):
    return pltpu.CompilerParams(dimension_semantics=sem, vmem_limit_bytes=VMEM_LIMIT, **kw)


def _in_proj_kernel(x_ref, nw_ref, w_ref, o_ref):
    x = x_ref[...]
    ms = jnp.mean(x * x, axis=-1, keepdims=True)
    h = (x * lax.rsqrt(ms + EPS) * nw_ref[...]).astype(jnp.bfloat16)
    n_col = o_ref.shape[1]
    for c in range(0, n_col, 512):
        o_ref[:, c:c + 512] = jnp.dot(h, w_ref[:, c:c + 512],
                                      preferred_element_type=jnp.float32).astype(o_ref.dtype)


def _in_proj(x2, nw, w_bf16):
    T = x2.shape[0]
    return pl.pallas_call(
        _in_proj_kernel,
        out_shape=jax.ShapeDtypeStruct((T, D_IN_PROJ), jnp.bfloat16),
        grid=(T // TM_PROJ,),
        in_specs=[pl.BlockSpec((TM_PROJ, D_MODEL), lambda i: (i, 0)),
                  pl.BlockSpec((1, D_MODEL), lambda i: (0, 0)),
                  pl.BlockSpec((D_MODEL, D_IN_PROJ), lambda i: (0, 0))],
        out_specs=pl.BlockSpec((TM_PROJ, D_IN_PROJ), lambda i: (i, 0)),
        compiler_params=_params(("arbitrary",)),
        name="in_proj",
    )(x2, nw, w_bf16)


def _prep_kernel(rq_ref, rk_ref, dq_ref, dk_ref, c2_ref, s2_ref, ra_ref, rp_ref, rn_ref,
                 qw_ref, kw_ref, rqo_ref, rko_ref, qs_ref, ks_ref):
    ts = rq_ref.shape[0]
    c2 = c2_ref[...]
    s2 = s2_ref[...]
    ra = ra_ref[...]
    rp = rp_ref[...]
    rn = rn_ref[...]
    lane = lax.broadcasted_iota(jnp.int32, (ts, LANES), 1)
    lo = lane < DIFF_DH

    def qk_norm_rot(x, w):
        x2 = x * x
        s_lo = jnp.sum(jnp.where(lo, x2, 0.0), axis=-1, keepdims=True)
        s_hi = jnp.sum(jnp.where(lo, 0.0, x2), axis=-1, keepdims=True)
        ms = jnp.where(lo, s_lo, s_hi) * (1.0 / DIFF_DH)
        xn = x * lax.rsqrt(ms + EPS) * w
        return xn * ra + pltpu.roll(xn, ROT_DIM // 2, 1) * rp + pltpu.roll(xn, LANES - ROT_DIM // 2, 1) * rn

    for h in range(RET_HEADS):
        sl = slice(h * LANES, (h + 1) * LANES)
        q = rq_ref[:, sl].astype(jnp.float32)
        k = rk_ref[:, sl].astype(jnp.float32)
        rqo_ref[:, sl] = (q * c2 + pltpu.roll(q, RET_DK // 2, 1) * s2).astype(rqo_ref.dtype)
        rko_ref[:, sl] = ((k * c2 + pltpu.roll(k, RET_DK // 2, 1) * s2) * (RET_DK ** -0.5)).astype(rko_ref.dtype)
    for h in range(DIFF_HEADS):
        sl = slice(h * LANES, (h + 1) * LANES)
        q = qk_norm_rot(dq_ref[:, sl].astype(jnp.float32), qw_ref[...]) * (DIFF_DH ** -0.5 * LOG2_E)
        k = qk_norm_rot(dk_ref[:, sl].astype(jnp.float32), kw_ref[...])
        qs_ref[h, 0] = jnp.where(lo, q, 0.0).astype(qs_ref.dtype)
        qs_ref[h, 1] = jnp.where(lo, 0.0, q).astype(qs_ref.dtype)
        ks_ref[:, sl] = k.astype(ks_ref.dtype)


def _prep(proj, tabs, qw2, kw2, B, S):
    T = B * S
    n_s = S // TS_PREP
    col = lambda cb: pl.BlockSpec((TS_PREP, 512), lambda b, i: (b * n_s + i, cb))
    tab = pl.BlockSpec((None, TS_PREP, LANES), lambda b, i: (b, i, 0))
    vec = pl.BlockSpec((1, LANES), lambda b, i: (0, 0))
    out_tok = pl.BlockSpec((TS_PREP, 512), lambda b, i: (b * n_s + i, 0))
    return pl.pallas_call(
        _prep_kernel,
        out_shape=(jax.ShapeDtypeStruct((T, 512), jnp.bfloat16),
                   jax.ShapeDtypeStruct((T, 512), jnp.bfloat16),
                   jax.ShapeDtypeStruct((B, DIFF_HEADS, 2, S, LANES), jnp.bfloat16),
                   jax.ShapeDtypeStruct((T, 512), jnp.bfloat16)),
        grid=(B, n_s),
        in_specs=[col(COL_RQ // 4), col(COL_RK // 4), col(COL_DQ // 4), col(COL_DK // 4),
                  tab, tab, tab, tab, tab, vec, vec],
        out_specs=(out_tok, out_tok,
                   pl.BlockSpec((None, DIFF_HEADS, 2, TS_PREP, LANES), lambda b, i: (b, 0, 0, i, 0)),
                   out_tok),
        compiler_params=_params(("arbitrary", "arbitrary")),
        name="prep",
    )(proj, proj, proj, proj, *tabs, qw2, kw2)


def _retention_kernel(ldf_ref, ldb_ref, q_ref, k_ref, v_ref, g_ref, nw_ref, o_ref, sb_ref):
    C = RET_CHUNK
    S = q_ref.shape[0]
    n_chunks = S // C
    h = pl.program_id(1)
    ldf = ldf_ref[h]
    ldb = ldb_ref[h]
    row = lax.broadcasted_iota(jnp.int32, (C, C), 0).astype(jnp.float32)
    colm = lax.broadcasted_iota(jnp.int32, (C, C), 1).astype(jnp.float32)
    dist = row - colm
    decay = jnp.where(dist >= 0, jnp.exp(ldf * jnp.maximum(dist, 0.0)), jnp.exp(ldb * jnp.maximum(-dist, 0.0)))
    idx = lax.broadcasted_iota(jnp.int32, (C, 1), 0).astype(jnp.float32)
    q_dec_f = jnp.exp(ldf * (idx + 1.0))
    k_dec_f = jnp.exp(ldf * (C - 1.0 - idx))
    q_dec_b = jnp.exp(ldb * (C - idx))
    k_dec_b = jnp.exp(ldb * idx)
    chunk_dec_f = jnp.exp(ldf * C)
    chunk_dec_b = jnp.exp(ldb * C)
    f32 = jnp.float32
    bf16 = jnp.bfloat16

    def kv_state(k, v, k_dec):
        kd = (k.astype(f32) * k_dec).astype(bf16)
        return lax.dot_general(kd, v, (((0,), (0,)), ((), ())), preferred_element_type=f32)

    def bwd_step(i, state):
        c = n_chunks - 1 - i
        r0 = pl.multiple_of(c * C, C)
        sb_ref[c] = state
        return state * chunk_dec_b + kv_state(k_ref[pl.ds(r0, C), :], v_ref[pl.ds(r0, C), :], k_dec_b)

    lax.fori_loop(0, n_chunks, bwd_step, jnp.zeros((RET_DK, LANES), f32), unroll=RET_UNROLL)

    def fwd_step(c, state):
        r0 = pl.multiple_of(c * C, C)
        q = q_ref[pl.ds(r0, C), :]
        k = k_ref[pl.ds(r0, C), :]
        v = v_ref[pl.ds(r0, C), :]
        scores = lax.dot_general(q, k, (((1,), (1,)), ((), ())), preferred_element_type=f32) * decay
        y = jnp.dot(scores.astype(bf16), v, preferred_element_type=f32)
        qf = q.astype(f32)
        y += jnp.dot((qf * q_dec_f).astype(bf16), state.astype(bf16), preferred_element_type=f32)
        y += jnp.dot((qf * q_dec_b).astype(bf16), sb_ref[c].astype(bf16), preferred_element_type=f32)
        yn = y * lax.rsqrt(jnp.mean(y * y, axis=-1, keepdims=True) + EPS) * nw_ref[...]
        g = g_ref[pl.ds(r0, C), :].astype(f32)
        o_ref[pl.ds(r0, C), :] = (yn * (g * jax.nn.sigmoid(g))).astype(o_ref.dtype)
        return state * chunk_dec_f + kv_state(k, v, k_dec_f)

    lax.fori_loop(0, n_chunks, fwd_step, jnp.zeros((RET_DK, LANES), f32), unroll=RET_UNROLL)


def _retention(ldf, ldb, rq_r, rk_r, proj, nw, B, S):
    T = B * S
    smem = pl.BlockSpec(memory_space=pltpu.SMEM)
    seq = lambda cb: pl.BlockSpec((S, LANES), lambda b, h: (b, cb + h))
    return pl.pallas_call(
        _retention_kernel,
        out_shape=jax.ShapeDtypeStruct((T, RET_WIDTH), jnp.bfloat16),
        grid=(B, RET_HEADS),
        in_specs=[smem, smem, seq(0), seq(0), seq(COL_RV), seq(COL_RG),
                  pl.BlockSpec((1, LANES), lambda b, h: (0, h))],
        out_specs=seq(0),
        scratch_shapes=[pltpu.VMEM((S // RET_CHUNK, RET_DK, LANES), jnp.float32)],
        compiler_params=_params(("arbitrary", "arbitrary")),
        name="retention",
    )(ldf, ldb, rq_r, rk_r, proj, proj, nw)


def _diff_attn_kernel(q_ref, k_ref, v_ref, lam_ref, nw_ref, o_ref, m_ref, l_ref, acc_ref):
    tq = q_ref.shape[1]
    S = k_ref.shape[0]
    f32 = jnp.float32
    q = q_ref[...].reshape(2 * tq, LANES)
    m_ref[...] = jnp.full(m_ref.shape, -jnp.inf, f32)
    l_ref[...] = jnp.zeros(l_ref.shape, f32)
    acc_ref[...] = jnp.zeros(acc_ref.shape, f32)
    n_tiles = TK_ATTN // LANES

    def kv_step(j, carry):
        r0 = pl.multiple_of(j * TK_ATTN, TK_ATTN)
        k = k_ref[pl.ds(r0, TK_ATTN), :]
        v = v_ref[pl.ds(r0, TK_ATTN), :]
        s = lax.dot_general(q, k, (((1,), (1,)), ((), ())), preferred_element_type=f32)
        tiles = [s[:, c * LANES:(c + 1) * LANES] for c in range(n_tiles)]
        part = tiles[0]
        for t in tiles[1:]:
            part = jnp.maximum(part, t)
        m_prev = m_ref[...]
        m_new = jnp.maximum(m_prev, jnp.max(part, axis=-1, keepdims=True))
        alpha = jnp.exp2(m_prev - m_new)
        probs = [jnp.exp2(t - m_new) for t in tiles]
        psum = probs[0]
        for p in probs[1:]:
            psum = psum + p
        l_ref[...] = alpha * l_ref[...] + psum
        p_bf = jnp.concatenate([p.astype(jnp.bfloat16) for p in probs], axis=1)
        acc_ref[...] = alpha * acc_ref[...] + jnp.dot(p_bf, v, preferred_element_type=f32)
        m_ref[...] = m_new
        return carry

    lax.fori_loop(0, S // TK_ATTN, kv_step, 0)
    o = acc_ref[...] / jnp.sum(l_ref[...], axis=-1, keepdims=True)
    d = o[:tq] - lam_ref[...] * o[tq:]
    dn = d * lax.rsqrt(jnp.mean(d * d, axis=-1, keepdims=True) + EPS) * nw_ref[...]
    o_ref[...] = (dn * (1.0 - LAMBDA_INIT)).astype(o_ref.dtype)


def _diff_attn(qs, ks, proj, lam, nw, B, S):
    T = B * S
    n_q = S // TQ_ATTN
    one = pl.BlockSpec((1, LANES), lambda b, h, i: (0, 0))
    return pl.pallas_call(
        _diff_attn_kernel,
        out_shape=jax.ShapeDtypeStruct((T, DIFF_WIDTH), jnp.bfloat16),
        grid=(B, DIFF_HEADS, n_q),
        in_specs=[pl.BlockSpec((None, None, 2, TQ_ATTN, LANES), lambda b, h, i: (b, h, 0, i, 0)),
                  pl.BlockSpec((S, LANES), lambda b, h, i: (b, h)),
                  pl.BlockSpec((S, LANES), lambda b, h, i: (b, COL_DV + h)),
                  one, one],
        out_specs=pl.BlockSpec((TQ_ATTN, LANES), lambda b, h, i: (b * n_q + i, h)),
        scratch_shapes=[pltpu.VMEM((2 * TQ_ATTN, LANES), jnp.float32)] * 3,
        compiler_params=_params(("arbitrary", "arbitrary", "arbitrary")),
        name="diff_attn",
    )(qs, ks, proj, lam, nw)


def _out_router_kernel(x_ref, yr_ref, yd_ref, wo_ref, n2_ref, wrt_ref, br_ref,
                       x1_ref, h2_ref, pos_ref, gate_t_ref, len_ref, off_ref, tot_ref):
    tm = x_ref.shape[0]
    f32 = jnp.float32
    bf16 = jnp.bfloat16

    @pl.when(pl.program_id(0) == 0)
    def _():
        tot_ref[...] = jnp.zeros(tot_ref.shape, f32)

    att = jnp.dot(yr_ref[...], wo_ref[:RET_WIDTH, :], preferred_element_type=f32)
    att += jnp.dot(yd_ref[...], wo_ref[RET_WIDTH:, :], preferred_element_type=f32)
    x1 = x_ref[...] + att
    x1_ref[...] = x1
    h2 = x1 * lax.rsqrt(jnp.mean(x1 * x1, axis=-1, keepdims=True) + EPS) * n2_ref[...]
    h2_ref[...] = h2.astype(h2_ref.dtype)
    logits = lax.dot_general(wrt_ref[...], h2, (((1,), (1,)), ((), ())),
                             precision=lax.Precision.HIGHEST, preferred_element_type=f32) + br_ref[...]
    e_iota = lax.broadcasted_iota(jnp.int32, (N_EXPERTS, tm), 0)
    work = logits
    vals, hots = [], []
    for _ in range(TOP_K):
        mx = jnp.max(work, axis=0, keepdims=True)
        ix = jnp.min(jnp.where(work == mx, e_iota, N_EXPERTS), axis=0, keepdims=True)
        hot = e_iota == ix
        vals.append(mx)
        hots.append(hot)
        work = jnp.where(hot, -jnp.inf, work)
    exps = [jnp.exp(v - vals[0]) for v in vals]
    denom = exps[0] + exps[1] + exps[2] + exps[3]
    gates = [e / denom for e in exps]
    sel = jnp.zeros((N_EXPERTS, tm), f32)
    for hot in hots:
        sel = jnp.where(hot, 1.0, sel)
    t_row = lax.broadcasted_iota(jnp.int32, (tm, tm), 0)
    t_col = lax.broadcasted_iota(jnp.int32, (tm, tm), 1)
    upper = jnp.where(t_row < t_col, 1.0, 0.0).astype(bf16)
    rank = jnp.dot(sel.astype(bf16), upper, preferred_element_type=f32)
    cnt = jnp.sum(sel, axis=1, keepdims=True)
    run_units = jnp.floor((cnt + (RUN_ALIGN - 1.0)) * (1.0 / RUN_ALIGN))
    run_len = jnp.broadcast_to(run_units * RUN_ALIGN, (N_EXPERTS, LANES))
    e_row = lax.broadcasted_iota(jnp.int32, (N_EXPERTS, N_EXPERTS), 0)
    e_col = lax.broadcasted_iota(jnp.int32, (N_EXPERTS, N_EXPERTS), 1)
    lower = jnp.where(e_col < e_row, 1.0, 0.0).astype(bf16)
    run_start = jnp.dot(lower, jnp.broadcast_to(run_units, (N_EXPERTS, LANES)).astype(bf16),
                        preferred_element_type=f32) * RUN_ALIGN
    pos_full = rank + run_start[:, 0:1]
    pos = [jnp.sum(jnp.where(hot, pos_full, 0.0), axis=0, keepdims=True) for hot in hots]
    for k in range(TOP_K):
        pos_ref[k:k + 1, :] = pos[k].astype(jnp.int32)
    rows = jnp.concatenate(gates + pos + [jnp.zeros((LANES - 2 * TOP_K, tm), f32)], axis=0)
    gate_t_ref[...] = rows.T
    len_ref[0] = run_len
    off_ref[0] = tot_ref[...]
    tot_ref[...] = tot_ref[...] + run_len


def _out_router(x2, y_ret, y_diff, wo_bf16, n2w, wrt, br):
    T = x2.shape[0]
    n_tiles = T // TILE
    tok = lambda w: pl.BlockSpec((TILE, w), lambda i: (i, 0))
    const = lambda s: pl.BlockSpec(s, lambda i: (0, 0))
    per_tile = pl.BlockSpec((1, N_EXPERTS, LANES), lambda i: (i, 0, 0))
    return pl.pallas_call(
        _out_router_kernel,
        out_shape=(jax.ShapeDtypeStruct((T, D_MODEL), jnp.float32),
                   jax.ShapeDtypeStruct((T, D_MODEL), jnp.bfloat16),
                   jax.ShapeDtypeStruct((TOP_K, T), jnp.int32),
                   jax.ShapeDtypeStruct((T, LANES), jnp.float32),
                   jax.ShapeDtypeStruct((n_tiles, N_EXPERTS, LANES), jnp.float32),
                   jax.ShapeDtypeStruct((n_tiles, N_EXPERTS, LANES), jnp.float32),
                   jax.ShapeDtypeStruct((N_EXPERTS, LANES), jnp.float32)),
        grid=(n_tiles,),
        in_specs=[tok(D_MODEL), tok(RET_WIDTH), tok(DIFF_WIDTH), const((D_MODEL, D_MODEL)),
                  const((1, D_MODEL)), const((N_EXPERTS, D_MODEL)), const((N_EXPERTS, 1))],
        out_specs=(tok(D_MODEL), tok(D_MODEL), pl.BlockSpec((TOP_K, TILE), lambda i: (0, i)), tok(LANES),
                   per_tile, per_tile, const((N_EXPERTS, LANES))),
        compiler_params=_params(("arbitrary",)),
        name="out_router",
    )(x2, y_ret, y_diff, wo_bf16, n2w, wrt, br)


def _run_copies(src_ref, len_ref, dst_ref, tile, make_copy):
    for e in range(N_EXPERTS):
        n = pl.multiple_of(len_ref[tile * N_EXPERTS + e], RUN_ALIGN)
        s = pl.multiple_of(src_ref[tile * N_EXPERTS + e], RUN_ALIGN)
        d = pl.multiple_of(dst_ref[tile * N_EXPERTS + e], RUN_ALIGN)

        @pl.when(n > 0)
        def _():
            make_copy(s, d, n).start()


def _dispatch_kernel(src_ref, len_ref, dst_ref, rows_ref, zlo_ref, zlen_ref, nu_ref,
                     pos_ref, h2_ref, xs_hbm, xbuf_ref, zero_ref, sems, zero_sem):
    i = pl.program_id(0)
    n_tiles = pl.num_programs(0)
    cur = i % 2
    n_rows, tm = xbuf_ref.shape[1], h2_ref.shape[0]

    @pl.when(i == 0)
    def _():
        zero_ref[...] = jnp.zeros(zero_ref.shape, zero_ref.dtype)

        def pad_copy(e):
            n = pl.multiple_of(zlen_ref[e], RUN_ALIGN)
            lo = pl.multiple_of(zlo_ref[e], RUN_ALIGN)
            return pltpu.make_async_copy(zero_ref.at[pl.ds(0, n)], xs_hbm.at[pl.ds(lo, n)], zero_sem)

        def tail_copy(j):
            return pltpu.make_async_copy(zero_ref, xs_hbm.at[pl.ds(j * MOE_BLOCK, MOE_BLOCK)], zero_sem)

        def guarded(copy, op):
            def body(e, c):
                @pl.when(zlen_ref[e] > 0)
                def _():
                    op(copy(e))
                return c
            return body

        lax.fori_loop(0, N_EXPERTS, guarded(pad_copy, lambda cp: cp.start()), 0)
        lax.fori_loop(0, N_EXPERTS, guarded(pad_copy, lambda cp: cp.wait()), 0)
        n_blocks = xs_hbm.shape[0] // MOE_BLOCK
        lax.fori_loop(nu_ref[0], n_blocks, lambda j, c: (tail_copy(j).start(), c)[1], 0)
        lax.fori_loop(nu_ref[0], n_blocks, lambda j, c: (tail_copy(j).wait(), c)[1], 0)

    p_iota = lax.broadcasted_iota(jnp.int32, (n_rows, tm), 0)
    onehot = jnp.zeros((n_rows, tm), jnp.float32)
    for k in range(TOP_K):
        onehot = jnp.where(p_iota == pos_ref[k:k + 1, :], 1.0, onehot)
    xbuf_ref[cur] = jnp.dot(onehot.astype(jnp.bfloat16), h2_ref[...], preferred_element_type=jnp.float32)

    _run_copies(src_ref, len_ref, dst_ref, i,
                lambda s, d, n: pltpu.make_async_copy(xbuf_ref.at[cur, pl.ds(s, n)], xs_hbm.at[pl.ds(d, n)],
                                                      sems.at[cur]))

    def wait_tile(tile, slot):
        rows = pl.multiple_of(rows_ref[tile], RUN_ALIGN)
        pltpu.make_async_copy(xbuf_ref.at[slot, pl.ds(0, rows)], xs_hbm.at[pl.ds(0, rows)], sems.at[slot]).wait()

    @pl.when(i > 0)
    def _():
        wait_tile(i - 1, 1 - cur)

    @pl.when(i == n_tiles - 1)
    def _():
        wait_tile(i, cur)


def _dispatch(run_src, run_len, run_dst, tile_rows, zero_lo, zero_len, n_used, pos, h2, P):
    T = h2.shape[0]
    n_pre = 7
    grid_spec = pltpu.PrefetchScalarGridSpec(
        num_scalar_prefetch=n_pre,
        grid=(T // TILE,),
        in_specs=[pl.BlockSpec((TOP_K, TILE), lambda i, *_: (0, i)),
                  pl.BlockSpec((TILE, D_MODEL), lambda i, *_: (i, 0))],
        out_specs=pl.BlockSpec(memory_space=pl.ANY),
        scratch_shapes=[pltpu.VMEM((2, TILE_ROWS, D_MODEL), jnp.float32),
                        pltpu.VMEM((MOE_BLOCK, D_MODEL), jnp.float32),
                        pltpu.SemaphoreType.DMA((2,)),
                        pltpu.SemaphoreType.DMA(())],
    )
    return pl.pallas_call(
        _dispatch_kernel,
        out_shape=jax.ShapeDtypeStruct((P, D_MODEL), jnp.float32),
        grid_spec=grid_spec,
        compiler_params=_params(("arbitrary",), has_side_effects=True),
        name="dispatch",
    )(run_src, run_len, run_dst, tile_rows, zero_lo, zero_len, n_used, pos, h2)


def _experts_kernel(be_ref, nu_ref, xs_ref, w1_ref, b1_ref, w2_ref, b2_ref, ys_ref, w1b_ref, w2b_ref):
    j = pl.program_id(0)

    @pl.when(j < nu_ref[0])
    def _():
        @pl.when((j == 0) | (be_ref[j] != be_ref[jnp.maximum(j - 1, 0)]))
        def _():
            w1b_ref[...] = w1_ref[...].astype(jnp.bfloat16)
            w2b_ref[...] = w2_ref[...].astype(jnp.bfloat16)

        x = xs_ref[...].astype(jnp.bfloat16)
        u = jnp.dot(x, w1b_ref[...], preferred_element_type=jnp.float32) + b1_ref[...]
        glu = jnp.minimum(u[:, :D_FF], SWIGLU_LIMIT)
        lin = jnp.clip(u[:, D_FF:], -SWIGLU_LIMIT, SWIGLU_LIMIT)
        act = glu * jax.nn.sigmoid(SWIGLU_ALPHA * glu) * (lin + 1.0)
        ys_ref[...] = jnp.dot(act.astype(jnp.bfloat16), w2b_ref[...],
                              preferred_element_type=jnp.float32) + b2_ref[...]

    @pl.when(j >= nu_ref[0])
    def _():
        ys_ref[...] = jnp.zeros(ys_ref.shape, ys_ref.dtype)


def _experts(block_e, n_used, xs, w1, b1, w2, b2):
    P = xs.shape[0]
    row_block = lambda j, be, nu: (j, 0)
    expert = lambda j, be, nu: (be[j], 0, 0)
    grid_spec = pltpu.PrefetchScalarGridSpec(
        num_scalar_prefetch=2,
        grid=(P // MOE_BLOCK,),
        in_specs=[pl.BlockSpec((MOE_BLOCK, D_MODEL), row_block),
                  pl.BlockSpec((None, D_MODEL, 2 * D_FF), expert),
                  pl.BlockSpec((None, 1, 2 * D_FF), expert),
                  pl.BlockSpec((None, D_FF, D_MODEL), expert),
                  pl.BlockSpec((None, 1, D_MODEL), expert)],
        out_specs=pl.BlockSpec((MOE_BLOCK, D_MODEL), row_block),
        scratch_shapes=[pltpu.VMEM((D_MODEL, 2 * D_FF), jnp.bfloat16),
                        pltpu.VMEM((D_FF, D_MODEL), jnp.bfloat16)],
    )
    return pl.pallas_call(
        _experts_kernel,
        out_shape=jax.ShapeDtypeStruct((P, D_MODEL), jnp.float32),
        grid_spec=grid_spec,
        compiler_params=_params(("arbitrary",)),
        name="experts",
    )(block_e, n_used, xs, w1, b1, w2, b2)


def _combine_kernel(src_ref, len_ref, dst_ref, rows_ref, gate_t_ref, x1_ref, ys_hbm, o_ref, ybuf_ref, sems):
    i = pl.program_id(0)
    n_tiles = pl.num_programs(0)
    cur = i % 2
    n_rows, tm = ybuf_ref.shape[1], x1_ref.shape[0]

    def fetch(tile, slot):
        _run_copies(src_ref, len_ref, dst_ref, tile,
                    lambda s, d, n: pltpu.make_async_copy(ys_hbm.at[pl.ds(d, n)], ybuf_ref.at[slot, pl.ds(s, n)],
                                                          sems.at[slot]))

    @pl.when(i == 0)
    def _():
        ybuf_ref[...] = jnp.zeros(ybuf_ref.shape, ybuf_ref.dtype)
        fetch(0, 0)

    @pl.when(i + 1 < n_tiles)
    def _():
        fetch(i + 1, 1 - cur)

    rows = pl.multiple_of(rows_ref[i], RUN_ALIGN)
    pltpu.make_async_copy(ys_hbm.at[pl.ds(0, rows)], ybuf_ref.at[cur, pl.ds(0, rows)], sems.at[cur]).wait()

    g = gate_t_ref[...]
    p_iota = lax.broadcasted_iota(jnp.int32, (tm, n_rows), 1)
    weights = jnp.zeros((tm, n_rows), jnp.float32)
    for k in range(TOP_K):
        pos_k = g[:, TOP_K + k:TOP_K + k + 1].astype(jnp.int32)
        weights = jnp.where(p_iota == pos_k, g[:, k:k + 1], weights)
    o_ref[...] = x1_ref[...] + jnp.dot(weights.astype(jnp.bfloat16), ybuf_ref[cur].astype(jnp.bfloat16),
                                       preferred_element_type=jnp.float32)


def _combine(run_src, run_len, run_dst, tile_rows, gate_t, x1, ys):
    T = x1.shape[0]
    tok = lambda w: pl.BlockSpec((TILE, w), lambda i, *_: (i, 0))
    grid_spec = pltpu.PrefetchScalarGridSpec(
        num_scalar_prefetch=4,
        grid=(T // TILE,),
        in_specs=[tok(LANES), tok(D_MODEL), pl.BlockSpec(memory_space=pl.ANY)],
        out_specs=tok(D_MODEL),
        scratch_shapes=[pltpu.VMEM((2, TILE_ROWS, D_MODEL), jnp.float32),
                        pltpu.SemaphoreType.DMA((2,))],
    )
    return pl.pallas_call(
        _combine_kernel,
        out_shape=jax.ShapeDtypeStruct((T, D_MODEL), jnp.float32),
        grid_spec=grid_spec,
        compiler_params=_params(("arbitrary",)),
        name="combine",
    )(run_src, run_len, run_dst, tile_rows, gate_t, x1, ys)


def _rotary_tables(positions):
    pos = positions.astype(jnp.float32)[..., None]
    lane = jnp.arange(LANES)
    half_r = RET_DK // 2
    inv_r = RET_ROPE_THETA ** (-jnp.linspace(0.0, 1.0, half_r, dtype=jnp.float32))
    ang = pos * inv_r[lane % half_r]
    c2 = jnp.cos(ang)
    s2 = jnp.sin(ang) * jnp.where(lane < half_r, -1.0, 1.0)
    half_d = ROT_DIM // 2
    inv_d = ROPE_THETA ** (-jnp.arange(0, ROT_DIM, 2, dtype=jnp.float32) / ROT_DIM)
    sub = lane % DIFF_DH
    ang_d = pos * inv_d[sub % half_d]
    cd, sd = jnp.cos(ang_d), jnp.sin(ang_d)
    ra = jnp.where(sub < ROT_DIM, cd, 1.0)
    rp = jnp.where((sub >= half_d) & (sub < ROT_DIM), sd, 0.0)
    rn = jnp.where(sub < half_d, -sd, 0.0)
    return c2, s2, ra, rp, rn


def kernel(x, positions, norm1_w, w_in, ret_log_decay_fwd, ret_log_decay_bwd, ret_norm_w, q_norm_w, k_norm_w, lambda_q1, lambda_k1, lambda_q2, lambda_k2, diff_norm_w, w_out, norm2_w, w_router, b_router, w1, b1, w2, b2):
    B, S, D = x.shape
    T = B * S
    f32 = jnp.float32
    bf16 = jnp.bfloat16
    x2 = x.reshape(T, D)

    proj = _in_proj(x2, norm1_w[0].reshape(1, D), w_in[0].astype(bf16))
    tabs = _rotary_tables(positions)
    dup = lambda w: jnp.concatenate([w, w]).reshape(1, LANES).astype(f32)
    rq_r, rk_r, qs, ks = _prep(proj, tabs, dup(q_norm_w[0]), dup(k_norm_w[0]), B, S)

    y_ret = _retention(ret_log_decay_fwd[0].astype(f32), ret_log_decay_bwd[0].astype(f32),
                       rq_r, rk_r, proj, ret_norm_w[0].reshape(1, RET_WIDTH).astype(f32), B, S)

    lam = (jnp.exp(jnp.sum(lambda_q1[0].astype(f32) * lambda_k1[0].astype(f32)))
           - jnp.exp(jnp.sum(lambda_q2[0].astype(f32) * lambda_k2[0].astype(f32))) + LAMBDA_INIT)
    lam_row = jnp.full((1, LANES), lam, f32)
    y_diff = _diff_attn(qs, ks, proj, lam_row, diff_norm_w[0].reshape(1, DIFF_DV).astype(f32), B, S)

    x1, h2, pos, gate_t, len_t, off_t, tot_t = _out_router(
        x2, y_ret, y_diff, w_out[0].astype(bf16), norm2_w[0].reshape(1, D),
        w_router[0].T.astype(f32), b_router[0].reshape(N_EXPERTS, 1).astype(f32))

    n_tiles = T // TILE
    run_len = len_t[:, :, 0].astype(jnp.int32)
    total = tot_t[:, 0].astype(jnp.int32)
    padded = ((total + MOE_BLOCK - 1) // MOE_BLOCK) * MOE_BLOCK
    pad_end = jnp.cumsum(padded)
    pad_start = pad_end - padded
    run_dst = pad_start[None, :] + off_t[:, :, 0].astype(jnp.int32)
    run_src = jnp.cumsum(run_len, axis=1) - run_len
    tile_rows = jnp.sum(run_len, axis=1)
    P = T * TOP_K + n_tiles * N_EXPERTS * RUN_ALIGN + N_EXPERTS * MOE_BLOCK
    n_blocks = P // MOE_BLOCK
    block_row = jnp.arange(n_blocks, dtype=jnp.int32) * MOE_BLOCK
    block_e = jnp.minimum(jnp.sum((pad_end[None, :] <= block_row[:, None]).astype(jnp.int32), axis=1),
                          N_EXPERTS - 1)
    n_used = (pad_end[-1:] // MOE_BLOCK).astype(jnp.int32)
    runs = (run_src.reshape(-1), run_len.reshape(-1), run_dst.reshape(-1), tile_rows)

    xs = _dispatch(*runs, pad_start + total, padded - total, n_used, pos, h2, P)
    ys = _experts(block_e, n_used, xs, w1[0], b1[0].reshape(N_EXPERTS, 1, 2 * D_FF),
                  w2[0], b2[0].reshape(N_EXPERTS, 1, D))
    out = _combine(*runs, gate_t, x1, ys)
    return out.reshape(B, S, D)
```

```python
import functools

import jax
import jax.numpy as jnp
from jax import lax
from jax.experimental import pallas as pl
from jax.experimental.pallas import tpu as pltpu

EPS = 1e-6
D_MODEL = 1024
RET_HEADS = 4
RET_DK = 128
RET_WIDTH = 512
RET_ROPE_THETA = 10000.0
DIFF_HEADS = 4
DIFF_DH = 64
DIFF_DV = 128
DIFF_WIDTH = 512
ROPE_THETA = 500000.0
ROT_DIM = DIFF_DH // 4
D_IN_PROJ = 3584
N_EXPERTS = 32
TOP_K = 4
D_FF = 1024
SWIGLU_LIMIT = 7.0
SWIGLU_ALPHA = 1.702
LAMBDA_INIT = 0.8 - 0.6 * 1.0

LOG2_E = 1.4426950408889634
SCORE_BOUND_SLACK = 1.02
MAX_SAFE_SCORE_BOUND = 60.0
LANES = 128
SUBLANES = 8
VMEM_LIMIT = 56 * 1024 * 1024

COL_RQ, COL_RK, COL_RV, COL_RG, COL_DQ, COL_DK, COL_DV = 0, 4, 8, 12, 16, 20, 24

TM_PROJ = 512
TS_PREP = 512
RET_CHUNK = 128
RET_UNROLL = 16
TQ_ATTN = 512
TK_ATTN = 2048
TILE = 512
MOE_BLOCK = 512
RUN_ALIGN = SUBLANES
TILE_ROWS = TOP_K * TILE + N_EXPERTS * RUN_ALIGN


def _params(sem, **kw):
    return pltpu.CompilerParams(dimension_semantics=sem, vmem_limit_bytes=VMEM_LIMIT, **kw)


def _in_proj_kernel(x_ref, nw_ref, w_ref, o_ref):
    x = x_ref[...]
    ms = jnp.mean(x * x, axis=-1, keepdims=True)
    h = (x * lax.rsqrt(ms + EPS) * nw_ref[...]).astype(jnp.bfloat16)
    n_col = o_ref.shape[1]
    for c in range(0, n_col, 512):
        o_ref[:, c:c + 512] = jnp.dot(h, w_ref[:, c:c + 512],
                                      preferred_element_type=jnp.float32).astype(o_ref.dtype)


def _in_proj(x2, nw, w_bf16):
    T = x2.shape[0]
    return pl.pallas_call(
        _in_proj_kernel,
        out_shape=jax.ShapeDtypeStruct((T, D_IN_PROJ), jnp.bfloat16),
        grid=(T // TM_PROJ,),
        in_specs=[pl.BlockSpec((TM_PROJ, D_MODEL), lambda i: (i, 0)),
                  pl.BlockSpec((1, D_MODEL), lambda i: (0, 0)),
                  pl.BlockSpec((D_MODEL, D_IN_PROJ), lambda i: (0, 0))],
        out_specs=pl.BlockSpec((TM_PROJ, D_IN_PROJ), lambda i: (i, 0)),
        compiler_params=_params(("arbitrary",)),
        name="in_proj",
    )(x2, nw, w_bf16)


def _prep_kernel(rq_ref, rk_ref, dq_ref, dk_ref, c2_ref, s2_ref, ra_ref, rp_ref, rn_ref,
                 qw_ref, kw_ref, rqo_ref, rko_ref, qs_ref, ks_ref):
    ts = rq_ref.shape[0]
    c2 = c2_ref[...]
    s2 = s2_ref[...]
    ra = ra_ref[...]
    rp = rp_ref[...]
    rn = rn_ref[...]
    lane = lax.broadcasted_iota(jnp.int32, (ts, LANES), 1)
    lo = lane < DIFF_DH

    def qk_norm_rot(x, w):
        x2 = x * x
        s_lo = jnp.sum(jnp.where(lo, x2, 0.0), axis=-1, keepdims=True)
        s_hi = jnp.sum(jnp.where(lo, 0.0, x2), axis=-1, keepdims=True)
        ms = jnp.where(lo, s_lo, s_hi) * (1.0 / DIFF_DH)
        xn = x * lax.rsqrt(ms + EPS) * w
        return xn * ra + pltpu.roll(xn, ROT_DIM // 2, 1) * rp + pltpu.roll(xn, LANES - ROT_DIM // 2, 1) * rn

    for h in range(RET_HEADS):
        sl = slice(h * LANES, (h + 1) * LANES)
        q = rq_ref[:, sl].astype(jnp.float32)
        k = rk_ref[:, sl].astype(jnp.float32)
        rqo_ref[:, sl] = (q * c2 + pltpu.roll(q, RET_DK // 2, 1) * s2).astype(rqo_ref.dtype)
        rko_ref[:, sl] = ((k * c2 + pltpu.roll(k, RET_DK // 2, 1) * s2) * (RET_DK ** -0.5)).astype(rko_ref.dtype)
    for h in range(DIFF_HEADS):
        sl = slice(h * LANES, (h + 1) * LANES)
        q = qk_norm_rot(dq_ref[:, sl].astype(jnp.float32), qw_ref[...]) * (DIFF_DH ** -0.5 * LOG2_E)
        k = qk_norm_rot(dk_ref[:, sl].astype(jnp.float32), kw_ref[...])
        qs_ref[h, 0] = jnp.where(lo, q, 0.0).astype(qs_ref.dtype)
        qs_ref[h, 1] = jnp.where(lo, 0.0, q).astype(qs_ref.dtype)
        ks_ref[:, sl] = k.astype(ks_ref.dtype)


def _prep(proj, tabs, qw2, kw2, B, S):
    T = B * S
    n_s = S // TS_PREP
    col = lambda cb: pl.BlockSpec((TS_PREP, 512), lambda b, i: (b * n_s + i, cb))
    tab = pl.BlockSpec((None, TS_PREP, LANES), lambda b, i: (b, i, 0))
    vec = pl.BlockSpec((1, LANES), lambda b, i: (0, 0))
    out_tok = pl.BlockSpec((TS_PREP, 512), lambda b, i: (b * n_s + i, 0))
    return pl.pallas_call(
        _prep_kernel,
        out_shape=(jax.ShapeDtypeStruct((T, 512), jnp.bfloat16),
                   jax.ShapeDtypeStruct((T, 512), jnp.bfloat16),
                   jax.ShapeDtypeStruct((B, DIFF_HEADS, 2, S, LANES), jnp.bfloat16),
                   jax.ShapeDtypeStruct((T, 512), jnp.bfloat16)),
        grid=(B, n_s),
        in_specs=[col(COL_RQ // 4), col(COL_RK // 4), col(COL_DQ // 4), col(COL_DK // 4),
                  tab, tab, tab, tab, tab, vec, vec],
        out_specs=(out_tok, out_tok,
                   pl.BlockSpec((None, DIFF_HEADS, 2, TS_PREP, LANES), lambda b, i: (b, 0, 0, i, 0)),
                   out_tok),
        compiler_params=_params(("arbitrary", "arbitrary")),
        name="prep",
    )(proj, proj, proj, proj, *tabs, qw2, kw2)


def _retention_kernel(ldf_ref, ldb_ref, q_ref, k_ref, v_ref, g_ref, nw_ref, o_ref, sb_ref):
    C = RET_CHUNK
    S = q_ref.shape[0]
    n_chunks = S // C
    h = pl.program_id(1)
    ldf = ldf_ref[h]
    ldb = ldb_ref[h]
    row = lax.broadcasted_iota(jnp.int32, (C, C), 0).astype(jnp.float32)
    colm = lax.broadcasted_iota(jnp.int32, (C, C), 1).astype(jnp.float32)
    dist = row - colm
    decay = jnp.where(dist >= 0, jnp.exp(ldf * jnp.maximum(dist, 0.0)), jnp.exp(ldb * jnp.maximum(-dist, 0.0)))
    idx = lax.broadcasted_iota(jnp.int32, (C, 1), 0).astype(jnp.float32)
    q_dec_f = jnp.exp(ldf * (idx + 1.0))
    k_dec_f = jnp.exp(ldf * (C - 1.0 - idx))
    q_dec_b = jnp.exp(ldb * (C - idx))
    k_dec_b = jnp.exp(ldb * idx)
    chunk_dec_f = jnp.exp(ldf * C)
    chunk_dec_b = jnp.exp(ldb * C)
    f32 = jnp.float32
    bf16 = jnp.bfloat16

    def kv_state(k, v, k_dec):
        kd = (k.astype(f32) * k_dec).astype(bf16)
        return lax.dot_general(kd, v, (((0,), (0,)), ((), ())), preferred_element_type=f32)

    def bwd_step(i, state):
        c = n_chunks - 1 - i
        r0 = pl.multiple_of(c * C, C)
        sb_ref[c] = state
        return state * chunk_dec_b + kv_state(k_ref[pl.ds(r0, C), :], v_ref[pl.ds(r0, C), :], k_dec_b)

    lax.fori_loop(0, n_chunks, bwd_step, jnp.zeros((RET_DK, LANES), f32), unroll=RET_UNROLL)

    def fwd_step(c, state):
        r0 = pl.multiple_of(c * C, C)
        q = q_ref[pl.ds(r0, C), :]
        k = k_ref[pl.ds(r0, C), :]
        v = v_ref[pl.ds(r0, C), :]
        scores = lax.dot_general(q, k, (((1,), (1,)), ((), ())), preferred_element_type=f32) * decay
        y = jnp.dot(scores.astype(bf16), v, preferred_element_type=f32)
        qf = q.astype(f32)
        y += jnp.dot((qf * q_dec_f).astype(bf16), state.astype(bf16), preferred_element_type=f32)
        y += jnp.dot((qf * q_dec_b).astype(bf16), sb_ref[c].astype(bf16), preferred_element_type=f32)
        yn = y * lax.rsqrt(jnp.mean(y * y, axis=-1, keepdims=True) + EPS) * nw_ref[...]
        g = g_ref[pl.ds(r0, C), :].astype(f32)
        o_ref[pl.ds(r0, C), :] = (yn * (g * jax.nn.sigmoid(g))).astype(o_ref.dtype)
        return state * chunk_dec_f + kv_state(k, v, k_dec_f)

    lax.fori_loop(0, n_chunks, fwd_step, jnp.zeros((RET_DK, LANES), f32), unroll=RET_UNROLL)


def _retention(ldf, ldb, rq_r, rk_r, proj, nw, B, S):
    T = B * S
    smem = pl.BlockSpec(memory_space=pltpu.SMEM)
    seq = lambda cb: pl.BlockSpec((S, LANES), lambda b, h: (b, cb + h))
    return pl.pallas_call(
        _retention_kernel,
        out_shape=jax.ShapeDtypeStruct((T, RET_WIDTH), jnp.bfloat16),
        grid=(B, RET_HEADS),
        in_specs=[smem, smem, seq(0), seq(0), seq(COL_RV), seq(COL_RG),
                  pl.BlockSpec((1, LANES), lambda b, h: (0, h))],
        out_specs=seq(0),
        scratch_shapes=[pltpu.VMEM((S // RET_CHUNK, RET_DK, LANES), jnp.float32)],
        compiler_params=_params(("arbitrary", "arbitrary")),
        name="retention",
    )(ldf, ldb, rq_r, rk_r, proj, proj, nw)


def _diff_attn_kernel(online_max, bound_ref, q_ref, k_ref, v_ref, lam_ref, nw_ref, o_ref, m_ref, l_ref, acc_ref):
    tq = q_ref.shape[1]
    S = k_ref.shape[0]
    f32 = jnp.float32
    q = q_ref[...].reshape(2 * tq, LANES)
    if online_max:
        m_ref[...] = jnp.full(m_ref.shape, -jnp.inf, f32)
    l_ref[...] = jnp.zeros(l_ref.shape, f32)
    acc_ref[...] = jnp.zeros(acc_ref.shape, f32)
    n_tiles = TK_ATTN // LANES

    def kv_step(j, carry):
        r0 = pl.multiple_of(j * TK_ATTN, TK_ATTN)
        k = k_ref[pl.ds(r0, TK_ATTN), :]
        v = v_ref[pl.ds(r0, TK_ATTN), :]
        s = lax.dot_general(q, k, (((1,), (1,)), ((), ())), preferred_element_type=f32)
        tiles = [s[:, c * LANES:(c + 1) * LANES] for c in range(n_tiles)]
        if online_max:
            part = tiles[0]
            for t in tiles[1:]:
                part = jnp.maximum(part, t)
            m_prev = m_ref[...]
            shift = jnp.maximum(m_prev, jnp.max(part, axis=-1, keepdims=True))
            alpha = jnp.exp2(m_prev - shift)
            m_ref[...] = shift
        else:
            shift = bound_ref[0]
        probs = [jnp.exp2(t - shift) for t in tiles]
        psum = probs[0]
        for p in probs[1:]:
            psum = psum + p
        pv = jnp.dot(jnp.concatenate([p.astype(jnp.bfloat16) for p in probs], axis=1), v,
                     preferred_element_type=f32)
        if online_max:
            l_ref[...] = alpha * l_ref[...] + psum
            acc_ref[...] = alpha * acc_ref[...] + pv
        else:
            l_ref[...] = l_ref[...] + psum
            acc_ref[...] = acc_ref[...] + pv
        return carry

    lax.fori_loop(0, S // TK_ATTN, kv_step, 0)
    o = acc_ref[...] / jnp.sum(l_ref[...], axis=-1, keepdims=True)
    d = o[:tq] - lam_ref[...] * o[tq:]
    dn = d * lax.rsqrt(jnp.mean(d * d, axis=-1, keepdims=True) + EPS) * nw_ref[...]
    o_ref[...] = (dn * (1.0 - LAMBDA_INIT)).astype(o_ref.dtype)


def _diff_attn(online_max, bound, qs, ks, proj, lam, nw, B, S):
    T = B * S
    n_q = S // TQ_ATTN
    one = pl.BlockSpec((1, LANES), lambda b, h, i, bd: (0, 0))
    grid_spec = pltpu.PrefetchScalarGridSpec(
        num_scalar_prefetch=1,
        grid=(B, DIFF_HEADS, n_q),
        in_specs=[pl.BlockSpec((None, None, 2, TQ_ATTN, LANES), lambda b, h, i, bd: (b, h, 0, i, 0)),
                  pl.BlockSpec((S, LANES), lambda b, h, i, bd: (b, h)),
                  pl.BlockSpec((S, LANES), lambda b, h, i, bd: (b, COL_DV + h)),
                  one, one],
        out_specs=pl.BlockSpec((TQ_ATTN, LANES), lambda b, h, i, bd: (b * n_q + i, h)),
        scratch_shapes=[pltpu.VMEM((2 * TQ_ATTN, LANES), jnp.float32)] * 3,
    )
    return pl.pallas_call(
        functools.partial(_diff_attn_kernel, online_max),
        out_shape=jax.ShapeDtypeStruct((T, DIFF_WIDTH), jnp.bfloat16),
        grid_spec=grid_spec,
        compiler_params=_params(("arbitrary", "arbitrary", "arbitrary")),
        name="diff_attn_online" if online_max else "diff_attn",
    )(bound, qs, ks, proj, lam, nw)


def _out_router_kernel(x_ref, yr_ref, yd_ref, wo_ref, n2_ref, wrt_ref, br_ref,
                       x1_ref, h2_ref, pos_ref, gate_t_ref, len_ref, off_ref, tot_ref):
    tm = x_ref.shape[0]
    f32 = jnp.float32
    bf16 = jnp.bfloat16

    @pl.when(pl.program_id(0) == 0)
    def _():
        tot_ref[...] = jnp.zeros(tot_ref.shape, f32)

    att = jnp.dot(yr_ref[...], wo_ref[:RET_WIDTH, :], preferred_element_type=f32)
    att += jnp.dot(yd_ref[...], wo_ref[RET_WIDTH:, :], preferred_element_type=f32)
    x1 = x_ref[...] + att
    x1_ref[...] = x1
    h2 = x1 * lax.rsqrt(jnp.mean(x1 * x1, axis=-1, keepdims=True) + EPS) * n2_ref[...]
    h2_ref[...] = h2.astype(h2_ref.dtype)
    logits = lax.dot_general(wrt_ref[...], h2, (((1,), (1,)), ((), ())),
                             precision=lax.Precision.HIGHEST, preferred_element_type=f32) + br_ref[...]
    e_iota = lax.broadcasted_iota(jnp.int32, (N_EXPERTS, tm), 0)
    work = logits
    vals, hots = [], []
    for _ in range(TOP_K):
        mx = jnp.max(work, axis=0, keepdims=True)
        ix = jnp.min(jnp.where(work == mx, e_iota, N_EXPERTS), axis=0, keepdims=True)
        hot = e_iota == ix
        vals.append(mx)
        hots.append(hot)
        work = jnp.where(hot, -jnp.inf, work)
    exps = [jnp.exp(v - vals[0]) for v in vals]
    denom = exps[0] + exps[1] + exps[2] + exps[3]
    gates = [e / denom for e in exps]
    sel = jnp.zeros((N_EXPERTS, tm), f32)
    for hot in hots:
        sel = jnp.where(hot, 1.0, sel)
    t_row = lax.broadcasted_iota(jnp.int32, (tm, tm), 0)
    t_col = lax.broadcasted_iota(jnp.int32, (tm, tm), 1)
    upper = jnp.where(t_row < t_col, 1.0, 0.0).astype(bf16)
    rank = jnp.dot(sel.astype(bf16), upper, preferred_element_type=f32)
    cnt = jnp.sum(sel, axis=1, keepdims=True)
    run_units = jnp.floor((cnt + (RUN_ALIGN - 1.0)) * (1.0 / RUN_ALIGN))
    run_len = jnp.broadcast_to(run_units * RUN_ALIGN, (N_EXPERTS, LANES))
    e_row = lax.broadcasted_iota(jnp.int32, (N_EXPERTS, N_EXPERTS), 0)
    e_col = lax.broadcasted_iota(jnp.int32, (N_EXPERTS, N_EXPERTS), 1)
    lower = jnp.where(e_col < e_row, 1.0, 0.0).astype(bf16)
    run_start = jnp.dot(lower, jnp.broadcast_to(run_units, (N_EXPERTS, LANES)).astype(bf16),
                        preferred_element_type=f32) * RUN_ALIGN
    pos_full = rank + run_start[:, 0:1]
    pos = [jnp.sum(jnp.where(hot, pos_full, 0.0), axis=0, keepdims=True) for hot in hots]
    for k in range(TOP_K):
        pos_ref[k:k + 1, :] = pos[k].astype(jnp.int32)
    rows = jnp.concatenate(gates + pos + [jnp.zeros((LANES - 2 * TOP_K, tm), f32)], axis=0)
    gate_t_ref[...] = rows.T
    len_ref[0] = run_len
    off_ref[0] = tot_ref[...]
    tot_ref[...] = tot_ref[...] + run_len


def _out_router(x2, y_ret, y_diff, wo_bf16, n2w, wrt, br):
    T = x2.shape[0]
    n_tiles = T // TILE
    tok = lambda w: pl.BlockSpec((TILE, w), lambda i: (i, 0))
    const = lambda s: pl.BlockSpec(s, lambda i: (0, 0))
    per_tile = pl.BlockSpec((1, N_EXPERTS, LANES), lambda i: (i, 0, 0))
    return pl.pallas_call(
        _out_router_kernel,
        out_shape=(jax.ShapeDtypeStruct((T, D_MODEL), jnp.float32),
                   jax.ShapeDtypeStruct((T, D_MODEL), jnp.bfloat16),
                   jax.ShapeDtypeStruct((TOP_K, T), jnp.int32),
                   jax.ShapeDtypeStruct((T, LANES), jnp.float32),
                   jax.ShapeDtypeStruct((n_tiles, N_EXPERTS, LANES), jnp.float32),
                   jax.ShapeDtypeStruct((n_tiles, N_EXPERTS, LANES), jnp.float32),
                   jax.ShapeDtypeStruct((N_EXPERTS, LANES), jnp.float32)),
        grid=(n_tiles,),
        in_specs=[tok(D_MODEL), tok(RET_WIDTH), tok(DIFF_WIDTH), const((D_MODEL, D_MODEL)),
                  const((1, D_MODEL)), const((N_EXPERTS, D_MODEL)), const((N_EXPERTS, 1))],
        out_specs=(tok(D_MODEL), tok(D_MODEL), pl.BlockSpec((TOP_K, TILE), lambda i: (0, i)), tok(LANES),
                   per_tile, per_tile, const((N_EXPERTS, LANES))),
        compiler_params=_params(("arbitrary",)),
        name="out_router",
    )(x2, y_ret, y_diff, wo_bf16, n2w, wrt, br)


def _run_copies(src_ref, len_ref, dst_ref, tile, make_copy):
    for e in range(N_EXPERTS):
        n = pl.multiple_of(len_ref[tile * N_EXPERTS + e], RUN_ALIGN)
        s = pl.multiple_of(src_ref[tile * N_EXPERTS + e], RUN_ALIGN)
        d = pl.multiple_of(dst_ref[tile * N_EXPERTS + e], RUN_ALIGN)

        @pl.when(n > 0)
        def _():
            make_copy(s, d, n).start()


def _dispatch_kernel(src_ref, len_ref, dst_ref, rows_ref, zlo_ref, zlen_ref, nu_ref,
                     pos_ref, h2_ref, xs_hbm, xbuf_ref, zero_ref, sems, zero_sem):
    i = pl.program_id(0)
    n_tiles = pl.num_programs(0)
    cur = i % 2
    n_rows, tm = xbuf_ref.shape[1], h2_ref.shape[0]

    @pl.when(i == 0)
    def _():
        zero_ref[...] = jnp.zeros(zero_ref.shape, zero_ref.dtype)

        def pad_copy(e):
            n = pl.multiple_of(zlen_ref[e], RUN_ALIGN)
            lo = pl.multiple_of(zlo_ref[e], RUN_ALIGN)
            return pltpu.make_async_copy(zero_ref.at[pl.ds(0, n)], xs_hbm.at[pl.ds(lo, n)], zero_sem)

        def tail_copy(j):
            return pltpu.make_async_copy(zero_ref, xs_hbm.at[pl.ds(j * MOE_BLOCK, MOE_BLOCK)], zero_sem)

        def guarded(copy, op):
            def body(e, c):
                @pl.when(zlen_ref[e] > 0)
                def _():
                    op(copy(e))
                return c
            return body

        lax.fori_loop(0, N_EXPERTS, guarded(pad_copy, lambda cp: cp.start()), 0)
        lax.fori_loop(0, N_EXPERTS, guarded(pad_copy, lambda cp: cp.wait()), 0)
        n_blocks = xs_hbm.shape[0] // MOE_BLOCK
        lax.fori_loop(nu_ref[0], n_blocks, lambda j, c: (tail_copy(j).start(), c)[1], 0)
        lax.fori_loop(nu_ref[0], n_blocks, lambda j, c: (tail_copy(j).wait(), c)[1], 0)

    p_iota = lax.broadcasted_iota(jnp.int32, (n_rows, tm), 0)
    onehot = jnp.zeros((n_rows, tm), jnp.float32)
    for k in range(TOP_K):
        onehot = jnp.where(p_iota == pos_ref[k:k + 1, :], 1.0, onehot)
    xbuf_ref[cur] = jnp.dot(onehot.astype(jnp.bfloat16), h2_ref[...], preferred_element_type=jnp.float32)

    _run_copies(src_ref, len_ref, dst_ref, i,
                lambda s, d, n: pltpu.make_async_copy(xbuf_ref.at[cur, pl.ds(s, n)], xs_hbm.at[pl.ds(d, n)],
                                                      sems.at[cur]))

    def wait_tile(tile, slot):
        rows = pl.multiple_of(rows_ref[tile], RUN_ALIGN)
        pltpu.make_async_copy(xbuf_ref.at[slot, pl.ds(0, rows)], xs_hbm.at[pl.ds(0, rows)], sems.at[slot]).wait()

    @pl.when(i > 0)
    def _():
        wait_tile(i - 1, 1 - cur)

    @pl.when(i == n_tiles - 1)
    def _():
        wait_tile(i, cur)


def _dispatch(run_src, run_len, run_dst, tile_rows, zero_lo, zero_len, n_used, pos, h2, P):
    T = h2.shape[0]
    n_pre = 7
    grid_spec = pltpu.PrefetchScalarGridSpec(
        num_scalar_prefetch=n_pre,
        grid=(T // TILE,),
        in_specs=[pl.BlockSpec((TOP_K, TILE), lambda i, *_: (0, i)),
                  pl.BlockSpec((TILE, D_MODEL), lambda i, *_: (i, 0))],
        out_specs=pl.BlockSpec(memory_space=pl.ANY),
        scratch_shapes=[pltpu.VMEM((2, TILE_ROWS, D_MODEL), jnp.float32),
                        pltpu.VMEM((MOE_BLOCK, D_MODEL), jnp.float32),
                        pltpu.SemaphoreType.DMA((2,)),
                        pltpu.SemaphoreType.DMA(())],
    )
    return pl.pallas_call(
        _dispatch_kernel,
        out_shape=jax.ShapeDtypeStruct((P, D_MODEL), jnp.float32),
        grid_spec=grid_spec,
        compiler_params=_params(("arbitrary",), has_side_effects=True),
        name="dispatch",
    )(run_src, run_len, run_dst, tile_rows, zero_lo, zero_len, n_used, pos, h2)


def _experts_kernel(be_ref, nu_ref, xs_ref, w1_ref, b1_ref, w2_ref, b2_ref, ys_ref, w1b_ref, w2b_ref):
    j = pl.program_id(0)

    @pl.when(j < nu_ref[0])
    def _():
        @pl.when((j == 0) | (be_ref[j] != be_ref[jnp.maximum(j - 1, 0)]))
        def _():
            w1b_ref[...] = w1_ref[...].astype(jnp.bfloat16)
            w2b_ref[...] = w2_ref[...].astype(jnp.bfloat16)

        x = xs_ref[...].astype(jnp.bfloat16)
        u = jnp.dot(x, w1b_ref[...], preferred_element_type=jnp.float32) + b1_ref[...]
        glu = jnp.minimum(u[:, :D_FF], SWIGLU_LIMIT)
        lin = jnp.clip(u[:, D_FF:], -SWIGLU_LIMIT, SWIGLU_LIMIT)
        act = glu * jax.nn.sigmoid(SWIGLU_ALPHA * glu) * (lin + 1.0)
        ys_ref[...] = jnp.dot(act.astype(jnp.bfloat16), w2b_ref[...],
                              preferred_element_type=jnp.float32) + b2_ref[...]

    @pl.when(j >= nu_ref[0])
    def _():
        ys_ref[...] = jnp.zeros(ys_ref.shape, ys_ref.dtype)


def _experts(block_e, n_used, xs, w1, b1, w2, b2):
    P = xs.shape[0]
    row_block = lambda j, be, nu: (j, 0)
    expert = lambda j, be, nu: (be[j], 0, 0)
    grid_spec = pltpu.PrefetchScalarGridSpec(
        num_scalar_prefetch=2,
        grid=(P // MOE_BLOCK,),
        in_specs=[pl.BlockSpec((MOE_BLOCK, D_MODEL), row_block),
                  pl.BlockSpec((None, D_MODEL, 2 * D_FF), expert),
                  pl.BlockSpec((None, 1, 2 * D_FF), expert),
                  pl.BlockSpec((None, D_FF, D_MODEL), expert),
                  pl.BlockSpec((None, 1, D_MODEL), expert)],
        out_specs=pl.BlockSpec((MOE_BLOCK, D_MODEL), row_block),
        scratch_shapes=[pltpu.VMEM((D_MODEL, 2 * D_FF), jnp.bfloat16),
                        pltpu.VMEM((D_FF, D_MODEL), jnp.bfloat16)],
    )
    return pl.pallas_call(
        _experts_kernel,
        out_shape=jax.ShapeDtypeStruct((P, D_MODEL), jnp.float32),
        grid_spec=grid_spec,
        compiler_params=_params(("arbitrary",)),
        name="experts",
    )(block_e, n_used, xs, w1, b1, w2, b2)


def _combine_kernel(src_ref, len_ref, dst_ref, rows_ref, gate_t_ref, x1_ref, ys_hbm, o_ref, ybuf_ref, sems):
    i = pl.program_id(0)
    n_tiles = pl.num_programs(0)
    cur = i % 2
    n_rows, tm = ybuf_ref.shape[1], x1_ref.shape[0]

    def fetch(tile, slot):
        _run_copies(src_ref, len_ref, dst_ref, tile,
                    lambda s, d, n: pltpu.make_async_copy(ys_hbm.at[pl.ds(d, n)], ybuf_ref.at[slot, pl.ds(s, n)],
                                                          sems.at[slot]))

    @pl.when(i == 0)
    def _():
        ybuf_ref[...] = jnp.zeros(ybuf_ref.shape, ybuf_ref.dtype)
        fetch(0, 0)

    @pl.when(i + 1 < n_tiles)
    def _():
        fetch(i + 1, 1 - cur)

    rows = pl.multiple_of(rows_ref[i], RUN_ALIGN)
    pltpu.make_async_copy(ys_hbm.at[pl.ds(0, rows)], ybuf_ref.at[cur, pl.ds(0, rows)], sems.at[cur]).wait()

    g = gate_t_ref[...]
    p_iota = lax.broadcasted_iota(jnp.int32, (tm, n_rows), 1)
    weights = jnp.zeros((tm, n_rows), jnp.float32)
    for k in range(TOP_K):
        pos_k = g[:, TOP_K + k:TOP_K + k + 1].astype(jnp.int32)
        weights = jnp.where(p_iota == pos_k, g[:, k:k + 1], weights)
    o_ref[...] = x1_ref[...] + jnp.dot(weights.astype(jnp.bfloat16), ybuf_ref[cur].astype(jnp.bfloat16),
                                       preferred_element_type=jnp.float32)


def _combine(run_src, run_len, run_dst, tile_rows, gate_t, x1, ys):
    T = x1.shape[0]
    tok = lambda w: pl.BlockSpec((TILE, w), lambda i, *_: (i, 0))
    grid_spec = pltpu.PrefetchScalarGridSpec(
        num_scalar_prefetch=4,
        grid=(T // TILE,),
        in_specs=[tok(LANES), tok(D_MODEL), pl.BlockSpec(memory_space=pl.ANY)],
        out_specs=tok(D_MODEL),
        scratch_shapes=[pltpu.VMEM((2, TILE_ROWS, D_MODEL), jnp.float32),
                        pltpu.SemaphoreType.DMA((2,))],
    )
    return pl.pallas_call(
        _combine_kernel,
        out_shape=jax.ShapeDtypeStruct((T, D_MODEL), jnp.float32),
        grid_spec=grid_spec,
        compiler_params=_params(("arbitrary",)),
        name="combine",
    )(run_src, run_len, run_dst, tile_rows, gate_t, x1, ys)


def _rotary_tables(positions):
    pos = positions.astype(jnp.float32)[..., None]
    lane = jnp.arange(LANES)
    half_r = RET_DK // 2
    inv_r = RET_ROPE_THETA ** (-jnp.linspace(0.0, 1.0, half_r, dtype=jnp.float32))
    ang = pos * inv_r[lane % half_r]
    c2 = jnp.cos(ang)
    s2 = jnp.sin(ang) * jnp.where(lane < half_r, -1.0, 1.0)
    half_d = ROT_DIM // 2
    inv_d = ROPE_THETA ** (-jnp.arange(0, ROT_DIM, 2, dtype=jnp.float32) / ROT_DIM)
    sub = lane % DIFF_DH
    ang_d = pos * inv_d[sub % half_d]
    cd, sd = jnp.cos(ang_d), jnp.sin(ang_d)
    ra = jnp.where(sub < ROT_DIM, cd, 1.0)
    rp = jnp.where((sub >= half_d) & (sub < ROT_DIM), sd, 0.0)
    rn = jnp.where(sub < half_d, -sd, 0.0)
    return c2, s2, ra, rp, rn


def kernel(x, positions, norm1_w, w_in, ret_log_decay_fwd, ret_log_decay_bwd, ret_norm_w, q_norm_w, k_norm_w, lambda_q1, lambda_k1, lambda_q2, lambda_k2, diff_norm_w, w_out, norm2_w, w_router, b_router, w1, b1, w2, b2):
    B, S, D = x.shape
    T = B * S
    f32 = jnp.float32
    bf16 = jnp.bfloat16
    x2 = x.reshape(T, D)

    proj = _in_proj(x2, norm1_w[0].reshape(1, D), w_in[0].astype(bf16))
    tabs = _rotary_tables(positions)
    dup = lambda w: jnp.concatenate([w, w]).reshape(1, LANES).astype(f32)
    rq_r, rk_r, qs, ks = _prep(proj, tabs, dup(q_norm_w[0]), dup(k_norm_w[0]), B, S)

    y_ret = _retention(ret_log_decay_fwd[0].astype(f32), ret_log_decay_bwd[0].astype(f32),
                       rq_r, rk_r, proj, ret_norm_w[0].reshape(1, RET_WIDTH).astype(f32), B, S)

    lam = (jnp.exp(jnp.sum(lambda_q1[0].astype(f32) * lambda_k1[0].astype(f32)))
           - jnp.exp(jnp.sum(lambda_q2[0].astype(f32) * lambda_k2[0].astype(f32))) + LAMBDA_INIT)
    lam_row = jnp.full((1, LANES), lam, f32)
    bound = (SCORE_BOUND_SLACK * DIFF_DH ** 0.5 * LOG2_E
             * jnp.max(jnp.abs(q_norm_w[0].astype(f32))) * jnp.max(jnp.abs(k_norm_w[0].astype(f32)))).reshape(1)
    attn_args = (bound, qs, ks, proj, lam_row, diff_norm_w[0].reshape(1, DIFF_DV).astype(f32), B, S)
    y_diff = lax.cond(bound[0] <= MAX_SAFE_SCORE_BOUND,
                      lambda: _diff_attn(False, *attn_args), lambda: _diff_attn(True, *attn_args))

    x1, h2, pos, gate_t, len_t, off_t, tot_t = _out_router(
        x2, y_ret, y_diff, w_out[0].astype(bf16), norm2_w[0].reshape(1, D),
        w_router[0].T.astype(f32), b_router[0].reshape(N_EXPERTS, 1).astype(f32))

    n_tiles = T // TILE
    run_len = len_t[:, :, 0].astype(jnp.int32)
    total = tot_t[:, 0].astype(jnp.int32)
    padded = ((total + MOE_BLOCK - 1) // MOE_BLOCK) * MOE_BLOCK
    pad_end = jnp.cumsum(padded)
    pad_start = pad_end - padded
    run_dst = pad_start[None, :] + off_t[:, :, 0].astype(jnp.int32)
    run_src = jnp.cumsum(run_len, axis=1) - run_len
    tile_rows = jnp.sum(run_len, axis=1)
    P = T * TOP_K + n_tiles * N_EXPERTS * RUN_ALIGN + N_EXPERTS * MOE_BLOCK
    n_blocks = P // MOE_BLOCK
    block_row = jnp.arange(n_blocks, dtype=jnp.int32) * MOE_BLOCK
    block_e = jnp.minimum(jnp.sum((pad_end[None, :] <= block_row[:, None]).astype(jnp.int32), axis=1),
                          N_EXPERTS - 1)
    n_used = (pad_end[-1:] // MOE_BLOCK).astype(jnp.int32)
    runs = (run_src.reshape(-1), run_len.reshape(-1), run_dst.reshape(-1), tile_rows)

    xs = _dispatch(*runs, pad_start + total, padded - total, n_used, pos, h2, P)
    ys = _experts(block_e, n_used, xs, w1[0], b1[0].reshape(N_EXPERTS, 1, 2 * D_FF),
                  w2[0], b2[0].reshape(N_EXPERTS, 1, D))
    out = _combine(*runs, gate_t, x1, ys)
    return out.reshape(B, S, D)
```

```python
import functools

import jax
import jax.numpy as jnp
from jax import lax
from jax.experimental import pallas as pl
from jax.experimental.pallas import tpu as pltpu

EPS = 1e-6
D_MODEL = 1024
RET_HEADS = 4
RET_DK = 128
RET_WIDTH = 512
RET_ROPE_THETA = 10000.0
DIFF_HEADS = 4
DIFF_DH = 64
DIFF_DV = 128
DIFF_WIDTH = 512
ROPE_THETA = 500000.0
ROT_DIM = DIFF_DH // 4
D_IN_PROJ = 3584
N_EXPERTS = 32
TOP_K = 4
D_FF = 1024
SWIGLU_LIMIT = 7.0
SWIGLU_ALPHA = 1.702
LAMBDA_INIT = 0.8 - 0.6 * 1.0

LOG2_E = 1.4426950408889634
SCORE_BOUND_SLACK = 1.02
MAX_SAFE_SCORE_BOUND = 60.0
LANES = 128
SUBLANES = 8
VMEM_LIMIT = 56 * 1024 * 1024

COL_RQ, COL_RK, COL_RV, COL_RG, COL_DQ, COL_DK, COL_DV = 0, 4, 8, 12, 16, 20, 24
VGV_RV, VGV_RG, VGV_DV = 0, 4, 8

TM_PROJ = 512
RET_CHUNK = 128
RET_UNROLL = 16
TQ_ATTN = 512
TK_ATTN = 2048
TILE = 512
MOE_BLOCK = 512
RUN_ALIGN = SUBLANES
TILE_ROWS = TOP_K * TILE + N_EXPERTS * RUN_ALIGN


def _params(sem, **kw):
    return pltpu.CompilerParams(dimension_semantics=sem, vmem_limit_bytes=VMEM_LIMIT, **kw)


def _in_proj_kernel(x_ref, nw_ref, w_ref, c2_ref, s2_ref, ra_ref, rp_ref, rn_ref, qw_ref, kw_ref,
                    vgv_ref, rqo_ref, rko_ref, qs_ref, ks_ref):
    ts = x_ref.shape[0]
    x = x_ref[...]
    hn = (x * lax.rsqrt(jnp.mean(x * x, axis=-1, keepdims=True) + EPS) * nw_ref[...]).astype(jnp.bfloat16)

    def proj(col_block):
        c0 = col_block * LANES
        return jnp.dot(hn, w_ref[:, c0:c0 + 4 * LANES], preferred_element_type=jnp.float32)

    rq, rk, dq, dk = proj(COL_RQ), proj(COL_RK), proj(COL_DQ), proj(COL_DK)
    c2 = c2_ref[...]
    s2 = s2_ref[...]
    ra = ra_ref[...]
    rp = rp_ref[...]
    rn = rn_ref[...]
    lane = lax.broadcasted_iota(jnp.int32, (ts, LANES), 1)
    lo = lane < DIFF_DH

    def qk_norm_rot(x, w):
        x2 = x * x
        s_lo = jnp.sum(jnp.where(lo, x2, 0.0), axis=-1, keepdims=True)
        s_hi = jnp.sum(jnp.where(lo, 0.0, x2), axis=-1, keepdims=True)
        ms = jnp.where(lo, s_lo, s_hi) * (1.0 / DIFF_DH)
        xn = x * lax.rsqrt(ms + EPS) * w
        return xn * ra + pltpu.roll(xn, ROT_DIM // 2, 1) * rp + pltpu.roll(xn, LANES - ROT_DIM // 2, 1) * rn

    for h in range(RET_HEADS):
        sl = slice(h * LANES, (h + 1) * LANES)
        q = rq[:, sl]
        k = rk[:, sl]
        rqo_ref[:, sl] = (q * c2 + pltpu.roll(q, RET_DK // 2, 1) * s2).astype(rqo_ref.dtype)
        rko_ref[:, sl] = ((k * c2 + pltpu.roll(k, RET_DK // 2, 1) * s2) * (RET_DK ** -0.5)).astype(rko_ref.dtype)
    for h in range(DIFF_HEADS):
        sl = slice(h * LANES, (h + 1) * LANES)
        q = qk_norm_rot(dq[:, sl], qw_ref[...]) * (DIFF_DH ** -0.5 * LOG2_E)
        k = qk_norm_rot(dk[:, sl], kw_ref[...])
        qs_ref[h, 0] = jnp.where(lo, q, 0.0).astype(qs_ref.dtype)
        qs_ref[h, 1] = jnp.where(lo, 0.0, q).astype(qs_ref.dtype)
        ks_ref[:, sl] = k.astype(ks_ref.dtype)
    for slot, col_block in enumerate((COL_RV, COL_RG, COL_DV)):
        vgv_ref[:, slot * 4 * LANES:(slot + 1) * 4 * LANES] = proj(col_block).astype(vgv_ref.dtype)


def _in_proj(x2, nw, w_bf16, tabs, qw2, kw2, B, S):
    T = B * S
    n_s = S // TM_PROJ
    tok = lambda w: pl.BlockSpec((TM_PROJ, w), lambda i: (i, 0))
    const = lambda s: pl.BlockSpec(s, lambda i: (0, 0))
    tab = pl.BlockSpec((None, TM_PROJ, LANES), lambda i: (i // n_s, i % n_s, 0))
    bf16 = jnp.bfloat16
    return pl.pallas_call(
        _in_proj_kernel,
        out_shape=(jax.ShapeDtypeStruct((T, 3 * 4 * LANES), bf16),
                   jax.ShapeDtypeStruct((T, 4 * LANES), bf16),
                   jax.ShapeDtypeStruct((T, 4 * LANES), bf16),
                   jax.ShapeDtypeStruct((B, DIFF_HEADS, 2, S, LANES), bf16),
                   jax.ShapeDtypeStruct((T, 4 * LANES), bf16)),
        grid=(T // TM_PROJ,),
        in_specs=[tok(D_MODEL), const((1, D_MODEL)), const((D_MODEL, D_IN_PROJ)),
                  tab, tab, tab, tab, tab, const((1, LANES)), const((1, LANES))],
        out_specs=(tok(3 * 4 * LANES), tok(4 * LANES), tok(4 * LANES),
                   pl.BlockSpec((None, DIFF_HEADS, 2, TM_PROJ, LANES), lambda i: (i // n_s, 0, 0, i % n_s, 0)),
                   tok(4 * LANES)),
        compiler_params=_params(("arbitrary",)),
        name="in_proj",
    )(x2, nw, w_bf16, *tabs, qw2, kw2)


def _retention_kernel(ldf_ref, ldb_ref, q_ref, k_ref, v_ref, g_ref, nw_ref, o_ref, sb_ref):
    C = RET_CHUNK
    S = q_ref.shape[0]
    n_chunks = S // C
    h = pl.program_id(1)
    ldf = ldf_ref[h]
    ldb = ldb_ref[h]
    row = lax.broadcasted_iota(jnp.int32, (C, C), 0).astype(jnp.float32)
    colm = lax.broadcasted_iota(jnp.int32, (C, C), 1).astype(jnp.float32)
    dist = row - colm
    decay = jnp.where(dist >= 0, jnp.exp(ldf * jnp.maximum(dist, 0.0)), jnp.exp(ldb * jnp.maximum(-dist, 0.0)))
    idx = lax.broadcasted_iota(jnp.int32, (C, 1), 0).astype(jnp.float32)
    q_dec_f = jnp.exp(ldf * (idx + 1.0))
    k_dec_f = jnp.exp(ldf * (C - 1.0 - idx))
    q_dec_b = jnp.exp(ldb * (C - idx))
    k_dec_b = jnp.exp(ldb * idx)
    chunk_dec_f = jnp.exp(ldf * C)
    chunk_dec_b = jnp.exp(ldb * C)
    f32 = jnp.float32
    bf16 = jnp.bfloat16

    def kv_state(k, v, k_dec):
        kd = (k.astype(f32) * k_dec).astype(bf16)
        return lax.dot_general(kd, v, (((0,), (0,)), ((), ())), preferred_element_type=f32)

    def bwd_step(i, state):
        c = n_chunks - 1 - i
        r0 = pl.multiple_of(c * C, C)
        sb_ref[c] = state
        return state * chunk_dec_b + kv_state(k_ref[pl.ds(r0, C), :], v_ref[pl.ds(r0, C), :], k_dec_b)

    lax.fori_loop(0, n_chunks, bwd_step, jnp.zeros((RET_DK, LANES), f32), unroll=RET_UNROLL)

    def fwd_step(c, state):
        r0 = pl.multiple_of(c * C, C)
        q = q_ref[pl.ds(r0, C), :]
        k = k_ref[pl.ds(r0, C), :]
        v = v_ref[pl.ds(r0, C), :]
        scores = lax.dot_general(q, k, (((1,), (1,)), ((), ())), preferred_element_type=f32) * decay
        y = jnp.dot(scores.astype(bf16), v, preferred_element_type=f32)
        qf = q.astype(f32)
        y += jnp.dot((qf * q_dec_f).astype(bf16), state.astype(bf16), preferred_element_type=f32)
        y += jnp.dot((qf * q_dec_b).astype(bf16), sb_ref[c].astype(bf16), preferred_element_type=f32)
        yn = y * lax.rsqrt(jnp.mean(y * y, axis=-1, keepdims=True) + EPS) * nw_ref[...]
        g = g_ref[pl.ds(r0, C), :].astype(f32)
        o_ref[pl.ds(r0, C), :] = (yn * (g * jax.nn.sigmoid(g))).astype(o_ref.dtype)
        return state * chunk_dec_f + kv_state(k, v, k_dec_f)

    lax.fori_loop(0, n_chunks, fwd_step, jnp.zeros((RET_DK, LANES), f32), unroll=RET_UNROLL)


def _retention(ldf, ldb, rq_r, rk_r, proj, nw, B, S):
    T = B * S
    smem = pl.BlockSpec(memory_space=pltpu.SMEM)
    seq = lambda cb: pl.BlockSpec((S, LANES), lambda b, h: (b, cb + h))
    return pl.pallas_call(
        _retention_kernel,
        out_shape=jax.ShapeDtypeStruct((T, RET_WIDTH), jnp.bfloat16),
        grid=(B, RET_HEADS),
        in_specs=[smem, smem, seq(0), seq(0), seq(VGV_RV), seq(VGV_RG),
                  pl.BlockSpec((1, LANES), lambda b, h: (0, h))],
        out_specs=seq(0),
        scratch_shapes=[pltpu.VMEM((S // RET_CHUNK, RET_DK, LANES), jnp.float32)],
        compiler_params=_params(("arbitrary", "arbitrary")),
        name="retention",
    )(ldf, ldb, rq_r, rk_r, proj, proj, nw)


def _diff_attn_kernel(online_max, bound_ref, q_ref, k_ref, v_ref, lam_ref, nw_ref, o_ref, m_ref, l_ref, acc_ref):
    tq = q_ref.shape[1]
    S = k_ref.shape[0]
    f32 = jnp.float32
    q = q_ref[...].reshape(2 * tq, LANES)
    if online_max:
        m_ref[...] = jnp.full(m_ref.shape, -jnp.inf, f32)
    l_ref[...] = jnp.zeros(l_ref.shape, f32)
    acc_ref[...] = jnp.zeros(acc_ref.shape, f32)
    n_tiles = TK_ATTN // LANES

    def kv_step(j, carry):
        r0 = pl.multiple_of(j * TK_ATTN, TK_ATTN)
        k = k_ref[pl.ds(r0, TK_ATTN), :]
        v = v_ref[pl.ds(r0, TK_ATTN), :]
        s = lax.dot_general(q, k, (((1,), (1,)), ((), ())), preferred_element_type=f32)
        tiles = [s[:, c * LANES:(c + 1) * LANES] for c in range(n_tiles)]
        if online_max:
            part = tiles[0]
            for t in tiles[1:]:
                part = jnp.maximum(part, t)
            m_prev = m_ref[...]
            shift = jnp.maximum(m_prev, jnp.max(part, axis=-1, keepdims=True))
            alpha = jnp.exp2(m_prev - shift)
            m_ref[...] = shift
        else:
            shift = bound_ref[0]
        probs = [jnp.exp2(t - shift) for t in tiles]
        psum = probs[0]
        for p in probs[1:]:
            psum = psum + p
        pv = jnp.dot(jnp.concatenate([p.astype(jnp.bfloat16) for p in probs], axis=1), v,
                     preferred_element_type=f32)
        if online_max:
            l_ref[...] = alpha * l_ref[...] + psum
            acc_ref[...] = alpha * acc_ref[...] + pv
        else:
            l_ref[...] = l_ref[...] + psum
            acc_ref[...] = acc_ref[...] + pv
        return carry

    lax.fori_loop(0, S // TK_ATTN, kv_step, 0)
    o = acc_ref[...] / jnp.sum(l_ref[...], axis=-1, keepdims=True)
    d = o[:tq] - lam_ref[...] * o[tq:]
    dn = d * lax.rsqrt(jnp.mean(d * d, axis=-1, keepdims=True) + EPS) * nw_ref[...]
    o_ref[...] = (dn * (1.0 - LAMBDA_INIT)).astype(o_ref.dtype)


def _diff_attn(online_max, bound, qs, ks, proj, lam, nw, B, S):
    T = B * S
    n_q = S // TQ_ATTN
    one = pl.BlockSpec((1, LANES), lambda b, h, i, bd: (0, 0))
    grid_spec = pltpu.PrefetchScalarGridSpec(
        num_scalar_prefetch=1,
        grid=(B, DIFF_HEADS, n_q),
        in_specs=[pl.BlockSpec((None, None, 2, TQ_ATTN, LANES), lambda b, h, i, bd: (b, h, 0, i, 0)),
                  pl.BlockSpec((S, LANES), lambda b, h, i, bd: (b, h)),
                  pl.BlockSpec((S, LANES), lambda b, h, i, bd: (b, VGV_DV + h)),
                  one, one],
        out_specs=pl.BlockSpec((TQ_ATTN, LANES), lambda b, h, i, bd: (b * n_q + i, h)),
        scratch_shapes=[pltpu.VMEM((2 * TQ_ATTN, LANES), jnp.float32)] * 3,
    )
    return pl.pallas_call(
        functools.partial(_diff_attn_kernel, online_max),
        out_shape=jax.ShapeDtypeStruct((T, DIFF_WIDTH), jnp.bfloat16),
        grid_spec=grid_spec,
        compiler_params=_params(("arbitrary", "arbitrary", "arbitrary")),
        name="diff_attn_online" if online_max else "diff_attn",
    )(bound, qs, ks, proj, lam, nw)


def _out_router_kernel(x_ref, yr_ref, yd_ref, wo_ref, n2_ref, wrt_ref, br_ref,
                       x1_ref, h2_ref, pos_ref, gate_t_ref, len_ref, off_ref, tot_ref):
    tm = x_ref.shape[0]
    f32 = jnp.float32
    bf16 = jnp.bfloat16

    @pl.when(pl.program_id(0) == 0)
    def _():
        tot_ref[...] = jnp.zeros(tot_ref.shape, f32)

    att = jnp.dot(yr_ref[...], wo_ref[:RET_WIDTH, :], preferred_element_type=f32)
    att += jnp.dot(yd_ref[...], wo_ref[RET_WIDTH:, :], preferred_element_type=f32)
    x1 = x_ref[...] + att
    x1_ref[...] = x1
    h2 = x1 * lax.rsqrt(jnp.mean(x1 * x1, axis=-1, keepdims=True) + EPS) * n2_ref[...]
    h2_ref[...] = h2.astype(h2_ref.dtype)
    logits = lax.dot_general(wrt_ref[...], h2, (((1,), (1,)), ((), ())),
                             precision=lax.Precision.HIGHEST, preferred_element_type=f32) + br_ref[...]
    e_iota = lax.broadcasted_iota(jnp.int32, (N_EXPERTS, tm), 0)
    work = logits
    vals, hots = [], []
    for _ in range(TOP_K):
        mx = jnp.max(work, axis=0, keepdims=True)
        ix = jnp.min(jnp.where(work == mx, e_iota, N_EXPERTS), axis=0, keepdims=True)
        hot = e_iota == ix
        vals.append(mx)
        hots.append(hot)
        work = jnp.where(hot, -jnp.inf, work)
    exps = [jnp.exp(v - vals[0]) for v in vals]
    denom = exps[0] + exps[1] + exps[2] + exps[3]
    gates = [e / denom for e in exps]
    sel = jnp.zeros((N_EXPERTS, tm), f32)
    for hot in hots:
        sel = jnp.where(hot, 1.0, sel)
    t_row = lax.broadcasted_iota(jnp.int32, (tm, tm), 0)
    t_col = lax.broadcasted_iota(jnp.int32, (tm, tm), 1)
    upper = jnp.where(t_row < t_col, 1.0, 0.0).astype(bf16)
    rank = jnp.dot(sel.astype(bf16), upper, preferred_element_type=f32)
    cnt = jnp.sum(sel, axis=1, keepdims=True)
    run_units = jnp.floor((cnt + (RUN_ALIGN - 1.0)) * (1.0 / RUN_ALIGN))
    run_len = jnp.broadcast_to(run_units * RUN_ALIGN, (N_EXPERTS, LANES))
    e_row = lax.broadcasted_iota(jnp.int32, (N_EXPERTS, N_EXPERTS), 0)
    e_col = lax.broadcasted_iota(jnp.int32, (N_EXPERTS, N_EXPERTS), 1)
    lower = jnp.where(e_col < e_row, 1.0, 0.0).astype(bf16)
    run_start = jnp.dot(lower, jnp.broadcast_to(run_units, (N_EXPERTS, LANES)).astype(bf16),
                        preferred_element_type=f32) * RUN_ALIGN
    pos_full = rank + run_start[:, 0:1]
    pos = [jnp.sum(jnp.where(hot, pos_full, 0.0), axis=0, keepdims=True) for hot in hots]
    for k in range(TOP_K):
        pos_ref[k:k + 1, :] = pos[k].astype(jnp.int32)
    rows = jnp.concatenate(gates + pos + [jnp.zeros((LANES - 2 * TOP_K, tm), f32)], axis=0)
    gate_t_ref[...] = rows.T
    len_ref[0] = run_len
    off_ref[0] = tot_ref[...]
    tot_ref[...] = tot_ref[...] + run_len


def _out_router(x2, y_ret, y_diff, wo_bf16, n2w, wrt, br):
    T = x2.shape[0]
    n_tiles = T // TILE
    tok = lambda w: pl.BlockSpec((TILE, w), lambda i: (i, 0))
    const = lambda s: pl.BlockSpec(s, lambda i: (0, 0))
    per_tile = pl.BlockSpec((1, N_EXPERTS, LANES), lambda i: (i, 0, 0))
    return pl.pallas_call(
        _out_router_kernel,
        out_shape=(jax.ShapeDtypeStruct((T, D_MODEL), jnp.float32),
                   jax.ShapeDtypeStruct((T, D_MODEL), jnp.bfloat16),
                   jax.ShapeDtypeStruct((TOP_K, T), jnp.int32),
                   jax.ShapeDtypeStruct((T, LANES), jnp.float32),
                   jax.ShapeDtypeStruct((n_tiles, N_EXPERTS, LANES), jnp.float32),
                   jax.ShapeDtypeStruct((n_tiles, N_EXPERTS, LANES), jnp.float32),
                   jax.ShapeDtypeStruct((N_EXPERTS, LANES), jnp.float32)),
        grid=(n_tiles,),
        in_specs=[tok(D_MODEL), tok(RET_WIDTH), tok(DIFF_WIDTH), const((D_MODEL, D_MODEL)),
                  const((1, D_MODEL)), const((N_EXPERTS, D_MODEL)), const((N_EXPERTS, 1))],
        out_specs=(tok(D_MODEL), tok(D_MODEL), pl.BlockSpec((TOP_K, TILE), lambda i: (0, i)), tok(LANES),
                   per_tile, per_tile, const((N_EXPERTS, LANES))),
        compiler_params=_params(("arbitrary",)),
        name="out_router",
    )(x2, y_ret, y_diff, wo_bf16, n2w, wrt, br)


def _run_copies(src_ref, len_ref, dst_ref, tile, make_copy):
    for e in range(N_EXPERTS):
        n = pl.multiple_of(len_ref[tile * N_EXPERTS + e], RUN_ALIGN)
        s = pl.multiple_of(src_ref[tile * N_EXPERTS + e], RUN_ALIGN)
        d = pl.multiple_of(dst_ref[tile * N_EXPERTS + e], RUN_ALIGN)

        @pl.when(n > 0)
        def _():
            make_copy(s, d, n).start()


def _dispatch_kernel(src_ref, len_ref, dst_ref, rows_ref, zlo_ref, zlen_ref, nu_ref,
                     pos_ref, h2_ref, xs_hbm, xbuf_ref, zero_ref, sems, zero_sem):
    i = pl.program_id(0)
    n_tiles = pl.num_programs(0)
    cur = i % 2
    n_rows, tm = xbuf_ref.shape[1], h2_ref.shape[0]

    @pl.when(i == 0)
    def _():
        zero_ref[...] = jnp.zeros(zero_ref.shape, zero_ref.dtype)

        def pad_copy(e):
            n = pl.multiple_of(zlen_ref[e], RUN_ALIGN)
            lo = pl.multiple_of(zlo_ref[e], RUN_ALIGN)
            return pltpu.make_async_copy(zero_ref.at[pl.ds(0, n)], xs_hbm.at[pl.ds(lo, n)], zero_sem)

        def tail_copy(j):
            return pltpu.make_async_copy(zero_ref, xs_hbm.at[pl.ds(j * MOE_BLOCK, MOE_BLOCK)], zero_sem)

        def guarded(copy, op):
            def body(e, c):
                @pl.when(zlen_ref[e] > 0)
                def _():
                    op(copy(e))
                return c
            return body

        lax.fori_loop(0, N_EXPERTS, guarded(pad_copy, lambda cp: cp.start()), 0)
        lax.fori_loop(0, N_EXPERTS, guarded(pad_copy, lambda cp: cp.wait()), 0)
        n_blocks = xs_hbm.shape[0] // MOE_BLOCK
        lax.fori_loop(nu_ref[0], n_blocks, lambda j, c: (tail_copy(j).start(), c)[1], 0)
        lax.fori_loop(nu_ref[0], n_blocks, lambda j, c: (tail_copy(j).wait(), c)[1], 0)

    p_iota = lax.broadcasted_iota(jnp.int32, (n_rows, tm), 0)
    onehot = jnp.zeros((n_rows, tm), jnp.float32)
    for k in range(TOP_K):
        onehot = jnp.where(p_iota == pos_ref[k:k + 1, :], 1.0, onehot)
    xbuf_ref[cur] = jnp.dot(onehot.astype(jnp.bfloat16), h2_ref[...], preferred_element_type=jnp.float32)

    _run_copies(src_ref, len_ref, dst_ref, i,
                lambda s, d, n: pltpu.make_async_copy(xbuf_ref.at[cur, pl.ds(s, n)], xs_hbm.at[pl.ds(d, n)],
                                                      sems.at[cur]))

    def wait_tile(tile, slot):
        rows = pl.multiple_of(rows_ref[tile], RUN_ALIGN)
        pltpu.make_async_copy(xbuf_ref.at[slot, pl.ds(0, rows)], xs_hbm.at[pl.ds(0, rows)], sems.at[slot]).wait()

    @pl.when(i > 0)
    def _():
        wait_tile(i - 1, 1 - cur)

    @pl.when(i == n_tiles - 1)
    def _():
        wait_tile(i, cur)


def _dispatch(run_src, run_len, run_dst, tile_rows, zero_lo, zero_len, n_used, pos, h2, P):
    T = h2.shape[0]
    n_pre = 7
    grid_spec = pltpu.PrefetchScalarGridSpec(
        num_scalar_prefetch=n_pre,
        grid=(T // TILE,),
        in_specs=[pl.BlockSpec((TOP_K, TILE), lambda i, *_: (0, i)),
                  pl.BlockSpec((TILE, D_MODEL), lambda i, *_: (i, 0))],
        out_specs=pl.BlockSpec(memory_space=pl.ANY),
        scratch_shapes=[pltpu.VMEM((2, TILE_ROWS, D_MODEL), jnp.float32),
                        pltpu.VMEM((MOE_BLOCK, D_MODEL), jnp.float32),
                        pltpu.SemaphoreType.DMA((2,)),
                        pltpu.SemaphoreType.DMA(())],
    )
    return pl.pallas_call(
        _dispatch_kernel,
        out_shape=jax.ShapeDtypeStruct((P, D_MODEL), jnp.float32),
        grid_spec=grid_spec,
        compiler_params=_params(("arbitrary",), has_side_effects=True),
        name="dispatch",
    )(run_src, run_len, run_dst, tile_rows, zero_lo, zero_len, n_used, pos, h2)


def _experts_kernel(be_ref, nu_ref, xs_ref, w1_ref, b1_ref, w2_ref, b2_ref, ys_ref, w1b_ref, w2b_ref):
    j = pl.program_id(0)

    @pl.when(j < nu_ref[0])
    def _():
        @pl.when((j == 0) | (be_ref[j] != be_ref[jnp.maximum(j - 1, 0)]))
        def _():
            w1b_ref[...] = w1_ref[...].astype(jnp.bfloat16)
            w2b_ref[...] = w2_ref[...].astype(jnp.bfloat16)

        x = xs_ref[...].astype(jnp.bfloat16)
        u = jnp.dot(x, w1b_ref[...], preferred_element_type=jnp.float32) + b1_ref[...]
        glu = jnp.minimum(u[:, :D_FF], SWIGLU_LIMIT)
        lin = jnp.clip(u[:, D_FF:], -SWIGLU_LIMIT, SWIGLU_LIMIT)
        act = glu * jax.nn.sigmoid(SWIGLU_ALPHA * glu) * (lin + 1.0)
        ys_ref[...] = jnp.dot(act.astype(jnp.bfloat16), w2b_ref[...],
                              preferred_element_type=jnp.float32) + b2_ref[...]

    @pl.when(j >= nu_ref[0])
    def _():
        ys_ref[...] = jnp.zeros(ys_ref.shape, ys_ref.dtype)


def _experts(block_e, n_used, xs, w1, b1, w2, b2):
    P = xs.shape[0]
    row_block = lambda j, be, nu: (j, 0)
    expert = lambda j, be, nu: (be[j], 0, 0)
    grid_spec = pltpu.PrefetchScalarGridSpec(
        num_scalar_prefetch=2,
        grid=(P // MOE_BLOCK,),
        in_specs=[pl.BlockSpec((MOE_BLOCK, D_MODEL), row_block),
                  pl.BlockSpec((None, D_MODEL, 2 * D_FF), expert),
                  pl.BlockSpec((None, 1, 2 * D_FF), expert),
                  pl.BlockSpec((None, D_FF, D_MODEL), expert),
                  pl.BlockSpec((None, 1, D_MODEL), expert)],
        out_specs=pl.BlockSpec((MOE_BLOCK, D_MODEL), row_block),
        scratch_shapes=[pltpu.VMEM((D_MODEL, 2 * D_FF), jnp.bfloat16),
                        pltpu.VMEM((D_FF, D_MODEL), jnp.bfloat16)],
    )
    return pl.pallas_call(
        _experts_kernel,
        out_shape=jax.ShapeDtypeStruct((P, D_MODEL), jnp.float32),
        grid_spec=grid_spec,
        compiler_params=_params(("arbitrary",)),
        name="experts",
    )(block_e, n_used, xs, w1, b1, w2, b2)


def _combine_kernel(src_ref, len_ref, dst_ref, rows_ref, gate_t_ref, x1_ref, ys_hbm, o_ref, ybuf_ref, sems):
    i = pl.program_id(0)
    n_tiles = pl.num_programs(0)
    cur = i % 2
    n_rows, tm = ybuf_ref.shape[1], x1_ref.shape[0]

    def fetch(tile, slot):
        _run_copies(src_ref, len_ref, dst_ref, tile,
                    lambda s, d, n: pltpu.make_async_copy(ys_hbm.at[pl.ds(d, n)], ybuf_ref.at[slot, pl.ds(s, n)],
                                                          sems.at[slot]))

    @pl.when(i == 0)
    def _():
        ybuf_ref[...] = jnp.zeros(ybuf_ref.shape, ybuf_ref.dtype)
        fetch(0, 0)

    @pl.when(i + 1 < n_tiles)
    def _():
        fetch(i + 1, 1 - cur)

    rows = pl.multiple_of(rows_ref[i], RUN_ALIGN)
    pltpu.make_async_copy(ys_hbm.at[pl.ds(0, rows)], ybuf_ref.at[cur, pl.ds(0, rows)], sems.at[cur]).wait()

    g = gate_t_ref[...]
    p_iota = lax.broadcasted_iota(jnp.int32, (tm, n_rows), 1)
    weights = jnp.zeros((tm, n_rows), jnp.float32)
    for k in range(TOP_K):
        pos_k = g[:, TOP_K + k:TOP_K + k + 1].astype(jnp.int32)
        weights = jnp.where(p_iota == pos_k, g[:, k:k + 1], weights)
    o_ref[...] = x1_ref[...] + jnp.dot(weights.astype(jnp.bfloat16), ybuf_ref[cur].astype(jnp.bfloat16),
                                       preferred_element_type=jnp.float32)


def _combine(run_src, run_len, run_dst, tile_rows, gate_t, x1, ys):
    T = x1.shape[0]
    tok = lambda w: pl.BlockSpec((TILE, w), lambda i, *_: (i, 0))
    grid_spec = pltpu.PrefetchScalarGridSpec(
        num_scalar_prefetch=4,
        grid=(T // TILE,),
        in_specs=[tok(LANES), tok(D_MODEL), pl.BlockSpec(memory_space=pl.ANY)],
        out_specs=tok(D_MODEL),
        scratch_shapes=[pltpu.VMEM((2, TILE_ROWS, D_MODEL), jnp.float32),
                        pltpu.SemaphoreType.DMA((2,))],
    )
    return pl.pallas_call(
        _combine_kernel,
        out_shape=jax.ShapeDtypeStruct((T, D_MODEL), jnp.float32),
        grid_spec=grid_spec,
        compiler_params=_params(("arbitrary",)),
        name="combine",
    )(run_src, run_len, run_dst, tile_rows, gate_t, x1, ys)


def _rotary_tables(positions):
    pos = positions.astype(jnp.float32)[..., None]
    lane = jnp.arange(LANES)
    half_r = RET_DK // 2
    inv_r = RET_ROPE_THETA ** (-jnp.linspace(0.0, 1.0, half_r, dtype=jnp.float32))
    ang = pos * inv_r[lane % half_r]
    c2 = jnp.cos(ang)
    s2 = jnp.sin(ang) * jnp.where(lane < half_r, -1.0, 1.0)
    half_d = ROT_DIM // 2
    inv_d = ROPE_THETA ** (-jnp.arange(0, ROT_DIM, 2, dtype=jnp.float32) / ROT_DIM)
    sub = lane % DIFF_DH
    ang_d = pos * inv_d[sub % half_d]
    cd, sd = jnp.cos(ang_d), jnp.sin(ang_d)
    ra = jnp.where(sub < ROT_DIM, cd, 1.0)
    rp = jnp.where((sub >= half_d) & (sub < ROT_DIM), sd, 0.0)
    rn = jnp.where(sub < half_d, -sd, 0.0)
    return c2, s2, ra, rp, rn


def kernel(x, positions, norm1_w, w_in, ret_log_decay_fwd, ret_log_decay_bwd, ret_norm_w, q_norm_w, k_norm_w, lambda_q1, lambda_k1, lambda_q2, lambda_k2, diff_norm_w, w_out, norm2_w, w_router, b_router, w1, b1, w2, b2):
    B, S, D = x.shape
    T = B * S
    f32 = jnp.float32
    bf16 = jnp.bfloat16
    x2 = x.reshape(T, D)

    dup = lambda w: jnp.concatenate([w, w]).reshape(1, LANES).astype(f32)
    proj, rq_r, rk_r, qs, ks = _in_proj(x2, norm1_w[0].reshape(1, D), w_in[0].astype(bf16),
                                        _rotary_tables(positions), dup(q_norm_w[0]), dup(k_norm_w[0]), B, S)

    y_ret = _retention(ret_log_decay_fwd[0].astype(f32), ret_log_decay_bwd[0].astype(f32),
                       rq_r, rk_r, proj, ret_norm_w[0].reshape(1, RET_WIDTH).astype(f32), B, S)

    lam = (jnp.exp(jnp.sum(lambda_q1[0].astype(f32) * lambda_k1[0].astype(f32)))
           - jnp.exp(jnp.sum(lambda_q2[0].astype(f32) * lambda_k2[0].astype(f32))) + LAMBDA_INIT)
    lam_row = jnp.full((1, LANES), lam, f32)
    bound = (SCORE_BOUND_SLACK * DIFF_DH ** 0.5 * LOG2_E
             * jnp.max(jnp.abs(q_norm_w[0].astype(f32))) * jnp.max(jnp.abs(k_norm_w[0].astype(f32)))).reshape(1)
    attn_args = (bound, qs, ks, proj, lam_row, diff_norm_w[0].reshape(1, DIFF_DV).astype(f32), B, S)
    y_diff = lax.cond(bound[0] <= MAX_SAFE_SCORE_BOUND,
                      lambda: _diff_attn(False, *attn_args), lambda: _diff_attn(True, *attn_args))

    x1, h2, pos, gate_t, len_t, off_t, tot_t = _out_router(
        x2, y_ret, y_diff, w_out[0].astype(bf16), norm2_w[0].reshape(1, D),
        w_router[0].T.astype(f32), b_router[0].reshape(N_EXPERTS, 1).astype(f32))

    n_tiles = T // TILE
    run_len = len_t[:, :, 0].astype(jnp.int32)
    total = tot_t[:, 0].astype(jnp.int32)
    padded = ((total + MOE_BLOCK - 1) // MOE_BLOCK) * MOE_BLOCK
    pad_end = jnp.cumsum(padded)
    pad_start = pad_end - padded
    run_dst = pad_start[None, :] + off_t[:, :, 0].astype(jnp.int32)
    run_src = jnp.cumsum(run_len, axis=1) - run_len
    tile_rows = jnp.sum(run_len, axis=1)
    P = T * TOP_K + n_tiles * N_EXPERTS * RUN_ALIGN + N_EXPERTS * MOE_BLOCK
    n_blocks = P // MOE_BLOCK
    block_row = jnp.arange(n_blocks, dtype=jnp.int32) * MOE_BLOCK
    block_e = jnp.minimum(jnp.sum((pad_end[None, :] <= block_row[:, None]).astype(jnp.int32), axis=1),
                          N_EXPERTS - 1)
    n_used = (pad_end[-1:] // MOE_BLOCK).astype(jnp.int32)
    runs = (run_src.reshape(-1), run_len.reshape(-1), run_dst.reshape(-1), tile_rows)

    xs = _dispatch(*runs, pad_start + total, padded - total, n_used, pos, h2, P)
    ys = _experts(block_e, n_used, xs, w1[0], b1[0].reshape(N_EXPERTS, 1, 2 * D_FF),
                  w2[0], b2[0].reshape(N_EXPERTS, 1, D))
    out = _combine(*runs, gate_t, x1, ys)
    return out.reshape(B, S, D)
```

```python
import functools

import jax
import jax.numpy as jnp
from jax import lax
from jax.experimental import pallas as pl
from jax.experimental.pallas import tpu as pltpu

EPS = 1e-6
D_MODEL = 1024
RET_HEADS = 4
RET_DK = 128
RET_WIDTH = 512
RET_ROPE_THETA = 10000.0
DIFF_HEADS = 4
DIFF_DH = 64
DIFF_DV = 128
DIFF_WIDTH = 512
ROPE_THETA = 500000.0
ROT_DIM = DIFF_DH // 4
D_IN_PROJ = 3584
N_EXPERTS = 32
TOP_K = 4
D_FF = 1024
SWIGLU_LIMIT = 7.0
SWIGLU_ALPHA = 1.702
LAMBDA_INIT = 0.8 - 0.6 * 1.0

LOG2_E = 1.4426950408889634
SCORE_BOUND_SLACK = 1.02
MAX_SAFE_SCORE_BOUND = 60.0
LANES = 128
SUBLANES = 8
VMEM_LIMIT = 56 * 1024 * 1024

COL_RQ, COL_RK, COL_RV, COL_RG, COL_DQ, COL_DK, COL_DV = 0, 4, 8, 12, 16, 20, 24
VGV_RV, VGV_RG, VGV_DV = 0, 4, 8

TM_PROJ = 512
RET_CHUNK = 128
RET_UNROLL = 16
TQ_ATTN = 512
TK_ATTN = 2048
TILE = 512
MOE_BLOCK = 512
RUN_ALIGN = SUBLANES
TILE_ROWS = TOP_K * TILE + N_EXPERTS * RUN_ALIGN


def _params(sem, **kw):
    return pltpu.CompilerParams(dimension_semantics=sem, vmem_limit_bytes=VMEM_LIMIT, **kw)


def _in_proj_kernel(x_ref, nw_ref, w_ref, c2_ref, s2_ref, ra_ref, rp_ref, rn_ref, qw_ref, kw_ref,
                    vgv_ref, rqo_ref, rko_ref, qs_ref, ks_ref):
    ts = x_ref.shape[0]
    x = x_ref[...]
    hn = (x * lax.rsqrt(jnp.mean(x * x, axis=-1, keepdims=True) + EPS) * nw_ref[...]).astype(jnp.bfloat16)

    def proj(col_block):
        c0 = col_block * LANES
        return jnp.dot(hn, w_ref[:, c0:c0 + 4 * LANES], preferred_element_type=jnp.float32)

    rq, rk, dq, dk = proj(COL_RQ), proj(COL_RK), proj(COL_DQ), proj(COL_DK)
    c2 = c2_ref[...]
    s2 = s2_ref[...]
    ra = ra_ref[...]
    rp = rp_ref[...]
    rn = rn_ref[...]
    lane = lax.broadcasted_iota(jnp.int32, (ts, LANES), 1)
    lo = lane < DIFF_DH

    def qk_norm_rot(x, w):
        x2 = x * x
        s_lo = jnp.sum(jnp.where(lo, x2, 0.0), axis=-1, keepdims=True)
        s_hi = jnp.sum(jnp.where(lo, 0.0, x2), axis=-1, keepdims=True)
        ms = jnp.where(lo, s_lo, s_hi) * (1.0 / DIFF_DH)
        xn = x * lax.rsqrt(ms + EPS) * w
        return xn * ra + pltpu.roll(xn, ROT_DIM // 2, 1) * rp + pltpu.roll(xn, LANES - ROT_DIM // 2, 1) * rn

    for h in range(RET_HEADS):
        sl = slice(h * LANES, (h + 1) * LANES)
        q = rq[:, sl]
        k = rk[:, sl]
        rqo_ref[:, sl] = (q * c2 + pltpu.roll(q, RET_DK // 2, 1) * s2).astype(rqo_ref.dtype)
        rko_ref[:, sl] = ((k * c2 + pltpu.roll(k, RET_DK // 2, 1) * s2) * (RET_DK ** -0.5)).astype(rko_ref.dtype)
    for h in range(DIFF_HEADS):
        sl = slice(h * LANES, (h + 1) * LANES)
        q = qk_norm_rot(dq[:, sl], qw_ref[...]) * (DIFF_DH ** -0.5 * LOG2_E)
        k = qk_norm_rot(dk[:, sl], kw_ref[...])
        qs_ref[h, 0] = jnp.where(lo, q, 0.0).astype(qs_ref.dtype)
        qs_ref[h, 1] = jnp.where(lo, 0.0, q).astype(qs_ref.dtype)
        ks_ref[:, sl] = k.astype(ks_ref.dtype)
    for slot, col_block in enumerate((COL_RV, COL_RG, COL_DV)):
        vgv_ref[:, slot * 4 * LANES:(slot + 1) * 4 * LANES] = proj(col_block).astype(vgv_ref.dtype)


def _in_proj(x2, nw, w_bf16, tabs, qw2, kw2, B, S):
    T = B * S
    n_s = S // TM_PROJ
    tok = lambda w: pl.BlockSpec((TM_PROJ, w), lambda i: (i, 0))
    const = lambda s: pl.BlockSpec(s, lambda i: (0, 0))
    tab = pl.BlockSpec((None, TM_PROJ, LANES), lambda i: (i // n_s, i % n_s, 0))
    bf16 = jnp.bfloat16
    return pl.pallas_call(
        _in_proj_kernel,
        out_shape=(jax.ShapeDtypeStruct((T, 3 * 4 * LANES), bf16),
                   jax.ShapeDtypeStruct((T, 4 * LANES), bf16),
                   jax.ShapeDtypeStruct((T, 4 * LANES), bf16),
                   jax.ShapeDtypeStruct((B, DIFF_HEADS, 2, S, LANES), bf16),
                   jax.ShapeDtypeStruct((T, 4 * LANES), bf16)),
        grid=(T // TM_PROJ,),
        in_specs=[tok(D_MODEL), const((1, D_MODEL)), const((D_MODEL, D_IN_PROJ)),
                  tab, tab, tab, tab, tab, const((1, LANES)), const((1, LANES))],
        out_specs=(tok(3 * 4 * LANES), tok(4 * LANES), tok(4 * LANES),
                   pl.BlockSpec((None, DIFF_HEADS, 2, TM_PROJ, LANES), lambda i: (i // n_s, 0, 0, i % n_s, 0)),
                   tok(4 * LANES)),
        compiler_params=_params(("arbitrary",)),
        name="in_proj",
    )(x2, nw, w_bf16, *tabs, qw2, kw2)


def _retention_kernel(ldf_ref, ldb_ref, q_ref, k_ref, v_ref, g_ref, nw_ref, o_ref, sb_ref):
    C = RET_CHUNK
    S = q_ref.shape[0]
    n_chunks = S // C
    h = pl.program_id(1)
    ldf = ldf_ref[h]
    ldb = ldb_ref[h]
    row = lax.broadcasted_iota(jnp.int32, (C, C), 0).astype(jnp.float32)
    colm = lax.broadcasted_iota(jnp.int32, (C, C), 1).astype(jnp.float32)
    dist = row - colm
    decay = jnp.where(dist >= 0, jnp.exp(ldf * jnp.maximum(dist, 0.0)), jnp.exp(ldb * jnp.maximum(-dist, 0.0)))
    idx = lax.broadcasted_iota(jnp.int32, (C, 1), 0).astype(jnp.float32)
    q_dec_f = jnp.exp(ldf * (idx + 1.0))
    k_dec_f = jnp.exp(ldf * (C - 1.0 - idx))
    q_dec_b = jnp.exp(ldb * (C - idx))
    k_dec_b = jnp.exp(ldb * idx)
    chunk_dec_f = jnp.exp(ldf * C)
    chunk_dec_b = jnp.exp(ldb * C)
    f32 = jnp.float32
    bf16 = jnp.bfloat16

    def kv_state(k, v, k_dec):
        kd = (k.astype(f32) * k_dec).astype(bf16)
        return lax.dot_general(kd, v, (((0,), (0,)), ((), ())), preferred_element_type=f32)

    def bwd_step(i, state):
        c = n_chunks - 1 - i
        r0 = pl.multiple_of(c * C, C)
        sb_ref[c] = state
        return state * chunk_dec_b + kv_state(k_ref[pl.ds(r0, C), :], v_ref[pl.ds(r0, C), :], k_dec_b)

    lax.fori_loop(0, n_chunks, bwd_step, jnp.zeros((RET_DK, LANES), f32), unroll=RET_UNROLL)

    def fwd_step(c, state):
        r0 = pl.multiple_of(c * C, C)
        q = q_ref[pl.ds(r0, C), :]
        k = k_ref[pl.ds(r0, C), :]
        v = v_ref[pl.ds(r0, C), :]
        scores = lax.dot_general(q, k, (((1,), (1,)), ((), ())), preferred_element_type=f32) * decay
        y = jnp.dot(scores.astype(bf16), v, preferred_element_type=f32)
        qf = q.astype(f32)
        y += jnp.dot((qf * q_dec_f).astype(bf16), state.astype(bf16), preferred_element_type=f32)
        y += jnp.dot((qf * q_dec_b).astype(bf16), sb_ref[c].astype(bf16), preferred_element_type=f32)
        yn = y * lax.rsqrt(jnp.mean(y * y, axis=-1, keepdims=True) + EPS) * nw_ref[...]
        g = g_ref[pl.ds(r0, C), :].astype(f32)
        o_ref[pl.ds(r0, C), :] = (yn * (g * jax.nn.sigmoid(g))).astype(o_ref.dtype)
        return state * chunk_dec_f + kv_state(k, v, k_dec_f)

    lax.fori_loop(0, n_chunks, fwd_step, jnp.zeros((RET_DK, LANES), f32), unroll=RET_UNROLL)


def _retention(ldf, ldb, rq_r, rk_r, proj, nw, B, S):
    T = B * S
    smem = pl.BlockSpec(memory_space=pltpu.SMEM)
    seq = lambda cb: pl.BlockSpec((S, LANES), lambda b, h: (b, cb + h))
    return pl.pallas_call(
        _retention_kernel,
        out_shape=jax.ShapeDtypeStruct((T, RET_WIDTH), jnp.bfloat16),
        grid=(B, RET_HEADS),
        in_specs=[smem, smem, seq(0), seq(0), seq(VGV_RV), seq(VGV_RG),
                  pl.BlockSpec((1, LANES), lambda b, h: (0, h))],
        out_specs=seq(0),
        scratch_shapes=[pltpu.VMEM((S // RET_CHUNK, RET_DK, LANES), jnp.float32)],
        compiler_params=_params(("arbitrary", "arbitrary")),
        name="retention",
    )(ldf, ldb, rq_r, rk_r, proj, proj, nw)


def _diff_attn_kernel(online_max, bound_ref, q_ref, k_ref, v_ref, lam_ref, nw_ref, o_ref, m_ref, l_ref, acc_ref):
    tq = q_ref.shape[1]
    S = k_ref.shape[0]
    f32 = jnp.float32
    q = q_ref[...].reshape(2 * tq, LANES)
    if online_max:
        m_ref[...] = jnp.full(m_ref.shape, -jnp.inf, f32)
    l_ref[...] = jnp.zeros(l_ref.shape, f32)
    acc_ref[...] = jnp.zeros(acc_ref.shape, f32)
    n_tiles = TK_ATTN // LANES

    def kv_step(j, carry):
        r0 = pl.multiple_of(j * TK_ATTN, TK_ATTN)
        k = k_ref[pl.ds(r0, TK_ATTN), :]
        v = v_ref[pl.ds(r0, TK_ATTN), :]
        s = lax.dot_general(q, k, (((1,), (1,)), ((), ())), preferred_element_type=f32)
        tiles = [s[:, c * LANES:(c + 1) * LANES] for c in range(n_tiles)]
        if online_max:
            part = tiles[0]
            for t in tiles[1:]:
                part = jnp.maximum(part, t)
            m_prev = m_ref[...]
            shift = jnp.maximum(m_prev, jnp.max(part, axis=-1, keepdims=True))
            alpha = jnp.exp2(m_prev - shift)
            m_ref[...] = shift
        else:
            shift = bound_ref[0]
        probs = [jnp.exp2(t - shift) for t in tiles]
        psum = probs[0]
        for p in probs[1:]:
            psum = psum + p
        pv = jnp.dot(jnp.concatenate([p.astype(jnp.bfloat16) for p in probs], axis=1), v,
                     preferred_element_type=f32)
        if online_max:
            l_ref[...] = alpha * l_ref[...] + psum
            acc_ref[...] = alpha * acc_ref[...] + pv
        else:
            l_ref[...] = l_ref[...] + psum
            acc_ref[...] = acc_ref[...] + pv
        return carry

    lax.fori_loop(0, S // TK_ATTN, kv_step, 0)
    o = acc_ref[...] / jnp.sum(l_ref[...], axis=-1, keepdims=True)
    d = o[:tq] - lam_ref[...] * o[tq:]
    dn = d * lax.rsqrt(jnp.mean(d * d, axis=-1, keepdims=True) + EPS) * nw_ref[...]
    o_ref[...] = (dn * (1.0 - LAMBDA_INIT)).astype(o_ref.dtype)


def _diff_attn(online_max, bound, qs, ks, proj, lam, nw, B, S):
    T = B * S
    n_q = S // TQ_ATTN
    one = pl.BlockSpec((1, LANES), lambda b, h, i, bd: (0, 0))
    grid_spec = pltpu.PrefetchScalarGridSpec(
        num_scalar_prefetch=1,
        grid=(B, DIFF_HEADS, n_q),
        in_specs=[pl.BlockSpec((None, None, 2, TQ_ATTN, LANES), lambda b, h, i, bd: (b, h, 0, i, 0)),
                  pl.BlockSpec((S, LANES), lambda b, h, i, bd: (b, h)),
                  pl.BlockSpec((S, LANES), lambda b, h, i, bd: (b, VGV_DV + h)),
                  one, one],
        out_specs=pl.BlockSpec((TQ_ATTN, LANES), lambda b, h, i, bd: (b * n_q + i, h)),
        scratch_shapes=[pltpu.VMEM((2 * TQ_ATTN, LANES), jnp.float32)] * 3,
    )
    return pl.pallas_call(
        functools.partial(_diff_attn_kernel, online_max),
        out_shape=jax.ShapeDtypeStruct((T, DIFF_WIDTH), jnp.bfloat16),
        grid_spec=grid_spec,
        compiler_params=_params(("arbitrary", "arbitrary", "arbitrary")),
        name="diff_attn_online" if online_max else "diff_attn",
    )(bound, qs, ks, proj, lam, nw)


def _out_router_kernel(x_ref, yr_ref, yd_ref, wo_ref, n2_ref, wrt_ref, br_ref,
                       x1_ref, h2_ref, pos_ref, gate_t_ref, len_ref, off_ref, tot_ref):
    tm = x_ref.shape[0]
    f32 = jnp.float32
    bf16 = jnp.bfloat16

    @pl.when(pl.program_id(0) == 0)
    def _():
        tot_ref[...] = jnp.zeros(tot_ref.shape, f32)

    att = jnp.dot(yr_ref[...], wo_ref[:RET_WIDTH, :], preferred_element_type=f32)
    att += jnp.dot(yd_ref[...], wo_ref[RET_WIDTH:, :], preferred_element_type=f32)
    x1 = x_ref[...] + att
    x1_ref[...] = x1
    h2 = x1 * lax.rsqrt(jnp.mean(x1 * x1, axis=-1, keepdims=True) + EPS) * n2_ref[...]
    h2_ref[...] = h2.astype(h2_ref.dtype)
    logits = lax.dot_general(wrt_ref[...], h2, (((1,), (1,)), ((), ())),
                             precision=lax.Precision.HIGHEST, preferred_element_type=f32) + br_ref[...]
    e_iota = lax.broadcasted_iota(jnp.int32, (N_EXPERTS, tm), 0)
    work = logits
    vals, hots = [], []
    for _ in range(TOP_K):
        mx = jnp.max(work, axis=0, keepdims=True)
        ix = jnp.min(jnp.where(work == mx, e_iota, N_EXPERTS), axis=0, keepdims=True)
        hot = e_iota == ix
        vals.append(mx)
        hots.append(hot)
        work = jnp.where(hot, -jnp.inf, work)
    exps = [jnp.exp(v - vals[0]) for v in vals]
    denom = exps[0] + exps[1] + exps[2] + exps[3]
    gates = [e / denom for e in exps]
    sel = jnp.zeros((N_EXPERTS, tm), f32)
    for hot in hots:
        sel = jnp.where(hot, 1.0, sel)
    t_row = lax.broadcasted_iota(jnp.int32, (tm, tm), 0)
    t_col = lax.broadcasted_iota(jnp.int32, (tm, tm), 1)
    upper = jnp.where(t_row < t_col, 1.0, 0.0).astype(bf16)
    rank = jnp.dot(sel.astype(bf16), upper, preferred_element_type=f32)
    cnt = jnp.sum(sel, axis=1, keepdims=True)
    run_units = jnp.floor((cnt + (RUN_ALIGN - 1.0)) * (1.0 / RUN_ALIGN))
    run_len = jnp.broadcast_to(run_units * RUN_ALIGN, (N_EXPERTS, LANES))
    e_row = lax.broadcasted_iota(jnp.int32, (N_EXPERTS, N_EXPERTS), 0)
    e_col = lax.broadcasted_iota(jnp.int32, (N_EXPERTS, N_EXPERTS), 1)
    lower = jnp.where(e_col < e_row, 1.0, 0.0).astype(bf16)
    run_start = jnp.dot(lower, jnp.broadcast_to(run_units, (N_EXPERTS, LANES)).astype(bf16),
                        preferred_element_type=f32) * RUN_ALIGN
    pos_full = rank + run_start[:, 0:1]
    pos = [jnp.sum(jnp.where(hot, pos_full, 0.0), axis=0, keepdims=True) for hot in hots]
    for k in range(TOP_K):
        pos_ref[k:k + 1, :] = pos[k].astype(jnp.int32)
    rows = jnp.concatenate(gates + pos + [jnp.zeros((LANES - 2 * TOP_K, tm), f32)], axis=0)
    gate_t_ref[...] = rows.T
    len_ref[0] = run_len
    off_ref[0] = tot_ref[...]
    tot_ref[...] = tot_ref[...] + run_len


def _out_router(x2, y_ret, y_diff, wo_bf16, n2w, wrt, br):
    T = x2.shape[0]
    n_tiles = T // TILE
    tok = lambda w: pl.BlockSpec((TILE, w), lambda i: (i, 0))
    const = lambda s: pl.BlockSpec(s, lambda i: (0, 0))
    per_tile = pl.BlockSpec((1, N_EXPERTS, LANES), lambda i: (i, 0, 0))
    return pl.pallas_call(
        _out_router_kernel,
        out_shape=(jax.ShapeDtypeStruct((T, D_MODEL), jnp.float32),
                   jax.ShapeDtypeStruct((T, D_MODEL), jnp.bfloat16),
                   jax.ShapeDtypeStruct((TOP_K, T), jnp.int32),
                   jax.ShapeDtypeStruct((T, LANES), jnp.float32),
                   jax.ShapeDtypeStruct((n_tiles, N_EXPERTS, LANES), jnp.float32),
                   jax.ShapeDtypeStruct((n_tiles, N_EXPERTS, LANES), jnp.float32),
                   jax.ShapeDtypeStruct((N_EXPERTS, LANES), jnp.float32)),
        grid=(n_tiles,),
        in_specs=[tok(D_MODEL), tok(RET_WIDTH), tok(DIFF_WIDTH), const((D_MODEL, D_MODEL)),
                  const((1, D_MODEL)), const((N_EXPERTS, D_MODEL)), const((N_EXPERTS, 1))],
        out_specs=(tok(D_MODEL), tok(D_MODEL), pl.BlockSpec((TOP_K, TILE), lambda i: (0, i)), tok(LANES),
                   per_tile, per_tile, const((N_EXPERTS, LANES))),
        compiler_params=_params(("arbitrary",)),
        name="out_router",
    )(x2, y_ret, y_diff, wo_bf16, n2w, wrt, br)


def _run_copies(src_ref, len_ref, dst_ref, tile, make_copy):
    for e in range(N_EXPERTS):
        n = pl.multiple_of(len_ref[tile * N_EXPERTS + e], RUN_ALIGN)
        s = pl.multiple_of(src_ref[tile * N_EXPERTS + e], RUN_ALIGN)
        d = pl.multiple_of(dst_ref[tile * N_EXPERTS + e], RUN_ALIGN)

        @pl.when(n > 0)
        def _():
            make_copy(s, d, n).start()


def _dispatch_kernel(src_ref, len_ref, dst_ref, rows_ref, zlo_ref, zlen_ref, nu_ref,
                     pos_ref, h2_ref, xs_hbm, xbuf_ref, zero_ref, sems, zero_sem):
    i = pl.program_id(0)
    n_tiles = pl.num_programs(0)
    cur = i % 2
    n_rows, tm = xbuf_ref.shape[1], h2_ref.shape[0]

    @pl.when(i == 0)
    def _():
        zero_ref[...] = jnp.zeros(zero_ref.shape, zero_ref.dtype)

        def pad_copy(e):
            n = pl.multiple_of(zlen_ref[e], RUN_ALIGN)
            lo = pl.multiple_of(zlo_ref[e], RUN_ALIGN)
            return pltpu.make_async_copy(zero_ref.at[pl.ds(0, n)], xs_hbm.at[pl.ds(lo, n)], zero_sem)

        def tail_copy(j):
            return pltpu.make_async_copy(zero_ref, xs_hbm.at[pl.ds(j * MOE_BLOCK, MOE_BLOCK)], zero_sem)

        def guarded(copy, op):
            def body(e, c):
                @pl.when(zlen_ref[e] > 0)
                def _():
                    op(copy(e))
                return c
            return body

        lax.fori_loop(0, N_EXPERTS, guarded(pad_copy, lambda cp: cp.start()), 0)
        lax.fori_loop(0, N_EXPERTS, guarded(pad_copy, lambda cp: cp.wait()), 0)
        n_blocks = xs_hbm.shape[0] // MOE_BLOCK
        lax.fori_loop(nu_ref[0], n_blocks, lambda j, c: (tail_copy(j).start(), c)[1], 0)
        lax.fori_loop(nu_ref[0], n_blocks, lambda j, c: (tail_copy(j).wait(), c)[1], 0)

    p_iota = lax.broadcasted_iota(jnp.int32, (n_rows, tm), 0)
    onehot = jnp.zeros((n_rows, tm), jnp.float32)
    for k in range(TOP_K):
        onehot = jnp.where(p_iota == pos_ref[k:k + 1, :], 1.0, onehot)
    xbuf_ref[cur] = jnp.dot(onehot.astype(jnp.bfloat16), h2_ref[...], preferred_element_type=jnp.float32)

    _run_copies(src_ref, len_ref, dst_ref, i,
                lambda s, d, n: pltpu.make_async_copy(xbuf_ref.at[cur, pl.ds(s, n)], xs_hbm.at[pl.ds(d, n)],
                                                      sems.at[cur]))

    def wait_tile(tile, slot):
        rows = pl.multiple_of(rows_ref[tile], RUN_ALIGN)
        pltpu.make_async_copy(xbuf_ref.at[slot, pl.ds(0, rows)], xs_hbm.at[pl.ds(0, rows)], sems.at[slot]).wait()

    @pl.when(i > 0)
    def _():
        wait_tile(i - 1, 1 - cur)

    @pl.when(i == n_tiles - 1)
    def _():
        wait_tile(i, cur)


def _dispatch(run_src, run_len, run_dst, tile_rows, zero_lo, zero_len, n_used, pos, h2, P):
    T = h2.shape[0]
    n_pre = 7
    grid_spec = pltpu.PrefetchScalarGridSpec(
        num_scalar_prefetch=n_pre,
        grid=(T // TILE,),
        in_specs=[pl.BlockSpec((TOP_K, TILE), lambda i, *_: (0, i)),
                  pl.BlockSpec((TILE, D_MODEL), lambda i, *_: (i, 0))],
        out_specs=pl.BlockSpec(memory_space=pl.ANY),
        scratch_shapes=[pltpu.VMEM((2, TILE_ROWS, D_MODEL), jnp.float32),
                        pltpu.VMEM((MOE_BLOCK, D_MODEL), jnp.float32),
                        pltpu.SemaphoreType.DMA((2,)),
                        pltpu.SemaphoreType.DMA(())],
    )
    return pl.pallas_call(
        _dispatch_kernel,
        out_shape=jax.ShapeDtypeStruct((P, D_MODEL), jnp.float32),
        grid_spec=grid_spec,
        compiler_params=_params(("arbitrary",), has_side_effects=True),
        name="dispatch",
    )(run_src, run_len, run_dst, tile_rows, zero_lo, zero_len, n_used, pos, h2)


def _experts_kernel(base_ref, nblk_ref, w1_ref, b1_ref, w2_ref, b2_ref, xs_hbm, ys_hbm,
                    w1b_ref, w2b_ref, xbuf_ref, ybuf_ref, in_sems, out_sems):
    e = pl.program_id(0)
    n = nblk_ref[e]
    base = base_ref[e]

    def rows(j):
        return pl.ds(pl.multiple_of(base + j * MOE_BLOCK, MOE_BLOCK), MOE_BLOCK)

    def in_copy(j, slot):
        return pltpu.make_async_copy(xs_hbm.at[rows(j)], xbuf_ref.at[slot], in_sems.at[slot])

    def out_copy(j, slot):
        return pltpu.make_async_copy(ybuf_ref.at[slot], ys_hbm.at[rows(j)], out_sems.at[slot])

    @pl.when(n > 0)
    def _():
        in_copy(0, 0).start()
        w1b_ref[...] = w1_ref[...].astype(jnp.bfloat16)
        w2b_ref[...] = w2_ref[...].astype(jnp.bfloat16)

        def block(j, carry):
            slot = j % 2

            @pl.when(j + 1 < n)
            def _():
                in_copy(j + 1, 1 - slot).start()

            in_copy(j, slot).wait()

            @pl.when(j >= 2)
            def _():
                out_copy(j - 2, slot).wait()

            x = xbuf_ref[slot].astype(jnp.bfloat16)
            u = jnp.dot(x, w1b_ref[...], preferred_element_type=jnp.float32) + b1_ref[...]
            glu = jnp.minimum(u[:, :D_FF], SWIGLU_LIMIT)
            lin = jnp.clip(u[:, D_FF:], -SWIGLU_LIMIT, SWIGLU_LIMIT)
            act = glu * jax.nn.sigmoid(SWIGLU_ALPHA * glu) * (lin + 1.0)
            ybuf_ref[slot] = jnp.dot(act.astype(jnp.bfloat16), w2b_ref[...],
                                     preferred_element_type=jnp.float32) + b2_ref[...]
            out_copy(j, slot).start()
            return carry

        lax.fori_loop(0, n, block, 0)

        @pl.when(n >= 2)
        def _():
            out_copy(n - 2, n % 2).wait()

        out_copy(n - 1, (n - 1) % 2).wait()

    @pl.when(e == N_EXPERTS - 1)
    def _():
        ybuf_ref[0] = jnp.zeros(ybuf_ref.shape[1:], ybuf_ref.dtype)
        first_unused = (base + n * MOE_BLOCK) // MOE_BLOCK
        n_blocks = ys_hbm.shape[0] // MOE_BLOCK

        def tail_copy(j):
            return pltpu.make_async_copy(ybuf_ref.at[0], ys_hbm.at[pl.ds(j * MOE_BLOCK, MOE_BLOCK)],
                                         out_sems.at[0])

        lax.fori_loop(first_unused, n_blocks, lambda j, c: (tail_copy(j).start(), c)[1], 0)
        lax.fori_loop(first_unused, n_blocks, lambda j, c: (tail_copy(j).wait(), c)[1], 0)


def _experts(base, n_blk, xs, w1, b1, w2, b2):
    P = xs.shape[0]
    expert = lambda e, bs, nb: (e, 0, 0)
    grid_spec = pltpu.PrefetchScalarGridSpec(
        num_scalar_prefetch=2,
        grid=(N_EXPERTS,),
        in_specs=[pl.BlockSpec((None, D_MODEL, 2 * D_FF), expert),
                  pl.BlockSpec((None, 1, 2 * D_FF), expert),
                  pl.BlockSpec((None, D_FF, D_MODEL), expert),
                  pl.BlockSpec((None, 1, D_MODEL), expert),
                  pl.BlockSpec(memory_space=pl.ANY)],
        out_specs=pl.BlockSpec(memory_space=pl.ANY),
        scratch_shapes=[pltpu.VMEM((D_MODEL, 2 * D_FF), jnp.bfloat16),
                        pltpu.VMEM((D_FF, D_MODEL), jnp.bfloat16),
                        pltpu.VMEM((2, MOE_BLOCK, D_MODEL), jnp.float32),
                        pltpu.VMEM((2, MOE_BLOCK, D_MODEL), jnp.float32),
                        pltpu.SemaphoreType.DMA((2,)),
                        pltpu.SemaphoreType.DMA((2,))],
    )
    return pl.pallas_call(
        _experts_kernel,
        out_shape=jax.ShapeDtypeStruct((P, D_MODEL), jnp.float32),
        grid_spec=grid_spec,
        compiler_params=_params(("arbitrary",)),
        name="experts",
    )(base, n_blk, w1, b1, w2, b2, xs)


def _combine_kernel(src_ref, len_ref, dst_ref, rows_ref, gate_t_ref, x1_ref, ys_hbm, o_ref, ybuf_ref, sems):
    i = pl.program_id(0)
    n_tiles = pl.num_programs(0)
    cur = i % 2
    n_rows, tm = ybuf_ref.shape[1], x1_ref.shape[0]

    def fetch(tile, slot):
        _run_copies(src_ref, len_ref, dst_ref, tile,
                    lambda s, d, n: pltpu.make_async_copy(ys_hbm.at[pl.ds(d, n)], ybuf_ref.at[slot, pl.ds(s, n)],
                                                          sems.at[slot]))

    @pl.when(i == 0)
    def _():
        ybuf_ref[...] = jnp.zeros(ybuf_ref.shape, ybuf_ref.dtype)
        fetch(0, 0)

    @pl.when(i + 1 < n_tiles)
    def _():
        fetch(i + 1, 1 - cur)

    rows = pl.multiple_of(rows_ref[i], RUN_ALIGN)
    pltpu.make_async_copy(ys_hbm.at[pl.ds(0, rows)], ybuf_ref.at[cur, pl.ds(0, rows)], sems.at[cur]).wait()

    g = gate_t_ref[...]
    p_iota = lax.broadcasted_iota(jnp.int32, (tm, n_rows), 1)
    weights = jnp.zeros((tm, n_rows), jnp.float32)
    for k in range(TOP_K):
        pos_k = g[:, TOP_K + k:TOP_K + k + 1].astype(jnp.int32)
        weights = jnp.where(p_iota == pos_k, g[:, k:k + 1], weights)
    o_ref[...] = x1_ref[...] + jnp.dot(weights.astype(jnp.bfloat16), ybuf_ref[cur].astype(jnp.bfloat16),
                                       preferred_element_type=jnp.float32)


def _combine(run_src, run_len, run_dst, tile_rows, gate_t, x1, ys):
    T = x1.shape[0]
    tok = lambda w: pl.BlockSpec((TILE, w), lambda i, *_: (i, 0))
    grid_spec = pltpu.PrefetchScalarGridSpec(
        num_scalar_prefetch=4,
        grid=(T // TILE,),
        in_specs=[tok(LANES), tok(D_MODEL), pl.BlockSpec(memory_space=pl.ANY)],
        out_specs=tok(D_MODEL),
        scratch_shapes=[pltpu.VMEM((2, TILE_ROWS, D_MODEL), jnp.float32),
                        pltpu.SemaphoreType.DMA((2,))],
    )
    return pl.pallas_call(
        _combine_kernel,
        out_shape=jax.ShapeDtypeStruct((T, D_MODEL), jnp.float32),
        grid_spec=grid_spec,
        compiler_params=_params(("arbitrary",)),
        name="combine",
    )(run_src, run_len, run_dst, tile_rows, gate_t, x1, ys)


def _rotary_tables(positions):
    pos = positions.astype(jnp.float32)[..., None]
    lane = jnp.arange(LANES)
    half_r = RET_DK // 2
    inv_r = RET_ROPE_THETA ** (-jnp.linspace(0.0, 1.0, half_r, dtype=jnp.float32))
    ang = pos * inv_r[lane % half_r]
    c2 = jnp.cos(ang)
    s2 = jnp.sin(ang) * jnp.where(lane < half_r, -1.0, 1.0)
    half_d = ROT_DIM // 2
    inv_d = ROPE_THETA ** (-jnp.arange(0, ROT_DIM, 2, dtype=jnp.float32) / ROT_DIM)
    sub = lane % DIFF_DH
    ang_d = pos * inv_d[sub % half_d]
    cd, sd = jnp.cos(ang_d), jnp.sin(ang_d)
    ra = jnp.where(sub < ROT_DIM, cd, 1.0)
    rp = jnp.where((sub >= half_d) & (sub < ROT_DIM), sd, 0.0)
    rn = jnp.where(sub < half_d, -sd, 0.0)
    return c2, s2, ra, rp, rn


def kernel(x, positions, norm1_w, w_in, ret_log_decay_fwd, ret_log_decay_bwd, ret_norm_w, q_norm_w, k_norm_w, lambda_q1, lambda_k1, lambda_q2, lambda_k2, diff_norm_w, w_out, norm2_w, w_router, b_router, w1, b1, w2, b2):
    B, S, D = x.shape
    T = B * S
    f32 = jnp.float32
    bf16 = jnp.bfloat16
    x2 = x.reshape(T, D)

    dup = lambda w: jnp.concatenate([w, w]).reshape(1, LANES).astype(f32)
    proj, rq_r, rk_r, qs, ks = _in_proj(x2, norm1_w[0].reshape(1, D), w_in[0].astype(bf16),
                                        _rotary_tables(positions), dup(q_norm_w[0]), dup(k_norm_w[0]), B, S)

    y_ret = _retention(ret_log_decay_fwd[0].astype(f32), ret_log_decay_bwd[0].astype(f32),
                       rq_r, rk_r, proj, ret_norm_w[0].reshape(1, RET_WIDTH).astype(f32), B, S)

    lam = (jnp.exp(jnp.sum(lambda_q1[0].astype(f32) * lambda_k1[0].astype(f32)))
           - jnp.exp(jnp.sum(lambda_q2[0].astype(f32) * lambda_k2[0].astype(f32))) + LAMBDA_INIT)
    lam_row = jnp.full((1, LANES), lam, f32)
    bound = (SCORE_BOUND_SLACK * DIFF_DH ** 0.5 * LOG2_E
             * jnp.max(jnp.abs(q_norm_w[0].astype(f32))) * jnp.max(jnp.abs(k_norm_w[0].astype(f32)))).reshape(1)
    attn_args = (bound, qs, ks, proj, lam_row, diff_norm_w[0].reshape(1, DIFF_DV).astype(f32), B, S)
    y_diff = lax.cond(bound[0] <= MAX_SAFE_SCORE_BOUND,
                      lambda: _diff_attn(False, *attn_args), lambda: _diff_attn(True, *attn_args))

    x1, h2, pos, gate_t, len_t, off_t, tot_t = _out_router(
        x2, y_ret, y_diff, w_out[0].astype(bf16), norm2_w[0].reshape(1, D),
        w_router[0].T.astype(f32), b_router[0].reshape(N_EXPERTS, 1).astype(f32))

    n_tiles = T // TILE
    run_len = len_t[:, :, 0].astype(jnp.int32)
    total = tot_t[:, 0].astype(jnp.int32)
    padded = ((total + MOE_BLOCK - 1) // MOE_BLOCK) * MOE_BLOCK
    pad_end = jnp.cumsum(padded)
    pad_start = pad_end - padded
    run_dst = pad_start[None, :] + off_t[:, :, 0].astype(jnp.int32)
    run_src = jnp.cumsum(run_len, axis=1) - run_len
    tile_rows = jnp.sum(run_len, axis=1)
    P = T * TOP_K + n_tiles * N_EXPERTS * RUN_ALIGN + N_EXPERTS * MOE_BLOCK
    n_used = (pad_end[-1:] // MOE_BLOCK).astype(jnp.int32)
    runs = (run_src.reshape(-1), run_len.reshape(-1), run_dst.reshape(-1), tile_rows)

    xs = _dispatch(*runs, pad_start + total, padded - total, n_used, pos, h2, P)
    ys = _experts(pad_start, padded // MOE_BLOCK, xs, w1[0], b1[0].reshape(N_EXPERTS, 1, 2 * D_FF),
                  w2[0], b2[0].reshape(N_EXPERTS, 1, D))
    out = _combine(*runs, gate_t, x1, ys)
    return out.reshape(B, S, D)
```

```python
import functools

import jax
import jax.numpy as jnp
from jax import lax
from jax.experimental import pallas as pl
from jax.experimental.pallas import tpu as pltpu

EPS = 1e-6
D_MODEL = 1024
RET_HEADS = 4
RET_DK = 128
RET_WIDTH = 512
RET_ROPE_THETA = 10000.0
DIFF_HEADS = 4
DIFF_DH = 64
DIFF_DV = 128
DIFF_WIDTH = 512
ROPE_THETA = 500000.0
ROT_DIM = DIFF_DH // 4
D_IN_PROJ = 3584
N_EXPERTS = 32
TOP_K = 4
D_FF = 1024
SWIGLU_LIMIT = 7.0
SWIGLU_ALPHA = 1.702
LAMBDA_INIT = 0.8 - 0.6 * 1.0

LOG2_E = 1.4426950408889634
SCORE_BOUND_SLACK = 1.02
MAX_SAFE_SCORE_BOUND = 60.0
LANES = 128
SUBLANES = 8
VMEM_LIMIT = 56 * 1024 * 1024

COL_RQ, COL_RK, COL_RV, COL_RG, COL_DQ, COL_DK, COL_DV = 0, 4, 8, 12, 16, 20, 24
VGV_RV, VGV_RG, VGV_DV = 0, 4, 8

TM_PROJ = 512
RET_CHUNK = 128
RET_UNROLL = 16
TQ_ATTN = 512
TK_ATTN = 2048
TILE = 512
MOE_BLOCK = 512
RUN_ALIGN = SUBLANES
TILE_ROWS = TOP_K * TILE + N_EXPERTS * RUN_ALIGN


def _params(sem, **kw):
    return pltpu.CompilerParams(dimension_semantics=sem, vmem_limit_bytes=VMEM_LIMIT, **kw)


def _in_proj_kernel(x_ref, nw_ref, w_ref, c2_ref, s2_ref, ra_ref, rp_ref, rn_ref, qw_ref, kw_ref,
                    vgv_ref, rqo_ref, rko_ref, qs_ref, ks_ref):
    ts = x_ref.shape[0]
    x = x_ref[...]
    hn = (x * lax.rsqrt(jnp.mean(x * x, axis=-1, keepdims=True) + EPS) * nw_ref[...]).astype(jnp.bfloat16)

    def proj(col_block):
        c0 = col_block * LANES
        return jnp.dot(hn, w_ref[:, c0:c0 + 4 * LANES], preferred_element_type=jnp.float32)

    rq, rk, dq, dk = proj(COL_RQ), proj(COL_RK), proj(COL_DQ), proj(COL_DK)
    c2 = c2_ref[...]
    s2 = s2_ref[...]
    ra = ra_ref[...]
    rp = rp_ref[...]
    rn = rn_ref[...]
    lane = lax.broadcasted_iota(jnp.int32, (ts, LANES), 1)
    lo = lane < DIFF_DH

    def qk_norm_rot(x, w):
        x2 = x * x
        s_lo = jnp.sum(jnp.where(lo, x2, 0.0), axis=-1, keepdims=True)
        s_hi = jnp.sum(jnp.where(lo, 0.0, x2), axis=-1, keepdims=True)
        ms = jnp.where(lo, s_lo, s_hi) * (1.0 / DIFF_DH)
        xn = x * lax.rsqrt(ms + EPS) * w
        return xn * ra + pltpu.roll(xn, ROT_DIM // 2, 1) * rp + pltpu.roll(xn, LANES - ROT_DIM // 2, 1) * rn

    for h in range(RET_HEADS):
        sl = slice(h * LANES, (h + 1) * LANES)
        q = rq[:, sl]
        k = rk[:, sl]
        rqo_ref[:, sl] = (q * c2 + pltpu.roll(q, RET_DK // 2, 1) * s2).astype(rqo_ref.dtype)
        rko_ref[:, sl] = ((k * c2 + pltpu.roll(k, RET_DK // 2, 1) * s2) * (RET_DK ** -0.5)).astype(rko_ref.dtype)
    for h in range(DIFF_HEADS):
        sl = slice(h * LANES, (h + 1) * LANES)
        q = qk_norm_rot(dq[:, sl], qw_ref[...]) * (DIFF_DH ** -0.5 * LOG2_E)
        k = qk_norm_rot(dk[:, sl], kw_ref[...])
        qs_ref[h, 0] = jnp.where(lo, q, 0.0).astype(qs_ref.dtype)
        qs_ref[h, 1] = jnp.where(lo, 0.0, q).astype(qs_ref.dtype)
        ks_ref[:, sl] = k.astype(ks_ref.dtype)
    for slot, col_block in enumerate((COL_RV, COL_RG, COL_DV)):
        vgv_ref[:, slot * 4 * LANES:(slot + 1) * 4 * LANES] = proj(col_block).astype(vgv_ref.dtype)


def _in_proj(x2, nw, w_bf16, tabs, qw2, kw2, B, S):
    T = B * S
    n_s = S // TM_PROJ
    tok = lambda w: pl.BlockSpec((TM_PROJ, w), lambda i: (i, 0))
    const = lambda s: pl.BlockSpec(s, lambda i: (0, 0))
    tab = pl.BlockSpec((None, TM_PROJ, LANES), lambda i: (i // n_s, i % n_s, 0))
    bf16 = jnp.bfloat16
    return pl.pallas_call(
        _in_proj_kernel,
        out_shape=(jax.ShapeDtypeStruct((T, 3 * 4 * LANES), bf16),
                   jax.ShapeDtypeStruct((T, 4 * LANES), bf16),
                   jax.ShapeDtypeStruct((T, 4 * LANES), bf16),
                   jax.ShapeDtypeStruct((B, DIFF_HEADS, 2, S, LANES), bf16),
                   jax.ShapeDtypeStruct((T, 4 * LANES), bf16)),
        grid=(T // TM_PROJ,),
        in_specs=[tok(D_MODEL), const((1, D_MODEL)), const((D_MODEL, D_IN_PROJ)),
                  tab, tab, tab, tab, tab, const((1, LANES)), const((1, LANES))],
        out_specs=(tok(3 * 4 * LANES), tok(4 * LANES), tok(4 * LANES),
                   pl.BlockSpec((None, DIFF_HEADS, 2, TM_PROJ, LANES), lambda i: (i // n_s, 0, 0, i % n_s, 0)),
                   tok(4 * LANES)),
        compiler_params=_params(("arbitrary",)),
        name="in_proj",
    )(x2, nw, w_bf16, *tabs, qw2, kw2)


def _retention_kernel(ldf_ref, ldb_ref, q_ref, k_ref, v_ref, g_ref, nw_ref, o_ref, sb_ref):
    C = RET_CHUNK
    S = q_ref.shape[0]
    n_chunks = S // C
    h = pl.program_id(1)
    ldf = ldf_ref[h]
    ldb = ldb_ref[h]
    row = lax.broadcasted_iota(jnp.int32, (C, C), 0).astype(jnp.float32)
    colm = lax.broadcasted_iota(jnp.int32, (C, C), 1).astype(jnp.float32)
    dist = row - colm
    decay = jnp.where(dist >= 0, jnp.exp(ldf * jnp.maximum(dist, 0.0)), jnp.exp(ldb * jnp.maximum(-dist, 0.0)))
    idx = lax.broadcasted_iota(jnp.int32, (C, 1), 0).astype(jnp.float32)
    q_dec_f = jnp.exp(ldf * (idx + 1.0))
    k_dec_f = jnp.exp(ldf * (C - 1.0 - idx))
    q_dec_b = jnp.exp(ldb * (C - idx))
    k_dec_b = jnp.exp(ldb * idx)
    chunk_dec_f = jnp.exp(ldf * C)
    chunk_dec_b = jnp.exp(ldb * C)
    f32 = jnp.float32
    bf16 = jnp.bfloat16

    def kv_state(k, v, k_dec):
        kd = (k.astype(f32) * k_dec).astype(bf16)
        return lax.dot_general(kd, v, (((0,), (0,)), ((), ())), preferred_element_type=f32)

    def bwd_step(i, state):
        c = n_chunks - 1 - i
        r0 = pl.multiple_of(c * C, C)
        sb_ref[c] = state
        return state * chunk_dec_b + kv_state(k_ref[pl.ds(r0, C), :], v_ref[pl.ds(r0, C), :], k_dec_b)

    lax.fori_loop(0, n_chunks, bwd_step, jnp.zeros((RET_DK, LANES), f32), unroll=RET_UNROLL)

    def fwd_step(c, state):
        r0 = pl.multiple_of(c * C, C)
        q = q_ref[pl.ds(r0, C), :]
        k = k_ref[pl.ds(r0, C), :]
        v = v_ref[pl.ds(r0, C), :]
        scores = lax.dot_general(q, k, (((1,), (1,)), ((), ())), preferred_element_type=f32) * decay
        y = jnp.dot(scores.astype(bf16), v, preferred_element_type=f32)
        qf = q.astype(f32)
        y += jnp.dot((qf * q_dec_f).astype(bf16), state.astype(bf16), preferred_element_type=f32)
        y += jnp.dot((qf * q_dec_b).astype(bf16), sb_ref[c].astype(bf16), preferred_element_type=f32)
        yn = y * lax.rsqrt(jnp.mean(y * y, axis=-1, keepdims=True) + EPS) * nw_ref[...]
        g = g_ref[pl.ds(r0, C), :].astype(f32)
        o_ref[pl.ds(r0, C), :] = (yn * (g * jax.nn.sigmoid(g))).astype(o_ref.dtype)
        return state * chunk_dec_f + kv_state(k, v, k_dec_f)

    lax.fori_loop(0, n_chunks, fwd_step, jnp.zeros((RET_DK, LANES), f32), unroll=RET_UNROLL)


def _retention(ldf, ldb, rq_r, rk_r, proj, nw, B, S):
    T = B * S
    smem = pl.BlockSpec(memory_space=pltpu.SMEM)
    seq = lambda cb: pl.BlockSpec((S, LANES), lambda b, h: (b, cb + h))
    return pl.pallas_call(
        _retention_kernel,
        out_shape=jax.ShapeDtypeStruct((T, RET_WIDTH), jnp.bfloat16),
        grid=(B, RET_HEADS),
        in_specs=[smem, smem, seq(0), seq(0), seq(VGV_RV), seq(VGV_RG),
                  pl.BlockSpec((1, LANES), lambda b, h: (0, h))],
        out_specs=seq(0),
        scratch_shapes=[pltpu.VMEM((S // RET_CHUNK, RET_DK, LANES), jnp.float32)],
        compiler_params=_params(("arbitrary", "arbitrary")),
        name="retention",
    )(ldf, ldb, rq_r, rk_r, proj, proj, nw)


def _diff_attn_kernel(online_max, bound_ref, q_ref, k_ref, v_ref, lam_ref, nw_ref, o_ref, m_ref, l_ref, acc_ref):
    tq = q_ref.shape[1]
    S = k_ref.shape[0]
    f32 = jnp.float32
    q = q_ref[...].reshape(2 * tq, LANES)
    if online_max:
        m_ref[...] = jnp.full(m_ref.shape, -jnp.inf, f32)
    l_ref[...] = jnp.zeros(l_ref.shape, f32)
    acc_ref[...] = jnp.zeros(acc_ref.shape, f32)
    n_tiles = TK_ATTN // LANES

    def kv_step(j, carry):
        r0 = pl.multiple_of(j * TK_ATTN, TK_ATTN)
        k = k_ref[pl.ds(r0, TK_ATTN), :]
        v = v_ref[pl.ds(r0, TK_ATTN), :]
        s = lax.dot_general(q, k, (((1,), (1,)), ((), ())), preferred_element_type=f32)
        tiles = [s[:, c * LANES:(c + 1) * LANES] for c in range(n_tiles)]
        if online_max:
            part = tiles[0]
            for t in tiles[1:]:
                part = jnp.maximum(part, t)
            m_prev = m_ref[...]
            shift = jnp.maximum(m_prev, jnp.max(part, axis=-1, keepdims=True))
            alpha = jnp.exp2(m_prev - shift)
            m_ref[...] = shift
        else:
            shift = bound_ref[0]
        probs = [jnp.exp2(t - shift) for t in tiles]
        psum = probs[0]
        for p in probs[1:]:
            psum = psum + p
        pv = jnp.dot(jnp.concatenate([p.astype(jnp.bfloat16) for p in probs], axis=1), v,
                     preferred_element_type=f32)
        if online_max:
            l_ref[...] = alpha * l_ref[...] + psum
            acc_ref[...] = alpha * acc_ref[...] + pv
        else:
            l_ref[...] = l_ref[...] + psum
            acc_ref[...] = acc_ref[...] + pv
        return carry

    lax.fori_loop(0, S // TK_ATTN, kv_step, 0)
    o = acc_ref[...] / jnp.sum(l_ref[...], axis=-1, keepdims=True)
    d = o[:tq] - lam_ref[...] * o[tq:]
    dn = d * lax.rsqrt(jnp.mean(d * d, axis=-1, keepdims=True) + EPS) * nw_ref[...]
    o_ref[...] = (dn * (1.0 - LAMBDA_INIT)).astype(o_ref.dtype)


def _diff_attn(online_max, bound, qs, ks, proj, lam, nw, B, S):
    T = B * S
    n_q = S // TQ_ATTN
    one = pl.BlockSpec((1, LANES), lambda b, h, i, bd: (0, 0))
    grid_spec = pltpu.PrefetchScalarGridSpec(
        num_scalar_prefetch=1,
        grid=(B, DIFF_HEADS, n_q),
        in_specs=[pl.BlockSpec((None, None, 2, TQ_ATTN, LANES), lambda b, h, i, bd: (b, h, 0, i, 0)),
                  pl.BlockSpec((S, LANES), lambda b, h, i, bd: (b, h)),
                  pl.BlockSpec((S, LANES), lambda b, h, i, bd: (b, VGV_DV + h)),
                  one, one],
        out_specs=pl.BlockSpec((TQ_ATTN, LANES), lambda b, h, i, bd: (b * n_q + i, h)),
        scratch_shapes=[pltpu.VMEM((2 * TQ_ATTN, LANES), jnp.float32)] * 3,
    )
    return pl.pallas_call(
        functools.partial(_diff_attn_kernel, online_max),
        out_shape=jax.ShapeDtypeStruct((T, DIFF_WIDTH), jnp.bfloat16),
        grid_spec=grid_spec,
        compiler_params=_params(("arbitrary", "arbitrary", "arbitrary")),
        name="diff_attn_online" if online_max else "diff_attn",
    )(bound, qs, ks, proj, lam, nw)


def _out_router_kernel(x_ref, yr_ref, yd_ref, wo_ref, n2_ref, wrt_ref, br_ref,
                       x1_ref, h2_ref, pos_ref, gate_t_ref, len_ref, off_ref, tot_ref):
    tm = x_ref.shape[0]
    f32 = jnp.float32
    bf16 = jnp.bfloat16

    @pl.when(pl.program_id(0) == 0)
    def _():
        tot_ref[...] = jnp.zeros(tot_ref.shape, f32)

    att = jnp.dot(yr_ref[...], wo_ref[:RET_WIDTH, :], preferred_element_type=f32)
    att += jnp.dot(yd_ref[...], wo_ref[RET_WIDTH:, :], preferred_element_type=f32)
    x1 = x_ref[...] + att
    x1_ref[...] = x1
    h2 = x1 * lax.rsqrt(jnp.mean(x1 * x1, axis=-1, keepdims=True) + EPS) * n2_ref[...]
    h2_ref[...] = h2.astype(h2_ref.dtype)
    logits = lax.dot_general(wrt_ref[...], h2, (((1,), (1,)), ((), ())),
                             precision=lax.Precision.HIGHEST, preferred_element_type=f32) + br_ref[...]
    e_iota = lax.broadcasted_iota(jnp.int32, (N_EXPERTS, tm), 0)
    work = logits
    vals, hots = [], []
    for _ in range(TOP_K):
        mx = jnp.max(work, axis=0, keepdims=True)
        ix = jnp.min(jnp.where(work == mx, e_iota, N_EXPERTS), axis=0, keepdims=True)
        hot = e_iota == ix
        vals.append(mx)
        hots.append(hot)
        work = jnp.where(hot, -jnp.inf, work)
    exps = [jnp.exp(v - vals[0]) for v in vals]
    denom = exps[0] + exps[1] + exps[2] + exps[3]
    gates = [e / denom for e in exps]
    sel = jnp.zeros((N_EXPERTS, tm), f32)
    for hot in hots:
        sel = jnp.where(hot, 1.0, sel)
    t_row = lax.broadcasted_iota(jnp.int32, (tm, tm), 0)
    t_col = lax.broadcasted_iota(jnp.int32, (tm, tm), 1)
    upper = jnp.where(t_row < t_col, 1.0, 0.0).astype(bf16)
    rank = jnp.dot(sel.astype(bf16), upper, preferred_element_type=f32)
    cnt = jnp.sum(sel, axis=1, keepdims=True)
    run_units = jnp.floor((cnt + (RUN_ALIGN - 1.0)) * (1.0 / RUN_ALIGN))
    run_len = jnp.broadcast_to(run_units * RUN_ALIGN, (N_EXPERTS, LANES))
    e_row = lax.broadcasted_iota(jnp.int32, (N_EXPERTS, N_EXPERTS), 0)
    e_col = lax.broadcasted_iota(jnp.int32, (N_EXPERTS, N_EXPERTS), 1)
    lower = jnp.where(e_col < e_row, 1.0, 0.0).astype(bf16)
    run_start = jnp.dot(lower, jnp.broadcast_to(run_units, (N_EXPERTS, LANES)).astype(bf16),
                        preferred_element_type=f32) * RUN_ALIGN
    pos_full = rank + run_start[:, 0:1]
    pos = [jnp.sum(jnp.where(hot, pos_full, 0.0), axis=0, keepdims=True) for hot in hots]
    for k in range(TOP_K):
        pos_ref[k:k + 1, :] = pos[k].astype(jnp.int32)
    rows = jnp.concatenate(gates + pos + [jnp.zeros((LANES - 2 * TOP_K, tm), f32)], axis=0)
    gate_t_ref[...] = rows.T
    len_ref[0] = run_len
    off_ref[0] = tot_ref[...]
    tot_ref[...] = tot_ref[...] + run_len


def _out_router(x2, y_ret, y_diff, wo_bf16, n2w, wrt, br):
    T = x2.shape[0]
    n_tiles = T // TILE
    tok = lambda w: pl.BlockSpec((TILE, w), lambda i: (i, 0))
    const = lambda s: pl.BlockSpec(s, lambda i: (0, 0))
    per_tile = pl.BlockSpec((1, N_EXPERTS, LANES), lambda i: (i, 0, 0))
    return pl.pallas_call(
        _out_router_kernel,
        out_shape=(jax.ShapeDtypeStruct((T, D_MODEL), jnp.float32),
                   jax.ShapeDtypeStruct((T, D_MODEL), jnp.bfloat16),
                   jax.ShapeDtypeStruct((TOP_K, T), jnp.int32),
                   jax.ShapeDtypeStruct((T, LANES), jnp.float32),
                   jax.ShapeDtypeStruct((n_tiles, N_EXPERTS, LANES), jnp.float32),
                   jax.ShapeDtypeStruct((n_tiles, N_EXPERTS, LANES), jnp.float32),
                   jax.ShapeDtypeStruct((N_EXPERTS, LANES), jnp.float32)),
        grid=(n_tiles,),
        in_specs=[tok(D_MODEL), tok(RET_WIDTH), tok(DIFF_WIDTH), const((D_MODEL, D_MODEL)),
                  const((1, D_MODEL)), const((N_EXPERTS, D_MODEL)), const((N_EXPERTS, 1))],
        out_specs=(tok(D_MODEL), tok(D_MODEL), pl.BlockSpec((TOP_K, TILE), lambda i: (0, i)), tok(LANES),
                   per_tile, per_tile, const((N_EXPERTS, LANES))),
        compiler_params=_params(("arbitrary",)),
        name="out_router",
    )(x2, y_ret, y_diff, wo_bf16, n2w, wrt, br)


def _run_copies(src_ref, len_ref, dst_ref, tile, make_copy):
    for e in range(N_EXPERTS):
        n = pl.multiple_of(len_ref[tile * N_EXPERTS + e], RUN_ALIGN)
        s = pl.multiple_of(src_ref[tile * N_EXPERTS + e], RUN_ALIGN)
        d = pl.multiple_of(dst_ref[tile * N_EXPERTS + e], RUN_ALIGN)

        @pl.when(n > 0)
        def _():
            make_copy(s, d, n).start()


def _dispatch_kernel(src_ref, len_ref, dst_ref, rows_ref, zlo_ref, zlen_ref, nu_ref,
                     pos_ref, h2_ref, xs_hbm, xbuf_ref, zero_ref, sems, zero_sem):
    i = pl.program_id(0)
    n_tiles = pl.num_programs(0)
    cur = i % 2
    n_rows, tm = xbuf_ref.shape[1], h2_ref.shape[0]

    @pl.when(i == 0)
    def _():
        zero_ref[...] = jnp.zeros(zero_ref.shape, zero_ref.dtype)

        def pad_copy(e):
            n = pl.multiple_of(zlen_ref[e], RUN_ALIGN)
            lo = pl.multiple_of(zlo_ref[e], RUN_ALIGN)
            return pltpu.make_async_copy(zero_ref.at[pl.ds(0, n)], xs_hbm.at[pl.ds(lo, n)], zero_sem)

        def tail_copy(j):
            return pltpu.make_async_copy(zero_ref, xs_hbm.at[pl.ds(j * MOE_BLOCK, MOE_BLOCK)], zero_sem)

        def guarded(copy, op):
            def body(e, c):
                @pl.when(zlen_ref[e] > 0)
                def _():
                    op(copy(e))
                return c
            return body

        lax.fori_loop(0, N_EXPERTS, guarded(pad_copy, lambda cp: cp.start()), 0)
        lax.fori_loop(0, N_EXPERTS, guarded(pad_copy, lambda cp: cp.wait()), 0)
        n_blocks = xs_hbm.shape[0] // MOE_BLOCK
        lax.fori_loop(nu_ref[0], n_blocks, lambda j, c: (tail_copy(j).start(), c)[1], 0)
        lax.fori_loop(nu_ref[0], n_blocks, lambda j, c: (tail_copy(j).wait(), c)[1], 0)

    p_iota = lax.broadcasted_iota(jnp.int32, (n_rows, tm), 0)
    onehot = jnp.zeros((n_rows, tm), jnp.float32)
    for k in range(TOP_K):
        onehot = jnp.where(p_iota == pos_ref[k:k + 1, :], 1.0, onehot)
    xbuf_ref[cur] = jnp.dot(onehot.astype(jnp.bfloat16), h2_ref[...], preferred_element_type=jnp.float32)

    _run_copies(src_ref, len_ref, dst_ref, i,
                lambda s, d, n: pltpu.make_async_copy(xbuf_ref.at[cur, pl.ds(s, n)], xs_hbm.at[pl.ds(d, n)],
                                                      sems.at[cur]))

    def wait_tile(tile, slot):
        rows = pl.multiple_of(rows_ref[tile], RUN_ALIGN)
        pltpu.make_async_copy(xbuf_ref.at[slot, pl.ds(0, rows)], xs_hbm.at[pl.ds(0, rows)], sems.at[slot]).wait()

    @pl.when(i > 0)
    def _():
        wait_tile(i - 1, 1 - cur)

    @pl.when(i == n_tiles - 1)
    def _():
        wait_tile(i, cur)


def _dispatch(run_src, run_len, run_dst, tile_rows, zero_lo, zero_len, n_used, pos, h2, P):
    T = h2.shape[0]
    n_pre = 7
    grid_spec = pltpu.PrefetchScalarGridSpec(
        num_scalar_prefetch=n_pre,
        grid=(T // TILE,),
        in_specs=[pl.BlockSpec((TOP_K, TILE), lambda i, *_: (0, i)),
                  pl.BlockSpec((TILE, D_MODEL), lambda i, *_: (i, 0))],
        out_specs=pl.BlockSpec(memory_space=pl.ANY),
        scratch_shapes=[pltpu.VMEM((2, TILE_ROWS, D_MODEL), jnp.float32),
                        pltpu.VMEM((MOE_BLOCK, D_MODEL), jnp.float32),
                        pltpu.SemaphoreType.DMA((2,)),
                        pltpu.SemaphoreType.DMA(())],
    )
    return pl.pallas_call(
        _dispatch_kernel,
        out_shape=jax.ShapeDtypeStruct((P, D_MODEL), jnp.float32),
        grid_spec=grid_spec,
        compiler_params=_params(("arbitrary",), has_side_effects=True),
        name="dispatch",
    )(run_src, run_len, run_dst, tile_rows, zero_lo, zero_len, n_used, pos, h2)


def _experts_kernel(base_ref, nblk_ref, w1_ref, b1_ref, w2_ref, b2_ref, xs_hbm, ys_hbm,
                    w1b_ref, w2b_ref, xbuf_ref, ybuf_ref, in_sems, out_sems, busy_ref):
    e = pl.program_id(0)
    n = nblk_ref[e]

    def rows(expert, j):
        return pl.ds(pl.multiple_of(base_ref[expert] + j * MOE_BLOCK, MOE_BLOCK), MOE_BLOCK)

    def in_copy(expert, j, slot):
        return pltpu.make_async_copy(xs_hbm.at[rows(expert, j)], xbuf_ref.at[slot], in_sems.at[slot])

    def out_copy(j, slot):
        return pltpu.make_async_copy(ybuf_ref.at[slot], ys_hbm.at[rows(e, j)], out_sems.at[slot])

    def wait_out(slot):
        @pl.when(busy_ref[slot] == 1)
        def _():
            out_copy(0, slot).wait()
            busy_ref[slot] = 0

    @pl.when(e == 0)
    def _():
        busy_ref[0] = 0
        busy_ref[1] = 0

        @pl.when(n > 0)
        def _():
            in_copy(0, 0, 0).start()

    @pl.when(n > 0)
    def _():
        w1b_ref[...] = w1_ref[...].astype(jnp.bfloat16)
        w2b_ref[...] = w2_ref[...].astype(jnp.bfloat16)

        def block(j, carry):
            slot = j % 2

            @pl.when(j + 1 < n)
            def _():
                in_copy(e, j + 1, 1 - slot).start()

            in_copy(e, j, slot).wait()
            wait_out(slot)
            x = xbuf_ref[slot].astype(jnp.bfloat16)
            u = jnp.dot(x, w1b_ref[...], preferred_element_type=jnp.float32) + b1_ref[...]
            glu = jnp.minimum(u[:, :D_FF], SWIGLU_LIMIT)
            lin = jnp.clip(u[:, D_FF:], -SWIGLU_LIMIT, SWIGLU_LIMIT)
            act = glu * jax.nn.sigmoid(SWIGLU_ALPHA * glu) * (lin + 1.0)
            ybuf_ref[slot] = jnp.dot(act.astype(jnp.bfloat16), w2b_ref[...],
                                     preferred_element_type=jnp.float32) + b2_ref[...]
            out_copy(j, slot).start()
            busy_ref[slot] = 1
            return carry

        lax.fori_loop(0, n, block, 0)

    e_next = jnp.minimum(e + 1, N_EXPERTS - 1)

    @pl.when((e + 1 < N_EXPERTS) & (nblk_ref[e_next] > 0))
    def _():
        in_copy(e_next, 0, 0).start()

    @pl.when(e == N_EXPERTS - 1)
    def _():
        wait_out(0)
        wait_out(1)
        ybuf_ref[0] = jnp.zeros(ybuf_ref.shape[1:], ybuf_ref.dtype)
        first_unused = (base_ref[e] + n * MOE_BLOCK) // MOE_BLOCK
        n_blocks = ys_hbm.shape[0] // MOE_BLOCK

        def tail_copy(j):
            return pltpu.make_async_copy(ybuf_ref.at[0], ys_hbm.at[pl.ds(j * MOE_BLOCK, MOE_BLOCK)],
                                         out_sems.at[0])

        lax.fori_loop(first_unused, n_blocks, lambda j, c: (tail_copy(j).start(), c)[1], 0)
        lax.fori_loop(first_unused, n_blocks, lambda j, c: (tail_copy(j).wait(), c)[1], 0)


def _experts(base, n_blk, xs, w1, b1, w2, b2):
    P = xs.shape[0]
    expert = lambda e, bs, nb: (e, 0, 0)
    grid_spec = pltpu.PrefetchScalarGridSpec(
        num_scalar_prefetch=2,
        grid=(N_EXPERTS,),
        in_specs=[pl.BlockSpec((None, D_MODEL, 2 * D_FF), expert),
                  pl.BlockSpec((None, 1, 2 * D_FF), expert),
                  pl.BlockSpec((None, D_FF, D_MODEL), expert),
                  pl.BlockSpec((None, 1, D_MODEL), expert),
                  pl.BlockSpec(memory_space=pl.ANY)],
        out_specs=pl.BlockSpec(memory_space=pl.ANY),
        scratch_shapes=[pltpu.VMEM((D_MODEL, 2 * D_FF), jnp.bfloat16),
                        pltpu.VMEM((D_FF, D_MODEL), jnp.bfloat16),
                        pltpu.VMEM((2, MOE_BLOCK, D_MODEL), jnp.float32),
                        pltpu.VMEM((2, MOE_BLOCK, D_MODEL), jnp.float32),
                        pltpu.SemaphoreType.DMA((2,)),
                        pltpu.SemaphoreType.DMA((2,)),
                        pltpu.SMEM((2,), jnp.int32)],
    )
    return pl.pallas_call(
        _experts_kernel,
        out_shape=jax.ShapeDtypeStruct((P, D_MODEL), jnp.float32),
        grid_spec=grid_spec,
        compiler_params=_params(("arbitrary",)),
        name="experts",
    )(base, n_blk, w1, b1, w2, b2, xs)


def _combine_kernel(src_ref, len_ref, dst_ref, rows_ref, gate_t_ref, x1_ref, ys_hbm, o_ref, ybuf_ref, sems):
    i = pl.program_id(0)
    n_tiles = pl.num_programs(0)
    cur = i % 2
    n_rows, tm = ybuf_ref.shape[1], x1_ref.shape[0]

    def fetch(tile, slot):
        _run_copies(src_ref, len_ref, dst_ref, tile,
                    lambda s, d, n: pltpu.make_async_copy(ys_hbm.at[pl.ds(d, n)], ybuf_ref.at[slot, pl.ds(s, n)],
                                                          sems.at[slot]))

    @pl.when(i == 0)
    def _():
        ybuf_ref[...] = jnp.zeros(ybuf_ref.shape, ybuf_ref.dtype)
        fetch(0, 0)

    @pl.when(i + 1 < n_tiles)
    def _():
        fetch(i + 1, 1 - cur)

    rows = pl.multiple_of(rows_ref[i], RUN_ALIGN)
    pltpu.make_async_copy(ys_hbm.at[pl.ds(0, rows)], ybuf_ref.at[cur, pl.ds(0, rows)], sems.at[cur]).wait()

    g = gate_t_ref[...]
    p_iota = lax.broadcasted_iota(jnp.int32, (tm, n_rows), 1)
    weights = jnp.zeros((tm, n_rows), jnp.float32)
    for k in range(TOP_K):
        pos_k = g[:, TOP_K + k:TOP_K + k + 1].astype(jnp.int32)
        weights = jnp.where(p_iota == pos_k, g[:, k:k + 1], weights)
    o_ref[...] = x1_ref[...] + jnp.dot(weights.astype(jnp.bfloat16), ybuf_ref[cur].astype(jnp.bfloat16),
                                       preferred_element_type=jnp.float32)


def _combine(run_src, run_len, run_dst, tile_rows, gate_t, x1, ys):
    T = x1.shape[0]
    tok = lambda w: pl.BlockSpec((TILE, w), lambda i, *_: (i, 0))
    grid_spec = pltpu.PrefetchScalarGridSpec(
        num_scalar_prefetch=4,
        grid=(T // TILE,),
        in_specs=[tok(LANES), tok(D_MODEL), pl.BlockSpec(memory_space=pl.ANY)],
        out_specs=tok(D_MODEL),
        scratch_shapes=[pltpu.VMEM((2, TILE_ROWS, D_MODEL), jnp.float32),
                        pltpu.SemaphoreType.DMA((2,))],
    )
    return pl.pallas_call(
        _combine_kernel,
        out_shape=jax.ShapeDtypeStruct((T, D_MODEL), jnp.float32),
        grid_spec=grid_spec,
        compiler_params=_params(("arbitrary",)),
        name="combine",
    )(run_src, run_len, run_dst, tile_rows, gate_t, x1, ys)


def _rotary_tables(positions):
    pos = positions.astype(jnp.float32)[..., None]
    lane = jnp.arange(LANES)
    half_r = RET_DK // 2
    inv_r = RET_ROPE_THETA ** (-jnp.linspace(0.0, 1.0, half_r, dtype=jnp.float32))
    ang = pos * inv_r[lane % half_r]
    c2 = jnp.cos(ang)
    s2 = jnp.sin(ang) * jnp.where(lane < half_r, -1.0, 1.0)
    half_d = ROT_DIM // 2
    inv_d = ROPE_THETA ** (-jnp.arange(0, ROT_DIM, 2, dtype=jnp.float32) / ROT_DIM)
    sub = lane % DIFF_DH
    ang_d = pos * inv_d[sub % half_d]
    cd, sd = jnp.cos(ang_d), jnp.sin(ang_d)
    ra = jnp.where(sub < ROT_DIM, cd, 1.0)
    rp = jnp.where((sub >= half_d) & (sub < ROT_DIM), sd, 0.0)
    rn = jnp.where(sub < half_d, -sd, 0.0)
    return c2, s2, ra, rp, rn


def kernel(x, positions, norm1_w, w_in, ret_log_decay_fwd, ret_log_decay_bwd, ret_norm_w, q_norm_w, k_norm_w, lambda_q1, lambda_k1, lambda_q2, lambda_k2, diff_norm_w, w_out, norm2_w, w_router, b_router, w1, b1, w2, b2):
    B, S, D = x.shape
    T = B * S
    f32 = jnp.float32
    bf16 = jnp.bfloat16
    x2 = x.reshape(T, D)

    dup = lambda w: jnp.concatenate([w, w]).reshape(1, LANES).astype(f32)
    proj, rq_r, rk_r, qs, ks = _in_proj(x2, norm1_w[0].reshape(1, D), w_in[0].astype(bf16),
                                        _rotary_tables(positions), dup(q_norm_w[0]), dup(k_norm_w[0]), B, S)

    y_ret = _retention(ret_log_decay_fwd[0].astype(f32), ret_log_decay_bwd[0].astype(f32),
                       rq_r, rk_r, proj, ret_norm_w[0].reshape(1, RET_WIDTH).astype(f32), B, S)

    lam = (jnp.exp(jnp.sum(lambda_q1[0].astype(f32) * lambda_k1[0].astype(f32)))
           - jnp.exp(jnp.sum(lambda_q2[0].astype(f32) * lambda_k2[0].astype(f32))) + LAMBDA_INIT)
    lam_row = jnp.full((1, LANES), lam, f32)
    bound = (SCORE_BOUND_SLACK * DIFF_DH ** 0.5 * LOG2_E
             * jnp.max(jnp.abs(q_norm_w[0].astype(f32))) * jnp.max(jnp.abs(k_norm_w[0].astype(f32)))).reshape(1)
    attn_args = (bound, qs, ks, proj, lam_row, diff_norm_w[0].reshape(1, DIFF_DV).astype(f32), B, S)
    y_diff = lax.cond(bound[0] <= MAX_SAFE_SCORE_BOUND,
                      lambda: _diff_attn(False, *attn_args), lambda: _diff_attn(True, *attn_args))

    x1, h2, pos, gate_t, len_t, off_t, tot_t = _out_router(
        x2, y_ret, y_diff, w_out[0].astype(bf16), norm2_w[0].reshape(1, D),
        w_router[0].T.astype(f32), b_router[0].reshape(N_EXPERTS, 1).astype(f32))

    n_tiles = T // TILE
    run_len = len_t[:, :, 0].astype(jnp.int32)
    total = tot_t[:, 0].astype(jnp.int32)
    padded = ((total + MOE_BLOCK - 1) // MOE_BLOCK) * MOE_BLOCK
    pad_end = jnp.cumsum(padded)
    pad_start = pad_end - padded
    run_dst = pad_start[None, :] + off_t[:, :, 0].astype(jnp.int32)
    run_src = jnp.cumsum(run_len, axis=1) - run_len
    tile_rows = jnp.sum(run_len, axis=1)
    P = T * TOP_K + n_tiles * N_EXPERTS * RUN_ALIGN + N_EXPERTS * MOE_BLOCK
    n_used = (pad_end[-1:] // MOE_BLOCK).astype(jnp.int32)
    runs = (run_src.reshape(-1), run_len.reshape(-1), run_dst.reshape(-1), tile_rows)

    xs = _dispatch(*runs, pad_start + total, padded - total, n_used, pos, h2, P)
    ys = _experts(pad_start, padded // MOE_BLOCK, xs, w1[0], b1[0].reshape(N_EXPERTS, 1, 2 * D_FF),
                  w2[0], b2[0].reshape(N_EXPERTS, 1, D))
    out = _combine(*runs, gate_t, x1, ys)
    return out.reshape(B, S, D)
```

```python
import functools

import jax
import jax.numpy as jnp
from jax import lax
from jax.experimental import pallas as pl
from jax.experimental.pallas import tpu as pltpu

EPS = 1e-6
D_MODEL = 1024
RET_HEADS = 4
RET_DK = 128
RET_WIDTH = 512
RET_ROPE_THETA = 10000.0
DIFF_HEADS = 4
DIFF_DH = 64
DIFF_DV = 128
DIFF_WIDTH = 512
ROPE_THETA = 500000.0
ROT_DIM = DIFF_DH // 4
D_IN_PROJ = 3584
N_EXPERTS = 32
TOP_K = 4
D_FF = 1024
SWIGLU_LIMIT = 7.0
SWIGLU_ALPHA = 1.702
LAMBDA_INIT = 0.8 - 0.6 * 1.0

LOG2_E = 1.4426950408889634
SCORE_BOUND_SLACK = 1.02
MAX_SAFE_SCORE_BOUND = 60.0
LANES = 128
SUBLANES = 8
VMEM_LIMIT = 56 * 1024 * 1024

COL_RQ, COL_RK, COL_RV, COL_RG, COL_DQ, COL_DK, COL_DV = 0, 4, 8, 12, 16, 20, 24
VGV_RV, VGV_RG, VGV_DV = 0, 4, 8

TM_PROJ = 512
RET_CHUNK = 128
RET_UNROLL = 16
TQ_ATTN = 512
TK_ATTN = 2048
TILE = 512
MOE_BLOCK = 512
RUN_ALIGN = SUBLANES
TILE_ROWS = TOP_K * TILE + N_EXPERTS * RUN_ALIGN
DISPATCH_BUFS = 3


def _params(sem, **kw):
    return pltpu.CompilerParams(dimension_semantics=sem, vmem_limit_bytes=VMEM_LIMIT, **kw)


def _in_proj_kernel(x_ref, nw_ref, w_ref, c2_ref, s2_ref, ra_ref, rp_ref, rn_ref, qw_ref, kw_ref,
                    vgv_ref, rqo_ref, rko_ref, qs_ref, ks_ref):
    ts = x_ref.shape[0]
    x = x_ref[...]
    hn = (x * lax.rsqrt(jnp.mean(x * x, axis=-1, keepdims=True) + EPS) * nw_ref[...]).astype(jnp.bfloat16)

    def proj(col_block):
        c0 = col_block * LANES
        return jnp.dot(hn, w_ref[:, c0:c0 + 4 * LANES], preferred_element_type=jnp.float32)

    rq, rk, dq, dk = proj(COL_RQ), proj(COL_RK), proj(COL_DQ), proj(COL_DK)
    c2 = c2_ref[...]
    s2 = s2_ref[...]
    ra = ra_ref[...]
    rp = rp_ref[...]
    rn = rn_ref[...]
    lane = lax.broadcasted_iota(jnp.int32, (ts, LANES), 1)
    lo = lane < DIFF_DH

    def qk_norm_rot(x, w):
        x2 = x * x
        s_lo = jnp.sum(jnp.where(lo, x2, 0.0), axis=-1, keepdims=True)
        s_hi = jnp.sum(jnp.where(lo, 0.0, x2), axis=-1, keepdims=True)
        ms = jnp.where(lo, s_lo, s_hi) * (1.0 / DIFF_DH)
        xn = x * lax.rsqrt(ms + EPS) * w
        return xn * ra + pltpu.roll(xn, ROT_DIM // 2, 1) * rp + pltpu.roll(xn, LANES - ROT_DIM // 2, 1) * rn

    for h in range(RET_HEADS):
        sl = slice(h * LANES, (h + 1) * LANES)
        q = rq[:, sl]
        k = rk[:, sl]
        rqo_ref[:, sl] = (q * c2 + pltpu.roll(q, RET_DK // 2, 1) * s2).astype(rqo_ref.dtype)
        rko_ref[:, sl] = ((k * c2 + pltpu.roll(k, RET_DK // 2, 1) * s2) * (RET_DK ** -0.5)).astype(rko_ref.dtype)
    for h in range(DIFF_HEADS):
        sl = slice(h * LANES, (h + 1) * LANES)
        q = qk_norm_rot(dq[:, sl], qw_ref[...]) * (DIFF_DH ** -0.5 * LOG2_E)
        k = qk_norm_rot(dk[:, sl], kw_ref[...])
        qs_ref[h, 0] = jnp.where(lo, q, 0.0).astype(qs_ref.dtype)
        qs_ref[h, 1] = jnp.where(lo, 0.0, q).astype(qs_ref.dtype)
        ks_ref[:, sl] = k.astype(ks_ref.dtype)
    for slot, col_block in enumerate((COL_RV, COL_RG, COL_DV)):
        vgv_ref[:, slot * 4 * LANES:(slot + 1) * 4 * LANES] = proj(col_block).astype(vgv_ref.dtype)


def _in_proj(x2, nw, w_bf16, tabs, qw2, kw2, B, S):
    T = B * S
    n_s = S // TM_PROJ
    tok = lambda w: pl.BlockSpec((TM_PROJ, w), lambda i: (i, 0))
    const = lambda s: pl.BlockSpec(s, lambda i: (0, 0))
    tab = pl.BlockSpec((None, TM_PROJ, LANES), lambda i: (i // n_s, i % n_s, 0))
    bf16 = jnp.bfloat16
    return pl.pallas_call(
        _in_proj_kernel,
        out_shape=(jax.ShapeDtypeStruct((T, 3 * 4 * LANES), bf16),
                   jax.ShapeDtypeStruct((T, 4 * LANES), bf16),
                   jax.ShapeDtypeStruct((T, 4 * LANES), bf16),
                   jax.ShapeDtypeStruct((B, DIFF_HEADS, 2, S, LANES), bf16),
                   jax.ShapeDtypeStruct((T, 4 * LANES), bf16)),
        grid=(T // TM_PROJ,),
        in_specs=[tok(D_MODEL), const((1, D_MODEL)), const((D_MODEL, D_IN_PROJ)),
                  tab, tab, tab, tab, tab, const((1, LANES)), const((1, LANES))],
        out_specs=(tok(3 * 4 * LANES), tok(4 * LANES), tok(4 * LANES),
                   pl.BlockSpec((None, DIFF_HEADS, 2, TM_PROJ, LANES), lambda i: (i // n_s, 0, 0, i % n_s, 0)),
                   tok(4 * LANES)),
        compiler_params=_params(("arbitrary",)),
        name="in_proj",
    )(x2, nw, w_bf16, *tabs, qw2, kw2)


def _retention_kernel(ldf_ref, ldb_ref, q_ref, k_ref, v_ref, g_ref, nw_ref, o_ref, sb_ref):
    C = RET_CHUNK
    S = q_ref.shape[0]
    n_chunks = S // C
    h = pl.program_id(1)
    ldf = ldf_ref[h]
    ldb = ldb_ref[h]
    row = lax.broadcasted_iota(jnp.int32, (C, C), 0).astype(jnp.float32)
    colm = lax.broadcasted_iota(jnp.int32, (C, C), 1).astype(jnp.float32)
    dist = row - colm
    decay = jnp.where(dist >= 0, jnp.exp(ldf * jnp.maximum(dist, 0.0)), jnp.exp(ldb * jnp.maximum(-dist, 0.0)))
    idx = lax.broadcasted_iota(jnp.int32, (C, 1), 0).astype(jnp.float32)
    q_dec_f = jnp.exp(ldf * (idx + 1.0))
    k_dec_f = jnp.exp(ldf * (C - 1.0 - idx))
    q_dec_b = jnp.exp(ldb * (C - idx))
    k_dec_b = jnp.exp(ldb * idx)
    chunk_dec_f = jnp.exp(ldf * C)
    chunk_dec_b = jnp.exp(ldb * C)
    f32 = jnp.float32
    bf16 = jnp.bfloat16

    def kv_state(k, v, k_dec):
        kd = (k.astype(f32) * k_dec).astype(bf16)
        return lax.dot_general(kd, v, (((0,), (0,)), ((), ())), preferred_element_type=f32)

    def bwd_step(i, state):
        c = n_chunks - 1 - i
        r0 = pl.multiple_of(c * C, C)
        sb_ref[c] = state
        return state * chunk_dec_b + kv_state(k_ref[pl.ds(r0, C), :], v_ref[pl.ds(r0, C), :], k_dec_b)

    lax.fori_loop(0, n_chunks, bwd_step, jnp.zeros((RET_DK, LANES), f32), unroll=RET_UNROLL)

    def fwd_step(c, state):
        r0 = pl.multiple_of(c * C, C)
        q = q_ref[pl.ds(r0, C), :]
        k = k_ref[pl.ds(r0, C), :]
        v = v_ref[pl.ds(r0, C), :]
        scores = lax.dot_general(q, k, (((1,), (1,)), ((), ())), preferred_element_type=f32) * decay
        y = jnp.dot(scores.astype(bf16), v, preferred_element_type=f32)
        qf = q.astype(f32)
        y += jnp.dot((qf * q_dec_f).astype(bf16), state.astype(bf16), preferred_element_type=f32)
        y += jnp.dot((qf * q_dec_b).astype(bf16), sb_ref[c].astype(bf16), preferred_element_type=f32)
        yn = y * lax.rsqrt(jnp.mean(y * y, axis=-1, keepdims=True) + EPS) * nw_ref[...]
        g = g_ref[pl.ds(r0, C), :].astype(f32)
        o_ref[pl.ds(r0, C), :] = (yn * (g * jax.nn.sigmoid(g))).astype(o_ref.dtype)
        return state * chunk_dec_f + kv_state(k, v, k_dec_f)

    lax.fori_loop(0, n_chunks, fwd_step, jnp.zeros((RET_DK, LANES), f32), unroll=RET_UNROLL)


def _retention(ldf, ldb, rq_r, rk_r, proj, nw, B, S):
    T = B * S
    smem = pl.BlockSpec(memory_space=pltpu.SMEM)
    seq = lambda cb: pl.BlockSpec((S, LANES), lambda b, h: (b, cb + h))
    return pl.pallas_call(
        _retention_kernel,
        out_shape=jax.ShapeDtypeStruct((T, RET_WIDTH), jnp.bfloat16),
        grid=(B, RET_HEADS),
        in_specs=[smem, smem, seq(0), seq(0), seq(VGV_RV), seq(VGV_RG),
                  pl.BlockSpec((1, LANES), lambda b, h: (0, h))],
        out_specs=seq(0),
        scratch_shapes=[pltpu.VMEM((S // RET_CHUNK, RET_DK, LANES), jnp.float32)],
        compiler_params=_params(("arbitrary", "arbitrary")),
        name="retention",
    )(ldf, ldb, rq_r, rk_r, proj, proj, nw)


def _diff_attn_kernel(online_max, bound_ref, q_ref, k_ref, v_ref, lam_ref, nw_ref, o_ref, m_ref, l_ref, acc_ref):
    tq = q_ref.shape[1]
    S = k_ref.shape[0]
    f32 = jnp.float32
    q = q_ref[...].reshape(2 * tq, LANES)
    if online_max:
        m_ref[...] = jnp.full(m_ref.shape, -jnp.inf, f32)
    l_ref[...] = jnp.zeros(l_ref.shape, f32)
    acc_ref[...] = jnp.zeros(acc_ref.shape, f32)
    n_tiles = TK_ATTN // LANES

    def kv_step(j, carry):
        r0 = pl.multiple_of(j * TK_ATTN, TK_ATTN)
        k = k_ref[pl.ds(r0, TK_ATTN), :]
        v = v_ref[pl.ds(r0, TK_ATTN), :]
        s = lax.dot_general(q, k, (((1,), (1,)), ((), ())), preferred_element_type=f32)
        tiles = [s[:, c * LANES:(c + 1) * LANES] for c in range(n_tiles)]
        if online_max:
            part = tiles[0]
            for t in tiles[1:]:
                part = jnp.maximum(part, t)
            m_prev = m_ref[...]
            shift = jnp.maximum(m_prev, jnp.max(part, axis=-1, keepdims=True))
            alpha = jnp.exp2(m_prev - shift)
            m_ref[...] = shift
        else:
            shift = bound_ref[0]
        probs = [jnp.exp2(t - shift) for t in tiles]
        psum = probs[0]
        for p in probs[1:]:
            psum = psum + p
        pv = jnp.dot(jnp.concatenate([p.astype(jnp.bfloat16) for p in probs], axis=1), v,
                     preferred_element_type=f32)
        if online_max:
            l_ref[...] = alpha * l_ref[...] + psum
            acc_ref[...] = alpha * acc_ref[...] + pv
        else:
            l_ref[...] = l_ref[...] + psum
            acc_ref[...] = acc_ref[...] + pv
        return carry

    lax.fori_loop(0, S // TK_ATTN, kv_step, 0)
    o = acc_ref[...] / jnp.sum(l_ref[...], axis=-1, keepdims=True)
    d = o[:tq] - lam_ref[...] * o[tq:]
    dn = d * lax.rsqrt(jnp.mean(d * d, axis=-1, keepdims=True) + EPS) * nw_ref[...]
    o_ref[...] = (dn * (1.0 - LAMBDA_INIT)).astype(o_ref.dtype)


def _diff_attn(online_max, bound, qs, ks, proj, lam, nw, B, S):
    T = B * S
    n_q = S // TQ_ATTN
    one = pl.BlockSpec((1, LANES), lambda b, h, i, bd: (0, 0))
    grid_spec = pltpu.PrefetchScalarGridSpec(
        num_scalar_prefetch=1,
        grid=(B, DIFF_HEADS, n_q),
        in_specs=[pl.BlockSpec((None, None, 2, TQ_ATTN, LANES), lambda b, h, i, bd: (b, h, 0, i, 0)),
                  pl.BlockSpec((S, LANES), lambda b, h, i, bd: (b, h)),
                  pl.BlockSpec((S, LANES), lambda b, h, i, bd: (b, VGV_DV + h)),
                  one, one],
        out_specs=pl.BlockSpec((TQ_ATTN, LANES), lambda b, h, i, bd: (b * n_q + i, h)),
        scratch_shapes=[pltpu.VMEM((2 * TQ_ATTN, LANES), jnp.float32)] * 3,
    )
    return pl.pallas_call(
        functools.partial(_diff_attn_kernel, online_max),
        out_shape=jax.ShapeDtypeStruct((T, DIFF_WIDTH), jnp.bfloat16),
        grid_spec=grid_spec,
        compiler_params=_params(("arbitrary", "arbitrary", "arbitrary")),
        name="diff_attn_online" if online_max else "diff_attn",
    )(bound, qs, ks, proj, lam, nw)


def _out_router_kernel(x_ref, yr_ref, yd_ref, wo_ref, n2_ref, wrt_ref, br_ref,
                       x1_ref, h2_ref, pos_ref, gate_t_ref, len_ref, off_ref, tot_ref):
    tm = x_ref.shape[0]
    f32 = jnp.float32
    bf16 = jnp.bfloat16

    @pl.when(pl.program_id(0) == 0)
    def _():
        tot_ref[...] = jnp.zeros(tot_ref.shape, f32)

    att = jnp.dot(yr_ref[...], wo_ref[:RET_WIDTH, :], preferred_element_type=f32)
    att += jnp.dot(yd_ref[...], wo_ref[RET_WIDTH:, :], preferred_element_type=f32)
    x1 = x_ref[...] + att
    x1_ref[...] = x1
    h2 = x1 * lax.rsqrt(jnp.mean(x1 * x1, axis=-1, keepdims=True) + EPS) * n2_ref[...]
    h2_ref[...] = h2.astype(h2_ref.dtype)
    logits = lax.dot_general(wrt_ref[...], h2, (((1,), (1,)), ((), ())),
                             precision=lax.Precision.HIGHEST, preferred_element_type=f32) + br_ref[...]
    e_iota = lax.broadcasted_iota(jnp.int32, (N_EXPERTS, tm), 0)
    work = logits
    vals, hots = [], []
    for _ in range(TOP_K):
        mx = jnp.max(work, axis=0, keepdims=True)
        ix = jnp.min(jnp.where(work == mx, e_iota, N_EXPERTS), axis=0, keepdims=True)
        hot = e_iota == ix
        vals.append(mx)
        hots.append(hot)
        work = jnp.where(hot, -jnp.inf, work)
    exps = [jnp.exp(v - vals[0]) for v in vals]
    denom = exps[0] + exps[1] + exps[2] + exps[3]
    gates = [e / denom for e in exps]
    sel = jnp.zeros((N_EXPERTS, tm), f32)
    for hot in hots:
        sel = jnp.where(hot, 1.0, sel)
    t_row = lax.broadcasted_iota(jnp.int32, (tm, tm), 0)
    t_col = lax.broadcasted_iota(jnp.int32, (tm, tm), 1)
    upper = jnp.where(t_row < t_col, 1.0, 0.0).astype(bf16)
    rank = jnp.dot(sel.astype(bf16), upper, preferred_element_type=f32)
    cnt = jnp.sum(sel, axis=1, keepdims=True)
    run_units = jnp.floor((cnt + (RUN_ALIGN - 1.0)) * (1.0 / RUN_ALIGN))
    run_len = jnp.broadcast_to(run_units * RUN_ALIGN, (N_EXPERTS, LANES))
    e_row = lax.broadcasted_iota(jnp.int32, (N_EXPERTS, N_EXPERTS), 0)
    e_col = lax.broadcasted_iota(jnp.int32, (N_EXPERTS, N_EXPERTS), 1)
    lower = jnp.where(e_col < e_row, 1.0, 0.0).astype(bf16)
    run_start = jnp.dot(lower, jnp.broadcast_to(run_units, (N_EXPERTS, LANES)).astype(bf16),
                        preferred_element_type=f32) * RUN_ALIGN
    pos_full = rank + run_start[:, 0:1]
    pos = [jnp.sum(jnp.where(hot, pos_full, 0.0), axis=0, keepdims=True) for hot in hots]
    for k in range(TOP_K):
        pos_ref[k:k + 1, :] = pos[k].astype(jnp.int32)
    rows = jnp.concatenate(gates + pos + [jnp.zeros((LANES - 2 * TOP_K, tm), f32)], axis=0)
    gate_t_ref[...] = rows.T
    len_ref[0] = run_len
    off_ref[0] = tot_ref[...]
    tot_ref[...] = tot_ref[...] + run_len


def _out_router(x2, y_ret, y_diff, wo_bf16, n2w, wrt, br):
    T = x2.shape[0]
    n_tiles = T // TILE
    tok = lambda w: pl.BlockSpec((TILE, w), lambda i: (i, 0))
    const = lambda s: pl.BlockSpec(s, lambda i: (0, 0))
    per_tile = pl.BlockSpec((1, N_EXPERTS, LANES), lambda i: (i, 0, 0))
    return pl.pallas_call(
        _out_router_kernel,
        out_shape=(jax.ShapeDtypeStruct((T, D_MODEL), jnp.float32),
                   jax.ShapeDtypeStruct((T, D_MODEL), jnp.bfloat16),
                   jax.ShapeDtypeStruct((TOP_K, T), jnp.int32),
                   jax.ShapeDtypeStruct((T, LANES), jnp.float32),
                   jax.ShapeDtypeStruct((n_tiles, N_EXPERTS, LANES), jnp.float32),
                   jax.ShapeDtypeStruct((n_tiles, N_EXPERTS, LANES), jnp.float32),
                   jax.ShapeDtypeStruct((N_EXPERTS, LANES), jnp.float32)),
        grid=(n_tiles,),
        in_specs=[tok(D_MODEL), tok(RET_WIDTH), tok(DIFF_WIDTH), const((D_MODEL, D_MODEL)),
                  const((1, D_MODEL)), const((N_EXPERTS, D_MODEL)), const((N_EXPERTS, 1))],
        out_specs=(tok(D_MODEL), tok(D_MODEL), pl.BlockSpec((TOP_K, TILE), lambda i: (0, i)), tok(LANES),
                   per_tile, per_tile, const((N_EXPERTS, LANES))),
        compiler_params=_params(("arbitrary",)),
        name="out_router",
    )(x2, y_ret, y_diff, wo_bf16, n2w, wrt, br)


def _run_copies(src_ref, len_ref, dst_ref, tile, make_copy):
    for e in range(N_EXPERTS):
        n = pl.multiple_of(len_ref[tile * N_EXPERTS + e], RUN_ALIGN)
        s = pl.multiple_of(src_ref[tile * N_EXPERTS + e], RUN_ALIGN)
        d = pl.multiple_of(dst_ref[tile * N_EXPERTS + e], RUN_ALIGN)

        @pl.when(n > 0)
        def _():
            make_copy(s, d, n).start()


def _dispatch_kernel(src_ref, len_ref, dst_ref, rows_ref, zlo_ref, zlen_ref, nu_ref,
                     pos_ref, h2_ref, xs_hbm, xbuf_ref, zero_ref, sems, zero_sem):
    i = pl.program_id(0)
    n_tiles = pl.num_programs(0)
    n_buf = xbuf_ref.shape[0]
    cur = i % n_buf
    n_rows, tm = xbuf_ref.shape[1], h2_ref.shape[0]

    @pl.when(i == 0)
    def _():
        zero_ref[...] = jnp.zeros(zero_ref.shape, zero_ref.dtype)

        def pad_copy(e):
            n = pl.multiple_of(zlen_ref[e], RUN_ALIGN)
            lo = pl.multiple_of(zlo_ref[e], RUN_ALIGN)
            return pltpu.make_async_copy(zero_ref.at[pl.ds(0, n)], xs_hbm.at[pl.ds(lo, n)], zero_sem)

        def tail_copy(j):
            return pltpu.make_async_copy(zero_ref, xs_hbm.at[pl.ds(j * MOE_BLOCK, MOE_BLOCK)], zero_sem)

        def guarded(copy, op):
            def body(e, c):
                @pl.when(zlen_ref[e] > 0)
                def _():
                    op(copy(e))
                return c
            return body

        lax.fori_loop(0, N_EXPERTS, guarded(pad_copy, lambda cp: cp.start()), 0)
        lax.fori_loop(0, N_EXPERTS, guarded(pad_copy, lambda cp: cp.wait()), 0)
        n_blocks = xs_hbm.shape[0] // MOE_BLOCK
        lax.fori_loop(nu_ref[0], n_blocks, lambda j, c: (tail_copy(j).start(), c)[1], 0)
        lax.fori_loop(nu_ref[0], n_blocks, lambda j, c: (tail_copy(j).wait(), c)[1], 0)

    p_iota = lax.broadcasted_iota(jnp.int32, (n_rows, tm), 0)
    onehot = jnp.zeros((n_rows, tm), jnp.float32)
    for k in range(TOP_K):
        onehot = jnp.where(p_iota == pos_ref[k:k + 1, :], 1.0, onehot)
    xbuf_ref[cur] = jnp.dot(onehot.astype(jnp.bfloat16), h2_ref[...], preferred_element_type=jnp.float32)

    _run_copies(src_ref, len_ref, dst_ref, i,
                lambda s, d, n: pltpu.make_async_copy(xbuf_ref.at[cur, pl.ds(s, n)], xs_hbm.at[pl.ds(d, n)],
                                                      sems.at[cur]))

    def wait_tile(tile, slot):
        rows = pl.multiple_of(rows_ref[tile], RUN_ALIGN)
        pltpu.make_async_copy(xbuf_ref.at[slot, pl.ds(0, rows)], xs_hbm.at[pl.ds(0, rows)], sems.at[slot]).wait()

    oldest = n_buf - 1

    @pl.when(i >= oldest)
    def _():
        wait_tile(i - oldest, (i + 1) % n_buf)

    @pl.when(i == n_tiles - 1)
    def _():
        for back in range(oldest - 1, -1, -1):
            wait_tile(i - back, (i - back) % n_buf)


def _dispatch(run_src, run_len, run_dst, tile_rows, zero_lo, zero_len, n_used, pos, h2, P):
    T = h2.shape[0]
    n_pre = 7
    grid_spec = pltpu.PrefetchScalarGridSpec(
        num_scalar_prefetch=n_pre,
        grid=(T // TILE,),
        in_specs=[pl.BlockSpec((TOP_K, TILE), lambda i, *_: (0, i)),
                  pl.BlockSpec((TILE, D_MODEL), lambda i, *_: (i, 0))],
        out_specs=pl.BlockSpec(memory_space=pl.ANY),
        scratch_shapes=[pltpu.VMEM((DISPATCH_BUFS, TILE_ROWS, D_MODEL), jnp.float32),
                        pltpu.VMEM((MOE_BLOCK, D_MODEL), jnp.float32),
                        pltpu.SemaphoreType.DMA((DISPATCH_BUFS,)),
                        pltpu.SemaphoreType.DMA(())],
    )
    return pl.pallas_call(
        _dispatch_kernel,
        out_shape=jax.ShapeDtypeStruct((P, D_MODEL), jnp.float32),
        grid_spec=grid_spec,
        compiler_params=_params(("arbitrary",), has_side_effects=True),
        name="dispatch",
    )(run_src, run_len, run_dst, tile_rows, zero_lo, zero_len, n_used, pos, h2)


def _experts_kernel(base_ref, nblk_ref, w1_ref, b1_ref, w2_ref, b2_ref, xs_hbm, ys_hbm,
                    w1b_ref, w2b_ref, xbuf_ref, ybuf_ref, in_sems, out_sems, busy_ref):
    e = pl.program_id(0)
    n = nblk_ref[e]

    def rows(expert, j):
        return pl.ds(pl.multiple_of(base_ref[expert] + j * MOE_BLOCK, MOE_BLOCK), MOE_BLOCK)

    def in_copy(expert, j, slot):
        return pltpu.make_async_copy(xs_hbm.at[rows(expert, j)], xbuf_ref.at[slot], in_sems.at[slot])

    def out_copy(j, slot):
        return pltpu.make_async_copy(ybuf_ref.at[slot], ys_hbm.at[rows(e, j)], out_sems.at[slot])

    def wait_out(slot):
        @pl.when(busy_ref[slot] == 1)
        def _():
            out_copy(0, slot).wait()
            busy_ref[slot] = 0

    @pl.when(e == 0)
    def _():
        busy_ref[0] = 0
        busy_ref[1] = 0

        @pl.when(n > 0)
        def _():
            in_copy(0, 0, 0).start()

    @pl.when(n > 0)
    def _():
        w1b_ref[...] = w1_ref[...].astype(jnp.bfloat16)
        w2b_ref[...] = w2_ref[...].astype(jnp.bfloat16)

        def block(j, carry):
            slot = j % 2

            @pl.when(j + 1 < n)
            def _():
                in_copy(e, j + 1, 1 - slot).start()

            in_copy(e, j, slot).wait()
            wait_out(slot)
            x = xbuf_ref[slot].astype(jnp.bfloat16)
            u = jnp.dot(x, w1b_ref[...], preferred_element_type=jnp.float32) + b1_ref[...]
            glu = jnp.minimum(u[:, :D_FF], SWIGLU_LIMIT)
            lin = jnp.clip(u[:, D_FF:], -SWIGLU_LIMIT, SWIGLU_LIMIT)
            act = glu * jax.nn.sigmoid(SWIGLU_ALPHA * glu) * (lin + 1.0)
            ybuf_ref[slot] = jnp.dot(act.astype(jnp.bfloat16), w2b_ref[...],
                                     preferred_element_type=jnp.float32) + b2_ref[...]
            out_copy(j, slot).start()
            busy_ref[slot] = 1
            return carry

        lax.fori_loop(0, n, block, 0)

    e_next = jnp.minimum(e + 1, N_EXPERTS - 1)

    @pl.when((e + 1 < N_EXPERTS) & (nblk_ref[e_next] > 0))
    def _():
        in_copy(e_next, 0, 0).start()

    @pl.when(e == N_EXPERTS - 1)
    def _():
        wait_out(0)
        wait_out(1)
        ybuf_ref[0] = jnp.zeros(ybuf_ref.shape[1:], ybuf_ref.dtype)
        first_unused = (base_ref[e] + n * MOE_BLOCK) // MOE_BLOCK
        n_blocks = ys_hbm.shape[0] // MOE_BLOCK

        def tail_copy(j):
            return pltpu.make_async_copy(ybuf_ref.at[0], ys_hbm.at[pl.ds(j * MOE_BLOCK, MOE_BLOCK)],
                                         out_sems.at[0])

        lax.fori_loop(first_unused, n_blocks, lambda j, c: (tail_copy(j).start(), c)[1], 0)
        lax.fori_loop(first_unused, n_blocks, lambda j, c: (tail_copy(j).wait(), c)[1], 0)


def _experts(base, n_blk, xs, w1, b1, w2, b2):
    P = xs.shape[0]
    expert = lambda e, bs, nb: (e, 0, 0)
    grid_spec = pltpu.PrefetchScalarGridSpec(
        num_scalar_prefetch=2,
        grid=(N_EXPERTS,),
        in_specs=[pl.BlockSpec((None, D_MODEL, 2 * D_FF), expert),
                  pl.BlockSpec((None, 1, 2 * D_FF), expert),
                  pl.BlockSpec((None, D_FF, D_MODEL), expert),
                  pl.BlockSpec((None, 1, D_MODEL), expert),
                  pl.BlockSpec(memory_space=pl.ANY)],
        out_specs=pl.BlockSpec(memory_space=pl.ANY),
        scratch_shapes=[pltpu.VMEM((D_MODEL, 2 * D_FF), jnp.bfloat16),
                        pltpu.VMEM((D_FF, D_MODEL), jnp.bfloat16),
                        pltpu.VMEM((2, MOE_BLOCK, D_MODEL), jnp.float32),
                        pltpu.VMEM((2, MOE_BLOCK, D_MODEL), jnp.float32),
                        pltpu.SemaphoreType.DMA((2,)),
                        pltpu.SemaphoreType.DMA((2,)),
                        pltpu.SMEM((2,), jnp.int32)],
    )
    return pl.pallas_call(
        _experts_kernel,
        out_shape=jax.ShapeDtypeStruct((P, D_MODEL), jnp.float32),
        grid_spec=grid_spec,
        compiler_params=_params(("arbitrary",)),
        name="experts",
    )(base, n_blk, w1, b1, w2, b2, xs)


def _combine_kernel(src_ref, len_ref, dst_ref, rows_ref, gate_t_ref, x1_ref, ys_hbm, o_ref, ybuf_ref, sems):
    i = pl.program_id(0)
    n_tiles = pl.num_programs(0)
    n_buf = ybuf_ref.shape[0]
    cur = i % n_buf
    n_rows, tm = ybuf_ref.shape[1], x1_ref.shape[0]

    def fetch(tile, slot):
        _run_copies(src_ref, len_ref, dst_ref, tile,
                    lambda s, d, n: pltpu.make_async_copy(ys_hbm.at[pl.ds(d, n)], ybuf_ref.at[slot, pl.ds(s, n)],
                                                          sems.at[slot]))

    @pl.when(i == 0)
    def _():
        ybuf_ref[...] = jnp.zeros(ybuf_ref.shape, ybuf_ref.dtype)
        for tile in range(n_buf - 1):
            fetch(tile, tile)

    ahead = i + n_buf - 1

    @pl.when(ahead < n_tiles)
    def _():
        fetch(ahead, ahead % n_buf)

    g = gate_t_ref[...]
    p_iota = lax.broadcasted_iota(jnp.int32, (tm, n_rows), 1)
    weights = jnp.zeros((tm, n_rows), jnp.float32)
    for k in range(TOP_K):
        pos_k = g[:, TOP_K + k:TOP_K + k + 1].astype(jnp.int32)
        weights = jnp.where(p_iota == pos_k, g[:, k:k + 1], weights)
    weights = weights.astype(jnp.bfloat16)

    rows = pl.multiple_of(rows_ref[i], RUN_ALIGN)
    pltpu.make_async_copy(ys_hbm.at[pl.ds(0, rows)], ybuf_ref.at[cur, pl.ds(0, rows)], sems.at[cur]).wait()
    o_ref[...] = x1_ref[...] + jnp.dot(weights, ybuf_ref[cur].astype(jnp.bfloat16),
                                       preferred_element_type=jnp.float32)


def _combine(run_src, run_len, run_dst, tile_rows, gate_t, x1, ys):
    T = x1.shape[0]
    tok = lambda w: pl.BlockSpec((TILE, w), lambda i, *_: (i, 0))
    grid_spec = pltpu.PrefetchScalarGridSpec(
        num_scalar_prefetch=4,
        grid=(T // TILE,),
        in_specs=[tok(LANES), tok(D_MODEL), pl.BlockSpec(memory_space=pl.ANY)],
        out_specs=tok(D_MODEL),
        scratch_shapes=[pltpu.VMEM((DISPATCH_BUFS, TILE_ROWS, D_MODEL), jnp.float32),
                        pltpu.SemaphoreType.DMA((DISPATCH_BUFS,))],
    )
    return pl.pallas_call(
        _combine_kernel,
        out_shape=jax.ShapeDtypeStruct((T, D_MODEL), jnp.float32),
        grid_spec=grid_spec,
        compiler_params=_params(("arbitrary",)),
        name="combine",
    )(run_src, run_len, run_dst, tile_rows, gate_t, x1, ys)


def _rotary_tables(positions):
    pos = positions.astype(jnp.float32)[..., None]
    lane = jnp.arange(LANES)
    half_r = RET_DK // 2
    inv_r = RET_ROPE_THETA ** (-jnp.linspace(0.0, 1.0, half_r, dtype=jnp.float32))
    ang = pos * inv_r[lane % half_r]
    c2 = jnp.cos(ang)
    s2 = jnp.sin(ang) * jnp.where(lane < half_r, -1.0, 1.0)
    half_d = ROT_DIM // 2
    inv_d = ROPE_THETA ** (-jnp.arange(0, ROT_DIM, 2, dtype=jnp.float32) / ROT_DIM)
    sub = lane % DIFF_DH
    ang_d = pos * inv_d[sub % half_d]
    cd, sd = jnp.cos(ang_d), jnp.sin(ang_d)
    ra = jnp.where(sub < ROT_DIM, cd, 1.0)
    rp = jnp.where((sub >= half_d) & (sub < ROT_DIM), sd, 0.0)
    rn = jnp.where(sub < half_d, -sd, 0.0)
    return c2, s2, ra, rp, rn


def kernel(x, positions, norm1_w, w_in, ret_log_decay_fwd, ret_log_decay_bwd, ret_norm_w, q_norm_w, k_norm_w, lambda_q1, lambda_k1, lambda_q2, lambda_k2, diff_norm_w, w_out, norm2_w, w_router, b_router, w1, b1, w2, b2):
    B, S, D = x.shape
    T = B * S
    f32 = jnp.float32
    bf16 = jnp.bfloat16
    x2 = x.reshape(T, D)

    dup = lambda w: jnp.concatenate([w, w]).reshape(1, LANES).astype(f32)
    proj, rq_r, rk_r, qs, ks = _in_proj(x2, norm1_w[0].reshape(1, D), w_in[0].astype(bf16),
                                        _rotary_tables(positions), dup(q_norm_w[0]), dup(k_norm_w[0]), B, S)

    y_ret = _retention(ret_log_decay_fwd[0].astype(f32), ret_log_decay_bwd[0].astype(f32),
                       rq_r, rk_r, proj, ret_norm_w[0].reshape(1, RET_WIDTH).astype(f32), B, S)

    lam = (jnp.exp(jnp.sum(lambda_q1[0].astype(f32) * lambda_k1[0].astype(f32)))
           - jnp.exp(jnp.sum(lambda_q2[0].astype(f32) * lambda_k2[0].astype(f32))) + LAMBDA_INIT)
    lam_row = jnp.full((1, LANES), lam, f32)
    bound = (SCORE_BOUND_SLACK * DIFF_DH ** 0.5 * LOG2_E
             * jnp.max(jnp.abs(q_norm_w[0].astype(f32))) * jnp.max(jnp.abs(k_norm_w[0].astype(f32)))).reshape(1)
    attn_args = (bound, qs, ks, proj, lam_row, diff_norm_w[0].reshape(1, DIFF_DV).astype(f32), B, S)
    y_diff = lax.cond(bound[0] <= MAX_SAFE_SCORE_BOUND,
                      lambda: _diff_attn(False, *attn_args), lambda: _diff_attn(True, *attn_args))

    x1, h2, pos, gate_t, len_t, off_t, tot_t = _out_router(
        x2, y_ret, y_diff, w_out[0].astype(bf16), norm2_w[0].reshape(1, D),
        w_router[0].T.astype(f32), b_router[0].reshape(N_EXPERTS, 1).astype(f32))

    n_tiles = T // TILE
    run_len = len_t[:, :, 0].astype(jnp.int32)
    total = tot_t[:, 0].astype(jnp.int32)
    padded = ((total + MOE_BLOCK - 1) // MOE_BLOCK) * MOE_BLOCK
    pad_end = jnp.cumsum(padded)
    pad_start = pad_end - padded
    run_dst = pad_start[None, :] + off_t[:, :, 0].astype(jnp.int32)
    run_src = jnp.cumsum(run_len, axis=1) - run_len
    tile_rows = jnp.sum(run_len, axis=1)
    P = T * TOP_K + n_tiles * N_EXPERTS * RUN_ALIGN + N_EXPERTS * MOE_BLOCK
    n_used = (pad_end[-1:] // MOE_BLOCK).astype(jnp.int32)
    runs = (run_src.reshape(-1), run_len.reshape(-1), run_dst.reshape(-1), tile_rows)

    xs = _dispatch(*runs, pad_start + total, padded - total, n_used, pos, h2, P)
    ys = _experts(pad_start, padded // MOE_BLOCK, xs, w1[0], b1[0].reshape(N_EXPERTS, 1, 2 * D_FF),
                  w2[0], b2[0].reshape(N_EXPERTS, 1, D))
    out = _combine(*runs, gate_t, x1, ys)
    return out.reshape(B, S, D)
```

```python
import functools

import jax
import jax.numpy as jnp
from jax import lax
from jax.experimental import pallas as pl
from jax.experimental.pallas import tpu as pltpu

EPS = 1e-6
D_MODEL = 1024
RET_HEADS = 4
RET_DK = 128
RET_WIDTH = 512
RET_ROPE_THETA = 10000.0
DIFF_HEADS = 4
DIFF_DH = 64
DIFF_DV = 128
DIFF_WIDTH = 512
ROPE_THETA = 500000.0
ROT_DIM = DIFF_DH // 4
D_IN_PROJ = 3584
N_EXPERTS = 32
TOP_K = 4
D_FF = 1024
SWIGLU_LIMIT = 7.0
SWIGLU_ALPHA = 1.702
LAMBDA_INIT = 0.8 - 0.6 * 1.0

LOG2_E = 1.4426950408889634
SCORE_BOUND_SLACK = 1.02
MAX_SAFE_SCORE_BOUND = 60.0
LANES = 128
SUBLANES = 8
VMEM_LIMIT = 56 * 1024 * 1024

COL_RQ, COL_RK, COL_RV, COL_RG, COL_DQ, COL_DK, COL_DV = 0, 4, 8, 12, 16, 20, 24
VGV_RV, VGV_RG, VGV_DV = 0, 4, 8

TM_PROJ = 512
RET_CHUNK = 128
RET_UNROLL = 16
TQ_ATTN = 512
TK_ATTN = 2048
TILE = 512
MOE_BLOCK = 512
RUN_ALIGN = SUBLANES
TILE_ROWS = TOP_K * TILE + N_EXPERTS * RUN_ALIGN
DISPATCH_BUFS = 2


def _params(sem, **kw):
    return pltpu.CompilerParams(dimension_semantics=sem, vmem_limit_bytes=VMEM_LIMIT, **kw)


def _in_proj_kernel(x_ref, nw_ref, w_ref, c2_ref, s2_ref, ra_ref, rp_ref, rn_ref, qw_ref, kw_ref,
                    vgv_ref, rqo_ref, rko_ref, qs_ref, ks_ref):
    ts = x_ref.shape[0]
    x = x_ref[...]
    hn = (x * lax.rsqrt(jnp.mean(x * x, axis=-1, keepdims=True) + EPS) * nw_ref[...]).astype(jnp.bfloat16)

    def proj(col_block):
        c0 = col_block * LANES
        return jnp.dot(hn, w_ref[:, c0:c0 + 4 * LANES], preferred_element_type=jnp.float32)

    rq, rk, dq, dk = proj(COL_RQ), proj(COL_RK), proj(COL_DQ), proj(COL_DK)
    c2 = c2_ref[...]
    s2 = s2_ref[...]
    ra = ra_ref[...]
    rp = rp_ref[...]
    rn = rn_ref[...]
    lane = lax.broadcasted_iota(jnp.int32, (ts, LANES), 1)
    lo = lane < DIFF_DH

    def qk_norm_rot(x, w):
        x2 = x * x
        s_lo = jnp.sum(jnp.where(lo, x2, 0.0), axis=-1, keepdims=True)
        s_hi = jnp.sum(jnp.where(lo, 0.0, x2), axis=-1, keepdims=True)
        ms = jnp.where(lo, s_lo, s_hi) * (1.0 / DIFF_DH)
        xn = x * lax.rsqrt(ms + EPS) * w
        return xn * ra + pltpu.roll(xn, ROT_DIM // 2, 1) * rp + pltpu.roll(xn, LANES - ROT_DIM // 2, 1) * rn

    for h in range(RET_HEADS):
        sl = slice(h * LANES, (h + 1) * LANES)
        q = rq[:, sl]
        k = rk[:, sl]
        rqo_ref[:, sl] = (q * c2 + pltpu.roll(q, RET_DK // 2, 1) * s2).astype(rqo_ref.dtype)
        rko_ref[:, sl] = ((k * c2 + pltpu.roll(k, RET_DK // 2, 1) * s2) * (RET_DK ** -0.5)).astype(rko_ref.dtype)
    for h in range(DIFF_HEADS):
        sl = slice(h * LANES, (h + 1) * LANES)
        q = qk_norm_rot(dq[:, sl], qw_ref[...]) * (DIFF_DH ** -0.5 * LOG2_E)
        k = qk_norm_rot(dk[:, sl], kw_ref[...])
        qs_ref[h, 0] = jnp.where(lo, q, 0.0).astype(qs_ref.dtype)
        qs_ref[h, 1] = jnp.where(lo, 0.0, q).astype(qs_ref.dtype)
        ks_ref[:, sl] = k.astype(ks_ref.dtype)
    for slot, col_block in enumerate((COL_RV, COL_RG, COL_DV)):
        vgv_ref[:, slot * 4 * LANES:(slot + 1) * 4 * LANES] = proj(col_block).astype(vgv_ref.dtype)


def _in_proj(x2, nw, w_bf16, tabs, qw2, kw2, B, S):
    T = B * S
    n_s = S // TM_PROJ
    tok = lambda w: pl.BlockSpec((TM_PROJ, w), lambda i: (i, 0))
    const = lambda s: pl.BlockSpec(s, lambda i: (0, 0))
    tab = pl.BlockSpec((None, TM_PROJ, LANES), lambda i: (i // n_s, i % n_s, 0))
    bf16 = jnp.bfloat16
    return pl.pallas_call(
        _in_proj_kernel,
        out_shape=(jax.ShapeDtypeStruct((T, 3 * 4 * LANES), bf16),
                   jax.ShapeDtypeStruct((T, 4 * LANES), bf16),
                   jax.ShapeDtypeStruct((T, 4 * LANES), bf16),
                   jax.ShapeDtypeStruct((B, DIFF_HEADS, 2, S, LANES), bf16),
                   jax.ShapeDtypeStruct((T, 4 * LANES), bf16)),
        grid=(T // TM_PROJ,),
        in_specs=[tok(D_MODEL), const((1, D_MODEL)), const((D_MODEL, D_IN_PROJ)),
                  tab, tab, tab, tab, tab, const((1, LANES)), const((1, LANES))],
        out_specs=(tok(3 * 4 * LANES), tok(4 * LANES), tok(4 * LANES),
                   pl.BlockSpec((None, DIFF_HEADS, 2, TM_PROJ, LANES), lambda i: (i // n_s, 0, 0, i % n_s, 0)),
                   tok(4 * LANES)),
        compiler_params=_params(("arbitrary",)),
        name="in_proj",
    )(x2, nw, w_bf16, *tabs, qw2, kw2)


def _retention_kernel(ldf_ref, ldb_ref, q_ref, k_ref, v_ref, g_ref, nw_ref, o_ref, sb_ref):
    C = RET_CHUNK
    S = q_ref.shape[0]
    n_chunks = S // C
    h = pl.program_id(1)
    ldf = ldf_ref[h]
    ldb = ldb_ref[h]
    row = lax.broadcasted_iota(jnp.int32, (C, C), 0).astype(jnp.float32)
    colm = lax.broadcasted_iota(jnp.int32, (C, C), 1).astype(jnp.float32)
    dist = row - colm
    decay = jnp.where(dist >= 0, jnp.exp(ldf * jnp.maximum(dist, 0.0)), jnp.exp(ldb * jnp.maximum(-dist, 0.0)))
    idx = lax.broadcasted_iota(jnp.int32, (C, 1), 0).astype(jnp.float32)
    q_dec_f = jnp.exp(ldf * (idx + 1.0))
    k_dec_f = jnp.exp(ldf * (C - 1.0 - idx))
    q_dec_b = jnp.exp(ldb * (C - idx))
    k_dec_b = jnp.exp(ldb * idx)
    chunk_dec_f = jnp.exp(ldf * C)
    chunk_dec_b = jnp.exp(ldb * C)
    f32 = jnp.float32
    bf16 = jnp.bfloat16

    def kv_state(k, v, k_dec):
        kd = (k.astype(f32) * k_dec).astype(bf16)
        return lax.dot_general(kd, v, (((0,), (0,)), ((), ())), preferred_element_type=f32)

    def bwd_step(i, state):
        c = n_chunks - 1 - i
        r0 = pl.multiple_of(c * C, C)
        sb_ref[c] = state
        return state * chunk_dec_b + kv_state(k_ref[pl.ds(r0, C), :], v_ref[pl.ds(r0, C), :], k_dec_b)

    lax.fori_loop(0, n_chunks, bwd_step, jnp.zeros((RET_DK, LANES), f32), unroll=RET_UNROLL)

    def fwd_step(c, state):
        r0 = pl.multiple_of(c * C, C)
        q = q_ref[pl.ds(r0, C), :]
        k = k_ref[pl.ds(r0, C), :]
        v = v_ref[pl.ds(r0, C), :]
        scores = lax.dot_general(q, k, (((1,), (1,)), ((), ())), preferred_element_type=f32) * decay
        y = jnp.dot(scores.astype(bf16), v, preferred_element_type=f32)
        qf = q.astype(f32)
        y += jnp.dot((qf * q_dec_f).astype(bf16), state.astype(bf16), preferred_element_type=f32)
        y += jnp.dot((qf * q_dec_b).astype(bf16), sb_ref[c].astype(bf16), preferred_element_type=f32)
        yn = y * lax.rsqrt(jnp.mean(y * y, axis=-1, keepdims=True) + EPS) * nw_ref[...]
        g = g_ref[pl.ds(r0, C), :].astype(f32)
        o_ref[pl.ds(r0, C), :] = (yn * (g * jax.nn.sigmoid(g))).astype(o_ref.dtype)
        return state * chunk_dec_f + kv_state(k, v, k_dec_f)

    lax.fori_loop(0, n_chunks, fwd_step, jnp.zeros((RET_DK, LANES), f32), unroll=RET_UNROLL)


def _retention(ldf, ldb, rq_r, rk_r, proj, nw, B, S):
    T = B * S
    smem = pl.BlockSpec(memory_space=pltpu.SMEM)
    seq = lambda cb: pl.BlockSpec((S, LANES), lambda b, h: (b, cb + h))
    return pl.pallas_call(
        _retention_kernel,
        out_shape=jax.ShapeDtypeStruct((T, RET_WIDTH), jnp.bfloat16),
        grid=(B, RET_HEADS),
        in_specs=[smem, smem, seq(0), seq(0), seq(VGV_RV), seq(VGV_RG),
                  pl.BlockSpec((1, LANES), lambda b, h: (0, h))],
        out_specs=seq(0),
        scratch_shapes=[pltpu.VMEM((S // RET_CHUNK, RET_DK, LANES), jnp.float32)],
        compiler_params=_params(("arbitrary", "arbitrary")),
        name="retention",
    )(ldf, ldb, rq_r, rk_r, proj, proj, nw)


def _diff_attn_kernel(online_max, bound_ref, q_ref, k_ref, v_ref, lam_ref, nw_ref, o_ref, m_ref, l_ref, acc_ref):
    tq = q_ref.shape[1]
    S = k_ref.shape[0]
    f32 = jnp.float32
    q = q_ref[...].reshape(2 * tq, LANES)
    if online_max:
        m_ref[...] = jnp.full(m_ref.shape, -jnp.inf, f32)
    l_ref[...] = jnp.zeros(l_ref.shape, f32)
    acc_ref[...] = jnp.zeros(acc_ref.shape, f32)
    n_tiles = TK_ATTN // LANES

    def kv_step(j, carry):
        r0 = pl.multiple_of(j * TK_ATTN, TK_ATTN)
        k = k_ref[pl.ds(r0, TK_ATTN), :]
        v = v_ref[pl.ds(r0, TK_ATTN), :]
        s = lax.dot_general(q, k, (((1,), (1,)), ((), ())), preferred_element_type=f32)
        tiles = [s[:, c * LANES:(c + 1) * LANES] for c in range(n_tiles)]
        if online_max:
            part = tiles[0]
            for t in tiles[1:]:
                part = jnp.maximum(part, t)
            m_prev = m_ref[...]
            shift = jnp.maximum(m_prev, jnp.max(part, axis=-1, keepdims=True))
            alpha = jnp.exp2(m_prev - shift)
            m_ref[...] = shift
        else:
            shift = bound_ref[0]
        probs = [jnp.exp2(t - shift) for t in tiles]
        psum = probs[0]
        for p in probs[1:]:
            psum = psum + p
        pv = jnp.dot(jnp.concatenate([p.astype(jnp.bfloat16) for p in probs], axis=1), v,
                     preferred_element_type=f32)
        if online_max:
            l_ref[...] = alpha * l_ref[...] + psum
            acc_ref[...] = alpha * acc_ref[...] + pv
        else:
            l_ref[...] = l_ref[...] + psum
            acc_ref[...] = acc_ref[...] + pv
        return carry

    lax.fori_loop(0, S // TK_ATTN, kv_step, 0)
    o = acc_ref[...] / jnp.sum(l_ref[...], axis=-1, keepdims=True)
    d = o[:tq] - lam_ref[...] * o[tq:]
    dn = d * lax.rsqrt(jnp.mean(d * d, axis=-1, keepdims=True) + EPS) * nw_ref[...]
    o_ref[...] = (dn * (1.0 - LAMBDA_INIT)).astype(o_ref.dtype)


def _diff_attn(online_max, bound, qs, ks, proj, lam, nw, B, S):
    T = B * S
    n_q = S // TQ_ATTN
    one = pl.BlockSpec((1, LANES), lambda b, h, i, bd: (0, 0))
    grid_spec = pltpu.PrefetchScalarGridSpec(
        num_scalar_prefetch=1,
        grid=(B, DIFF_HEADS, n_q),
        in_specs=[pl.BlockSpec((None, None, 2, TQ_ATTN, LANES), lambda b, h, i, bd: (b, h, 0, i, 0)),
                  pl.BlockSpec((S, LANES), lambda b, h, i, bd: (b, h)),
                  pl.BlockSpec((S, LANES), lambda b, h, i, bd: (b, VGV_DV + h)),
                  one, one],
        out_specs=pl.BlockSpec((TQ_ATTN, LANES), lambda b, h, i, bd: (b * n_q + i, h)),
        scratch_shapes=[pltpu.VMEM((2 * TQ_ATTN, LANES), jnp.float32)] * 3,
    )
    return pl.pallas_call(
        functools.partial(_diff_attn_kernel, online_max),
        out_shape=jax.ShapeDtypeStruct((T, DIFF_WIDTH), jnp.bfloat16),
        grid_spec=grid_spec,
        compiler_params=_params(("arbitrary", "arbitrary", "arbitrary")),
        name="diff_attn_online" if online_max else "diff_attn",
    )(bound, qs, ks, proj, lam, nw)


def _out_router_kernel(x_ref, yr_ref, yd_ref, wo_ref, n2_ref, wrt_ref, br_ref,
                       x1_ref, h2_ref, pos_ref, gate_t_ref, len_ref, off_ref, tot_ref):
    tm = x_ref.shape[0]
    f32 = jnp.float32
    bf16 = jnp.bfloat16

    @pl.when(pl.program_id(0) == 0)
    def _():
        tot_ref[...] = jnp.zeros(tot_ref.shape, f32)

    att = jnp.dot(yr_ref[...], wo_ref[:RET_WIDTH, :], preferred_element_type=f32)
    att += jnp.dot(yd_ref[...], wo_ref[RET_WIDTH:, :], preferred_element_type=f32)
    x1 = x_ref[...] + att
    x1_ref[...] = x1
    h2 = x1 * lax.rsqrt(jnp.mean(x1 * x1, axis=-1, keepdims=True) + EPS) * n2_ref[...]
    h2_ref[...] = h2.astype(h2_ref.dtype)
    logits = lax.dot_general(wrt_ref[...], h2, (((1,), (1,)), ((), ())),
                             precision=lax.Precision.HIGHEST, preferred_element_type=f32) + br_ref[...]
    e_iota = lax.broadcasted_iota(jnp.int32, (N_EXPERTS, tm), 0)
    work = logits
    vals, hots = [], []
    for _ in range(TOP_K):
        mx = jnp.max(work, axis=0, keepdims=True)
        ix = jnp.min(jnp.where(work == mx, e_iota, N_EXPERTS), axis=0, keepdims=True)
        hot = e_iota == ix
        vals.append(mx)
        hots.append(hot)
        work = jnp.where(hot, -jnp.inf, work)
    exps = [jnp.exp(v - vals[0]) for v in vals]
    denom = exps[0] + exps[1] + exps[2] + exps[3]
    gates = [e / denom for e in exps]
    sel = jnp.zeros((N_EXPERTS, tm), f32)
    for hot in hots:
        sel = jnp.where(hot, 1.0, sel)
    t_row = lax.broadcasted_iota(jnp.int32, (tm, tm), 0)
    t_col = lax.broadcasted_iota(jnp.int32, (tm, tm), 1)
    upper = jnp.where(t_row < t_col, 1.0, 0.0).astype(bf16)
    rank = jnp.dot(sel.astype(bf16), upper, preferred_element_type=f32)
    cnt = jnp.sum(sel, axis=1, keepdims=True)
    run_units = jnp.floor((cnt + (RUN_ALIGN - 1.0)) * (1.0 / RUN_ALIGN))
    run_len = jnp.broadcast_to(run_units * RUN_ALIGN, (N_EXPERTS, LANES))
    e_row = lax.broadcasted_iota(jnp.int32, (N_EXPERTS, N_EXPERTS), 0)
    e_col = lax.broadcasted_iota(jnp.int32, (N_EXPERTS, N_EXPERTS), 1)
    lower = jnp.where(e_col < e_row, 1.0, 0.0).astype(bf16)
    run_start = jnp.dot(lower, jnp.broadcast_to(run_units, (N_EXPERTS, LANES)).astype(bf16),
                        preferred_element_type=f32) * RUN_ALIGN
    pos_full = rank + run_start[:, 0:1]
    pos = [jnp.sum(jnp.where(hot, pos_full, 0.0), axis=0, keepdims=True) for hot in hots]
    for k in range(TOP_K):
        pos_ref[k:k + 1, :] = pos[k].astype(jnp.int32)
    rows = jnp.concatenate(gates + pos + [jnp.zeros((LANES - 2 * TOP_K, tm), f32)], axis=0)
    gate_t_ref[...] = rows.T
    len_ref[0] = run_len
    off_ref[0] = tot_ref[...]
    tot_ref[...] = tot_ref[...] + run_len


def _out_router(x2, y_ret, y_diff, wo_bf16, n2w, wrt, br):
    T = x2.shape[0]
    n_tiles = T // TILE
    tok = lambda w: pl.BlockSpec((TILE, w), lambda i: (i, 0))
    const = lambda s: pl.BlockSpec(s, lambda i: (0, 0))
    per_tile = pl.BlockSpec((1, N_EXPERTS, LANES), lambda i: (i, 0, 0))
    return pl.pallas_call(
        _out_router_kernel,
        out_shape=(jax.ShapeDtypeStruct((T, D_MODEL), jnp.float32),
                   jax.ShapeDtypeStruct((T, D_MODEL), jnp.bfloat16),
                   jax.ShapeDtypeStruct((TOP_K, T), jnp.int32),
                   jax.ShapeDtypeStruct((T, LANES), jnp.float32),
                   jax.ShapeDtypeStruct((n_tiles, N_EXPERTS, LANES), jnp.float32),
                   jax.ShapeDtypeStruct((n_tiles, N_EXPERTS, LANES), jnp.float32),
                   jax.ShapeDtypeStruct((N_EXPERTS, LANES), jnp.float32)),
        grid=(n_tiles,),
        in_specs=[tok(D_MODEL), tok(RET_WIDTH), tok(DIFF_WIDTH), const((D_MODEL, D_MODEL)),
                  const((1, D_MODEL)), const((N_EXPERTS, D_MODEL)), const((N_EXPERTS, 1))],
        out_specs=(tok(D_MODEL), tok(D_MODEL), pl.BlockSpec((TOP_K, TILE), lambda i: (0, i)), tok(LANES),
                   per_tile, per_tile, const((N_EXPERTS, LANES))),
        compiler_params=_params(("arbitrary",)),
        name="out_router",
    )(x2, y_ret, y_diff, wo_bf16, n2w, wrt, br)


def _pack_bf16_pairs(x):
    w = x.shape[1] // 2
    bits = lambda v: lax.bitcast_convert_type(v.astype(jnp.bfloat16).astype(jnp.float32), jnp.uint32)
    return (bits(x[:, :w]) >> 16) | (bits(x[:, w:]) & jnp.uint32(0xFFFF0000))


def _unpack_bf16_pairs(p):
    as_bf16 = lambda bits: lax.bitcast_convert_type(bits, jnp.float32).astype(jnp.bfloat16)
    return as_bf16(p << 16), as_bf16(p & jnp.uint32(0xFFFF0000))


def _run_copies(src_ref, len_ref, dst_ref, tile, make_copy):
    for e in range(N_EXPERTS):
        n = pl.multiple_of(len_ref[tile * N_EXPERTS + e], RUN_ALIGN)
        s = pl.multiple_of(src_ref[tile * N_EXPERTS + e], RUN_ALIGN)
        d = pl.multiple_of(dst_ref[tile * N_EXPERTS + e], RUN_ALIGN)

        @pl.when(n > 0)
        def _():
            make_copy(s, d, n).start()


def _dispatch_kernel(src_ref, len_ref, dst_ref, rows_ref, zlo_ref, zlen_ref, nu_ref,
                     pos_ref, h2_ref, xs_hbm, xbuf_ref, zero_ref, sems, zero_sem):
    i = pl.program_id(0)
    n_tiles = pl.num_programs(0)
    n_buf = xbuf_ref.shape[0]
    cur = i % n_buf
    n_rows, tm = xbuf_ref.shape[1], h2_ref.shape[0]

    @pl.when(i == 0)
    def _():
        zero_ref[...] = jnp.zeros(zero_ref.shape, zero_ref.dtype)

        def pad_copy(e):
            n = pl.multiple_of(zlen_ref[e], RUN_ALIGN)
            lo = pl.multiple_of(zlo_ref[e], RUN_ALIGN)
            return pltpu.make_async_copy(zero_ref.at[pl.ds(0, n)], xs_hbm.at[pl.ds(lo, n)], zero_sem)

        def tail_copy(j):
            return pltpu.make_async_copy(zero_ref, xs_hbm.at[pl.ds(j * MOE_BLOCK, MOE_BLOCK)], zero_sem)

        def guarded(copy, op):
            def body(e, c):
                @pl.when(zlen_ref[e] > 0)
                def _():
                    op(copy(e))
                return c
            return body

        lax.fori_loop(0, N_EXPERTS, guarded(pad_copy, lambda cp: cp.start()), 0)
        lax.fori_loop(0, N_EXPERTS, guarded(pad_copy, lambda cp: cp.wait()), 0)
        n_blocks = xs_hbm.shape[0] // MOE_BLOCK
        lax.fori_loop(nu_ref[0], n_blocks, lambda j, c: (tail_copy(j).start(), c)[1], 0)
        lax.fori_loop(nu_ref[0], n_blocks, lambda j, c: (tail_copy(j).wait(), c)[1], 0)

    p_iota = lax.broadcasted_iota(jnp.int32, (n_rows, tm), 0)
    onehot = jnp.zeros((n_rows, tm), jnp.float32)
    for k in range(TOP_K):
        onehot = jnp.where(p_iota == pos_ref[k:k + 1, :], 1.0, onehot)
    xbuf_ref[cur] = _pack_bf16_pairs(
        jnp.dot(onehot.astype(jnp.bfloat16), h2_ref[...], preferred_element_type=jnp.float32))

    _run_copies(src_ref, len_ref, dst_ref, i,
                lambda s, d, n: pltpu.make_async_copy(xbuf_ref.at[cur, pl.ds(s, n)], xs_hbm.at[pl.ds(d, n)],
                                                      sems.at[cur]))

    def wait_tile(tile, slot):
        rows = pl.multiple_of(rows_ref[tile], RUN_ALIGN)
        pltpu.make_async_copy(xbuf_ref.at[slot, pl.ds(0, rows)], xs_hbm.at[pl.ds(0, rows)], sems.at[slot]).wait()

    oldest = n_buf - 1

    @pl.when(i >= oldest)
    def _():
        wait_tile(i - oldest, (i + 1) % n_buf)

    @pl.when(i == n_tiles - 1)
    def _():
        for back in range(oldest - 1, -1, -1):
            wait_tile(i - back, (i - back) % n_buf)


def _dispatch(run_src, run_len, run_dst, tile_rows, zero_lo, zero_len, n_used, pos, h2, P):
    T = h2.shape[0]
    n_pre = 7
    grid_spec = pltpu.PrefetchScalarGridSpec(
        num_scalar_prefetch=n_pre,
        grid=(T // TILE,),
        in_specs=[pl.BlockSpec((TOP_K, TILE), lambda i, *_: (0, i)),
                  pl.BlockSpec((TILE, D_MODEL), lambda i, *_: (i, 0))],
        out_specs=pl.BlockSpec(memory_space=pl.ANY),
        scratch_shapes=[pltpu.VMEM((DISPATCH_BUFS, TILE_ROWS, D_MODEL // 2), jnp.uint32),
                        pltpu.VMEM((MOE_BLOCK, D_MODEL // 2), jnp.uint32),
                        pltpu.SemaphoreType.DMA((DISPATCH_BUFS,)),
                        pltpu.SemaphoreType.DMA(())],
    )
    return pl.pallas_call(
        _dispatch_kernel,
        out_shape=jax.ShapeDtypeStruct((P, D_MODEL // 2), jnp.uint32),
        grid_spec=grid_spec,
        compiler_params=_params(("arbitrary",), has_side_effects=True),
        name="dispatch",
    )(run_src, run_len, run_dst, tile_rows, zero_lo, zero_len, n_used, pos, h2)


def _experts_kernel(base_ref, nblk_ref, w1_ref, b1_ref, w2_ref, b2_ref, xs_hbm, ys_hbm,
                    w1b_ref, w2b_ref, xbuf_ref, ybuf_ref, in_sems, out_sems, busy_ref):
    e = pl.program_id(0)
    n = nblk_ref[e]

    def rows(expert, j):
        return pl.ds(pl.multiple_of(base_ref[expert] + j * MOE_BLOCK, MOE_BLOCK), MOE_BLOCK)

    def in_copy(expert, j, slot):
        return pltpu.make_async_copy(xs_hbm.at[rows(expert, j)], xbuf_ref.at[slot], in_sems.at[slot])

    def out_copy(j, slot):
        return pltpu.make_async_copy(ybuf_ref.at[slot], ys_hbm.at[rows(e, j)], out_sems.at[slot])

    def wait_out(slot):
        @pl.when(busy_ref[slot] == 1)
        def _():
            out_copy(0, slot).wait()
            busy_ref[slot] = 0

    @pl.when(e == 0)
    def _():
        busy_ref[0] = 0
        busy_ref[1] = 0

        @pl.when(n > 0)
        def _():
            in_copy(0, 0, 0).start()

    @pl.when(n > 0)
    def _():
        w1b_ref[...] = w1_ref[...].astype(jnp.bfloat16)
        w2b_ref[...] = w2_ref[...].astype(jnp.bfloat16)

        def block(j, carry):
            slot = j % 2

            @pl.when(j + 1 < n)
            def _():
                in_copy(e, j + 1, 1 - slot).start()

            in_copy(e, j, slot).wait()
            wait_out(slot)
            x = jnp.concatenate(_unpack_bf16_pairs(xbuf_ref[slot]), axis=1)
            u = jnp.dot(x, w1b_ref[...], preferred_element_type=jnp.float32) + b1_ref[...]
            glu = jnp.minimum(u[:, :D_FF], SWIGLU_LIMIT)
            lin = jnp.clip(u[:, D_FF:], -SWIGLU_LIMIT, SWIGLU_LIMIT)
            act = glu * jax.nn.sigmoid(SWIGLU_ALPHA * glu) * (lin + 1.0)
            ybuf_ref[slot] = _pack_bf16_pairs(jnp.dot(act.astype(jnp.bfloat16), w2b_ref[...],
                                                      preferred_element_type=jnp.float32) + b2_ref[...])
            out_copy(j, slot).start()
            busy_ref[slot] = 1
            return carry

        lax.fori_loop(0, n, block, 0)

    e_next = jnp.minimum(e + 1, N_EXPERTS - 1)

    @pl.when((e + 1 < N_EXPERTS) & (nblk_ref[e_next] > 0))
    def _():
        in_copy(e_next, 0, 0).start()

    @pl.when(e == N_EXPERTS - 1)
    def _():
        wait_out(0)
        wait_out(1)
        ybuf_ref[0] = jnp.zeros(ybuf_ref.shape[1:], ybuf_ref.dtype)
        first_unused = (base_ref[e] + n * MOE_BLOCK) // MOE_BLOCK
        n_blocks = ys_hbm.shape[0] // MOE_BLOCK

        def tail_copy(j):
            return pltpu.make_async_copy(ybuf_ref.at[0], ys_hbm.at[pl.ds(j * MOE_BLOCK, MOE_BLOCK)],
                                         out_sems.at[0])

        lax.fori_loop(first_unused, n_blocks, lambda j, c: (tail_copy(j).start(), c)[1], 0)
        lax.fori_loop(first_unused, n_blocks, lambda j, c: (tail_copy(j).wait(), c)[1], 0)


def _experts(base, n_blk, xs, w1, b1, w2, b2):
    P = xs.shape[0]
    expert = lambda e, bs, nb: (e, 0, 0)
    grid_spec = pltpu.PrefetchScalarGridSpec(
        num_scalar_prefetch=2,
        grid=(N_EXPERTS,),
        in_specs=[pl.BlockSpec((None, D_MODEL, 2 * D_FF), expert),
                  pl.BlockSpec((None, 1, 2 * D_FF), expert),
                  pl.BlockSpec((None, D_FF, D_MODEL), expert),
                  pl.BlockSpec((None, 1, D_MODEL), expert),
                  pl.BlockSpec(memory_space=pl.ANY)],
        out_specs=pl.BlockSpec(memory_space=pl.ANY),
        scratch_shapes=[pltpu.VMEM((D_MODEL, 2 * D_FF), jnp.bfloat16),
                        pltpu.VMEM((D_FF, D_MODEL), jnp.bfloat16),
                        pltpu.VMEM((2, MOE_BLOCK, D_MODEL // 2), jnp.uint32),
                        pltpu.VMEM((2, MOE_BLOCK, D_MODEL // 2), jnp.uint32),
                        pltpu.SemaphoreType.DMA((2,)),
                        pltpu.SemaphoreType.DMA((2,)),
                        pltpu.SMEM((2,), jnp.int32)],
    )
    return pl.pallas_call(
        _experts_kernel,
        out_shape=jax.ShapeDtypeStruct((P, D_MODEL // 2), jnp.uint32),
        grid_spec=grid_spec,
        compiler_params=_params(("arbitrary",)),
        name="experts",
    )(base, n_blk, w1, b1, w2, b2, xs)


def _combine_kernel(src_ref, len_ref, dst_ref, rows_ref, gate_t_ref, x1_ref, ys_hbm, o_ref, ybuf_ref, sems):
    i = pl.program_id(0)
    n_tiles = pl.num_programs(0)
    n_buf = ybuf_ref.shape[0]
    cur = i % n_buf
    n_rows, tm = ybuf_ref.shape[1], x1_ref.shape[0]

    def fetch(tile, slot):
        _run_copies(src_ref, len_ref, dst_ref, tile,
                    lambda s, d, n: pltpu.make_async_copy(ys_hbm.at[pl.ds(d, n)], ybuf_ref.at[slot, pl.ds(s, n)],
                                                          sems.at[slot]))

    @pl.when(i == 0)
    def _():
        ybuf_ref[...] = jnp.zeros(ybuf_ref.shape, ybuf_ref.dtype)
        for tile in range(n_buf - 1):
            fetch(tile, tile)

    ahead = i + n_buf - 1

    @pl.when(ahead < n_tiles)
    def _():
        fetch(ahead, ahead % n_buf)

    g = gate_t_ref[...]
    p_iota = lax.broadcasted_iota(jnp.int32, (tm, n_rows), 1)
    weights = jnp.zeros((tm, n_rows), jnp.float32)
    for k in range(TOP_K):
        pos_k = g[:, TOP_K + k:TOP_K + k + 1].astype(jnp.int32)
        weights = jnp.where(p_iota == pos_k, g[:, k:k + 1], weights)
    weights = weights.astype(jnp.bfloat16)

    rows = pl.multiple_of(rows_ref[i], RUN_ALIGN)
    pltpu.make_async_copy(ys_hbm.at[pl.ds(0, rows)], ybuf_ref.at[cur, pl.ds(0, rows)], sems.at[cur]).wait()
    halves = [jnp.dot(weights, y, preferred_element_type=jnp.float32) for y in _unpack_bf16_pairs(ybuf_ref[cur])]
    o_ref[...] = x1_ref[...] + jnp.concatenate(halves, axis=1)


def _combine(run_src, run_len, run_dst, tile_rows, gate_t, x1, ys):
    T = x1.shape[0]
    tok = lambda w: pl.BlockSpec((TILE, w), lambda i, *_: (i, 0))
    grid_spec = pltpu.PrefetchScalarGridSpec(
        num_scalar_prefetch=4,
        grid=(T // TILE,),
        in_specs=[tok(LANES), tok(D_MODEL), pl.BlockSpec(memory_space=pl.ANY)],
        out_specs=tok(D_MODEL),
        scratch_shapes=[pltpu.VMEM((DISPATCH_BUFS, TILE_ROWS, D_MODEL // 2), jnp.uint32),
                        pltpu.SemaphoreType.DMA((DISPATCH_BUFS,))],
    )
    return pl.pallas_call(
        _combine_kernel,
        out_shape=jax.ShapeDtypeStruct((T, D_MODEL), jnp.float32),
        grid_spec=grid_spec,
        compiler_params=_params(("arbitrary",)),
        name="combine",
    )(run_src, run_len, run_dst, tile_rows, gate_t, x1, ys)


def _rotary_tables(positions):
    pos = positions.astype(jnp.float32)[..., None]
    lane = jnp.arange(LANES)
    half_r = RET_DK // 2
    inv_r = RET_ROPE_THETA ** (-jnp.linspace(0.0, 1.0, half_r, dtype=jnp.float32))
    ang = pos * inv_r[lane % half_r]
    c2 = jnp.cos(ang)
    s2 = jnp.sin(ang) * jnp.where(lane < half_r, -1.0, 1.0)
    half_d = ROT_DIM // 2
    inv_d = ROPE_THETA ** (-jnp.arange(0, ROT_DIM, 2, dtype=jnp.float32) / ROT_DIM)
    sub = lane % DIFF_DH
    ang_d = pos * inv_d[sub % half_d]
    cd, sd = jnp.cos(ang_d), jnp.sin(ang_d)
    ra = jnp.where(sub < ROT_DIM, cd, 1.0)
    rp = jnp.where((sub >= half_d) & (sub < ROT_DIM), sd, 0.0)
    rn = jnp.where(sub < half_d, -sd, 0.0)
    return c2, s2, ra, rp, rn


def kernel(x, positions, norm1_w, w_in, ret_log_decay_fwd, ret_log_decay_bwd, ret_norm_w, q_norm_w, k_norm_w, lambda_q1, lambda_k1, lambda_q2, lambda_k2, diff_norm_w, w_out, norm2_w, w_router, b_router, w1, b1, w2, b2):
    B, S, D = x.shape
    T = B * S
    f32 = jnp.float32
    bf16 = jnp.bfloat16
    x2 = x.reshape(T, D)

    dup = lambda w: jnp.concatenate([w, w]).reshape(1, LANES).astype(f32)
    proj, rq_r, rk_r, qs, ks = _in_proj(x2, norm1_w[0].reshape(1, D), w_in[0].astype(bf16),
                                        _rotary_tables(positions), dup(q_norm_w[0]), dup(k_norm_w[0]), B, S)

    y_ret = _retention(ret_log_decay_fwd[0].astype(f32), ret_log_decay_bwd[0].astype(f32),
                       rq_r, rk_r, proj, ret_norm_w[0].reshape(1, RET_WIDTH).astype(f32), B, S)

    lam = (jnp.exp(jnp.sum(lambda_q1[0].astype(f32) * lambda_k1[0].astype(f32)))
           - jnp.exp(jnp.sum(lambda_q2[0].astype(f32) * lambda_k2[0].astype(f32))) + LAMBDA_INIT)
    lam_row = jnp.full((1, LANES), lam, f32)
    bound = (SCORE_BOUND_SLACK * DIFF_DH ** 0.5 * LOG2_E
             * jnp.max(jnp.abs(q_norm_w[0].astype(f32))) * jnp.max(jnp.abs(k_norm_w[0].astype(f32)))).reshape(1)
    attn_args = (bound, qs, ks, proj, lam_row, diff_norm_w[0].reshape(1, DIFF_DV).astype(f32), B, S)
    y_diff = lax.cond(bound[0] <= MAX_SAFE_SCORE_BOUND,
                      lambda: _diff_attn(False, *attn_args), lambda: _diff_attn(True, *attn_args))

    x1, h2, pos, gate_t, len_t, off_t, tot_t = _out_router(
        x2, y_ret, y_diff, w_out[0].astype(bf16), norm2_w[0].reshape(1, D),
        w_router[0].T.astype(f32), b_router[0].reshape(N_EXPERTS, 1).astype(f32))

    n_tiles = T // TILE
    run_len = len_t[:, :, 0].astype(jnp.int32)
    total = tot_t[:, 0].astype(jnp.int32)
    padded = ((total + MOE_BLOCK - 1) // MOE_BLOCK) * MOE_BLOCK
    pad_end = jnp.cumsum(padded)
    pad_start = pad_end - padded
    run_dst = pad_start[None, :] + off_t[:, :, 0].astype(jnp.int32)
    run_src = jnp.cumsum(run_len, axis=1) - run_len
    tile_rows = jnp.sum(run_len, axis=1)
    P = T * TOP_K + n_tiles * N_EXPERTS * RUN_ALIGN + N_EXPERTS * MOE_BLOCK
    n_used = (pad_end[-1:] // MOE_BLOCK).astype(jnp.int32)
    runs = (run_src.reshape(-1), run_len.reshape(-1), run_dst.reshape(-1), tile_rows)

    xs = _dispatch(*runs, pad_start + total, padded - total, n_used, pos, h2, P)
    ys = _experts(pad_start, padded // MOE_BLOCK, xs, w1[0], b1[0].reshape(N_EXPERTS, 1, 2 * D_FF),
                  w2[0], b2[0].reshape(N_EXPERTS, 1, D))
    out = _combine(*runs, gate_t, x1, ys)
    return out.reshape(B, S, D)
```

```python
import functools

import jax
import jax.numpy as jnp
from jax import lax
from jax.experimental import pallas as pl
from jax.experimental.pallas import tpu as pltpu

EPS = 1e-6
D_MODEL = 1024
RET_HEADS = 4
RET_DK = 128
RET_WIDTH = 512
RET_ROPE_THETA = 10000.0
DIFF_HEADS = 4
DIFF_DH = 64
DIFF_DV = 128
DIFF_WIDTH = 512
ROPE_THETA = 500000.0
ROT_DIM = DIFF_DH // 4
D_IN_PROJ = 3584
N_EXPERTS = 32
TOP_K = 4
D_FF = 1024
SWIGLU_LIMIT = 7.0
SWIGLU_ALPHA = 1.702
LAMBDA_INIT = 0.8 - 0.6 * 1.0

LOG2_E = 1.4426950408889634
SCORE_BOUND_SLACK = 1.02
MAX_SAFE_SCORE_BOUND = 60.0
LANES = 128
SUBLANES = 8
VMEM_LIMIT = 56 * 1024 * 1024

COL_RQ, COL_RK, COL_RV, COL_RG, COL_DQ, COL_DK, COL_DV = 0, 4, 8, 12, 16, 20, 24
VGV_RV, VGV_RG, VGV_DV = 0, 4, 8

TM_PROJ = 512
RET_CHUNK = 128
RET_UNROLL = 16
TQ_ATTN = 512
TK_ATTN = 2048
TILE = 512
MOE_BLOCK = 512
MOE_HALF = MOE_BLOCK // 2
RUN_ALIGN = SUBLANES
TILE_ROWS = TOP_K * TILE + N_EXPERTS * RUN_ALIGN
DISPATCH_BUFS = 2


def _params(sem, **kw):
    return pltpu.CompilerParams(dimension_semantics=sem, vmem_limit_bytes=VMEM_LIMIT, **kw)


def _in_proj_kernel(x_ref, nw_ref, w_ref, c2_ref, s2_ref, ra_ref, rp_ref, rn_ref, qw_ref, kw_ref,
                    vgv_ref, rqo_ref, rko_ref, qs_ref, ks_ref):
    ts = x_ref.shape[0]
    x = x_ref[...]
    hn = (x * lax.rsqrt(jnp.mean(x * x, axis=-1, keepdims=True) + EPS) * nw_ref[...]).astype(jnp.bfloat16)

    def proj(col_block):
        c0 = col_block * LANES
        return jnp.dot(hn, w_ref[:, c0:c0 + 4 * LANES], preferred_element_type=jnp.float32)

    rq, rk, dq, dk = proj(COL_RQ), proj(COL_RK), proj(COL_DQ), proj(COL_DK)
    c2 = c2_ref[...]
    s2 = s2_ref[...]
    ra = ra_ref[...]
    rp = rp_ref[...]
    rn = rn_ref[...]
    lane = lax.broadcasted_iota(jnp.int32, (ts, LANES), 1)
    lo = lane < DIFF_DH

    def qk_norm_rot(x, w):
        x2 = x * x
        s_lo = jnp.sum(jnp.where(lo, x2, 0.0), axis=-1, keepdims=True)
        s_hi = jnp.sum(jnp.where(lo, 0.0, x2), axis=-1, keepdims=True)
        ms = jnp.where(lo, s_lo, s_hi) * (1.0 / DIFF_DH)
        xn = x * lax.rsqrt(ms + EPS) * w
        return xn * ra + pltpu.roll(xn, ROT_DIM // 2, 1) * rp + pltpu.roll(xn, LANES - ROT_DIM // 2, 1) * rn

    for h in range(RET_HEADS):
        sl = slice(h * LANES, (h + 1) * LANES)
        q = rq[:, sl]
        k = rk[:, sl]
        rqo_ref[:, sl] = (q * c2 + pltpu.roll(q, RET_DK // 2, 1) * s2).astype(rqo_ref.dtype)
        rko_ref[:, sl] = ((k * c2 + pltpu.roll(k, RET_DK // 2, 1) * s2) * (RET_DK ** -0.5)).astype(rko_ref.dtype)
    for h in range(DIFF_HEADS):
        sl = slice(h * LANES, (h + 1) * LANES)
        q = qk_norm_rot(dq[:, sl], qw_ref[...]) * (DIFF_DH ** -0.5 * LOG2_E)
        k = qk_norm_rot(dk[:, sl], kw_ref[...])
        qs_ref[h, 0] = jnp.where(lo, q, 0.0).astype(qs_ref.dtype)
        qs_ref[h, 1] = jnp.where(lo, 0.0, q).astype(qs_ref.dtype)
        ks_ref[:, sl] = k.astype(ks_ref.dtype)
    for slot, col_block in enumerate((COL_RV, COL_RG, COL_DV)):
        vgv_ref[:, slot * 4 * LANES:(slot + 1) * 4 * LANES] = proj(col_block).astype(vgv_ref.dtype)


def _in_proj(x2, nw, w_bf16, tabs, qw2, kw2, B, S):
    T = B * S
    n_s = S // TM_PROJ
    tok = lambda w: pl.BlockSpec((TM_PROJ, w), lambda i: (i, 0))
    const = lambda s: pl.BlockSpec(s, lambda i: (0, 0))
    tab = pl.BlockSpec((None, TM_PROJ, LANES), lambda i: (i // n_s, i % n_s, 0))
    bf16 = jnp.bfloat16
    return pl.pallas_call(
        _in_proj_kernel,
        out_shape=(jax.ShapeDtypeStruct((T, 3 * 4 * LANES), bf16),
                   jax.ShapeDtypeStruct((T, 4 * LANES), bf16),
                   jax.ShapeDtypeStruct((T, 4 * LANES), bf16),
                   jax.ShapeDtypeStruct((B, DIFF_HEADS, 2, S, LANES), bf16),
                   jax.ShapeDtypeStruct((T, 4 * LANES), bf16)),
        grid=(T // TM_PROJ,),
        in_specs=[tok(D_MODEL), const((1, D_MODEL)), const((D_MODEL, D_IN_PROJ)),
                  tab, tab, tab, tab, tab, const((1, LANES)), const((1, LANES))],
        out_specs=(tok(3 * 4 * LANES), tok(4 * LANES), tok(4 * LANES),
                   pl.BlockSpec((None, DIFF_HEADS, 2, TM_PROJ, LANES), lambda i: (i // n_s, 0, 0, i % n_s, 0)),
                   tok(4 * LANES)),
        compiler_params=_params(("arbitrary",)),
        name="in_proj",
    )(x2, nw, w_bf16, *tabs, qw2, kw2)


def _retention_kernel(ldf_ref, ldb_ref, q_ref, k_ref, v_ref, g_ref, nw_ref, o_ref, sb_ref):
    C = RET_CHUNK
    S = q_ref.shape[0]
    n_chunks = S // C
    h = pl.program_id(1)
    ldf = ldf_ref[h]
    ldb = ldb_ref[h]
    row = lax.broadcasted_iota(jnp.int32, (C, C), 0).astype(jnp.float32)
    colm = lax.broadcasted_iota(jnp.int32, (C, C), 1).astype(jnp.float32)
    dist = row - colm
    decay = jnp.where(dist >= 0, jnp.exp(ldf * jnp.maximum(dist, 0.0)), jnp.exp(ldb * jnp.maximum(-dist, 0.0)))
    idx = lax.broadcasted_iota(jnp.int32, (C, 1), 0).astype(jnp.float32)
    q_dec_f = jnp.exp(ldf * (idx + 1.0))
    k_dec_f = jnp.exp(ldf * (C - 1.0 - idx))
    q_dec_b = jnp.exp(ldb * (C - idx))
    k_dec_b = jnp.exp(ldb * idx)
    chunk_dec_f = jnp.exp(ldf * C)
    chunk_dec_b = jnp.exp(ldb * C)
    f32 = jnp.float32
    bf16 = jnp.bfloat16

    def kv_state(k, v, k_dec):
        kd = (k.astype(f32) * k_dec).astype(bf16)
        return lax.dot_general(kd, v, (((0,), (0,)), ((), ())), preferred_element_type=f32)

    def bwd_step(i, state):
        c = n_chunks - 1 - i
        r0 = pl.multiple_of(c * C, C)
        sb_ref[c] = state
        return state * chunk_dec_b + kv_state(k_ref[pl.ds(r0, C), :], v_ref[pl.ds(r0, C), :], k_dec_b)

    lax.fori_loop(0, n_chunks, bwd_step, jnp.zeros((RET_DK, LANES), f32), unroll=RET_UNROLL)

    def fwd_step(c, state):
        r0 = pl.multiple_of(c * C, C)
        q = q_ref[pl.ds(r0, C), :]
        k = k_ref[pl.ds(r0, C), :]
        v = v_ref[pl.ds(r0, C), :]
        scores = lax.dot_general(q, k, (((1,), (1,)), ((), ())), preferred_element_type=f32) * decay
        y = jnp.dot(scores.astype(bf16), v, preferred_element_type=f32)
        qf = q.astype(f32)
        y += jnp.dot((qf * q_dec_f).astype(bf16), state.astype(bf16), preferred_element_type=f32)
        y += jnp.dot((qf * q_dec_b).astype(bf16), sb_ref[c].astype(bf16), preferred_element_type=f32)
        yn = y * lax.rsqrt(jnp.mean(y * y, axis=-1, keepdims=True) + EPS) * nw_ref[...]
        g = g_ref[pl.ds(r0, C), :].astype(f32)
        o_ref[pl.ds(r0, C), :] = (yn * (g * jax.nn.sigmoid(g))).astype(o_ref.dtype)
        return state * chunk_dec_f + kv_state(k, v, k_dec_f)

    lax.fori_loop(0, n_chunks, fwd_step, jnp.zeros((RET_DK, LANES), f32), unroll=RET_UNROLL)


def _retention(ldf, ldb, rq_r, rk_r, proj, nw, B, S):
    T = B * S
    smem = pl.BlockSpec(memory_space=pltpu.SMEM)
    seq = lambda cb: pl.BlockSpec((S, LANES), lambda b, h: (b, cb + h))
    return pl.pallas_call(
        _retention_kernel,
        out_shape=jax.ShapeDtypeStruct((T, RET_WIDTH), jnp.bfloat16),
        grid=(B, RET_HEADS),
        in_specs=[smem, smem, seq(0), seq(0), seq(VGV_RV), seq(VGV_RG),
                  pl.BlockSpec((1, LANES), lambda b, h: (0, h))],
        out_specs=seq(0),
        scratch_shapes=[pltpu.VMEM((S // RET_CHUNK, RET_DK, LANES), jnp.float32)],
        compiler_params=_params(("arbitrary", "arbitrary")),
        name="retention",
    )(ldf, ldb, rq_r, rk_r, proj, proj, nw)


def _diff_attn_kernel(online_max, bound_ref, q_ref, k_ref, v_ref, lam_ref, nw_ref, o_ref, m_ref, l_ref, acc_ref):
    tq = q_ref.shape[1]
    S = k_ref.shape[0]
    f32 = jnp.float32
    q = q_ref[...].reshape(2 * tq, LANES)
    if online_max:
        m_ref[...] = jnp.full(m_ref.shape, -jnp.inf, f32)
    l_ref[...] = jnp.zeros(l_ref.shape, f32)
    acc_ref[...] = jnp.zeros(acc_ref.shape, f32)
    n_tiles = TK_ATTN // LANES

    def kv_step(j, carry):
        r0 = pl.multiple_of(j * TK_ATTN, TK_ATTN)
        k = k_ref[pl.ds(r0, TK_ATTN), :]
        v = v_ref[pl.ds(r0, TK_ATTN), :]
        s = lax.dot_general(q, k, (((1,), (1,)), ((), ())), preferred_element_type=f32)
        tiles = [s[:, c * LANES:(c + 1) * LANES] for c in range(n_tiles)]
        if online_max:
            part = tiles[0]
            for t in tiles[1:]:
                part = jnp.maximum(part, t)
            m_prev = m_ref[...]
            shift = jnp.maximum(m_prev, jnp.max(part, axis=-1, keepdims=True))
            alpha = jnp.exp2(m_prev - shift)
            m_ref[...] = shift
        else:
            shift = bound_ref[0]
        probs = [jnp.exp2(t - shift) for t in tiles]
        psum = probs[0]
        for p in probs[1:]:
            psum = psum + p
        pv = jnp.dot(jnp.concatenate([p.astype(jnp.bfloat16) for p in probs], axis=1), v,
                     preferred_element_type=f32)
        if online_max:
            l_ref[...] = alpha * l_ref[...] + psum
            acc_ref[...] = alpha * acc_ref[...] + pv
        else:
            l_ref[...] = l_ref[...] + psum
            acc_ref[...] = acc_ref[...] + pv
        return carry

    lax.fori_loop(0, S // TK_ATTN, kv_step, 0)
    o = acc_ref[...] / jnp.sum(l_ref[...], axis=-1, keepdims=True)
    d = o[:tq] - lam_ref[...] * o[tq:]
    dn = d * lax.rsqrt(jnp.mean(d * d, axis=-1, keepdims=True) + EPS) * nw_ref[...]
    o_ref[...] = (dn * (1.0 - LAMBDA_INIT)).astype(o_ref.dtype)


def _diff_attn(online_max, bound, qs, ks, proj, lam, nw, B, S):
    T = B * S
    n_q = S // TQ_ATTN
    one = pl.BlockSpec((1, LANES), lambda b, h, i, bd: (0, 0))
    grid_spec = pltpu.PrefetchScalarGridSpec(
        num_scalar_prefetch=1,
        grid=(B, DIFF_HEADS, n_q),
        in_specs=[pl.BlockSpec((None, None, 2, TQ_ATTN, LANES), lambda b, h, i, bd: (b, h, 0, i, 0)),
                  pl.BlockSpec((S, LANES), lambda b, h, i, bd: (b, h)),
                  pl.BlockSpec((S, LANES), lambda b, h, i, bd: (b, VGV_DV + h)),
                  one, one],
        out_specs=pl.BlockSpec((TQ_ATTN, LANES), lambda b, h, i, bd: (b * n_q + i, h)),
        scratch_shapes=[pltpu.VMEM((2 * TQ_ATTN, LANES), jnp.float32)] * 3,
    )
    return pl.pallas_call(
        functools.partial(_diff_attn_kernel, online_max),
        out_shape=jax.ShapeDtypeStruct((T, DIFF_WIDTH), jnp.bfloat16),
        grid_spec=grid_spec,
        compiler_params=_params(("arbitrary", "arbitrary", "arbitrary")),
        name="diff_attn_online" if online_max else "diff_attn",
    )(bound, qs, ks, proj, lam, nw)


def _out_router_kernel(x_ref, yr_ref, yd_ref, wo_ref, n2_ref, wrt_ref, br_ref,
                       x1_ref, h2_ref, pos_ref, gate_t_ref, len_ref, off_ref, tot_ref):
    tm = x_ref.shape[0]
    f32 = jnp.float32
    bf16 = jnp.bfloat16

    @pl.when(pl.program_id(0) == 0)
    def _():
        tot_ref[...] = jnp.zeros(tot_ref.shape, f32)

    att = jnp.dot(yr_ref[...], wo_ref[:RET_WIDTH, :], preferred_element_type=f32)
    att += jnp.dot(yd_ref[...], wo_ref[RET_WIDTH:, :], preferred_element_type=f32)
    x1 = x_ref[...] + att
    x1_ref[...] = x1
    h2 = x1 * lax.rsqrt(jnp.mean(x1 * x1, axis=-1, keepdims=True) + EPS) * n2_ref[...]
    h2_ref[...] = h2.astype(h2_ref.dtype)
    logits = lax.dot_general(wrt_ref[...], h2, (((1,), (1,)), ((), ())),
                             precision=lax.Precision.HIGHEST, preferred_element_type=f32) + br_ref[...]
    e_iota = lax.broadcasted_iota(jnp.int32, (N_EXPERTS, tm), 0)
    work = logits
    vals, hots = [], []
    for _ in range(TOP_K):
        mx = jnp.max(work, axis=0, keepdims=True)
        ix = jnp.min(jnp.where(work == mx, e_iota, N_EXPERTS), axis=0, keepdims=True)
        hot = e_iota == ix
        vals.append(mx)
        hots.append(hot)
        work = jnp.where(hot, -jnp.inf, work)
    exps = [jnp.exp(v - vals[0]) for v in vals]
    denom = exps[0] + exps[1] + exps[2] + exps[3]
    gates = [e / denom for e in exps]
    sel = jnp.zeros((N_EXPERTS, tm), f32)
    for hot in hots:
        sel = jnp.where(hot, 1.0, sel)
    t_row = lax.broadcasted_iota(jnp.int32, (tm, tm), 0)
    t_col = lax.broadcasted_iota(jnp.int32, (tm, tm), 1)
    upper = jnp.where(t_row < t_col, 1.0, 0.0).astype(bf16)
    rank = jnp.dot(sel.astype(bf16), upper, preferred_element_type=f32)
    cnt = jnp.sum(sel, axis=1, keepdims=True)
    run_units = jnp.floor((cnt + (RUN_ALIGN - 1.0)) * (1.0 / RUN_ALIGN))
    run_len = jnp.broadcast_to(run_units * RUN_ALIGN, (N_EXPERTS, LANES))
    e_row = lax.broadcasted_iota(jnp.int32, (N_EXPERTS, N_EXPERTS), 0)
    e_col = lax.broadcasted_iota(jnp.int32, (N_EXPERTS, N_EXPERTS), 1)
    lower = jnp.where(e_col < e_row, 1.0, 0.0).astype(bf16)
    run_start = jnp.dot(lower, jnp.broadcast_to(run_units, (N_EXPERTS, LANES)).astype(bf16),
                        preferred_element_type=f32) * RUN_ALIGN
    pos_full = rank + run_start[:, 0:1]
    pos = [jnp.sum(jnp.where(hot, pos_full, 0.0), axis=0, keepdims=True) for hot in hots]
    for k in range(TOP_K):
        pos_ref[k:k + 1, :] = pos[k].astype(jnp.int32)
    rows = jnp.concatenate(gates + pos + [jnp.zeros((LANES - 2 * TOP_K, tm), f32)], axis=0)
    gate_t_ref[...] = rows.T
    len_ref[0] = run_len
    off_ref[0] = tot_ref[...]
    tot_ref[...] = tot_ref[...] + run_len


def _out_router(x2, y_ret, y_diff, wo_bf16, n2w, wrt, br):
    T = x2.shape[0]
    n_tiles = T // TILE
    tok = lambda w: pl.BlockSpec((TILE, w), lambda i: (i, 0))
    const = lambda s: pl.BlockSpec(s, lambda i: (0, 0))
    per_tile = pl.BlockSpec((1, N_EXPERTS, LANES), lambda i: (i, 0, 0))
    return pl.pallas_call(
        _out_router_kernel,
        out_shape=(jax.ShapeDtypeStruct((T, D_MODEL), jnp.float32),
                   jax.ShapeDtypeStruct((T, D_MODEL), jnp.bfloat16),
                   jax.ShapeDtypeStruct((TOP_K, T), jnp.int32),
                   jax.ShapeDtypeStruct((T, LANES), jnp.float32),
                   jax.ShapeDtypeStruct((n_tiles, N_EXPERTS, LANES), jnp.float32),
                   jax.ShapeDtypeStruct((n_tiles, N_EXPERTS, LANES), jnp.float32),
                   jax.ShapeDtypeStruct((N_EXPERTS, LANES), jnp.float32)),
        grid=(n_tiles,),
        in_specs=[tok(D_MODEL), tok(RET_WIDTH), tok(DIFF_WIDTH), const((D_MODEL, D_MODEL)),
                  const((1, D_MODEL)), const((N_EXPERTS, D_MODEL)), const((N_EXPERTS, 1))],
        out_specs=(tok(D_MODEL), tok(D_MODEL), pl.BlockSpec((TOP_K, TILE), lambda i: (0, i)), tok(LANES),
                   per_tile, per_tile, const((N_EXPERTS, LANES))),
        compiler_params=_params(("arbitrary",)),
        name="out_router",
    )(x2, y_ret, y_diff, wo_bf16, n2w, wrt, br)


def _pack_bf16_pairs(x):
    w = x.shape[1] // 2
    bits = lambda v: lax.bitcast_convert_type(v.astype(jnp.bfloat16).astype(jnp.float32), jnp.uint32)
    return (bits(x[:, :w]) >> 16) | (bits(x[:, w:]) & jnp.uint32(0xFFFF0000))


def _unpack_bf16_pairs(p):
    as_bf16 = lambda bits: lax.bitcast_convert_type(bits, jnp.float32).astype(jnp.bfloat16)
    return as_bf16(p << 16), as_bf16(p & jnp.uint32(0xFFFF0000))


def _run_copies(src_ref, len_ref, dst_ref, tile, make_copy):
    for e in range(N_EXPERTS):
        n = pl.multiple_of(len_ref[tile * N_EXPERTS + e], RUN_ALIGN)
        s = pl.multiple_of(src_ref[tile * N_EXPERTS + e], RUN_ALIGN)
        d = pl.multiple_of(dst_ref[tile * N_EXPERTS + e], RUN_ALIGN)

        @pl.when(n > 0)
        def _():
            make_copy(s, d, n).start()


def _dispatch_kernel(src_ref, len_ref, dst_ref, rows_ref, zlo_ref, zlen_ref, nu_ref,
                     pos_ref, h2_ref, xs_hbm, xbuf_ref, zero_ref, sems, zero_sem):
    i = pl.program_id(0)
    n_tiles = pl.num_programs(0)
    n_buf = xbuf_ref.shape[0]
    cur = i % n_buf
    n_rows, tm = xbuf_ref.shape[1], h2_ref.shape[0]

    @pl.when(i == 0)
    def _():
        zero_ref[...] = jnp.zeros(zero_ref.shape, zero_ref.dtype)

        def pad_copy(e):
            n = pl.multiple_of(zlen_ref[e], RUN_ALIGN)
            lo = pl.multiple_of(zlo_ref[e], RUN_ALIGN)
            return pltpu.make_async_copy(zero_ref.at[pl.ds(0, n)], xs_hbm.at[pl.ds(lo, n)], zero_sem)

        def tail_copy(j):
            return pltpu.make_async_copy(zero_ref, xs_hbm.at[pl.ds(j * MOE_HALF, MOE_HALF)], zero_sem)

        def guarded(copy, op):
            def body(e, c):
                @pl.when(zlen_ref[e] > 0)
                def _():
                    op(copy(e))
                return c
            return body

        lax.fori_loop(0, N_EXPERTS, guarded(pad_copy, lambda cp: cp.start()), 0)
        lax.fori_loop(0, N_EXPERTS, guarded(pad_copy, lambda cp: cp.wait()), 0)
        n_halves = xs_hbm.shape[0] // MOE_HALF
        lax.fori_loop(nu_ref[0], n_halves, lambda j, c: (tail_copy(j).start(), c)[1], 0)
        lax.fori_loop(nu_ref[0], n_halves, lambda j, c: (tail_copy(j).wait(), c)[1], 0)

    p_iota = lax.broadcasted_iota(jnp.int32, (n_rows, tm), 0)
    onehot = jnp.zeros((n_rows, tm), jnp.float32)
    for k in range(TOP_K):
        onehot = jnp.where(p_iota == pos_ref[k:k + 1, :], 1.0, onehot)
    xbuf_ref[cur] = _pack_bf16_pairs(
        jnp.dot(onehot.astype(jnp.bfloat16), h2_ref[...], preferred_element_type=jnp.float32))

    _run_copies(src_ref, len_ref, dst_ref, i,
                lambda s, d, n: pltpu.make_async_copy(xbuf_ref.at[cur, pl.ds(s, n)], xs_hbm.at[pl.ds(d, n)],
                                                      sems.at[cur]))

    def wait_tile(tile, slot):
        rows = pl.multiple_of(rows_ref[tile], RUN_ALIGN)
        pltpu.make_async_copy(xbuf_ref.at[slot, pl.ds(0, rows)], xs_hbm.at[pl.ds(0, rows)], sems.at[slot]).wait()

    oldest = n_buf - 1

    @pl.when(i >= oldest)
    def _():
        wait_tile(i - oldest, (i + 1) % n_buf)

    @pl.when(i == n_tiles - 1)
    def _():
        for back in range(oldest - 1, -1, -1):
            wait_tile(i - back, (i - back) % n_buf)


def _dispatch(run_src, run_len, run_dst, tile_rows, zero_lo, zero_len, n_used, pos, h2, P):
    T = h2.shape[0]
    n_pre = 7
    grid_spec = pltpu.PrefetchScalarGridSpec(
        num_scalar_prefetch=n_pre,
        grid=(T // TILE,),
        in_specs=[pl.BlockSpec((TOP_K, TILE), lambda i, *_: (0, i)),
                  pl.BlockSpec((TILE, D_MODEL), lambda i, *_: (i, 0))],
        out_specs=pl.BlockSpec(memory_space=pl.ANY),
        scratch_shapes=[pltpu.VMEM((DISPATCH_BUFS, TILE_ROWS, D_MODEL // 2), jnp.uint32),
                        pltpu.VMEM((MOE_HALF, D_MODEL // 2), jnp.uint32),
                        pltpu.SemaphoreType.DMA((DISPATCH_BUFS,)),
                        pltpu.SemaphoreType.DMA(())],
    )
    return pl.pallas_call(
        _dispatch_kernel,
        out_shape=jax.ShapeDtypeStruct((P, D_MODEL // 2), jnp.uint32),
        grid_spec=grid_spec,
        compiler_params=_params(("arbitrary",), has_side_effects=True),
        name="dispatch",
    )(run_src, run_len, run_dst, tile_rows, zero_lo, zero_len, n_used, pos, h2)


def _experts_kernel(base_ref, nblk_ref, half_ref, w1_ref, b1_ref, w2_ref, b2_ref, xs_hbm, ys_hbm,
                    w1b_ref, w2b_ref, xbuf_ref, ybuf_ref, xhalf_ref, yhalf_ref, in_sems, out_sems, busy_ref):
    e = pl.program_id(0)
    n = nblk_ref[e]
    has_half = half_ref[e] == 1
    HALF_BUF = 2

    def rows(expert, j):
        return pl.ds(pl.multiple_of(base_ref[expert] + j * MOE_BLOCK, MOE_HALF), MOE_BLOCK)

    def half_rows(first_row):
        return pl.ds(pl.multiple_of(first_row, MOE_HALF), MOE_HALF)

    def in_copy(expert, j, slot):
        return pltpu.make_async_copy(xs_hbm.at[rows(expert, j)], xbuf_ref.at[slot], in_sems.at[slot])

    def out_copy(j, slot):
        return pltpu.make_async_copy(ybuf_ref.at[slot], ys_hbm.at[rows(e, j)], out_sems.at[slot])

    half_row0 = base_ref[e] + n * MOE_BLOCK
    half_in = pltpu.make_async_copy(xs_hbm.at[half_rows(half_row0)], xhalf_ref, in_sems.at[HALF_BUF])

    def half_out(first_row):
        return pltpu.make_async_copy(yhalf_ref, ys_hbm.at[half_rows(first_row)], out_sems.at[HALF_BUF])

    def wait_out(buf, half=False):
        @pl.when(busy_ref[buf] == 1)
        def _():
            (half_out(0) if half else out_copy(0, buf)).wait()
            busy_ref[buf] = 0

    def mlp(x_packed):
        x = jnp.concatenate(_unpack_bf16_pairs(x_packed), axis=1)
        u = jnp.dot(x, w1b_ref[...], preferred_element_type=jnp.float32) + b1_ref[...]
        glu = jnp.minimum(u[:, :D_FF], SWIGLU_LIMIT)
        lin = jnp.clip(u[:, D_FF:], -SWIGLU_LIMIT, SWIGLU_LIMIT)
        act = glu * jax.nn.sigmoid(SWIGLU_ALPHA * glu) * (lin + 1.0)
        return _pack_bf16_pairs(jnp.dot(act.astype(jnp.bfloat16), w2b_ref[...],
                                        preferred_element_type=jnp.float32) + b2_ref[...])

    @pl.when(e == 0)
    def _():
        for buf in range(3):
            busy_ref[buf] = 0

        @pl.when(n > 0)
        def _():
            in_copy(0, 0, 0).start()

    @pl.when(has_half)
    def _():
        half_in.start()

    @pl.when((n > 0) | has_half)
    def _():
        w1b_ref[...] = w1_ref[...].astype(jnp.bfloat16)
        w2b_ref[...] = w2_ref[...].astype(jnp.bfloat16)

    def block(j, carry):
        slot = j % 2

        @pl.when(j + 1 < n)
        def _():
            in_copy(e, j + 1, 1 - slot).start()

        in_copy(e, j, slot).wait()
        wait_out(slot)
        ybuf_ref[slot] = mlp(xbuf_ref[slot])
        out_copy(j, slot).start()
        busy_ref[slot] = 1
        return carry

    lax.fori_loop(0, n, block, 0)

    @pl.when(has_half)
    def _():
        half_in.wait()
        wait_out(HALF_BUF, half=True)
        yhalf_ref[...] = mlp(xhalf_ref[...])
        half_out(half_row0).start()
        busy_ref[HALF_BUF] = 1

    e_next = jnp.minimum(e + 1, N_EXPERTS - 1)

    @pl.when((e + 1 < N_EXPERTS) & (nblk_ref[e_next] > 0))
    def _():
        in_copy(e_next, 0, 0).start()

    @pl.when(e == N_EXPERTS - 1)
    def _():
        wait_out(0)
        wait_out(1)
        wait_out(HALF_BUF, half=True)
        yhalf_ref[...] = jnp.zeros(yhalf_ref.shape, yhalf_ref.dtype)
        first_unused = (half_row0 + half_ref[e] * MOE_HALF) // MOE_HALF
        n_halves = ys_hbm.shape[0] // MOE_HALF
        lax.fori_loop(first_unused, n_halves, lambda j, c: (half_out(j * MOE_HALF).start(), c)[1], 0)
        lax.fori_loop(first_unused, n_halves, lambda j, c: (half_out(j * MOE_HALF).wait(), c)[1], 0)


def _experts(base, n_blk, n_half, xs, w1, b1, w2, b2):
    P = xs.shape[0]
    expert = lambda e, bs, nb, nh: (e, 0, 0)
    grid_spec = pltpu.PrefetchScalarGridSpec(
        num_scalar_prefetch=3,
        grid=(N_EXPERTS,),
        in_specs=[pl.BlockSpec((None, D_MODEL, 2 * D_FF), expert),
                  pl.BlockSpec((None, 1, 2 * D_FF), expert),
                  pl.BlockSpec((None, D_FF, D_MODEL), expert),
                  pl.BlockSpec((None, 1, D_MODEL), expert),
                  pl.BlockSpec(memory_space=pl.ANY)],
        out_specs=pl.BlockSpec(memory_space=pl.ANY),
        scratch_shapes=[pltpu.VMEM((D_MODEL, 2 * D_FF), jnp.bfloat16),
                        pltpu.VMEM((D_FF, D_MODEL), jnp.bfloat16),
                        pltpu.VMEM((2, MOE_BLOCK, D_MODEL // 2), jnp.uint32),
                        pltpu.VMEM((2, MOE_BLOCK, D_MODEL // 2), jnp.uint32),
                        pltpu.VMEM((MOE_HALF, D_MODEL // 2), jnp.uint32),
                        pltpu.VMEM((MOE_HALF, D_MODEL // 2), jnp.uint32),
                        pltpu.SemaphoreType.DMA((3,)),
                        pltpu.SemaphoreType.DMA((3,)),
                        pltpu.SMEM((3,), jnp.int32)],
    )
    return pl.pallas_call(
        _experts_kernel,
        out_shape=jax.ShapeDtypeStruct((P, D_MODEL // 2), jnp.uint32),
        grid_spec=grid_spec,
        compiler_params=_params(("arbitrary",)),
        name="experts",
    )(base, n_blk, n_half, w1, b1, w2, b2, xs)


def _combine_kernel(src_ref, len_ref, dst_ref, rows_ref, gate_t_ref, x1_ref, ys_hbm, o_ref, ybuf_ref, sems):
    i = pl.program_id(0)
    n_tiles = pl.num_programs(0)
    n_buf = ybuf_ref.shape[0]
    cur = i % n_buf
    n_rows, tm = ybuf_ref.shape[1], x1_ref.shape[0]

    def fetch(tile, slot):
        _run_copies(src_ref, len_ref, dst_ref, tile,
                    lambda s, d, n: pltpu.make_async_copy(ys_hbm.at[pl.ds(d, n)], ybuf_ref.at[slot, pl.ds(s, n)],
                                                          sems.at[slot]))

    @pl.when(i == 0)
    def _():
        ybuf_ref[...] = jnp.zeros(ybuf_ref.shape, ybuf_ref.dtype)
        for tile in range(n_buf - 1):
            fetch(tile, tile)

    ahead = i + n_buf - 1

    @pl.when(ahead < n_tiles)
    def _():
        fetch(ahead, ahead % n_buf)

    g = gate_t_ref[...]
    p_iota = lax.broadcasted_iota(jnp.int32, (tm, n_rows), 1)
    weights = jnp.zeros((tm, n_rows), jnp.float32)
    for k in range(TOP_K):
        pos_k = g[:, TOP_K + k:TOP_K + k + 1].astype(jnp.int32)
        weights = jnp.where(p_iota == pos_k, g[:, k:k + 1], weights)
    weights = weights.astype(jnp.bfloat16)

    rows = pl.multiple_of(rows_ref[i], RUN_ALIGN)
    pltpu.make_async_copy(ys_hbm.at[pl.ds(0, rows)], ybuf_ref.at[cur, pl.ds(0, rows)], sems.at[cur]).wait()
    halves = [jnp.dot(weights, y, preferred_element_type=jnp.float32) for y in _unpack_bf16_pairs(ybuf_ref[cur])]
    o_ref[...] = x1_ref[...] + jnp.concatenate(halves, axis=1)


def _combine(run_src, run_len, run_dst, tile_rows, gate_t, x1, ys):
    T = x1.shape[0]
    tok = lambda w: pl.BlockSpec((TILE, w), lambda i, *_: (i, 0))
    grid_spec = pltpu.PrefetchScalarGridSpec(
        num_scalar_prefetch=4,
        grid=(T // TILE,),
        in_specs=[tok(LANES), tok(D_MODEL), pl.BlockSpec(memory_space=pl.ANY)],
        out_specs=tok(D_MODEL),
        scratch_shapes=[pltpu.VMEM((DISPATCH_BUFS, TILE_ROWS, D_MODEL // 2), jnp.uint32),
                        pltpu.SemaphoreType.DMA((DISPATCH_BUFS,))],
    )
    return pl.pallas_call(
        _combine_kernel,
        out_shape=jax.ShapeDtypeStruct((T, D_MODEL), jnp.float32),
        grid_spec=grid_spec,
        compiler_params=_params(("arbitrary",)),
        name="combine",
    )(run_src, run_len, run_dst, tile_rows, gate_t, x1, ys)


def _rotary_tables(positions):
    pos = positions.astype(jnp.float32)[..., None]
    lane = jnp.arange(LANES)
    half_r = RET_DK // 2
    inv_r = RET_ROPE_THETA ** (-jnp.linspace(0.0, 1.0, half_r, dtype=jnp.float32))
    ang = pos * inv_r[lane % half_r]
    c2 = jnp.cos(ang)
    s2 = jnp.sin(ang) * jnp.where(lane < half_r, -1.0, 1.0)
    half_d = ROT_DIM // 2
    inv_d = ROPE_THETA ** (-jnp.arange(0, ROT_DIM, 2, dtype=jnp.float32) / ROT_DIM)
    sub = lane % DIFF_DH
    ang_d = pos * inv_d[sub % half_d]
    cd, sd = jnp.cos(ang_d), jnp.sin(ang_d)
    ra = jnp.where(sub < ROT_DIM, cd, 1.0)
    rp = jnp.where((sub >= half_d) & (sub < ROT_DIM), sd, 0.0)
    rn = jnp.where(sub < half_d, -sd, 0.0)
    return c2, s2, ra, rp, rn


def kernel(x, positions, norm1_w, w_in, ret_log_decay_fwd, ret_log_decay_bwd, ret_norm_w, q_norm_w, k_norm_w, lambda_q1, lambda_k1, lambda_q2, lambda_k2, diff_norm_w, w_out, norm2_w, w_router, b_router, w1, b1, w2, b2):
    B, S, D = x.shape
    T = B * S
    f32 = jnp.float32
    bf16 = jnp.bfloat16
    x2 = x.reshape(T, D)

    dup = lambda w: jnp.concatenate([w, w]).reshape(1, LANES).astype(f32)
    proj, rq_r, rk_r, qs, ks = _in_proj(x2, norm1_w[0].reshape(1, D), w_in[0].astype(bf16),
                                        _rotary_tables(positions), dup(q_norm_w[0]), dup(k_norm_w[0]), B, S)

    y_ret = _retention(ret_log_decay_fwd[0].astype(f32), ret_log_decay_bwd[0].astype(f32),
                       rq_r, rk_r, proj, ret_norm_w[0].reshape(1, RET_WIDTH).astype(f32), B, S)

    lam = (jnp.exp(jnp.sum(lambda_q1[0].astype(f32) * lambda_k1[0].astype(f32)))
           - jnp.exp(jnp.sum(lambda_q2[0].astype(f32) * lambda_k2[0].astype(f32))) + LAMBDA_INIT)
    lam_row = jnp.full((1, LANES), lam, f32)
    bound = (SCORE_BOUND_SLACK * DIFF_DH ** 0.5 * LOG2_E
             * jnp.max(jnp.abs(q_norm_w[0].astype(f32))) * jnp.max(jnp.abs(k_norm_w[0].astype(f32)))).reshape(1)
    attn_args = (bound, qs, ks, proj, lam_row, diff_norm_w[0].reshape(1, DIFF_DV).astype(f32), B, S)
    y_diff = lax.cond(bound[0] <= MAX_SAFE_SCORE_BOUND,
                      lambda: _diff_attn(False, *attn_args), lambda: _diff_attn(True, *attn_args))

    x1, h2, pos, gate_t, len_t, off_t, tot_t = _out_router(
        x2, y_ret, y_diff, w_out[0].astype(bf16), norm2_w[0].reshape(1, D),
        w_router[0].T.astype(f32), b_router[0].reshape(N_EXPERTS, 1).astype(f32))

    n_tiles = T // TILE
    run_len = len_t[:, :, 0].astype(jnp.int32)
    total = tot_t[:, 0].astype(jnp.int32)
    padded = ((total + MOE_HALF - 1) // MOE_HALF) * MOE_HALF
    pad_end = jnp.cumsum(padded)
    pad_start = pad_end - padded
    run_dst = pad_start[None, :] + off_t[:, :, 0].astype(jnp.int32)
    run_src = jnp.cumsum(run_len, axis=1) - run_len
    tile_rows = jnp.sum(run_len, axis=1)
    P = T * TOP_K + n_tiles * N_EXPERTS * RUN_ALIGN + N_EXPERTS * MOE_HALF
    n_used = (pad_end[-1:] // MOE_HALF).astype(jnp.int32)
    runs = (run_src.reshape(-1), run_len.reshape(-1), run_dst.reshape(-1), tile_rows)

    xs = _dispatch(*runs, pad_start + total, padded - total, n_used, pos, h2, P)
    ys = _experts(pad_start, padded // MOE_BLOCK, (padded // MOE_HALF) % 2, xs, w1[0],
                  b1[0].reshape(N_EXPERTS, 1, 2 * D_FF), w2[0], b2[0].reshape(N_EXPERTS, 1, D))
    out = _combine(*runs, gate_t, x1, ys)
    return out.reshape(B, S, D)
```

```python
import functools

import jax
import jax.numpy as jnp
from jax import lax
from jax.experimental import pallas as pl
from jax.experimental.pallas import tpu as pltpu

EPS = 1e-6
D_MODEL = 1024
RET_HEADS = 4
RET_DK = 128
RET_WIDTH = 512
RET_ROPE_THETA = 10000.0
DIFF_HEADS = 4
DIFF_DH = 64
DIFF_DV = 128
DIFF_WIDTH = 512
ROPE_THETA = 500000.0
ROT_DIM = DIFF_DH // 4
D_IN_PROJ = 3584
N_EXPERTS = 32
TOP_K = 4
D_FF = 1024
SWIGLU_LIMIT = 7.0
SWIGLU_ALPHA = 1.702
LAMBDA_INIT = 0.8 - 0.6 * 1.0

LOG2_E = 1.4426950408889634
SCORE_BOUND_SLACK = 1.02
MAX_SAFE_SCORE_BOUND = 60.0
LANES = 128
SUBLANES = 8
VMEM_LIMIT = 56 * 1024 * 1024

COL_RQ, COL_RK, COL_RV, COL_RG, COL_DQ, COL_DK, COL_DV = 0, 4, 8, 12, 16, 20, 24
VGV_RV, VGV_RG, VGV_DV = 0, 4, 8

TM_PROJ = 512
RET_CHUNK = 128
RET_UNROLL = 16
TQ_ATTN = 512
TK_ATTN = 2048
TILE = 512
MOE_BLOCK = 512
MOE_HALF = MOE_BLOCK // 2
RUN_ALIGN = SUBLANES
TILE_ROWS = TOP_K * TILE + N_EXPERTS * RUN_ALIGN
DISPATCH_BUFS = 2


def _params(sem, **kw):
    return pltpu.CompilerParams(dimension_semantics=sem, vmem_limit_bytes=VMEM_LIMIT, **kw)


def _in_proj_kernel(x_ref, nw_ref, w_ref, ret_cs_ref, diff_cs_ref, qw_ref, kw_ref,
                    vgv_ref, rqo_ref, rko_ref, qs_ref, ks_ref):
    ts = x_ref.shape[0]
    x = x_ref[...]
    hn = (x * lax.rsqrt(jnp.mean(x * x, axis=-1, keepdims=True) + EPS) * nw_ref[...]).astype(jnp.bfloat16)

    def proj(col_block):
        c0 = col_block * LANES
        return jnp.dot(hn, w_ref[:, c0:c0 + 4 * LANES], preferred_element_type=jnp.float32)

    dq, dk, rq, rk = proj(COL_DQ), proj(COL_DK), proj(COL_RQ), proj(COL_RK)
    lane = lax.broadcasted_iota(jnp.int32, (ts, LANES), 1)
    lo = lane < DIFF_DH
    ret_cs = ret_cs_ref[...]
    ret_sc = pltpu.roll(ret_cs, RET_DK // 2, 1)
    first_half = lane < RET_DK // 2
    c2 = jnp.where(first_half, ret_cs, ret_sc)
    s2 = jnp.where(first_half, -ret_sc, ret_cs)
    sub = lane % DIFF_DH
    sin_lanes = (sub >= ROT_DIM // 2) & (sub < ROT_DIM)
    diff_cs = diff_cs_ref[...]
    ra = jnp.where(sin_lanes, pltpu.roll(diff_cs, ROT_DIM // 2, 1), diff_cs)
    rp = jnp.where(sin_lanes, diff_cs, 0.0)
    rn = jnp.where(sub < ROT_DIM // 2, -pltpu.roll(diff_cs, LANES - ROT_DIM // 2, 1), 0.0)

    def qk_norm_rot(x, w):
        x2 = x * x
        s_lo = jnp.sum(jnp.where(lo, x2, 0.0), axis=-1, keepdims=True)
        s_hi = jnp.sum(jnp.where(lo, 0.0, x2), axis=-1, keepdims=True)
        ms = jnp.where(lo, s_lo, s_hi) * (1.0 / DIFF_DH)
        xn = x * lax.rsqrt(ms + EPS) * w
        return xn * ra + pltpu.roll(xn, ROT_DIM // 2, 1) * rp + pltpu.roll(xn, LANES - ROT_DIM // 2, 1) * rn

    for h in range(DIFF_HEADS):
        sl = slice(h * LANES, (h + 1) * LANES)
        q = qk_norm_rot(dq[:, sl], qw_ref[...]) * (DIFF_DH ** -0.5 * LOG2_E)
        k = qk_norm_rot(dk[:, sl], kw_ref[...])
        qs_ref[h, 0] = jnp.where(lo, q, 0.0).astype(qs_ref.dtype)
        qs_ref[h, 1] = jnp.where(lo, 0.0, q).astype(qs_ref.dtype)
        ks_ref[:, sl] = k.astype(ks_ref.dtype)
    for h in range(RET_HEADS):
        sl = slice(h * LANES, (h + 1) * LANES)
        q = rq[:, sl]
        k = rk[:, sl]
        rqo_ref[:, sl] = (q * c2 + pltpu.roll(q, RET_DK // 2, 1) * s2).astype(rqo_ref.dtype)
        rko_ref[:, sl] = ((k * c2 + pltpu.roll(k, RET_DK // 2, 1) * s2) * (RET_DK ** -0.5)).astype(rko_ref.dtype)
    for slot, col_block in enumerate((COL_RV, COL_RG, COL_DV)):
        vgv_ref[:, slot * 4 * LANES:(slot + 1) * 4 * LANES] = proj(col_block).astype(vgv_ref.dtype)


def _in_proj(x2, nw, w_bf16, tabs, qw2, kw2, B, S):
    T = B * S
    n_s = S // TM_PROJ
    tok = lambda w: pl.BlockSpec((TM_PROJ, w), lambda i: (i, 0))
    const = lambda s: pl.BlockSpec(s, lambda i: (0, 0))
    tab = pl.BlockSpec((None, TM_PROJ, LANES), lambda i: (i // n_s, i % n_s, 0))
    bf16 = jnp.bfloat16
    return pl.pallas_call(
        _in_proj_kernel,
        out_shape=(jax.ShapeDtypeStruct((T, 3 * 4 * LANES), bf16),
                   jax.ShapeDtypeStruct((T, 4 * LANES), bf16),
                   jax.ShapeDtypeStruct((T, 4 * LANES), bf16),
                   jax.ShapeDtypeStruct((B, DIFF_HEADS, 2, S, LANES), bf16),
                   jax.ShapeDtypeStruct((T, 4 * LANES), bf16)),
        grid=(T // TM_PROJ,),
        in_specs=[tok(D_MODEL), const((1, D_MODEL)), const((D_MODEL, D_IN_PROJ)),
                  tab, tab, const((1, LANES)), const((1, LANES))],
        out_specs=(tok(3 * 4 * LANES), tok(4 * LANES), tok(4 * LANES),
                   pl.BlockSpec((None, DIFF_HEADS, 2, TM_PROJ, LANES), lambda i: (i // n_s, 0, 0, i % n_s, 0)),
                   tok(4 * LANES)),
        compiler_params=_params(("arbitrary",)),
        name="in_proj",
    )(x2, nw, w_bf16, *tabs, qw2, kw2)


def _retention_kernel(ldf_ref, ldb_ref, q_ref, k_ref, v_ref, g_ref, nw_ref, o_ref, sb_ref):
    C = RET_CHUNK
    S = q_ref.shape[0]
    n_chunks = S // C
    h = pl.program_id(1)
    ldf = ldf_ref[h]
    ldb = ldb_ref[h]
    row = lax.broadcasted_iota(jnp.int32, (C, C), 0).astype(jnp.float32)
    colm = lax.broadcasted_iota(jnp.int32, (C, C), 1).astype(jnp.float32)
    dist = row - colm
    decay = jnp.where(dist >= 0, jnp.exp(ldf * jnp.maximum(dist, 0.0)), jnp.exp(ldb * jnp.maximum(-dist, 0.0)))
    idx = lax.broadcasted_iota(jnp.int32, (C, 1), 0).astype(jnp.float32)
    q_dec_f = jnp.exp(ldf * (idx + 1.0))
    k_dec_f = jnp.exp(ldf * (C - 1.0 - idx))
    q_dec_b = jnp.exp(ldb * (C - idx))
    k_dec_b = jnp.exp(ldb * idx)
    chunk_dec_f = jnp.exp(ldf * C)
    chunk_dec_b = jnp.exp(ldb * C)
    f32 = jnp.float32
    bf16 = jnp.bfloat16

    def kv_state(k, v, k_dec):
        kd = (k.astype(f32) * k_dec).astype(bf16)
        return lax.dot_general(kd, v, (((0,), (0,)), ((), ())), preferred_element_type=f32)

    def bwd_step(i, state):
        c = n_chunks - 1 - i
        r0 = pl.multiple_of(c * C, C)
        sb_ref[c] = state
        return state * chunk_dec_b + kv_state(k_ref[pl.ds(r0, C), :], v_ref[pl.ds(r0, C), :], k_dec_b)

    lax.fori_loop(0, n_chunks, bwd_step, jnp.zeros((RET_DK, LANES), f32), unroll=RET_UNROLL)

    def fwd_step(c, state):
        r0 = pl.multiple_of(c * C, C)
        q = q_ref[pl.ds(r0, C), :]
        k = k_ref[pl.ds(r0, C), :]
        v = v_ref[pl.ds(r0, C), :]
        scores = lax.dot_general(q, k, (((1,), (1,)), ((), ())), preferred_element_type=f32) * decay
        y = jnp.dot(scores.astype(bf16), v, preferred_element_type=f32)
        qf = q.astype(f32)
        y += jnp.dot((qf * q_dec_f).astype(bf16), state.astype(bf16), preferred_element_type=f32)
        y += jnp.dot((qf * q_dec_b).astype(bf16), sb_ref[c].astype(bf16), preferred_element_type=f32)
        yn = y * lax.rsqrt(jnp.mean(y * y, axis=-1, keepdims=True) + EPS) * nw_ref[...]
        g = g_ref[pl.ds(r0, C), :].astype(f32)
        o_ref[pl.ds(r0, C), :] = (yn * (g * jax.nn.sigmoid(g))).astype(o_ref.dtype)
        return state * chunk_dec_f + kv_state(k, v, k_dec_f)

    lax.fori_loop(0, n_chunks, fwd_step, jnp.zeros((RET_DK, LANES), f32), unroll=RET_UNROLL)


def _retention(ldf, ldb, rq_r, rk_r, proj, nw, B, S):
    T = B * S
    smem = pl.BlockSpec(memory_space=pltpu.SMEM)
    seq = lambda cb: pl.BlockSpec((S, LANES), lambda b, h: (b, cb + h))
    return pl.pallas_call(
        _retention_kernel,
        out_shape=jax.ShapeDtypeStruct((T, RET_WIDTH), jnp.bfloat16),
        grid=(B, RET_HEADS),
        in_specs=[smem, smem, seq(0), seq(0), seq(VGV_RV), seq(VGV_RG),
                  pl.BlockSpec((1, LANES), lambda b, h: (0, h))],
        out_specs=seq(0),
        scratch_shapes=[pltpu.VMEM((S // RET_CHUNK, RET_DK, LANES), jnp.float32)],
        compiler_params=_params(("arbitrary", "arbitrary")),
        name="retention",
    )(ldf, ldb, rq_r, rk_r, proj, proj, nw)


def _diff_attn_kernel(online_max, bound_ref, q_ref, k_ref, v_ref, lam_ref, nw_ref, o_ref, m_ref, l_ref, acc_ref):
    tq = q_ref.shape[1]
    S = k_ref.shape[0]
    f32 = jnp.float32
    q = q_ref[...].reshape(2 * tq, LANES)
    if online_max:
        m_ref[...] = jnp.full(m_ref.shape, -jnp.inf, f32)
    l_ref[...] = jnp.zeros(l_ref.shape, f32)
    acc_ref[...] = jnp.zeros(acc_ref.shape, f32)
    n_tiles = TK_ATTN // LANES

    def kv_step(j, carry):
        r0 = pl.multiple_of(j * TK_ATTN, TK_ATTN)
        k = k_ref[pl.ds(r0, TK_ATTN), :]
        v = v_ref[pl.ds(r0, TK_ATTN), :]
        s = lax.dot_general(q, k, (((1,), (1,)), ((), ())), preferred_element_type=f32)
        tiles = [s[:, c * LANES:(c + 1) * LANES] for c in range(n_tiles)]
        if online_max:
            part = tiles[0]
            for t in tiles[1:]:
                part = jnp.maximum(part, t)
            m_prev = m_ref[...]
            shift = jnp.maximum(m_prev, jnp.max(part, axis=-1, keepdims=True))
            alpha = jnp.exp2(m_prev - shift)
            m_ref[...] = shift
        else:
            shift = bound_ref[0]
        probs = [jnp.exp2(t - shift) for t in tiles]
        psum = probs[0]
        for p in probs[1:]:
            psum = psum + p
        pv = jnp.dot(jnp.concatenate([p.astype(jnp.bfloat16) for p in probs], axis=1), v,
                     preferred_element_type=f32)
        if online_max:
            l_ref[...] = alpha * l_ref[...] + psum
            acc_ref[...] = alpha * acc_ref[...] + pv
        else:
            l_ref[...] = l_ref[...] + psum
            acc_ref[...] = acc_ref[...] + pv
        return carry

    lax.fori_loop(0, S // TK_ATTN, kv_step, 0)
    o = acc_ref[...] / jnp.sum(l_ref[...], axis=-1, keepdims=True)
    d = o[:tq] - lam_ref[...] * o[tq:]
    dn = d * lax.rsqrt(jnp.mean(d * d, axis=-1, keepdims=True) + EPS) * nw_ref[...]
    o_ref[...] = (dn * (1.0 - LAMBDA_INIT)).astype(o_ref.dtype)


def _diff_attn(online_max, bound, qs, ks, proj, lam, nw, B, S):
    T = B * S
    n_q = S // TQ_ATTN
    one = pl.BlockSpec((1, LANES), lambda b, h, i, bd: (0, 0))
    grid_spec = pltpu.PrefetchScalarGridSpec(
        num_scalar_prefetch=1,
        grid=(B, DIFF_HEADS, n_q),
        in_specs=[pl.BlockSpec((None, None, 2, TQ_ATTN, LANES), lambda b, h, i, bd: (b, h, 0, i, 0)),
                  pl.BlockSpec((S, LANES), lambda b, h, i, bd: (b, h)),
                  pl.BlockSpec((S, LANES), lambda b, h, i, bd: (b, VGV_DV + h)),
                  one, one],
        out_specs=pl.BlockSpec((TQ_ATTN, LANES), lambda b, h, i, bd: (b * n_q + i, h)),
        scratch_shapes=[pltpu.VMEM((2 * TQ_ATTN, LANES), jnp.float32)] * 3,
    )
    return pl.pallas_call(
        functools.partial(_diff_attn_kernel, online_max),
        out_shape=jax.ShapeDtypeStruct((T, DIFF_WIDTH), jnp.bfloat16),
        grid_spec=grid_spec,
        compiler_params=_params(("arbitrary", "arbitrary", "arbitrary")),
        name="diff_attn_online" if online_max else "diff_attn",
    )(bound, qs, ks, proj, lam, nw)


def _out_router_kernel(x_ref, yr_ref, yd_ref, wo_ref, n2_ref, wrt_ref, br_ref,
                       x1_ref, h2_ref, pos_ref, gate_t_ref, len_ref, off_ref, tot_ref):
    tm = x_ref.shape[0]
    f32 = jnp.float32
    bf16 = jnp.bfloat16

    @pl.when(pl.program_id(0) == 0)
    def _():
        tot_ref[...] = jnp.zeros(tot_ref.shape, f32)

    att = jnp.dot(yr_ref[...], wo_ref[:RET_WIDTH, :], preferred_element_type=f32)
    att += jnp.dot(yd_ref[...], wo_ref[RET_WIDTH:, :], preferred_element_type=f32)
    x1 = x_ref[...] + att
    x1_ref[...] = x1
    h2 = x1 * lax.rsqrt(jnp.mean(x1 * x1, axis=-1, keepdims=True) + EPS) * n2_ref[...]
    h2_ref[...] = h2.astype(h2_ref.dtype)
    logits = lax.dot_general(wrt_ref[...], h2, (((1,), (1,)), ((), ())),
                             precision=lax.Precision.HIGHEST, preferred_element_type=f32) + br_ref[...]
    e_iota = lax.broadcasted_iota(jnp.int32, (N_EXPERTS, tm), 0)
    work = logits
    vals, hots = [], []
    for _ in range(TOP_K):
        mx = jnp.max(work, axis=0, keepdims=True)
        ix = jnp.min(jnp.where(work == mx, e_iota, N_EXPERTS), axis=0, keepdims=True)
        hot = e_iota == ix
        vals.append(mx)
        hots.append(hot)
        work = jnp.where(hot, -jnp.inf, work)
    exps = [jnp.exp(v - vals[0]) for v in vals]
    denom = exps[0] + exps[1] + exps[2] + exps[3]
    gates = [e / denom for e in exps]
    sel = jnp.zeros((N_EXPERTS, tm), f32)
    for hot in hots:
        sel = jnp.where(hot, 1.0, sel)
    t_row = lax.broadcasted_iota(jnp.int32, (tm, tm), 0)
    t_col = lax.broadcasted_iota(jnp.int32, (tm, tm), 1)
    upper = jnp.where(t_row < t_col, 1.0, 0.0).astype(bf16)
    rank = jnp.dot(sel.astype(bf16), upper, preferred_element_type=f32)
    cnt = jnp.sum(sel, axis=1, keepdims=True)
    run_units = jnp.floor((cnt + (RUN_ALIGN - 1.0)) * (1.0 / RUN_ALIGN))
    run_len = jnp.broadcast_to(run_units * RUN_ALIGN, (N_EXPERTS, LANES))
    e_row = lax.broadcasted_iota(jnp.int32, (N_EXPERTS, N_EXPERTS), 0)
    e_col = lax.broadcasted_iota(jnp.int32, (N_EXPERTS, N_EXPERTS), 1)
    lower = jnp.where(e_col < e_row, 1.0, 0.0).astype(bf16)
    run_start = jnp.dot(lower, jnp.broadcast_to(run_units, (N_EXPERTS, LANES)).astype(bf16),
                        preferred_element_type=f32) * RUN_ALIGN
    pos_full = rank + run_start[:, 0:1]
    pos = [jnp.sum(jnp.where(hot, pos_full, 0.0), axis=0, keepdims=True) for hot in hots]
    for k in range(TOP_K):
        pos_ref[k:k + 1, :] = pos[k].astype(jnp.int32)
    rows = jnp.concatenate(gates + pos + [jnp.zeros((LANES - 2 * TOP_K, tm), f32)], axis=0)
    gate_t_ref[...] = rows.T
    len_ref[0] = run_len
    off_ref[0] = tot_ref[...]
    tot_ref[...] = tot_ref[...] + run_len


def _out_router(x2, y_ret, y_diff, wo_bf16, n2w, wrt, br):
    T = x2.shape[0]
    n_tiles = T // TILE
    tok = lambda w: pl.BlockSpec((TILE, w), lambda i: (i, 0))
    const = lambda s: pl.BlockSpec(s, lambda i: (0, 0))
    per_tile = pl.BlockSpec((1, N_EXPERTS, LANES), lambda i: (i, 0, 0))
    return pl.pallas_call(
        _out_router_kernel,
        out_shape=(jax.ShapeDtypeStruct((T, D_MODEL), jnp.float32),
                   jax.ShapeDtypeStruct((T, D_MODEL), jnp.bfloat16),
                   jax.ShapeDtypeStruct((TOP_K, T), jnp.int32),
                   jax.ShapeDtypeStruct((T, LANES), jnp.float32),
                   jax.ShapeDtypeStruct((n_tiles, N_EXPERTS, LANES), jnp.float32),
                   jax.ShapeDtypeStruct((n_tiles, N_EXPERTS, LANES), jnp.float32),
                   jax.ShapeDtypeStruct((N_EXPERTS, LANES), jnp.float32)),
        grid=(n_tiles,),
        in_specs=[tok(D_MODEL), tok(RET_WIDTH), tok(DIFF_WIDTH), const((D_MODEL, D_MODEL)),
                  const((1, D_MODEL)), const((N_EXPERTS, D_MODEL)), const((N_EXPERTS, 1))],
        out_specs=(tok(D_MODEL), tok(D_MODEL), pl.BlockSpec((TOP_K, TILE), lambda i: (0, i)), tok(LANES),
                   per_tile, per_tile, const((N_EXPERTS, LANES))),
        compiler_params=_params(("arbitrary",)),
        name="out_router",
    )(x2, y_ret, y_diff, wo_bf16, n2w, wrt, br)


def _pack_bf16_pairs(x):
    w = x.shape[1] // 2
    bits = lambda v: lax.bitcast_convert_type(v.astype(jnp.bfloat16).astype(jnp.float32), jnp.uint32)
    return (bits(x[:, :w]) >> 16) | (bits(x[:, w:]) & jnp.uint32(0xFFFF0000))


def _unpack_bf16_pairs(p):
    as_bf16 = lambda bits: lax.bitcast_convert_type(bits, jnp.float32).astype(jnp.bfloat16)
    return as_bf16(p << 16), as_bf16(p & jnp.uint32(0xFFFF0000))


def _run_copies(src_ref, len_ref, dst_ref, tile, make_copy):
    for e in range(N_EXPERTS):
        n = pl.multiple_of(len_ref[tile * N_EXPERTS + e], RUN_ALIGN)
        s = pl.multiple_of(src_ref[tile * N_EXPERTS + e], RUN_ALIGN)
        d = pl.multiple_of(dst_ref[tile * N_EXPERTS + e], RUN_ALIGN)

        @pl.when(n > 0)
        def _():
            make_copy(s, d, n).start()


def _dispatch_kernel(src_ref, len_ref, dst_ref, rows_ref, zlo_ref, zlen_ref, nu_ref,
                     pos_ref, h2_ref, xs_hbm, xbuf_ref, zero_ref, sems, zero_sem):
    i = pl.program_id(0)
    n_tiles = pl.num_programs(0)
    n_buf = xbuf_ref.shape[0]
    cur = i % n_buf
    n_rows, tm = xbuf_ref.shape[1], h2_ref.shape[0]

    @pl.when(i == 0)
    def _():
        zero_ref[...] = jnp.zeros(zero_ref.shape, zero_ref.dtype)

        def pad_copy(e):
            n = pl.multiple_of(zlen_ref[e], RUN_ALIGN)
            lo = pl.multiple_of(zlo_ref[e], RUN_ALIGN)
            return pltpu.make_async_copy(zero_ref.at[pl.ds(0, n)], xs_hbm.at[pl.ds(lo, n)], zero_sem)

        def tail_copy(j):
            return pltpu.make_async_copy(zero_ref, xs_hbm.at[pl.ds(j * MOE_HALF, MOE_HALF)], zero_sem)

        def guarded(copy, op):
            def body(e, c):
                @pl.when(zlen_ref[e] > 0)
                def _():
                    op(copy(e))
                return c
            return body

        lax.fori_loop(0, N_EXPERTS, guarded(pad_copy, lambda cp: cp.start()), 0)
        lax.fori_loop(0, N_EXPERTS, guarded(pad_copy, lambda cp: cp.wait()), 0)
        n_halves = xs_hbm.shape[0] // MOE_HALF
        lax.fori_loop(nu_ref[0], n_halves, lambda j, c: (tail_copy(j).start(), c)[1], 0)
        lax.fori_loop(nu_ref[0], n_halves, lambda j, c: (tail_copy(j).wait(), c)[1], 0)

    p_iota = lax.broadcasted_iota(jnp.int32, (n_rows, tm), 0)
    onehot = jnp.zeros((n_rows, tm), jnp.float32)
    for k in range(TOP_K):
        onehot = jnp.where(p_iota == pos_ref[k:k + 1, :], 1.0, onehot)
    xbuf_ref[cur] = _pack_bf16_pairs(
        jnp.dot(onehot.astype(jnp.bfloat16), h2_ref[...], preferred_element_type=jnp.float32))

    _run_copies(src_ref, len_ref, dst_ref, i,
                lambda s, d, n: pltpu.make_async_copy(xbuf_ref.at[cur, pl.ds(s, n)], xs_hbm.at[pl.ds(d, n)],
                                                      sems.at[cur]))

    def wait_tile(tile, slot):
        rows = pl.multiple_of(rows_ref[tile], RUN_ALIGN)
        pltpu.make_async_copy(xbuf_ref.at[slot, pl.ds(0, rows)], xs_hbm.at[pl.ds(0, rows)], sems.at[slot]).wait()

    oldest = n_buf - 1

    @pl.when(i >= oldest)
    def _():
        wait_tile(i - oldest, (i + 1) % n_buf)

    @pl.when(i == n_tiles - 1)
    def _():
        for back in range(oldest - 1, -1, -1):
            wait_tile(i - back, (i - back) % n_buf)


def _dispatch(run_src, run_len, run_dst, tile_rows, zero_lo, zero_len, n_used, pos, h2, P):
    T = h2.shape[0]
    n_pre = 7
    grid_spec = pltpu.PrefetchScalarGridSpec(
        num_scalar_prefetch=n_pre,
        grid=(T // TILE,),
        in_specs=[pl.BlockSpec((TOP_K, TILE), lambda i, *_: (0, i)),
                  pl.BlockSpec((TILE, D_MODEL), lambda i, *_: (i, 0))],
        out_specs=pl.BlockSpec(memory_space=pl.ANY),
        scratch_shapes=[pltpu.VMEM((DISPATCH_BUFS, TILE_ROWS, D_MODEL // 2), jnp.uint32),
                        pltpu.VMEM((MOE_HALF, D_MODEL // 2), jnp.uint32),
                        pltpu.SemaphoreType.DMA((DISPATCH_BUFS,)),
                        pltpu.SemaphoreType.DMA(())],
    )
    return pl.pallas_call(
        _dispatch_kernel,
        out_shape=jax.ShapeDtypeStruct((P, D_MODEL // 2), jnp.uint32),
        grid_spec=grid_spec,
        compiler_params=_params(("arbitrary",), has_side_effects=True),
        name="dispatch",
    )(run_src, run_len, run_dst, tile_rows, zero_lo, zero_len, n_used, pos, h2)


def _experts_kernel(base_ref, nblk_ref, half_ref, w1_ref, b1_ref, w2_ref, b2_ref, xs_hbm, ys_hbm,
                    w1b_ref, w2b_ref, xbuf_ref, ybuf_ref, xhalf_ref, yhalf_ref, in_sems, out_sems, busy_ref):
    e = pl.program_id(0)
    n = nblk_ref[e]
    has_half = half_ref[e] == 1
    HALF_BUF = 2

    def rows(expert, j):
        return pl.ds(pl.multiple_of(base_ref[expert] + j * MOE_BLOCK, MOE_HALF), MOE_BLOCK)

    def half_rows(first_row):
        return pl.ds(pl.multiple_of(first_row, MOE_HALF), MOE_HALF)

    def in_copy(expert, j, slot):
        return pltpu.make_async_copy(xs_hbm.at[rows(expert, j)], xbuf_ref.at[slot], in_sems.at[slot])

    def out_copy(j, slot):
        return pltpu.make_async_copy(ybuf_ref.at[slot], ys_hbm.at[rows(e, j)], out_sems.at[slot])

    half_row0 = base_ref[e] + n * MOE_BLOCK
    half_in = pltpu.make_async_copy(xs_hbm.at[half_rows(half_row0)], xhalf_ref, in_sems.at[HALF_BUF])

    def half_out(first_row):
        return pltpu.make_async_copy(yhalf_ref, ys_hbm.at[half_rows(first_row)], out_sems.at[HALF_BUF])

    def wait_out(buf, half=False):
        @pl.when(busy_ref[buf] == 1)
        def _():
            (half_out(0) if half else out_copy(0, buf)).wait()
            busy_ref[buf] = 0

    def mlp(x_packed):
        x = jnp.concatenate(_unpack_bf16_pairs(x_packed), axis=1)
        u = jnp.dot(x, w1b_ref[...], preferred_element_type=jnp.float32) + b1_ref[...]
        glu = jnp.minimum(u[:, :D_FF], SWIGLU_LIMIT)
        lin = jnp.clip(u[:, D_FF:], -SWIGLU_LIMIT, SWIGLU_LIMIT)
        act = glu * jax.nn.sigmoid(SWIGLU_ALPHA * glu) * (lin + 1.0)
        return _pack_bf16_pairs(jnp.dot(act.astype(jnp.bfloat16), w2b_ref[...],
                                        preferred_element_type=jnp.float32) + b2_ref[...])

    @pl.when(e == 0)
    def _():
        for buf in range(3):
            busy_ref[buf] = 0

        @pl.when(n > 0)
        def _():
            in_copy(0, 0, 0).start()

    @pl.when(has_half)
    def _():
        half_in.start()

    @pl.when((n > 0) | has_half)
    def _():
        w1b_ref[...] = w1_ref[...].astype(jnp.bfloat16)
        w2b_ref[...] = w2_ref[...].astype(jnp.bfloat16)

    def block(j, carry):
        slot = j % 2

        @pl.when(j + 1 < n)
        def _():
            in_copy(e, j + 1, 1 - slot).start()

        in_copy(e, j, slot).wait()
        wait_out(slot)
        ybuf_ref[slot] = mlp(xbuf_ref[slot])
        out_copy(j, slot).start()
        busy_ref[slot] = 1
        return carry

    lax.fori_loop(0, n, block, 0)

    @pl.when(has_half)
    def _():
        half_in.wait()
        wait_out(HALF_BUF, half=True)
        yhalf_ref[...] = mlp(xhalf_ref[...])
        half_out(half_row0).start()
        busy_ref[HALF_BUF] = 1

    e_next = jnp.minimum(e + 1, N_EXPERTS - 1)

    @pl.when((e + 1 < N_EXPERTS) & (nblk_ref[e_next] > 0))
    def _():
        in_copy(e_next, 0, 0).start()

    @pl.when(e == N_EXPERTS - 1)
    def _():
        wait_out(0)
        wait_out(1)
        wait_out(HALF_BUF, half=True)
        yhalf_ref[...] = jnp.zeros(yhalf_ref.shape, yhalf_ref.dtype)
        first_unused = (half_row0 + half_ref[e] * MOE_HALF) // MOE_HALF
        n_halves = ys_hbm.shape[0] // MOE_HALF
        lax.fori_loop(first_unused, n_halves, lambda j, c: (half_out(j * MOE_HALF).start(), c)[1], 0)
        lax.fori_loop(first_unused, n_halves, lambda j, c: (half_out(j * MOE_HALF).wait(), c)[1], 0)


def _experts(base, n_blk, n_half, xs, w1, b1, w2, b2):
    P = xs.shape[0]
    expert = lambda e, bs, nb, nh: (e, 0, 0)
    grid_spec = pltpu.PrefetchScalarGridSpec(
        num_scalar_prefetch=3,
        grid=(N_EXPERTS,),
        in_specs=[pl.BlockSpec((None, D_MODEL, 2 * D_FF), expert),
                  pl.BlockSpec((None, 1, 2 * D_FF), expert),
                  pl.BlockSpec((None, D_FF, D_MODEL), expert),
                  pl.BlockSpec((None, 1, D_MODEL), expert),
                  pl.BlockSpec(memory_space=pl.ANY)],
        out_specs=pl.BlockSpec(memory_space=pl.ANY),
        scratch_shapes=[pltpu.VMEM((D_MODEL, 2 * D_FF), jnp.bfloat16),
                        pltpu.VMEM((D_FF, D_MODEL), jnp.bfloat16),
                        pltpu.VMEM((2, MOE_BLOCK, D_MODEL // 2), jnp.uint32),
                        pltpu.VMEM((2, MOE_BLOCK, D_MODEL // 2), jnp.uint32),
                        pltpu.VMEM((MOE_HALF, D_MODEL // 2), jnp.uint32),
                        pltpu.VMEM((MOE_HALF, D_MODEL // 2), jnp.uint32),
                        pltpu.SemaphoreType.DMA((3,)),
                        pltpu.SemaphoreType.DMA((3,)),
                        pltpu.SMEM((3,), jnp.int32)],
    )
    return pl.pallas_call(
        _experts_kernel,
        out_shape=jax.ShapeDtypeStruct((P, D_MODEL // 2), jnp.uint32),
        grid_spec=grid_spec,
        compiler_params=_params(("arbitrary",)),
        name="experts",
    )(base, n_blk, n_half, w1, b1, w2, b2, xs)


def _combine_kernel(src_ref, len_ref, dst_ref, rows_ref, gate_t_ref, x1_ref, ys_hbm, o_ref, ybuf_ref, sems):
    i = pl.program_id(0)
    n_tiles = pl.num_programs(0)
    n_buf = ybuf_ref.shape[0]
    cur = i % n_buf
    n_rows, tm = ybuf_ref.shape[1], x1_ref.shape[0]

    def fetch(tile, slot):
        _run_copies(src_ref, len_ref, dst_ref, tile,
                    lambda s, d, n: pltpu.make_async_copy(ys_hbm.at[pl.ds(d, n)], ybuf_ref.at[slot, pl.ds(s, n)],
                                                          sems.at[slot]))

    @pl.when(i == 0)
    def _():
        ybuf_ref[...] = jnp.zeros(ybuf_ref.shape, ybuf_ref.dtype)
        for tile in range(n_buf - 1):
            fetch(tile, tile)

    ahead = i + n_buf - 1

    @pl.when(ahead < n_tiles)
    def _():
        fetch(ahead, ahead % n_buf)

    g = gate_t_ref[...]
    p_iota = lax.broadcasted_iota(jnp.int32, (tm, n_rows), 1)
    weights = jnp.zeros((tm, n_rows), jnp.float32)
    for k in range(TOP_K):
        pos_k = g[:, TOP_K + k:TOP_K + k + 1].astype(jnp.int32)
        weights = jnp.where(p_iota == pos_k, g[:, k:k + 1], weights)
    weights = weights.astype(jnp.bfloat16)

    rows = pl.multiple_of(rows_ref[i], RUN_ALIGN)
    pltpu.make_async_copy(ys_hbm.at[pl.ds(0, rows)], ybuf_ref.at[cur, pl.ds(0, rows)], sems.at[cur]).wait()
    halves = [jnp.dot(weights, y, preferred_element_type=jnp.float32) for y in _unpack_bf16_pairs(ybuf_ref[cur])]
    o_ref[...] = x1_ref[...] + jnp.concatenate(halves, axis=1)


def _combine(run_src, run_len, run_dst, tile_rows, gate_t, x1, ys):
    T = x1.shape[0]
    tok = lambda w: pl.BlockSpec((TILE, w), lambda i, *_: (i, 0))
    grid_spec = pltpu.PrefetchScalarGridSpec(
        num_scalar_prefetch=4,
        grid=(T // TILE,),
        in_specs=[tok(LANES), tok(D_MODEL), pl.BlockSpec(memory_space=pl.ANY)],
        out_specs=tok(D_MODEL),
        scratch_shapes=[pltpu.VMEM((DISPATCH_BUFS, TILE_ROWS, D_MODEL // 2), jnp.uint32),
                        pltpu.SemaphoreType.DMA((DISPATCH_BUFS,))],
    )
    return pl.pallas_call(
        _combine_kernel,
        out_shape=jax.ShapeDtypeStruct((T, D_MODEL), jnp.float32),
        grid_spec=grid_spec,
        compiler_params=_params(("arbitrary",)),
        name="combine",
    )(run_src, run_len, run_dst, tile_rows, gate_t, x1, ys)


def _rotary_tables(positions):
    pos = positions.astype(jnp.float32)[..., None]
    lane = jnp.arange(LANES)
    half_r = RET_DK // 2
    inv_r = RET_ROPE_THETA ** (-jnp.linspace(0.0, 1.0, half_r, dtype=jnp.float32))
    ret_cs = jnp.cos(pos * inv_r[lane % half_r] - jnp.where(lane < half_r, 0.0, 0.5 * jnp.pi))
    half_d = ROT_DIM // 2
    inv_d = ROPE_THETA ** (-jnp.arange(0, ROT_DIM, 2, dtype=jnp.float32) / ROT_DIM)
    sub = lane % DIFF_DH
    ang_d = jnp.where(sub < ROT_DIM, pos * inv_d[sub % half_d], 0.0)
    diff_cs = jnp.cos(ang_d - jnp.where((sub >= half_d) & (sub < ROT_DIM), 0.5 * jnp.pi, 0.0))
    return ret_cs, diff_cs


def kernel(x, positions, norm1_w, w_in, ret_log_decay_fwd, ret_log_decay_bwd, ret_norm_w, q_norm_w, k_norm_w, lambda_q1, lambda_k1, lambda_q2, lambda_k2, diff_norm_w, w_out, norm2_w, w_router, b_router, w1, b1, w2, b2):
    B, S, D = x.shape
    T = B * S
    f32 = jnp.float32
    bf16 = jnp.bfloat16
    x2 = x.reshape(T, D)

    dup = lambda w: jnp.concatenate([w, w]).reshape(1, LANES).astype(f32)
    proj, rq_r, rk_r, qs, ks = _in_proj(x2, norm1_w[0].reshape(1, D), w_in[0].astype(bf16),
                                        _rotary_tables(positions), dup(q_norm_w[0]), dup(k_norm_w[0]), B, S)

    y_ret = _retention(ret_log_decay_fwd[0].astype(f32), ret_log_decay_bwd[0].astype(f32),
                       rq_r, rk_r, proj, ret_norm_w[0].reshape(1, RET_WIDTH).astype(f32), B, S)

    lam = (jnp.exp(jnp.sum(lambda_q1[0].astype(f32) * lambda_k1[0].astype(f32)))
           - jnp.exp(jnp.sum(lambda_q2[0].astype(f32) * lambda_k2[0].astype(f32))) + LAMBDA_INIT)
    lam_row = jnp.full((1, LANES), lam, f32)
    bound = (SCORE_BOUND_SLACK * DIFF_DH ** 0.5 * LOG2_E
             * jnp.max(jnp.abs(q_norm_w[0].astype(f32))) * jnp.max(jnp.abs(k_norm_w[0].astype(f32)))).reshape(1)
    attn_args = (bound, qs, ks, proj, lam_row, diff_norm_w[0].reshape(1, DIFF_DV).astype(f32), B, S)
    y_diff = lax.cond(bound[0] <= MAX_SAFE_SCORE_BOUND,
                      lambda: _diff_attn(False, *attn_args), lambda: _diff_attn(True, *attn_args))

    x1, h2, pos, gate_t, len_t, off_t, tot_t = _out_router(
        x2, y_ret, y_diff, w_out[0].astype(bf16), norm2_w[0].reshape(1, D),
        w_router[0].T.astype(f32), b_router[0].reshape(N_EXPERTS, 1).astype(f32))

    n_tiles = T // TILE
    run_len = len_t[:, :, 0].astype(jnp.int32)
    total = tot_t[:, 0].astype(jnp.int32)
    padded = ((total + MOE_HALF - 1) // MOE_HALF) * MOE_HALF
    pad_end = jnp.cumsum(padded)
    pad_start = pad_end - padded
    run_dst = pad_start[None, :] + off_t[:, :, 0].astype(jnp.int32)
    run_src = jnp.cumsum(run_len, axis=1) - run_len
    tile_rows = jnp.sum(run_len, axis=1)
    P = T * TOP_K + n_tiles * N_EXPERTS * RUN_ALIGN + N_EXPERTS * MOE_HALF
    n_used = (pad_end[-1:] // MOE_HALF).astype(jnp.int32)
    runs = (run_src.reshape(-1), run_len.reshape(-1), run_dst.reshape(-1), tile_rows)

    xs = _dispatch(*runs, pad_start + total, padded - total, n_used, pos, h2, P)
    ys = _experts(pad_start, padded // MOE_BLOCK, (padded // MOE_HALF) % 2, xs, w1[0],
                  b1[0].reshape(N_EXPERTS, 1, 2 * D_FF), w2[0], b2[0].reshape(N_EXPERTS, 1, D))
    out = _combine(*runs, gate_t, x1, ys)
    return out.reshape(B, S, D)
```

```python
import functools

import jax
import jax.numpy as jnp
from jax import lax
from jax.experimental import pallas as pl
from jax.experimental.pallas import tpu as pltpu

EPS = 1e-6
D_MODEL = 1024
RET_HEADS = 4
RET_DK = 128
RET_WIDTH = 512
RET_ROPE_THETA = 10000.0
DIFF_HEADS = 4
DIFF_DH = 64
DIFF_DV = 128
DIFF_WIDTH = 512
ROPE_THETA = 500000.0
ROT_DIM = DIFF_DH // 4
D_IN_PROJ = 3584
N_EXPERTS = 32
TOP_K = 4
D_FF = 1024
SWIGLU_LIMIT = 7.0
SWIGLU_ALPHA = 1.702
LAMBDA_INIT = 0.8 - 0.6 * 1.0

LOG2_E = 1.4426950408889634
SCORE_BOUND_SLACK = 1.02
MAX_SAFE_SCORE_BOUND = 60.0
LANES = 128
SUBLANES = 8
VMEM_LIMIT = 56 * 1024 * 1024

COL_RQ, COL_RK, COL_RV, COL_RG, COL_DQ, COL_DK, COL_DV = 0, 4, 8, 12, 16, 20, 24
VGV_RV, VGV_RG, VGV_DV = 0, 4, 8

TM_PROJ = 512
RET_CHUNK = 128
RET_UNROLL = 16
TQ_ATTN = 512
TK_ATTN = 2048
TILE = 512
MOE_BLOCK = 512
MOE_HALF = MOE_BLOCK // 2
RUN_ALIGN = SUBLANES
TILE_ROWS = TOP_K * TILE + N_EXPERTS * RUN_ALIGN
DISPATCH_BUFS = 2


def _params(sem, **kw):
    return pltpu.CompilerParams(dimension_semantics=sem, vmem_limit_bytes=VMEM_LIMIT, **kw)


def _in_proj_kernel(x_ref, nw_ref, w_ref, ret_cs_ref, diff_cs_ref, qw_ref, kw_ref,
                    vgv_ref, rqo_ref, rko_ref, qs_ref, ks_ref):
    ts = x_ref.shape[0]
    x = x_ref[...]
    hn = (x * lax.rsqrt(jnp.mean(x * x, axis=-1, keepdims=True) + EPS) * nw_ref[...]).astype(jnp.bfloat16)

    def proj(col_block):
        c0 = col_block * LANES
        return jnp.dot(hn, w_ref[:, c0:c0 + 4 * LANES], preferred_element_type=jnp.float32)

    dq, dk, rq, rk = proj(COL_DQ), proj(COL_DK), proj(COL_RQ), proj(COL_RK)
    lane = lax.broadcasted_iota(jnp.int32, (ts, LANES), 1)
    lo = lane < DIFF_DH
    ret_cs = ret_cs_ref[...]
    ret_sc = pltpu.roll(ret_cs, RET_DK // 2, 1)
    first_half = lane < RET_DK // 2
    c2 = jnp.where(first_half, ret_cs, ret_sc)
    s2 = jnp.where(first_half, -ret_sc, ret_cs)
    sub = lane % DIFF_DH
    sin_lanes = (sub >= ROT_DIM // 2) & (sub < ROT_DIM)
    diff_cs = diff_cs_ref[...]
    ra = jnp.where(sin_lanes, pltpu.roll(diff_cs, ROT_DIM // 2, 1), diff_cs)
    rp = jnp.where(sin_lanes, diff_cs, 0.0)
    rn = jnp.where(sub < ROT_DIM // 2, -pltpu.roll(diff_cs, LANES - ROT_DIM // 2, 1), 0.0)

    def qk_norm_rot(x, w):
        x2 = x * x
        s_lo = jnp.sum(jnp.where(lo, x2, 0.0), axis=-1, keepdims=True)
        s_hi = jnp.sum(jnp.where(lo, 0.0, x2), axis=-1, keepdims=True)
        ms = jnp.where(lo, s_lo, s_hi) * (1.0 / DIFF_DH)
        xn = x * lax.rsqrt(ms + EPS) * w
        return xn * ra + pltpu.roll(xn, ROT_DIM // 2, 1) * rp + pltpu.roll(xn, LANES - ROT_DIM // 2, 1) * rn

    for h in range(DIFF_HEADS):
        sl = slice(h * LANES, (h + 1) * LANES)
        q = qk_norm_rot(dq[:, sl], qw_ref[...]) * (DIFF_DH ** -0.5 * LOG2_E)
        k = qk_norm_rot(dk[:, sl], kw_ref[...])
        qs_ref[h, 0] = jnp.where(lo, q, 0.0).astype(qs_ref.dtype)
        qs_ref[h, 1] = jnp.where(lo, 0.0, q).astype(qs_ref.dtype)
        ks_ref[:, sl] = k.astype(ks_ref.dtype)
    for h in range(RET_HEADS):
        sl = slice(h * LANES, (h + 1) * LANES)
        q = rq[:, sl]
        k = rk[:, sl]
        rqo_ref[:, sl] = (q * c2 + pltpu.roll(q, RET_DK // 2, 1) * s2).astype(rqo_ref.dtype)
        rko_ref[:, sl] = ((k * c2 + pltpu.roll(k, RET_DK // 2, 1) * s2) * (RET_DK ** -0.5)).astype(rko_ref.dtype)
    for slot, col_block in enumerate((COL_RV, COL_RG, COL_DV)):
        vgv_ref[:, slot * 4 * LANES:(slot + 1) * 4 * LANES] = proj(col_block).astype(vgv_ref.dtype)


def _in_proj(x2, nw, w_bf16, tabs, qw2, kw2, B, S):
    T = B * S
    n_s = S // TM_PROJ
    tok = lambda w: pl.BlockSpec((TM_PROJ, w), lambda i: (i, 0))
    const = lambda s: pl.BlockSpec(s, lambda i: (0, 0))
    tab = pl.BlockSpec((None, TM_PROJ, LANES), lambda i: (i // n_s, i % n_s, 0))
    bf16 = jnp.bfloat16
    return pl.pallas_call(
        _in_proj_kernel,
        out_shape=(jax.ShapeDtypeStruct((T, 3 * 4 * LANES), bf16),
                   jax.ShapeDtypeStruct((T, 4 * LANES), bf16),
                   jax.ShapeDtypeStruct((T, 4 * LANES), bf16),
                   jax.ShapeDtypeStruct((B, DIFF_HEADS, 2, S, LANES), bf16),
                   jax.ShapeDtypeStruct((T, 4 * LANES), bf16)),
        grid=(T // TM_PROJ,),
        in_specs=[tok(D_MODEL), const((1, D_MODEL)), const((D_MODEL, D_IN_PROJ)),
                  tab, tab, const((1, LANES)), const((1, LANES))],
        out_specs=(tok(3 * 4 * LANES), tok(4 * LANES), tok(4 * LANES),
                   pl.BlockSpec((None, DIFF_HEADS, 2, TM_PROJ, LANES), lambda i: (i // n_s, 0, 0, i % n_s, 0)),
                   tok(4 * LANES)),
        compiler_params=_params(("arbitrary",)),
        name="in_proj",
    )(x2, nw, w_bf16, *tabs, qw2, kw2)


def _retention_kernel(ldf_ref, ldb_ref, q_ref, k_ref, v_ref, g_ref, nw_ref, o_ref, sb_ref):
    C = RET_CHUNK
    S = q_ref.shape[0]
    n_chunks = S // C
    h = pl.program_id(1)
    ldf = ldf_ref[h]
    ldb = ldb_ref[h]
    row = lax.broadcasted_iota(jnp.int32, (C, C), 0).astype(jnp.float32)
    colm = lax.broadcasted_iota(jnp.int32, (C, C), 1).astype(jnp.float32)
    dist = row - colm
    decay = jnp.where(dist >= 0, jnp.exp(ldf * jnp.maximum(dist, 0.0)), jnp.exp(ldb * jnp.maximum(-dist, 0.0)))
    idx = lax.broadcasted_iota(jnp.int32, (C, 1), 0).astype(jnp.float32)
    q_dec_f = jnp.exp(ldf * (idx + 1.0))
    k_dec_f = jnp.exp(ldf * (C - 1.0 - idx))
    q_dec_b = jnp.exp(ldb * (C - idx))
    k_dec_b = jnp.exp(ldb * idx)
    chunk_dec_f = jnp.exp(ldf * C)
    chunk_dec_b = jnp.exp(ldb * C)
    f32 = jnp.float32
    bf16 = jnp.bfloat16

    def kv_state(k, v, k_dec):
        kd = (k.astype(f32) * k_dec).astype(bf16)
        return lax.dot_general(kd, v, (((0,), (0,)), ((), ())), preferred_element_type=f32)

    def bwd_step(i, state):
        c = n_chunks - 1 - i
        r0 = pl.multiple_of(c * C, C)
        sb_ref[c] = state
        return state * chunk_dec_b + kv_state(k_ref[pl.ds(r0, C), :], v_ref[pl.ds(r0, C), :], k_dec_b)

    lax.fori_loop(0, n_chunks, bwd_step, jnp.zeros((RET_DK, LANES), f32), unroll=RET_UNROLL)

    def fwd_step(c, state):
        r0 = pl.multiple_of(c * C, C)
        q = q_ref[pl.ds(r0, C), :]
        k = k_ref[pl.ds(r0, C), :]
        v = v_ref[pl.ds(r0, C), :]
        scores = lax.dot_general(q, k, (((1,), (1,)), ((), ())), preferred_element_type=f32) * decay
        y = jnp.dot(scores.astype(bf16), v, preferred_element_type=f32)
        qf = q.astype(f32)
        y += jnp.dot((qf * q_dec_f).astype(bf16), state.astype(bf16), preferred_element_type=f32)
        y += jnp.dot((qf * q_dec_b).astype(bf16), sb_ref[c].astype(bf16), preferred_element_type=f32)
        yn = y * lax.rsqrt(jnp.mean(y * y, axis=-1, keepdims=True) + EPS) * nw_ref[...]
        g = g_ref[pl.ds(r0, C), :].astype(f32)
        o_ref[pl.ds(r0, C), :] = (yn * (g * jax.nn.sigmoid(g))).astype(o_ref.dtype)
        return state * chunk_dec_f + kv_state(k, v, k_dec_f)

    lax.fori_loop(0, n_chunks, fwd_step, jnp.zeros((RET_DK, LANES), f32), unroll=RET_UNROLL)


def _retention(ldf, ldb, rq_r, rk_r, proj, nw, B, S):
    T = B * S
    smem = pl.BlockSpec(memory_space=pltpu.SMEM)
    seq = lambda cb: pl.BlockSpec((S, LANES), lambda b, h: (b, cb + h))
    return pl.pallas_call(
        _retention_kernel,
        out_shape=jax.ShapeDtypeStruct((T, RET_WIDTH), jnp.bfloat16),
        grid=(B, RET_HEADS),
        in_specs=[smem, smem, seq(0), seq(0), seq(VGV_RV), seq(VGV_RG),
                  pl.BlockSpec((1, LANES), lambda b, h: (0, h))],
        out_specs=seq(0),
        scratch_shapes=[pltpu.VMEM((S // RET_CHUNK, RET_DK, LANES), jnp.float32)],
        compiler_params=_params(("arbitrary", "arbitrary")),
        name="retention",
    )(ldf, ldb, rq_r, rk_r, proj, proj, nw)


def _diff_attn_kernel(online_max, bound_ref, q_ref, k_ref, v_ref, lam_ref, nw_ref, o_ref, m_ref, l_ref, acc_ref):
    tq = q_ref.shape[1]
    S = k_ref.shape[0]
    f32 = jnp.float32
    q = q_ref[...].reshape(2 * tq, LANES)
    if online_max:
        m_ref[...] = jnp.full(m_ref.shape, -jnp.inf, f32)
    l_ref[...] = jnp.zeros(l_ref.shape, f32)
    acc_ref[...] = jnp.zeros(acc_ref.shape, f32)
    n_tiles = TK_ATTN // LANES

    def kv_step(j, carry):
        r0 = pl.multiple_of(j * TK_ATTN, TK_ATTN)
        k = k_ref[pl.ds(r0, TK_ATTN), :]
        v = v_ref[pl.ds(r0, TK_ATTN), :]
        s = lax.dot_general(q, k, (((1,), (1,)), ((), ())), preferred_element_type=f32)
        tiles = [s[:, c * LANES:(c + 1) * LANES] for c in range(n_tiles)]
        if online_max:
            part = tiles[0]
            for t in tiles[1:]:
                part = jnp.maximum(part, t)
            m_prev = m_ref[...]
            shift = jnp.maximum(m_prev, jnp.max(part, axis=-1, keepdims=True))
            alpha = jnp.exp2(m_prev - shift)
            m_ref[...] = shift
        else:
            shift = bound_ref[0]
        probs = [jnp.exp2(t - shift) for t in tiles]
        psum = probs[0]
        for p in probs[1:]:
            psum = psum + p
        pv = jnp.dot(jnp.concatenate([p.astype(jnp.bfloat16) for p in probs], axis=1), v,
                     preferred_element_type=f32)
        if online_max:
            l_ref[...] = alpha * l_ref[...] + psum
            acc_ref[...] = alpha * acc_ref[...] + pv
        else:
            l_ref[...] = l_ref[...] + psum
            acc_ref[...] = acc_ref[...] + pv
        return carry

    lax.fori_loop(0, S // TK_ATTN, kv_step, 0)
    o = acc_ref[...] / jnp.sum(l_ref[...], axis=-1, keepdims=True)
    d = o[:tq] - lam_ref[...] * o[tq:]
    dn = d * lax.rsqrt(jnp.mean(d * d, axis=-1, keepdims=True) + EPS) * nw_ref[...]
    o_ref[...] = (dn * (1.0 - LAMBDA_INIT)).astype(o_ref.dtype)


def _diff_attn(online_max, bound, qs, ks, proj, lam, nw, B, S):
    T = B * S
    n_q = S // TQ_ATTN
    one = pl.BlockSpec((1, LANES), lambda b, h, i, bd: (0, 0))
    grid_spec = pltpu.PrefetchScalarGridSpec(
        num_scalar_prefetch=1,
        grid=(B, DIFF_HEADS, n_q),
        in_specs=[pl.BlockSpec((None, None, 2, TQ_ATTN, LANES), lambda b, h, i, bd: (b, h, 0, i, 0)),
                  pl.BlockSpec((S, LANES), lambda b, h, i, bd: (b, h)),
                  pl.BlockSpec((S, LANES), lambda b, h, i, bd: (b, VGV_DV + h)),
                  one, one],
        out_specs=pl.BlockSpec((TQ_ATTN, LANES), lambda b, h, i, bd: (b * n_q + i, h)),
        scratch_shapes=[pltpu.VMEM((2 * TQ_ATTN, LANES), jnp.float32)] * 3,
    )
    return pl.pallas_call(
        functools.partial(_diff_attn_kernel, online_max),
        out_shape=jax.ShapeDtypeStruct((T, DIFF_WIDTH), jnp.bfloat16),
        grid_spec=grid_spec,
        compiler_params=_params(("arbitrary", "arbitrary", "arbitrary")),
        name="diff_attn_online" if online_max else "diff_attn",
    )(bound, qs, ks, proj, lam, nw)


def _out_router_kernel(x_ref, yr_ref, yd_ref, wo_ref, n2_ref, wrt_ref, br_ref,
                       x1_ref, h2_ref, pos_ref, gate_t_ref, len_ref, off_ref, tot_ref):
    tm = x_ref.shape[0]
    f32 = jnp.float32
    bf16 = jnp.bfloat16

    @pl.when(pl.program_id(0) == 0)
    def _():
        tot_ref[...] = jnp.zeros(tot_ref.shape, f32)

    att = jnp.dot(yr_ref[...], wo_ref[:RET_WIDTH, :], preferred_element_type=f32)
    att += jnp.dot(yd_ref[...], wo_ref[RET_WIDTH:, :], preferred_element_type=f32)
    x1 = x_ref[...] + att
    x1_ref[...] = x1
    h2 = x1 * lax.rsqrt(jnp.mean(x1 * x1, axis=-1, keepdims=True) + EPS) * n2_ref[...]
    h2_ref[...] = h2.astype(h2_ref.dtype)
    nt = (((1,), (1,)), ((), ()))
    h_hi = h2.astype(bf16)
    h_lo = (h2 - h_hi.astype(f32)).astype(bf16)
    w = wrt_ref[...]
    w_hi = w.astype(bf16)
    w_lo = (w - w_hi.astype(f32)).astype(bf16)
    logits = (lax.dot_general(w_hi, h_hi, nt, preferred_element_type=f32)
              + lax.dot_general(w_lo, h_hi, nt, preferred_element_type=f32)
              + lax.dot_general(w_hi, h_lo, nt, preferred_element_type=f32)) + br_ref[...]
    e_iota = lax.broadcasted_iota(jnp.int32, (N_EXPERTS, tm), 0)
    work = logits
    vals, hots = [], []
    for _ in range(TOP_K):
        mx = jnp.max(work, axis=0, keepdims=True)
        ix = jnp.min(jnp.where(work == mx, e_iota, N_EXPERTS), axis=0, keepdims=True)
        hot = e_iota == ix
        vals.append(mx)
        hots.append(hot)
        work = jnp.where(hot, -jnp.inf, work)
    exps = [jnp.exp(v - vals[0]) for v in vals]
    denom = exps[0] + exps[1] + exps[2] + exps[3]
    gates = [e / denom for e in exps]
    sel = jnp.zeros((N_EXPERTS, tm), f32)
    for hot in hots:
        sel = jnp.where(hot, 1.0, sel)
    t_row = lax.broadcasted_iota(jnp.int32, (tm, tm), 0)
    t_col = lax.broadcasted_iota(jnp.int32, (tm, tm), 1)
    upper = jnp.where(t_row < t_col, 1.0, 0.0).astype(bf16)
    rank = jnp.dot(sel.astype(bf16), upper, preferred_element_type=f32)
    cnt = jnp.sum(sel, axis=1, keepdims=True)
    run_units = jnp.floor((cnt + (RUN_ALIGN - 1.0)) * (1.0 / RUN_ALIGN))
    run_len = jnp.broadcast_to(run_units * RUN_ALIGN, (N_EXPERTS, LANES))
    e_row = lax.broadcasted_iota(jnp.int32, (N_EXPERTS, N_EXPERTS), 0)
    e_col = lax.broadcasted_iota(jnp.int32, (N_EXPERTS, N_EXPERTS), 1)
    lower = jnp.where(e_col < e_row, 1.0, 0.0).astype(bf16)
    run_start = jnp.dot(lower, jnp.broadcast_to(run_units, (N_EXPERTS, LANES)).astype(bf16),
                        preferred_element_type=f32) * RUN_ALIGN
    pos_full = rank + run_start[:, 0:1]
    pos = [jnp.sum(jnp.where(hot, pos_full, 0.0), axis=0, keepdims=True) for hot in hots]
    for k in range(TOP_K):
        pos_ref[k:k + 1, :] = pos[k].astype(jnp.int32)
    rows = jnp.concatenate(gates + pos + [jnp.zeros((LANES - 2 * TOP_K, tm), f32)], axis=0)
    gate_t_ref[...] = rows.T
    len_ref[0] = run_len
    off_ref[0] = tot_ref[...]
    tot_ref[...] = tot_ref[...] + run_len


def _out_router(x2, y_ret, y_diff, wo_bf16, n2w, wrt, br):
    T = x2.shape[0]
    n_tiles = T // TILE
    tok = lambda w: pl.BlockSpec((TILE, w), lambda i: (i, 0))
    const = lambda s: pl.BlockSpec(s, lambda i: (0, 0))
    per_tile = pl.BlockSpec((1, N_EXPERTS, LANES), lambda i: (i, 0, 0))
    return pl.pallas_call(
        _out_router_kernel,
        out_shape=(jax.ShapeDtypeStruct((T, D_MODEL), jnp.float32),
                   jax.ShapeDtypeStruct((T, D_MODEL), jnp.bfloat16),
                   jax.ShapeDtypeStruct((TOP_K, T), jnp.int32),
                   jax.ShapeDtypeStruct((T, LANES), jnp.float32),
                   jax.ShapeDtypeStruct((n_tiles, N_EXPERTS, LANES), jnp.float32),
                   jax.ShapeDtypeStruct((n_tiles, N_EXPERTS, LANES), jnp.float32),
                   jax.ShapeDtypeStruct((N_EXPERTS, LANES), jnp.float32)),
        grid=(n_tiles,),
        in_specs=[tok(D_MODEL), tok(RET_WIDTH), tok(DIFF_WIDTH), const((D_MODEL, D_MODEL)),
                  const((1, D_MODEL)), const((N_EXPERTS, D_MODEL)), const((N_EXPERTS, 1))],
        out_specs=(tok(D_MODEL), tok(D_MODEL), pl.BlockSpec((TOP_K, TILE), lambda i: (0, i)), tok(LANES),
                   per_tile, per_tile, const((N_EXPERTS, LANES))),
        compiler_params=_params(("arbitrary",)),
        name="out_router",
    )(x2, y_ret, y_diff, wo_bf16, n2w, wrt, br)


def _pack_bf16_pairs(x):
    w = x.shape[1] // 2
    bits = lambda v: lax.bitcast_convert_type(v.astype(jnp.bfloat16).astype(jnp.float32), jnp.uint32)
    return (bits(x[:, :w]) >> 16) | (bits(x[:, w:]) & jnp.uint32(0xFFFF0000))


def _unpack_bf16_pairs(p):
    as_bf16 = lambda bits: lax.bitcast_convert_type(bits, jnp.float32).astype(jnp.bfloat16)
    return as_bf16(p << 16), as_bf16(p & jnp.uint32(0xFFFF0000))


def _run_copies(src_ref, len_ref, dst_ref, tile, make_copy):
    for e in range(N_EXPERTS):
        n = pl.multiple_of(len_ref[tile * N_EXPERTS + e], RUN_ALIGN)
        s = pl.multiple_of(src_ref[tile * N_EXPERTS + e], RUN_ALIGN)
        d = pl.multiple_of(dst_ref[tile * N_EXPERTS + e], RUN_ALIGN)

        @pl.when(n > 0)
        def _():
            make_copy(s, d, n).start()


def _dispatch_kernel(src_ref, len_ref, dst_ref, rows_ref, zlo_ref, zlen_ref, nu_ref,
                     pos_ref, h2_ref, xs_hbm, xbuf_ref, zero_ref, sems, zero_sem):
    i = pl.program_id(0)
    n_tiles = pl.num_programs(0)
    n_buf = xbuf_ref.shape[0]
    cur = i % n_buf
    n_rows, tm = xbuf_ref.shape[1], h2_ref.shape[0]

    @pl.when(i == 0)
    def _():
        zero_ref[...] = jnp.zeros(zero_ref.shape, zero_ref.dtype)

        def pad_copy(e):
            n = pl.multiple_of(zlen_ref[e], RUN_ALIGN)
            lo = pl.multiple_of(zlo_ref[e], RUN_ALIGN)
            return pltpu.make_async_copy(zero_ref.at[pl.ds(0, n)], xs_hbm.at[pl.ds(lo, n)], zero_sem)

        def tail_copy(j):
            return pltpu.make_async_copy(zero_ref, xs_hbm.at[pl.ds(j * MOE_HALF, MOE_HALF)], zero_sem)

        def guarded(copy, op):
            def body(e, c):
                @pl.when(zlen_ref[e] > 0)
                def _():
                    op(copy(e))
                return c
            return body

        lax.fori_loop(0, N_EXPERTS, guarded(pad_copy, lambda cp: cp.start()), 0)
        lax.fori_loop(0, N_EXPERTS, guarded(pad_copy, lambda cp: cp.wait()), 0)
        n_halves = xs_hbm.shape[0] // MOE_HALF
        lax.fori_loop(nu_ref[0], n_halves, lambda j, c: (tail_copy(j).start(), c)[1], 0)
        lax.fori_loop(nu_ref[0], n_halves, lambda j, c: (tail_copy(j).wait(), c)[1], 0)

    p_iota = lax.broadcasted_iota(jnp.int32, (n_rows, tm), 0)
    onehot = jnp.zeros((n_rows, tm), jnp.float32)
    for k in range(TOP_K):
        onehot = jnp.where(p_iota == pos_ref[k:k + 1, :], 1.0, onehot)
    xbuf_ref[cur] = _pack_bf16_pairs(
        jnp.dot(onehot.astype(jnp.bfloat16), h2_ref[...], preferred_element_type=jnp.float32))

    _run_copies(src_ref, len_ref, dst_ref, i,
                lambda s, d, n: pltpu.make_async_copy(xbuf_ref.at[cur, pl.ds(s, n)], xs_hbm.at[pl.ds(d, n)],
                                                      sems.at[cur]))

    def wait_tile(tile, slot):
        rows = pl.multiple_of(rows_ref[tile], RUN_ALIGN)
        pltpu.make_async_copy(xbuf_ref.at[slot, pl.ds(0, rows)], xs_hbm.at[pl.ds(0, rows)], sems.at[slot]).wait()

    oldest = n_buf - 1

    @pl.when(i >= oldest)
    def _():
        wait_tile(i - oldest, (i + 1) % n_buf)

    @pl.when(i == n_tiles - 1)
    def _():
        for back in range(oldest - 1, -1, -1):
            wait_tile(i - back, (i - back) % n_buf)


def _dispatch(run_src, run_len, run_dst, tile_rows, zero_lo, zero_len, n_used, pos, h2, P):
    T = h2.shape[0]
    n_pre = 7
    grid_spec = pltpu.PrefetchScalarGridSpec(
        num_scalar_prefetch=n_pre,
        grid=(T // TILE,),
        in_specs=[pl.BlockSpec((TOP_K, TILE), lambda i, *_: (0, i)),
                  pl.BlockSpec((TILE, D_MODEL), lambda i, *_: (i, 0))],
        out_specs=pl.BlockSpec(memory_space=pl.ANY),
        scratch_shapes=[pltpu.VMEM((DISPATCH_BUFS, TILE_ROWS, D_MODEL // 2), jnp.uint32),
                        pltpu.VMEM((MOE_HALF, D_MODEL // 2), jnp.uint32),
                        pltpu.SemaphoreType.DMA((DISPATCH_BUFS,)),
                        pltpu.SemaphoreType.DMA(())],
    )
    return pl.pallas_call(
        _dispatch_kernel,
        out_shape=jax.ShapeDtypeStruct((P, D_MODEL // 2), jnp.uint32),
        grid_spec=grid_spec,
        compiler_params=_params(("arbitrary",), has_side_effects=True),
        name="dispatch",
    )(run_src, run_len, run_dst, tile_rows, zero_lo, zero_len, n_used, pos, h2)


def _experts_kernel(base_ref, nblk_ref, half_ref, w1_ref, b1_ref, w2_ref, b2_ref, xs_hbm, ys_hbm,
                    w1b_ref, w2b_ref, xbuf_ref, ybuf_ref, xhalf_ref, yhalf_ref, in_sems, out_sems, busy_ref):
    e = pl.program_id(0)
    n = nblk_ref[e]
    has_half = half_ref[e] == 1
    HALF_BUF = 2

    def rows(expert, j):
        return pl.ds(pl.multiple_of(base_ref[expert] + j * MOE_BLOCK, MOE_HALF), MOE_BLOCK)

    def half_rows(first_row):
        return pl.ds(pl.multiple_of(first_row, MOE_HALF), MOE_HALF)

    def in_copy(expert, j, slot):
        return pltpu.make_async_copy(xs_hbm.at[rows(expert, j)], xbuf_ref.at[slot], in_sems.at[slot])

    def out_copy(j, slot):
        return pltpu.make_async_copy(ybuf_ref.at[slot], ys_hbm.at[rows(e, j)], out_sems.at[slot])

    half_row0 = base_ref[e] + n * MOE_BLOCK
    half_in = pltpu.make_async_copy(xs_hbm.at[half_rows(half_row0)], xhalf_ref, in_sems.at[HALF_BUF])

    def half_out(first_row):
        return pltpu.make_async_copy(yhalf_ref, ys_hbm.at[half_rows(first_row)], out_sems.at[HALF_BUF])

    def wait_out(buf, half=False):
        @pl.when(busy_ref[buf] == 1)
        def _():
            (half_out(0) if half else out_copy(0, buf)).wait()
            busy_ref[buf] = 0

    def mlp(x_packed):
        x = jnp.concatenate(_unpack_bf16_pairs(x_packed), axis=1)
        u = jnp.dot(x, w1b_ref[...], preferred_element_type=jnp.float32) + b1_ref[...]
        glu = jnp.minimum(u[:, :D_FF], SWIGLU_LIMIT)
        lin = jnp.clip(u[:, D_FF:], -SWIGLU_LIMIT, SWIGLU_LIMIT)
        act = glu * jax.nn.sigmoid(SWIGLU_ALPHA * glu) * (lin + 1.0)
        return _pack_bf16_pairs(jnp.dot(act.astype(jnp.bfloat16), w2b_ref[...],
                                        preferred_element_type=jnp.float32) + b2_ref[...])

    @pl.when(e == 0)
    def _():
        for buf in range(3):
            busy_ref[buf] = 0

        @pl.when(n > 0)
        def _():
            in_copy(0, 0, 0).start()

    @pl.when(has_half)
    def _():
        half_in.start()

    @pl.when((n > 0) | has_half)
    def _():
        w1b_ref[...] = w1_ref[...].astype(jnp.bfloat16)
        w2b_ref[...] = w2_ref[...].astype(jnp.bfloat16)

    def block(j, carry):
        slot = j % 2

        @pl.when(j + 1 < n)
        def _():
            in_copy(e, j + 1, 1 - slot).start()

        in_copy(e, j, slot).wait()
        wait_out(slot)
        ybuf_ref[slot] = mlp(xbuf_ref[slot])
        out_copy(j, slot).start()
        busy_ref[slot] = 1
        return carry

    lax.fori_loop(0, n, block, 0)

    @pl.when(has_half)
    def _():
        half_in.wait()
        wait_out(HALF_BUF, half=True)
        yhalf_ref[...] = mlp(xhalf_ref[...])
        half_out(half_row0).start()
        busy_ref[HALF_BUF] = 1

    e_next = jnp.minimum(e + 1, N_EXPERTS - 1)

    @pl.when((e + 1 < N_EXPERTS) & (nblk_ref[e_next] > 0))
    def _():
        in_copy(e_next, 0, 0).start()

    @pl.when(e == N_EXPERTS - 1)
    def _():
        wait_out(0)
        wait_out(1)
        wait_out(HALF_BUF, half=True)
        yhalf_ref[...] = jnp.zeros(yhalf_ref.shape, yhalf_ref.dtype)
        first_unused = (half_row0 + half_ref[e] * MOE_HALF) // MOE_HALF
        n_halves = ys_hbm.shape[0] // MOE_HALF
        lax.fori_loop(first_unused, n_halves, lambda j, c: (half_out(j * MOE_HALF).start(), c)[1], 0)
        lax.fori_loop(first_unused, n_halves, lambda j, c: (half_out(j * MOE_HALF).wait(), c)[1], 0)


def _experts(base, n_blk, n_half, xs, w1, b1, w2, b2):
    P = xs.shape[0]
    expert = lambda e, bs, nb, nh: (e, 0, 0)
    grid_spec = pltpu.PrefetchScalarGridSpec(
        num_scalar_prefetch=3,
        grid=(N_EXPERTS,),
        in_specs=[pl.BlockSpec((None, D_MODEL, 2 * D_FF), expert),
                  pl.BlockSpec((None, 1, 2 * D_FF), expert),
                  pl.BlockSpec((None, D_FF, D_MODEL), expert),
                  pl.BlockSpec((None, 1, D_MODEL), expert),
                  pl.BlockSpec(memory_space=pl.ANY)],
        out_specs=pl.BlockSpec(memory_space=pl.ANY),
        scratch_shapes=[pltpu.VMEM((D_MODEL, 2 * D_FF), jnp.bfloat16),
                        pltpu.VMEM((D_FF, D_MODEL), jnp.bfloat16),
                        pltpu.VMEM((2, MOE_BLOCK, D_MODEL // 2), jnp.uint32),
                        pltpu.VMEM((2, MOE_BLOCK, D_MODEL // 2), jnp.uint32),
                        pltpu.VMEM((MOE_HALF, D_MODEL // 2), jnp.uint32),
                        pltpu.VMEM((MOE_HALF, D_MODEL // 2), jnp.uint32),
                        pltpu.SemaphoreType.DMA((3,)),
                        pltpu.SemaphoreType.DMA((3,)),
                        pltpu.SMEM((3,), jnp.int32)],
    )
    return pl.pallas_call(
        _experts_kernel,
        out_shape=jax.ShapeDtypeStruct((P, D_MODEL // 2), jnp.uint32),
        grid_spec=grid_spec,
        compiler_params=_params(("arbitrary",)),
        name="experts",
    )(base, n_blk, n_half, w1, b1, w2, b2, xs)


def _combine_kernel(src_ref, len_ref, dst_ref, rows_ref, gate_t_ref, x1_ref, ys_hbm, o_ref, ybuf_ref, sems):
    i = pl.program_id(0)
    n_tiles = pl.num_programs(0)
    n_buf = ybuf_ref.shape[0]
    cur = i % n_buf
    n_rows, tm = ybuf_ref.shape[1], x1_ref.shape[0]

    def fetch(tile, slot):
        _run_copies(src_ref, len_ref, dst_ref, tile,
                    lambda s, d, n: pltpu.make_async_copy(ys_hbm.at[pl.ds(d, n)], ybuf_ref.at[slot, pl.ds(s, n)],
                                                          sems.at[slot]))

    @pl.when(i == 0)
    def _():
        ybuf_ref[...] = jnp.zeros(ybuf_ref.shape, ybuf_ref.dtype)
        for tile in range(n_buf - 1):
            fetch(tile, tile)

    ahead = i + n_buf - 1

    @pl.when(ahead < n_tiles)
    def _():
        fetch(ahead, ahead % n_buf)

    g = gate_t_ref[...]
    p_iota = lax.broadcasted_iota(jnp.int32, (tm, n_rows), 1)
    weights = jnp.zeros((tm, n_rows), jnp.float32)
    for k in range(TOP_K):
        pos_k = g[:, TOP_K + k:TOP_K + k + 1].astype(jnp.int32)
        weights = jnp.where(p_iota == pos_k, g[:, k:k + 1], weights)
    weights = weights.astype(jnp.bfloat16)

    rows = pl.multiple_of(rows_ref[i], RUN_ALIGN)
    pltpu.make_async_copy(ys_hbm.at[pl.ds(0, rows)], ybuf_ref.at[cur, pl.ds(0, rows)], sems.at[cur]).wait()
    halves = [jnp.dot(weights, y, preferred_element_type=jnp.float32) for y in _unpack_bf16_pairs(ybuf_ref[cur])]
    o_ref[...] = x1_ref[...] + jnp.concatenate(halves, axis=1)


def _combine(run_src, run_len, run_dst, tile_rows, gate_t, x1, ys):
    T = x1.shape[0]
    tok = lambda w: pl.BlockSpec((TILE, w), lambda i, *_: (i, 0))
    grid_spec = pltpu.PrefetchScalarGridSpec(
        num_scalar_prefetch=4,
        grid=(T // TILE,),
        in_specs=[tok(LANES), tok(D_MODEL), pl.BlockSpec(memory_space=pl.ANY)],
        out_specs=tok(D_MODEL),
        scratch_shapes=[pltpu.VMEM((DISPATCH_BUFS, TILE_ROWS, D_MODEL // 2), jnp.uint32),
                        pltpu.SemaphoreType.DMA((DISPATCH_BUFS,))],
    )
    return pl.pallas_call(
        _combine_kernel,
        out_shape=jax.ShapeDtypeStruct((T, D_MODEL), jnp.float32),
        grid_spec=grid_spec,
        compiler_params=_params(("arbitrary",)),
        name="combine",
    )(run_src, run_len, run_dst, tile_rows, gate_t, x1, ys)


def _rotary_tables(positions):
    pos = positions.astype(jnp.float32)[..., None]
    lane = jnp.arange(LANES)
    half_r = RET_DK // 2
    inv_r = RET_ROPE_THETA ** (-jnp.linspace(0.0, 1.0, half_r, dtype=jnp.float32))
    ret_cs = jnp.cos(pos * inv_r[lane % half_r] - jnp.where(lane < half_r, 0.0, 0.5 * jnp.pi))
    half_d = ROT_DIM // 2
    inv_d = ROPE_THETA ** (-jnp.arange(0, ROT_DIM, 2, dtype=jnp.float32) / ROT_DIM)
    sub = lane % DIFF_DH
    ang_d = jnp.where(sub < ROT_DIM, pos * inv_d[sub % half_d], 0.0)
    diff_cs = jnp.cos(ang_d - jnp.where((sub >= half_d) & (sub < ROT_DIM), 0.5 * jnp.pi, 0.0))
    return ret_cs, diff_cs


def kernel(x, positions, norm1_w, w_in, ret_log_decay_fwd, ret_log_decay_bwd, ret_norm_w, q_norm_w, k_norm_w, lambda_q1, lambda_k1, lambda_q2, lambda_k2, diff_norm_w, w_out, norm2_w, w_router, b_router, w1, b1, w2, b2):
    B, S, D = x.shape
    T = B * S
    f32 = jnp.float32
    bf16 = jnp.bfloat16
    x2 = x.reshape(T, D)

    dup = lambda w: jnp.concatenate([w, w]).reshape(1, LANES).astype(f32)
    proj, rq_r, rk_r, qs, ks = _in_proj(x2, norm1_w[0].reshape(1, D), w_in[0].astype(bf16),
                                        _rotary_tables(positions), dup(q_norm_w[0]), dup(k_norm_w[0]), B, S)

    y_ret = _retention(ret_log_decay_fwd[0].astype(f32), ret_log_decay_bwd[0].astype(f32),
                       rq_r, rk_r, proj, ret_norm_w[0].reshape(1, RET_WIDTH).astype(f32), B, S)

    lam = (jnp.exp(jnp.sum(lambda_q1[0].astype(f32) * lambda_k1[0].astype(f32)))
           - jnp.exp(jnp.sum(lambda_q2[0].astype(f32) * lambda_k2[0].astype(f32))) + LAMBDA_INIT)
    lam_row = jnp.full((1, LANES), lam, f32)
    bound = (SCORE_BOUND_SLACK * DIFF_DH ** 0.5 * LOG2_E
             * jnp.max(jnp.abs(q_norm_w[0].astype(f32))) * jnp.max(jnp.abs(k_norm_w[0].astype(f32)))).reshape(1)
    attn_args = (bound, qs, ks, proj, lam_row, diff_norm_w[0].reshape(1, DIFF_DV).astype(f32), B, S)
    y_diff = lax.cond(bound[0] <= MAX_SAFE_SCORE_BOUND,
                      lambda: _diff_attn(False, *attn_args), lambda: _diff_attn(True, *attn_args))

    x1, h2, pos, gate_t, len_t, off_t, tot_t = _out_router(
        x2, y_ret, y_diff, w_out[0].astype(bf16), norm2_w[0].reshape(1, D),
        w_router[0].T.astype(f32), b_router[0].reshape(N_EXPERTS, 1).astype(f32))

    n_tiles = T // TILE
    run_len = len_t[:, :, 0].astype(jnp.int32)
    total = tot_t[:, 0].astype(jnp.int32)
    padded = ((total + MOE_HALF - 1) // MOE_HALF) * MOE_HALF
    pad_end = jnp.cumsum(padded)
    pad_start = pad_end - padded
    run_dst = pad_start[None, :] + off_t[:, :, 0].astype(jnp.int32)
    run_src = jnp.cumsum(run_len, axis=1) - run_len
    tile_rows = jnp.sum(run_len, axis=1)
    P = T * TOP_K + n_tiles * N_EXPERTS * RUN_ALIGN + N_EXPERTS * MOE_HALF
    n_used = (pad_end[-1:] // MOE_HALF).astype(jnp.int32)
    runs = (run_src.reshape(-1), run_len.reshape(-1), run_dst.reshape(-1), tile_rows)

    xs = _dispatch(*runs, pad_start + total, padded - total, n_used, pos, h2, P)
    ys = _experts(pad_start, padded // MOE_BLOCK, (padded // MOE_HALF) % 2, xs, w1[0],
                  b1[0].reshape(N_EXPERTS, 1, 2 * D_FF), w2[0], b2[0].reshape(N_EXPERTS, 1, D))
    out = _combine(*runs, gate_t, x1, ys)
    return out.reshape(B, S, D)
```

```python
import functools

import jax
import jax.numpy as jnp
from jax import lax
from jax.experimental import pallas as pl
from jax.experimental.pallas import tpu as pltpu

EPS = 1e-6
D_MODEL = 1024
RET_HEADS = 4
RET_DK = 128
RET_WIDTH = 512
RET_ROPE_THETA = 10000.0
DIFF_HEADS = 4
DIFF_DH = 64
DIFF_DV = 128
DIFF_WIDTH = 512
ROPE_THETA = 500000.0
ROT_DIM = DIFF_DH // 4
D_IN_PROJ = 3584
N_EXPERTS = 32
TOP_K = 4
D_FF = 1024
SWIGLU_LIMIT = 7.0
SWIGLU_ALPHA = 1.702
LAMBDA_INIT = 0.8 - 0.6 * 1.0

LOG2_E = 1.4426950408889634
SCORE_BOUND_SLACK = 1.02
MAX_SAFE_SCORE_BOUND = 60.0
LANES = 128
SUBLANES = 8
VMEM_LIMIT = 56 * 1024 * 1024

COL_RQ, COL_RK, COL_RV, COL_RG, COL_DQ, COL_DK, COL_DV = 0, 4, 8, 12, 16, 20, 24
VGV_RV, VGV_RG, VGV_DV = 0, 4, 8

TM_PROJ = 512
RET_CHUNK = 128
RET_UNROLL = 32
TQ_ATTN = 1024
TK_ATTN = 2048
TILE = 512
MOE_BLOCK = 512
MOE_HALF = MOE_BLOCK // 2
RUN_ALIGN = SUBLANES
TILE_ROWS = TOP_K * TILE + N_EXPERTS * RUN_ALIGN
DISPATCH_BUFS = 2


def _params(sem, **kw):
    return pltpu.CompilerParams(dimension_semantics=sem, vmem_limit_bytes=VMEM_LIMIT, **kw)


def _in_proj_kernel(x_ref, nw_ref, w_ref, ret_cs_ref, diff_cs_ref, qw_ref, kw_ref,
                    vgv_ref, rqo_ref, rko_ref, qs_ref, ks_ref):
    ts = x_ref.shape[0]
    x = x_ref[...]
    hn = (x * lax.rsqrt(jnp.mean(x * x, axis=-1, keepdims=True) + EPS) * nw_ref[...]).astype(jnp.bfloat16)

    def proj(col_block):
        c0 = col_block * LANES
        return jnp.dot(hn, w_ref[:, c0:c0 + 4 * LANES], preferred_element_type=jnp.float32)

    dq, dk, rq, rk = proj(COL_DQ), proj(COL_DK), proj(COL_RQ), proj(COL_RK)
    lane = lax.broadcasted_iota(jnp.int32, (ts, LANES), 1)
    lo = lane < DIFF_DH
    ret_cs = ret_cs_ref[...]
    ret_sc = pltpu.roll(ret_cs, RET_DK // 2, 1)
    first_half = lane < RET_DK // 2
    c2 = jnp.where(first_half, ret_cs, ret_sc)
    s2 = jnp.where(first_half, -ret_sc, ret_cs)
    sub = lane % DIFF_DH
    sin_lanes = (sub >= ROT_DIM // 2) & (sub < ROT_DIM)
    diff_cs = diff_cs_ref[...]
    ra = jnp.where(sin_lanes, pltpu.roll(diff_cs, ROT_DIM // 2, 1), diff_cs)
    rp = jnp.where(sin_lanes, diff_cs, 0.0)
    rn = jnp.where(sub < ROT_DIM // 2, -pltpu.roll(diff_cs, LANES - ROT_DIM // 2, 1), 0.0)

    def qk_norm_rot(x, w):
        x2 = x * x
        s_lo = jnp.sum(jnp.where(lo, x2, 0.0), axis=-1, keepdims=True)
        s_hi = jnp.sum(jnp.where(lo, 0.0, x2), axis=-1, keepdims=True)
        ms = jnp.where(lo, s_lo, s_hi) * (1.0 / DIFF_DH)
        xn = x * lax.rsqrt(ms + EPS) * w
        return xn * ra + pltpu.roll(xn, ROT_DIM // 2, 1) * rp + pltpu.roll(xn, LANES - ROT_DIM // 2, 1) * rn

    for h in range(DIFF_HEADS):
        sl = slice(h * LANES, (h + 1) * LANES)
        q = qk_norm_rot(dq[:, sl], qw_ref[...]) * (DIFF_DH ** -0.5 * LOG2_E)
        k = qk_norm_rot(dk[:, sl], kw_ref[...])
        qs_ref[h, 0] = jnp.where(lo, q, 0.0).astype(qs_ref.dtype)
        qs_ref[h, 1] = jnp.where(lo, 0.0, q).astype(qs_ref.dtype)
        ks_ref[:, sl] = k.astype(ks_ref.dtype)
    for h in range(RET_HEADS):
        sl = slice(h * LANES, (h + 1) * LANES)
        q = rq[:, sl]
        k = rk[:, sl]
        rqo_ref[:, sl] = (q * c2 + pltpu.roll(q, RET_DK // 2, 1) * s2).astype(rqo_ref.dtype)
        rko_ref[:, sl] = ((k * c2 + pltpu.roll(k, RET_DK // 2, 1) * s2) * (RET_DK ** -0.5)).astype(rko_ref.dtype)
    for slot, col_block in enumerate((COL_RV, COL_RG, COL_DV)):
        vgv_ref[:, slot * 4 * LANES:(slot + 1) * 4 * LANES] = proj(col_block).astype(vgv_ref.dtype)


def _in_proj(x2, nw, w_bf16, tabs, qw2, kw2, B, S):
    T = B * S
    n_s = S // TM_PROJ
    tok = lambda w: pl.BlockSpec((TM_PROJ, w), lambda i: (i, 0))
    const = lambda s: pl.BlockSpec(s, lambda i: (0, 0))
    tab = pl.BlockSpec((None, TM_PROJ, LANES), lambda i: (i // n_s, i % n_s, 0))
    bf16 = jnp.bfloat16
    return pl.pallas_call(
        _in_proj_kernel,
        out_shape=(jax.ShapeDtypeStruct((T, 3 * 4 * LANES), bf16),
                   jax.ShapeDtypeStruct((T, 4 * LANES), bf16),
                   jax.ShapeDtypeStruct((T, 4 * LANES), bf16),
                   jax.ShapeDtypeStruct((B, DIFF_HEADS, 2, S, LANES), bf16),
                   jax.ShapeDtypeStruct((T, 4 * LANES), bf16)),
        grid=(T // TM_PROJ,),
        in_specs=[tok(D_MODEL), const((1, D_MODEL)), const((D_MODEL, D_IN_PROJ)),
                  tab, tab, const((1, LANES)), const((1, LANES))],
        out_specs=(tok(3 * 4 * LANES), tok(4 * LANES), tok(4 * LANES),
                   pl.BlockSpec((None, DIFF_HEADS, 2, TM_PROJ, LANES), lambda i: (i // n_s, 0, 0, i % n_s, 0)),
                   tok(4 * LANES)),
        compiler_params=_params(("arbitrary",)),
        name="in_proj",
    )(x2, nw, w_bf16, *tabs, qw2, kw2)


def _retention_kernel(ldf_ref, ldb_ref, q_ref, k_ref, v_ref, g_ref, nw_ref, o_ref, sb_ref):
    C = RET_CHUNK
    S = q_ref.shape[0]
    n_chunks = S // C
    h = pl.program_id(1)
    ldf = ldf_ref[h]
    ldb = ldb_ref[h]
    row = lax.broadcasted_iota(jnp.int32, (C, C), 0).astype(jnp.float32)
    colm = lax.broadcasted_iota(jnp.int32, (C, C), 1).astype(jnp.float32)
    dist = row - colm
    decay = jnp.where(dist >= 0, jnp.exp(ldf * jnp.maximum(dist, 0.0)), jnp.exp(ldb * jnp.maximum(-dist, 0.0)))
    idx = lax.broadcasted_iota(jnp.int32, (C, 1), 0).astype(jnp.float32)
    q_dec_f = jnp.exp(ldf * (idx + 1.0))
    k_dec_f = jnp.exp(ldf * (C - 1.0 - idx))
    q_dec_b = jnp.exp(ldb * (C - idx))
    k_dec_b = jnp.exp(ldb * idx)
    chunk_dec_f = jnp.exp(ldf * C)
    chunk_dec_b = jnp.exp(ldb * C)
    f32 = jnp.float32
    bf16 = jnp.bfloat16

    def kv_state(k, v, k_dec):
        kd = (k.astype(f32) * k_dec).astype(bf16)
        return lax.dot_general(kd, v, (((0,), (0,)), ((), ())), preferred_element_type=f32)

    def bwd_step(i, state):
        c = n_chunks - 1 - i
        r0 = pl.multiple_of(c * C, C)
        sb_ref[c] = state
        return state * chunk_dec_b + kv_state(k_ref[pl.ds(r0, C), :], v_ref[pl.ds(r0, C), :], k_dec_b)

    lax.fori_loop(0, n_chunks, bwd_step, jnp.zeros((RET_DK, LANES), f32), unroll=RET_UNROLL)

    def fwd_step(c, state):
        r0 = pl.multiple_of(c * C, C)
        q = q_ref[pl.ds(r0, C), :]
        k = k_ref[pl.ds(r0, C), :]
        v = v_ref[pl.ds(r0, C), :]
        scores = lax.dot_general(q, k, (((1,), (1,)), ((), ())), preferred_element_type=f32) * decay
        y = jnp.dot(scores.astype(bf16), v, preferred_element_type=f32)
        qf = q.astype(f32)
        y += jnp.dot((qf * q_dec_f).astype(bf16), state.astype(bf16), preferred_element_type=f32)
        y += jnp.dot((qf * q_dec_b).astype(bf16), sb_ref[c].astype(bf16), preferred_element_type=f32)
        yn = y * lax.rsqrt(jnp.mean(y * y, axis=-1, keepdims=True) + EPS) * nw_ref[...]
        g = g_ref[pl.ds(r0, C), :].astype(f32)
        o_ref[pl.ds(r0, C), :] = (yn * (g * jax.nn.sigmoid(g))).astype(o_ref.dtype)
        return state * chunk_dec_f + kv_state(k, v, k_dec_f)

    lax.fori_loop(0, n_chunks, fwd_step, jnp.zeros((RET_DK, LANES), f32), unroll=RET_UNROLL)


def _retention(ldf, ldb, rq_r, rk_r, proj, nw, B, S):
    T = B * S
    smem = pl.BlockSpec(memory_space=pltpu.SMEM)
    seq = lambda cb: pl.BlockSpec((S, LANES), lambda b, h: (b, cb + h))
    return pl.pallas_call(
        _retention_kernel,
        out_shape=jax.ShapeDtypeStruct((T, RET_WIDTH), jnp.bfloat16),
        grid=(B, RET_HEADS),
        in_specs=[smem, smem, seq(0), seq(0), seq(VGV_RV), seq(VGV_RG),
                  pl.BlockSpec((1, LANES), lambda b, h: (0, h))],
        out_specs=seq(0),
        scratch_shapes=[pltpu.VMEM((S // RET_CHUNK, RET_DK, LANES), jnp.float32)],
        compiler_params=_params(("arbitrary", "arbitrary")),
        name="retention",
    )(ldf, ldb, rq_r, rk_r, proj, proj, nw)


def _diff_attn_kernel(online_max, bound_ref, q_ref, k_ref, v_ref, lam_ref, nw_ref, o_ref, m_ref, l_ref, acc_ref):
    tq = q_ref.shape[1]
    S = k_ref.shape[0]
    f32 = jnp.float32
    q = q_ref[...].reshape(2 * tq, LANES)
    if online_max:
        m_ref[...] = jnp.full(m_ref.shape, -jnp.inf, f32)
    l_ref[...] = jnp.zeros(l_ref.shape, f32)
    acc_ref[...] = jnp.zeros(acc_ref.shape, f32)
    n_tiles = TK_ATTN // LANES

    def kv_step(j, carry):
        r0 = pl.multiple_of(j * TK_ATTN, TK_ATTN)
        k = k_ref[pl.ds(r0, TK_ATTN), :]
        v = v_ref[pl.ds(r0, TK_ATTN), :]
        s = lax.dot_general(q, k, (((1,), (1,)), ((), ())), preferred_element_type=f32)
        tiles = [s[:, c * LANES:(c + 1) * LANES] for c in range(n_tiles)]
        if online_max:
            part = tiles[0]
            for t in tiles[1:]:
                part = jnp.maximum(part, t)
            m_prev = m_ref[...]
            shift = jnp.maximum(m_prev, jnp.max(part, axis=-1, keepdims=True))
            alpha = jnp.exp2(m_prev - shift)
            m_ref[...] = shift
        else:
            shift = bound_ref[0]
        probs = [jnp.exp2(t - shift) for t in tiles]
        psum = probs[0]
        for p in probs[1:]:
            psum = psum + p
        pv = jnp.dot(jnp.concatenate([p.astype(jnp.bfloat16) for p in probs], axis=1), v,
                     preferred_element_type=f32)
        if online_max:
            l_ref[...] = alpha * l_ref[...] + psum
            acc_ref[...] = alpha * acc_ref[...] + pv
        else:
            l_ref[...] = l_ref[...] + psum
            acc_ref[...] = acc_ref[...] + pv
        return carry

    lax.fori_loop(0, S // TK_ATTN, kv_step, 0)
    o = acc_ref[...] / jnp.sum(l_ref[...], axis=-1, keepdims=True)
    d = o[:tq] - lam_ref[...] * o[tq:]
    dn = d * lax.rsqrt(jnp.mean(d * d, axis=-1, keepdims=True) + EPS) * nw_ref[...]
    o_ref[...] = (dn * (1.0 - LAMBDA_INIT)).astype(o_ref.dtype)


def _diff_attn(online_max, bound, qs, ks, proj, lam, nw, B, S):
    T = B * S
    n_q = S // TQ_ATTN
    one = pl.BlockSpec((1, LANES), lambda b, h, i, bd: (0, 0))
    grid_spec = pltpu.PrefetchScalarGridSpec(
        num_scalar_prefetch=1,
        grid=(B, DIFF_HEADS, n_q),
        in_specs=[pl.BlockSpec((None, None, 2, TQ_ATTN, LANES), lambda b, h, i, bd: (b, h, 0, i, 0)),
                  pl.BlockSpec((S, LANES), lambda b, h, i, bd: (b, h)),
                  pl.BlockSpec((S, LANES), lambda b, h, i, bd: (b, VGV_DV + h)),
                  one, one],
        out_specs=pl.BlockSpec((TQ_ATTN, LANES), lambda b, h, i, bd: (b * n_q + i, h)),
        scratch_shapes=[pltpu.VMEM((2 * TQ_ATTN, LANES), jnp.float32)] * 3,
    )
    return pl.pallas_call(
        functools.partial(_diff_attn_kernel, online_max),
        out_shape=jax.ShapeDtypeStruct((T, DIFF_WIDTH), jnp.bfloat16),
        grid_spec=grid_spec,
        compiler_params=_params(("arbitrary", "arbitrary", "arbitrary")),
        name="diff_attn_online" if online_max else "diff_attn",
    )(bound, qs, ks, proj, lam, nw)


def _out_router_kernel(x_ref, yr_ref, yd_ref, wo_ref, n2_ref, wrt_ref, br_ref,
                       x1_ref, h2_ref, pos_ref, gate_t_ref, len_ref, off_ref, tot_ref):
    tm = x_ref.shape[0]
    f32 = jnp.float32
    bf16 = jnp.bfloat16

    @pl.when(pl.program_id(0) == 0)
    def _():
        tot_ref[...] = jnp.zeros(tot_ref.shape, f32)

    att = jnp.dot(yr_ref[...], wo_ref[:RET_WIDTH, :], preferred_element_type=f32)
    att += jnp.dot(yd_ref[...], wo_ref[RET_WIDTH:, :], preferred_element_type=f32)
    x1 = x_ref[...] + att
    x1_ref[...] = x1
    h2 = x1 * lax.rsqrt(jnp.mean(x1 * x1, axis=-1, keepdims=True) + EPS) * n2_ref[...]
    h2_ref[...] = h2.astype(h2_ref.dtype)
    nt = (((1,), (1,)), ((), ()))
    h_hi = h2.astype(bf16)
    h_lo = (h2 - h_hi.astype(f32)).astype(bf16)
    w = wrt_ref[...]
    w_hi = w.astype(bf16)
    w_lo = (w - w_hi.astype(f32)).astype(bf16)
    logits = (lax.dot_general(w_hi, h_hi, nt, preferred_element_type=f32)
              + lax.dot_general(w_lo, h_hi, nt, preferred_element_type=f32)
              + lax.dot_general(w_hi, h_lo, nt, preferred_element_type=f32)) + br_ref[...]
    e_iota = lax.broadcasted_iota(jnp.int32, (N_EXPERTS, tm), 0)
    work = logits
    vals, hots = [], []
    for _ in range(TOP_K):
        mx = jnp.max(work, axis=0, keepdims=True)
        ix = jnp.min(jnp.where(work == mx, e_iota, N_EXPERTS), axis=0, keepdims=True)
        hot = e_iota == ix
        vals.append(mx)
        hots.append(hot)
        work = jnp.where(hot, -jnp.inf, work)
    exps = [jnp.exp(v - vals[0]) for v in vals]
    denom = exps[0] + exps[1] + exps[2] + exps[3]
    gates = [e / denom for e in exps]
    sel = jnp.zeros((N_EXPERTS, tm), f32)
    for hot in hots:
        sel = jnp.where(hot, 1.0, sel)
    t_row = lax.broadcasted_iota(jnp.int32, (tm, tm), 0)
    t_col = lax.broadcasted_iota(jnp.int32, (tm, tm), 1)
    upper = jnp.where(t_row < t_col, 1.0, 0.0).astype(bf16)
    rank = jnp.dot(sel.astype(bf16), upper, preferred_element_type=f32)
    cnt = jnp.sum(sel, axis=1, keepdims=True)
    run_units = jnp.floor((cnt + (RUN_ALIGN - 1.0)) * (1.0 / RUN_ALIGN))
    run_len = jnp.broadcast_to(run_units * RUN_ALIGN, (N_EXPERTS, LANES))
    e_row = lax.broadcasted_iota(jnp.int32, (N_EXPERTS, N_EXPERTS), 0)
    e_col = lax.broadcasted_iota(jnp.int32, (N_EXPERTS, N_EXPERTS), 1)
    lower = jnp.where(e_col < e_row, 1.0, 0.0).astype(bf16)
    run_start = jnp.dot(lower, jnp.broadcast_to(run_units, (N_EXPERTS, LANES)).astype(bf16),
                        preferred_element_type=f32) * RUN_ALIGN
    pos_full = rank + run_start[:, 0:1]
    pos = [jnp.sum(jnp.where(hot, pos_full, 0.0), axis=0, keepdims=True) for hot in hots]
    for k in range(TOP_K):
        pos_ref[k:k + 1, :] = pos[k].astype(jnp.int32)
    rows = jnp.concatenate(gates + pos + [jnp.zeros((LANES - 2 * TOP_K, tm), f32)], axis=0)
    gate_t_ref[...] = rows.T
    len_ref[0] = run_len
    off_ref[0] = tot_ref[...]
    tot_ref[...] = tot_ref[...] + run_len


def _out_router(x2, y_ret, y_diff, wo_bf16, n2w, wrt, br):
    T = x2.shape[0]
    n_tiles = T // TILE
    tok = lambda w: pl.BlockSpec((TILE, w), lambda i: (i, 0))
    const = lambda s: pl.BlockSpec(s, lambda i: (0, 0))
    per_tile = pl.BlockSpec((1, N_EXPERTS, LANES), lambda i: (i, 0, 0))
    return pl.pallas_call(
        _out_router_kernel,
        out_shape=(jax.ShapeDtypeStruct((T, D_MODEL), jnp.float32),
                   jax.ShapeDtypeStruct((T, D_MODEL), jnp.bfloat16),
                   jax.ShapeDtypeStruct((TOP_K, T), jnp.int32),
                   jax.ShapeDtypeStruct((T, LANES), jnp.float32),
                   jax.ShapeDtypeStruct((n_tiles, N_EXPERTS, LANES), jnp.float32),
                   jax.ShapeDtypeStruct((n_tiles, N_EXPERTS, LANES), jnp.float32),
                   jax.ShapeDtypeStruct((N_EXPERTS, LANES), jnp.float32)),
        grid=(n_tiles,),
        in_specs=[tok(D_MODEL), tok(RET_WIDTH), tok(DIFF_WIDTH), const((D_MODEL, D_MODEL)),
                  const((1, D_MODEL)), const((N_EXPERTS, D_MODEL)), const((N_EXPERTS, 1))],
        out_specs=(tok(D_MODEL), tok(D_MODEL), pl.BlockSpec((TOP_K, TILE), lambda i: (0, i)), tok(LANES),
                   per_tile, per_tile, const((N_EXPERTS, LANES))),
        compiler_params=_params(("arbitrary",)),
        name="out_router",
    )(x2, y_ret, y_diff, wo_bf16, n2w, wrt, br)


def _pack_bf16_pairs(x):
    w = x.shape[1] // 2
    bits = lambda v: lax.bitcast_convert_type(v.astype(jnp.bfloat16).astype(jnp.float32), jnp.uint32)
    return (bits(x[:, :w]) >> 16) | (bits(x[:, w:]) & jnp.uint32(0xFFFF0000))


def _unpack_bf16_pairs(p):
    as_bf16 = lambda bits: lax.bitcast_convert_type(bits, jnp.float32).astype(jnp.bfloat16)
    return as_bf16(p << 16), as_bf16(p & jnp.uint32(0xFFFF0000))


def _run_copies(src_ref, len_ref, dst_ref, tile, make_copy):
    for e in range(N_EXPERTS):
        n = pl.multiple_of(len_ref[tile * N_EXPERTS + e], RUN_ALIGN)
        s = pl.multiple_of(src_ref[tile * N_EXPERTS + e], RUN_ALIGN)
        d = pl.multiple_of(dst_ref[tile * N_EXPERTS + e], RUN_ALIGN)

        @pl.when(n > 0)
        def _():
            make_copy(s, d, n).start()


def _dispatch_kernel(src_ref, len_ref, dst_ref, rows_ref, zlo_ref, zlen_ref, nu_ref,
                     pos_ref, h2_ref, xs_hbm, xbuf_ref, zero_ref, sems, zero_sem):
    i = pl.program_id(0)
    n_tiles = pl.num_programs(0)
    n_buf = xbuf_ref.shape[0]
    cur = i % n_buf
    n_rows, tm = xbuf_ref.shape[1], h2_ref.shape[0]

    @pl.when(i == 0)
    def _():
        zero_ref[...] = jnp.zeros(zero_ref.shape, zero_ref.dtype)

        def pad_copy(e):
            n = pl.multiple_of(zlen_ref[e], RUN_ALIGN)
            lo = pl.multiple_of(zlo_ref[e], RUN_ALIGN)
            return pltpu.make_async_copy(zero_ref.at[pl.ds(0, n)], xs_hbm.at[pl.ds(lo, n)], zero_sem)

        def tail_copy(j):
            return pltpu.make_async_copy(zero_ref, xs_hbm.at[pl.ds(j * MOE_HALF, MOE_HALF)], zero_sem)

        def guarded(copy, op):
            def body(e, c):
                @pl.when(zlen_ref[e] > 0)
                def _():
                    op(copy(e))
                return c
            return body

        lax.fori_loop(0, N_EXPERTS, guarded(pad_copy, lambda cp: cp.start()), 0)
        lax.fori_loop(0, N_EXPERTS, guarded(pad_copy, lambda cp: cp.wait()), 0)
        n_halves = xs_hbm.shape[0] // MOE_HALF
        lax.fori_loop(nu_ref[0], n_halves, lambda j, c: (tail_copy(j).start(), c)[1], 0)
        lax.fori_loop(nu_ref[0], n_halves, lambda j, c: (tail_copy(j).wait(), c)[1], 0)

    p_iota = lax.broadcasted_iota(jnp.int32, (n_rows, tm), 0)
    onehot = jnp.zeros((n_rows, tm), jnp.float32)
    for k in range(TOP_K):
        onehot = jnp.where(p_iota == pos_ref[k:k + 1, :], 1.0, onehot)
    xbuf_ref[cur] = _pack_bf16_pairs(
        jnp.dot(onehot.astype(jnp.bfloat16), h2_ref[...], preferred_element_type=jnp.float32))

    _run_copies(src_ref, len_ref, dst_ref, i,
                lambda s, d, n: pltpu.make_async_copy(xbuf_ref.at[cur, pl.ds(s, n)], xs_hbm.at[pl.ds(d, n)],
                                                      sems.at[cur]))

    def wait_tile(tile, slot):
        rows = pl.multiple_of(rows_ref[tile], RUN_ALIGN)
        pltpu.make_async_copy(xbuf_ref.at[slot, pl.ds(0, rows)], xs_hbm.at[pl.ds(0, rows)], sems.at[slot]).wait()

    oldest = n_buf - 1

    @pl.when(i >= oldest)
    def _():
        wait_tile(i - oldest, (i + 1) % n_buf)

    @pl.when(i == n_tiles - 1)
    def _():
        for back in range(oldest - 1, -1, -1):
            wait_tile(i - back, (i - back) % n_buf)


def _dispatch(run_src, run_len, run_dst, tile_rows, zero_lo, zero_len, n_used, pos, h2, P):
    T = h2.shape[0]
    n_pre = 7
    grid_spec = pltpu.PrefetchScalarGridSpec(
        num_scalar_prefetch=n_pre,
        grid=(T // TILE,),
        in_specs=[pl.BlockSpec((TOP_K, TILE), lambda i, *_: (0, i)),
                  pl.BlockSpec((TILE, D_MODEL), lambda i, *_: (i, 0))],
        out_specs=pl.BlockSpec(memory_space=pl.ANY),
        scratch_shapes=[pltpu.VMEM((DISPATCH_BUFS, TILE_ROWS, D_MODEL // 2), jnp.uint32),
                        pltpu.VMEM((MOE_HALF, D_MODEL // 2), jnp.uint32),
                        pltpu.SemaphoreType.DMA((DISPATCH_BUFS,)),
                        pltpu.SemaphoreType.DMA(())],
    )
    return pl.pallas_call(
        _dispatch_kernel,
        out_shape=jax.ShapeDtypeStruct((P, D_MODEL // 2), jnp.uint32),
        grid_spec=grid_spec,
        compiler_params=_params(("arbitrary",), has_side_effects=True),
        name="dispatch",
    )(run_src, run_len, run_dst, tile_rows, zero_lo, zero_len, n_used, pos, h2)


def _experts_kernel(base_ref, nblk_ref, half_ref, w1_ref, b1_ref, w2_ref, b2_ref, xs_hbm, ys_hbm,
                    w1b_ref, w2b_ref, xbuf_ref, ybuf_ref, xhalf_ref, yhalf_ref, in_sems, out_sems, busy_ref):
    e = pl.program_id(0)
    n = nblk_ref[e]
    has_half = half_ref[e] == 1
    HALF_BUF = 2

    def rows(expert, j):
        return pl.ds(pl.multiple_of(base_ref[expert] + j * MOE_BLOCK, MOE_HALF), MOE_BLOCK)

    def half_rows(first_row):
        return pl.ds(pl.multiple_of(first_row, MOE_HALF), MOE_HALF)

    def in_copy(expert, j, slot):
        return pltpu.make_async_copy(xs_hbm.at[rows(expert, j)], xbuf_ref.at[slot], in_sems.at[slot])

    def out_copy(j, slot):
        return pltpu.make_async_copy(ybuf_ref.at[slot], ys_hbm.at[rows(e, j)], out_sems.at[slot])

    half_row0 = base_ref[e] + n * MOE_BLOCK
    half_in = pltpu.make_async_copy(xs_hbm.at[half_rows(half_row0)], xhalf_ref, in_sems.at[HALF_BUF])

    def half_out(first_row):
        return pltpu.make_async_copy(yhalf_ref, ys_hbm.at[half_rows(first_row)], out_sems.at[HALF_BUF])

    def wait_out(buf, half=False):
        @pl.when(busy_ref[buf] == 1)
        def _():
            (half_out(0) if half else out_copy(0, buf)).wait()
            busy_ref[buf] = 0

    def mlp(x_packed):
        x = jnp.concatenate(_unpack_bf16_pairs(x_packed), axis=1)
        u = jnp.dot(x, w1b_ref[...], preferred_element_type=jnp.float32) + b1_ref[...]
        glu = jnp.minimum(u[:, :D_FF], SWIGLU_LIMIT)
        lin = jnp.clip(u[:, D_FF:], -SWIGLU_LIMIT, SWIGLU_LIMIT)
        act = glu * jax.nn.sigmoid(SWIGLU_ALPHA * glu) * (lin + 1.0)
        return _pack_bf16_pairs(jnp.dot(act.astype(jnp.bfloat16), w2b_ref[...],
                                        preferred_element_type=jnp.float32) + b2_ref[...])

    @pl.when(e == 0)
    def _():
        for buf in range(3):
            busy_ref[buf] = 0

        @pl.when(n > 0)
        def _():
            in_copy(0, 0, 0).start()

    @pl.when(has_half)
    def _():
        half_in.start()

    @pl.when((n > 0) | has_half)
    def _():
        w1b_ref[...] = w1_ref[...].astype(jnp.bfloat16)
        w2b_ref[...] = w2_ref[...].astype(jnp.bfloat16)

    def block(j, carry):
        slot = j % 2

        @pl.when(j + 1 < n)
        def _():
            in_copy(e, j + 1, 1 - slot).start()

        in_copy(e, j, slot).wait()
        wait_out(slot)
        ybuf_ref[slot] = mlp(xbuf_ref[slot])
        out_copy(j, slot).start()
        busy_ref[slot] = 1
        return carry

    lax.fori_loop(0, n, block, 0)

    @pl.when(has_half)
    def _():
        half_in.wait()
        wait_out(HALF_BUF, half=True)
        yhalf_ref[...] = mlp(xhalf_ref[...])
        half_out(half_row0).start()
        busy_ref[HALF_BUF] = 1

    e_next = jnp.minimum(e + 1, N_EXPERTS - 1)

    @pl.when((e + 1 < N_EXPERTS) & (nblk_ref[e_next] > 0))
    def _():
        in_copy(e_next, 0, 0).start()

    @pl.when(e == N_EXPERTS - 1)
    def _():
        wait_out(0)
        wait_out(1)
        wait_out(HALF_BUF, half=True)
        yhalf_ref[...] = jnp.zeros(yhalf_ref.shape, yhalf_ref.dtype)
        first_unused = (half_row0 + half_ref[e] * MOE_HALF) // MOE_HALF
        n_halves = ys_hbm.shape[0] // MOE_HALF
        lax.fori_loop(first_unused, n_halves, lambda j, c: (half_out(j * MOE_HALF).start(), c)[1], 0)
        lax.fori_loop(first_unused, n_halves, lambda j, c: (half_out(j * MOE_HALF).wait(), c)[1], 0)


def _experts(base, n_blk, n_half, xs, w1, b1, w2, b2):
    P = xs.shape[0]
    expert = lambda e, bs, nb, nh: (e, 0, 0)
    grid_spec = pltpu.PrefetchScalarGridSpec(
        num_scalar_prefetch=3,
        grid=(N_EXPERTS,),
        in_specs=[pl.BlockSpec((None, D_MODEL, 2 * D_FF), expert),
                  pl.BlockSpec((None, 1, 2 * D_FF), expert),
                  pl.BlockSpec((None, D_FF, D_MODEL), expert),
                  pl.BlockSpec((None, 1, D_MODEL), expert),
                  pl.BlockSpec(memory_space=pl.ANY)],
        out_specs=pl.BlockSpec(memory_space=pl.ANY),
        scratch_shapes=[pltpu.VMEM((D_MODEL, 2 * D_FF), jnp.bfloat16),
                        pltpu.VMEM((D_FF, D_MODEL), jnp.bfloat16),
                        pltpu.VMEM((2, MOE_BLOCK, D_MODEL // 2), jnp.uint32),
                        pltpu.VMEM((2, MOE_BLOCK, D_MODEL // 2), jnp.uint32),
                        pltpu.VMEM((MOE_HALF, D_MODEL // 2), jnp.uint32),
                        pltpu.VMEM((MOE_HALF, D_MODEL // 2), jnp.uint32),
                        pltpu.SemaphoreType.DMA((3,)),
                        pltpu.SemaphoreType.DMA((3,)),
                        pltpu.SMEM((3,), jnp.int32)],
    )
    return pl.pallas_call(
        _experts_kernel,
        out_shape=jax.ShapeDtypeStruct((P, D_MODEL // 2), jnp.uint32),
        grid_spec=grid_spec,
        compiler_params=_params(("arbitrary",)),
        name="experts",
    )(base, n_blk, n_half, w1, b1, w2, b2, xs)


def _combine_kernel(src_ref, len_ref, dst_ref, rows_ref, gate_t_ref, x1_ref, ys_hbm, o_ref, ybuf_ref, sems):
    i = pl.program_id(0)
    n_tiles = pl.num_programs(0)
    n_buf = ybuf_ref.shape[0]
    cur = i % n_buf
    n_rows, tm = ybuf_ref.shape[1], x1_ref.shape[0]

    def fetch(tile, slot):
        _run_copies(src_ref, len_ref, dst_ref, tile,
                    lambda s, d, n: pltpu.make_async_copy(ys_hbm.at[pl.ds(d, n)], ybuf_ref.at[slot, pl.ds(s, n)],
                                                          sems.at[slot]))

    @pl.when(i == 0)
    def _():
        ybuf_ref[...] = jnp.zeros(ybuf_ref.shape, ybuf_ref.dtype)
        for tile in range(n_buf - 1):
            fetch(tile, tile)

    ahead = i + n_buf - 1

    @pl.when(ahead < n_tiles)
    def _():
        fetch(ahead, ahead % n_buf)

    g = gate_t_ref[...]
    p_iota = lax.broadcasted_iota(jnp.int32, (tm, n_rows), 1)
    weights = jnp.zeros((tm, n_rows), jnp.float32)
    for k in range(TOP_K):
        pos_k = g[:, TOP_K + k:TOP_K + k + 1].astype(jnp.int32)
        weights = jnp.where(p_iota == pos_k, g[:, k:k + 1], weights)
    weights = weights.astype(jnp.bfloat16)

    rows = pl.multiple_of(rows_ref[i], RUN_ALIGN)
    pltpu.make_async_copy(ys_hbm.at[pl.ds(0, rows)], ybuf_ref.at[cur, pl.ds(0, rows)], sems.at[cur]).wait()
    halves = [jnp.dot(weights, y, preferred_element_type=jnp.float32) for y in _unpack_bf16_pairs(ybuf_ref[cur])]
    o_ref[...] = x1_ref[...] + jnp.concatenate(halves, axis=1)


def _combine(run_src, run_len, run_dst, tile_rows, gate_t, x1, ys):
    T = x1.shape[0]
    tok = lambda w: pl.BlockSpec((TILE, w), lambda i, *_: (i, 0))
    grid_spec = pltpu.PrefetchScalarGridSpec(
        num_scalar_prefetch=4,
        grid=(T // TILE,),
        in_specs=[tok(LANES), tok(D_MODEL), pl.BlockSpec(memory_space=pl.ANY)],
        out_specs=tok(D_MODEL),
        scratch_shapes=[pltpu.VMEM((DISPATCH_BUFS, TILE_ROWS, D_MODEL // 2), jnp.uint32),
                        pltpu.SemaphoreType.DMA((DISPATCH_BUFS,))],
    )
    return pl.pallas_call(
        _combine_kernel,
        out_shape=jax.ShapeDtypeStruct((T, D_MODEL), jnp.float32),
        grid_spec=grid_spec,
        compiler_params=_params(("arbitrary",)),
        name="combine",
    )(run_src, run_len, run_dst, tile_rows, gate_t, x1, ys)


def _rotary_tables(positions):
    pos = positions.astype(jnp.float32)[..., None]
    lane = jnp.arange(LANES)
    half_r = RET_DK // 2
    inv_r = RET_ROPE_THETA ** (-jnp.linspace(0.0, 1.0, half_r, dtype=jnp.float32))
    ret_cs = jnp.cos(pos * inv_r[lane % half_r] - jnp.where(lane < half_r, 0.0, 0.5 * jnp.pi))
    half_d = ROT_DIM // 2
    inv_d = ROPE_THETA ** (-jnp.arange(0, ROT_DIM, 2, dtype=jnp.float32) / ROT_DIM)
    sub = lane % DIFF_DH
    ang_d = jnp.where(sub < ROT_DIM, pos * inv_d[sub % half_d], 0.0)
    diff_cs = jnp.cos(ang_d - jnp.where((sub >= half_d) & (sub < ROT_DIM), 0.5 * jnp.pi, 0.0))
    return ret_cs, diff_cs


def kernel(x, positions, norm1_w, w_in, ret_log_decay_fwd, ret_log_decay_bwd, ret_norm_w, q_norm_w, k_norm_w, lambda_q1, lambda_k1, lambda_q2, lambda_k2, diff_norm_w, w_out, norm2_w, w_router, b_router, w1, b1, w2, b2):
    B, S, D = x.shape
    T = B * S
    f32 = jnp.float32
    bf16 = jnp.bfloat16
    x2 = x.reshape(T, D)

    dup = lambda w: jnp.concatenate([w, w]).reshape(1, LANES).astype(f32)
    proj, rq_r, rk_r, qs, ks = _in_proj(x2, norm1_w[0].reshape(1, D), w_in[0].astype(bf16),
                                        _rotary_tables(positions), dup(q_norm_w[0]), dup(k_norm_w[0]), B, S)

    y_ret = _retention(ret_log_decay_fwd[0].astype(f32), ret_log_decay_bwd[0].astype(f32),
                       rq_r, rk_r, proj, ret_norm_w[0].reshape(1, RET_WIDTH).astype(f32), B, S)

    lam = (jnp.exp(jnp.sum(lambda_q1[0].astype(f32) * lambda_k1[0].astype(f32)))
           - jnp.exp(jnp.sum(lambda_q2[0].astype(f32) * lambda_k2[0].astype(f32))) + LAMBDA_INIT)
    lam_row = jnp.full((1, LANES), lam, f32)
    bound = (SCORE_BOUND_SLACK * DIFF_DH ** 0.5 * LOG2_E
             * jnp.max(jnp.abs(q_norm_w[0].astype(f32))) * jnp.max(jnp.abs(k_norm_w[0].astype(f32)))).reshape(1)
    attn_args = (bound, qs, ks, proj, lam_row, diff_norm_w[0].reshape(1, DIFF_DV).astype(f32), B, S)
    y_diff = lax.cond(bound[0] <= MAX_SAFE_SCORE_BOUND,
                      lambda: _diff_attn(False, *attn_args), lambda: _diff_attn(True, *attn_args))

    x1, h2, pos, gate_t, len_t, off_t, tot_t = _out_router(
        x2, y_ret, y_diff, w_out[0].astype(bf16), norm2_w[0].reshape(1, D),
        w_router[0].T.astype(f32), b_router[0].reshape(N_EXPERTS, 1).astype(f32))

    n_tiles = T // TILE
    run_len = len_t[:, :, 0].astype(jnp.int32)
    total = tot_t[:, 0].astype(jnp.int32)
    padded = ((total + MOE_HALF - 1) // MOE_HALF) * MOE_HALF
    pad_end = jnp.cumsum(padded)
    pad_start = pad_end - padded
    run_dst = pad_start[None, :] + off_t[:, :, 0].astype(jnp.int32)
    run_src = jnp.cumsum(run_len, axis=1) - run_len
    tile_rows = jnp.sum(run_len, axis=1)
    P = T * TOP_K + n_tiles * N_EXPERTS * RUN_ALIGN + N_EXPERTS * MOE_HALF
    n_used = (pad_end[-1:] // MOE_HALF).astype(jnp.int32)
    runs = (run_src.reshape(-1), run_len.reshape(-1), run_dst.reshape(-1), tile_rows)

    xs = _dispatch(*runs, pad_start + total, padded - total, n_used, pos, h2, P)
    ys = _experts(pad_start, padded // MOE_BLOCK, (padded // MOE_HALF) % 2, xs, w1[0],
                  b1[0].reshape(N_EXPERTS, 1, 2 * D_FF), w2[0], b2[0].reshape(N_EXPERTS, 1, D))
    out = _combine(*runs, gate_t, x1, ys)
    return out.reshape(B, S, D)
```

```python
import functools

import jax
import jax.numpy as jnp
from jax import lax
from jax.experimental import pallas as pl
from jax.experimental.pallas import tpu as pltpu

EPS = 1e-6
D_MODEL = 1024
RET_HEADS = 4
RET_DK = 128
RET_WIDTH = 512
RET_ROPE_THETA = 10000.0
DIFF_HEADS = 4
DIFF_DH = 64
DIFF_DV = 128
DIFF_WIDTH = 512
ROPE_THETA = 500000.0
ROT_DIM = DIFF_DH // 4
D_IN_PROJ = 3584
N_EXPERTS = 32
TOP_K = 4
D_FF = 1024
SWIGLU_LIMIT = 7.0
SWIGLU_ALPHA = 1.702
LAMBDA_INIT = 0.8 - 0.6 * 1.0

LOG2_E = 1.4426950408889634
SCORE_BOUND_SLACK = 1.02
MAX_SAFE_SCORE_BOUND = 60.0
LANES = 128
SUBLANES = 8
VMEM_LIMIT = 56 * 1024 * 1024

COL_RQ, COL_RK, COL_RV, COL_RG, COL_DQ, COL_DK, COL_DV = 0, 4, 8, 12, 16, 20, 24
VGV_RV, VGV_RG, VGV_DV = 0, 4, 8

TM_PROJ = 512
RET_CHUNK = 128
RET_UNROLL = 32
TQ_ATTN = 1024
TK_ATTN = 2048
TILE = 512
MOE_BLOCK = 512
MOE_HALF = MOE_BLOCK // 2
BLOCK_COPY_PRIORITY = 1
RUN_ALIGN = SUBLANES
TILE_ROWS = TOP_K * TILE + N_EXPERTS * RUN_ALIGN
DISPATCH_BUFS = 2


def _params(sem, **kw):
    return pltpu.CompilerParams(dimension_semantics=sem, vmem_limit_bytes=VMEM_LIMIT, **kw)


def _in_proj_kernel(x_ref, nw_ref, w_ref, ret_cs_ref, diff_cs_ref, qw_ref, kw_ref,
                    vgv_ref, rqo_ref, rko_ref, qs_ref, ks_ref):
    ts = x_ref.shape[0]
    x = x_ref[...]
    hn = (x * lax.rsqrt(jnp.mean(x * x, axis=-1, keepdims=True) + EPS) * nw_ref[...]).astype(jnp.bfloat16)

    def proj(col_block):
        c0 = col_block * LANES
        return jnp.dot(hn, w_ref[:, c0:c0 + 4 * LANES], preferred_element_type=jnp.float32)

    dq, dk, rq, rk = proj(COL_DQ), proj(COL_DK), proj(COL_RQ), proj(COL_RK)
    lane = lax.broadcasted_iota(jnp.int32, (ts, LANES), 1)
    lo = lane < DIFF_DH
    ret_cs = ret_cs_ref[...]
    ret_sc = pltpu.roll(ret_cs, RET_DK // 2, 1)
    first_half = lane < RET_DK // 2
    c2 = jnp.where(first_half, ret_cs, ret_sc)
    s2 = jnp.where(first_half, -ret_sc, ret_cs)
    sub = lane % DIFF_DH
    sin_lanes = (sub >= ROT_DIM // 2) & (sub < ROT_DIM)
    diff_cs = diff_cs_ref[...]
    ra = jnp.where(sin_lanes, pltpu.roll(diff_cs, ROT_DIM // 2, 1), diff_cs)
    rp = jnp.where(sin_lanes, diff_cs, 0.0)
    rn = jnp.where(sub < ROT_DIM // 2, -pltpu.roll(diff_cs, LANES - ROT_DIM // 2, 1), 0.0)

    def qk_norm_rot(x, w):
        x2 = x * x
        s_lo = jnp.sum(jnp.where(lo, x2, 0.0), axis=-1, keepdims=True)
        s_hi = jnp.sum(jnp.where(lo, 0.0, x2), axis=-1, keepdims=True)
        ms = jnp.where(lo, s_lo, s_hi) * (1.0 / DIFF_DH)
        xn = x * lax.rsqrt(ms + EPS) * w
        return xn * ra + pltpu.roll(xn, ROT_DIM // 2, 1) * rp + pltpu.roll(xn, LANES - ROT_DIM // 2, 1) * rn

    for h in range(DIFF_HEADS):
        sl = slice(h * LANES, (h + 1) * LANES)
        q = qk_norm_rot(dq[:, sl], qw_ref[...]) * (DIFF_DH ** -0.5 * LOG2_E)
        k = qk_norm_rot(dk[:, sl], kw_ref[...])
        qs_ref[h, 0] = jnp.where(lo, q, 0.0).astype(qs_ref.dtype)
        qs_ref[h, 1] = jnp.where(lo, 0.0, q).astype(qs_ref.dtype)
        ks_ref[:, sl] = k.astype(ks_ref.dtype)
    for h in range(RET_HEADS):
        sl = slice(h * LANES, (h + 1) * LANES)
        q = rq[:, sl]
        k = rk[:, sl]
        rqo_ref[:, sl] = (q * c2 + pltpu.roll(q, RET_DK // 2, 1) * s2).astype(rqo_ref.dtype)
        rko_ref[:, sl] = ((k * c2 + pltpu.roll(k, RET_DK // 2, 1) * s2) * (RET_DK ** -0.5)).astype(rko_ref.dtype)
    for slot, col_block in enumerate((COL_RV, COL_RG, COL_DV)):
        vgv_ref[:, slot * 4 * LANES:(slot + 1) * 4 * LANES] = proj(col_block).astype(vgv_ref.dtype)


def _in_proj(x2, nw, w_bf16, tabs, qw2, kw2, B, S):
    T = B * S
    n_s = S // TM_PROJ
    tok = lambda w: pl.BlockSpec((TM_PROJ, w), lambda i: (i, 0))
    const = lambda s: pl.BlockSpec(s, lambda i: (0, 0))
    tab = pl.BlockSpec((None, TM_PROJ, LANES), lambda i: (i // n_s, i % n_s, 0))
    bf16 = jnp.bfloat16
    return pl.pallas_call(
        _in_proj_kernel,
        out_shape=(jax.ShapeDtypeStruct((T, 3 * 4 * LANES), bf16),
                   jax.ShapeDtypeStruct((T, 4 * LANES), bf16),
                   jax.ShapeDtypeStruct((T, 4 * LANES), bf16),
                   jax.ShapeDtypeStruct((B, DIFF_HEADS, 2, S, LANES), bf16),
                   jax.ShapeDtypeStruct((T, 4 * LANES), bf16)),
        grid=(T // TM_PROJ,),
        in_specs=[tok(D_MODEL), const((1, D_MODEL)), const((D_MODEL, D_IN_PROJ)),
                  tab, tab, const((1, LANES)), const((1, LANES))],
        out_specs=(tok(3 * 4 * LANES), tok(4 * LANES), tok(4 * LANES),
                   pl.BlockSpec((None, DIFF_HEADS, 2, TM_PROJ, LANES), lambda i: (i // n_s, 0, 0, i % n_s, 0)),
                   tok(4 * LANES)),
        compiler_params=_params(("arbitrary",)),
        name="in_proj",
    )(x2, nw, w_bf16, *tabs, qw2, kw2)


def _retention_kernel(ldf_ref, ldb_ref, q_ref, k_ref, v_ref, g_ref, nw_ref, o_ref, sb_ref):
    C = RET_CHUNK
    S = q_ref.shape[0]
    n_chunks = S // C
    h = pl.program_id(1)
    ldf = ldf_ref[h]
    ldb = ldb_ref[h]
    row = lax.broadcasted_iota(jnp.int32, (C, C), 0).astype(jnp.float32)
    colm = lax.broadcasted_iota(jnp.int32, (C, C), 1).astype(jnp.float32)
    dist = row - colm
    decay = jnp.where(dist >= 0, jnp.exp(ldf * jnp.maximum(dist, 0.0)), jnp.exp(ldb * jnp.maximum(-dist, 0.0)))
    idx = lax.broadcasted_iota(jnp.int32, (C, 1), 0).astype(jnp.float32)
    q_dec_f = jnp.exp(ldf * (idx + 1.0))
    k_dec_f = jnp.exp(ldf * (C - 1.0 - idx))
    q_dec_b = jnp.exp(ldb * (C - idx))
    k_dec_b = jnp.exp(ldb * idx)
    chunk_dec_f = jnp.exp(ldf * C)
    chunk_dec_b = jnp.exp(ldb * C)
    f32 = jnp.float32
    bf16 = jnp.bfloat16

    def kv_state(k, v, k_dec):
        kd = (k.astype(f32) * k_dec).astype(bf16)
        return lax.dot_general(kd, v, (((0,), (0,)), ((), ())), preferred_element_type=f32)

    def bwd_step(i, state):
        c = n_chunks - 1 - i
        r0 = pl.multiple_of(c * C, C)
        sb_ref[c] = state
        return state * chunk_dec_b + kv_state(k_ref[pl.ds(r0, C), :], v_ref[pl.ds(r0, C), :], k_dec_b)

    lax.fori_loop(0, n_chunks, bwd_step, jnp.zeros((RET_DK, LANES), f32), unroll=RET_UNROLL)

    def fwd_step(c, state):
        r0 = pl.multiple_of(c * C, C)
        q = q_ref[pl.ds(r0, C), :]
        k = k_ref[pl.ds(r0, C), :]
        v = v_ref[pl.ds(r0, C), :]
        scores = lax.dot_general(q, k, (((1,), (1,)), ((), ())), preferred_element_type=f32) * decay
        y = jnp.dot(scores.astype(bf16), v, preferred_element_type=f32)
        qf = q.astype(f32)
        y += jnp.dot((qf * q_dec_f).astype(bf16), state.astype(bf16), preferred_element_type=f32)
        y += jnp.dot((qf * q_dec_b).astype(bf16), sb_ref[c].astype(bf16), preferred_element_type=f32)
        yn = y * lax.rsqrt(jnp.mean(y * y, axis=-1, keepdims=True) + EPS) * nw_ref[...]
        g = g_ref[pl.ds(r0, C), :].astype(f32)
        o_ref[pl.ds(r0, C), :] = (yn * (g * jax.nn.sigmoid(g))).astype(o_ref.dtype)
        return state * chunk_dec_f + kv_state(k, v, k_dec_f)

    lax.fori_loop(0, n_chunks, fwd_step, jnp.zeros((RET_DK, LANES), f32), unroll=RET_UNROLL)


def _retention(ldf, ldb, rq_r, rk_r, proj, nw, B, S):
    T = B * S
    smem = pl.BlockSpec(memory_space=pltpu.SMEM)
    seq = lambda cb: pl.BlockSpec((S, LANES), lambda b, h: (b, cb + h))
    return pl.pallas_call(
        _retention_kernel,
        out_shape=jax.ShapeDtypeStruct((T, RET_WIDTH), jnp.bfloat16),
        grid=(B, RET_HEADS),
        in_specs=[smem, smem, seq(0), seq(0), seq(VGV_RV), seq(VGV_RG),
                  pl.BlockSpec((1, LANES), lambda b, h: (0, h))],
        out_specs=seq(0),
        scratch_shapes=[pltpu.VMEM((S // RET_CHUNK, RET_DK, LANES), jnp.float32)],
        compiler_params=_params(("arbitrary", "arbitrary")),
        name="retention",
    )(ldf, ldb, rq_r, rk_r, proj, proj, nw)


def _diff_attn_kernel(online_max, bound_ref, q_ref, k_ref, v_ref, lam_ref, nw_ref, o_ref, m_ref, l_ref, acc_ref):
    tq = q_ref.shape[1]
    S = k_ref.shape[0]
    f32 = jnp.float32
    q = q_ref[...].reshape(2 * tq, LANES)
    if online_max:
        m_ref[...] = jnp.full(m_ref.shape, -jnp.inf, f32)
    l_ref[...] = jnp.zeros(l_ref.shape, f32)
    acc_ref[...] = jnp.zeros(acc_ref.shape, f32)
    n_tiles = TK_ATTN // LANES

    def kv_step(j, carry):
        r0 = pl.multiple_of(j * TK_ATTN, TK_ATTN)
        k = k_ref[pl.ds(r0, TK_ATTN), :]
        v = v_ref[pl.ds(r0, TK_ATTN), :]
        s = lax.dot_general(q, k, (((1,), (1,)), ((), ())), preferred_element_type=f32)
        tiles = [s[:, c * LANES:(c + 1) * LANES] for c in range(n_tiles)]
        if online_max:
            part = tiles[0]
            for t in tiles[1:]:
                part = jnp.maximum(part, t)
            m_prev = m_ref[...]
            shift = jnp.maximum(m_prev, jnp.max(part, axis=-1, keepdims=True))
            alpha = jnp.exp2(m_prev - shift)
            m_ref[...] = shift
        else:
            shift = bound_ref[0]
        probs = [jnp.exp2(t - shift) for t in tiles]
        psum = probs[0]
        for p in probs[1:]:
            psum = psum + p
        pv = jnp.dot(jnp.concatenate([p.astype(jnp.bfloat16) for p in probs], axis=1), v,
                     preferred_element_type=f32)
        if online_max:
            l_ref[...] = alpha * l_ref[...] + psum
            acc_ref[...] = alpha * acc_ref[...] + pv
        else:
            l_ref[...] = l_ref[...] + psum
            acc_ref[...] = acc_ref[...] + pv
        return carry

    lax.fori_loop(0, S // TK_ATTN, kv_step, 0)
    o = acc_ref[...] / jnp.sum(l_ref[...], axis=-1, keepdims=True)
    d = o[:tq] - lam_ref[...] * o[tq:]
    dn = d * lax.rsqrt(jnp.mean(d * d, axis=-1, keepdims=True) + EPS) * nw_ref[...]
    o_ref[...] = (dn * (1.0 - LAMBDA_INIT)).astype(o_ref.dtype)


def _diff_attn(online_max, bound, qs, ks, proj, lam, nw, B, S):
    T = B * S
    n_q = S // TQ_ATTN
    one = pl.BlockSpec((1, LANES), lambda b, h, i, bd: (0, 0))
    grid_spec = pltpu.PrefetchScalarGridSpec(
        num_scalar_prefetch=1,
        grid=(B, DIFF_HEADS, n_q),
        in_specs=[pl.BlockSpec((None, None, 2, TQ_ATTN, LANES), lambda b, h, i, bd: (b, h, 0, i, 0)),
                  pl.BlockSpec((S, LANES), lambda b, h, i, bd: (b, h)),
                  pl.BlockSpec((S, LANES), lambda b, h, i, bd: (b, VGV_DV + h)),
                  one, one],
        out_specs=pl.BlockSpec((TQ_ATTN, LANES), lambda b, h, i, bd: (b * n_q + i, h)),
        scratch_shapes=[pltpu.VMEM((2 * TQ_ATTN, LANES), jnp.float32)] * 3,
    )
    return pl.pallas_call(
        functools.partial(_diff_attn_kernel, online_max),
        out_shape=jax.ShapeDtypeStruct((T, DIFF_WIDTH), jnp.bfloat16),
        grid_spec=grid_spec,
        compiler_params=_params(("arbitrary", "arbitrary", "arbitrary")),
        name="diff_attn_online" if online_max else "diff_attn",
    )(bound, qs, ks, proj, lam, nw)


def _out_router_kernel(x_ref, yr_ref, yd_ref, wo_ref, n2_ref, wrt_ref, br_ref,
                       x1_ref, h2_ref, pos_ref, gate_t_ref, len_ref, off_ref, tot_ref):
    tm = x_ref.shape[0]
    f32 = jnp.float32
    bf16 = jnp.bfloat16

    @pl.when(pl.program_id(0) == 0)
    def _():
        tot_ref[...] = jnp.zeros(tot_ref.shape, f32)

    att = jnp.dot(yr_ref[...], wo_ref[:RET_WIDTH, :], preferred_element_type=f32)
    att += jnp.dot(yd_ref[...], wo_ref[RET_WIDTH:, :], preferred_element_type=f32)
    x1 = x_ref[...] + att
    x1_ref[...] = x1
    h2 = x1 * lax.rsqrt(jnp.mean(x1 * x1, axis=-1, keepdims=True) + EPS) * n2_ref[...]
    h2_ref[...] = h2.astype(h2_ref.dtype)
    nt = (((1,), (1,)), ((), ()))
    h_hi = h2.astype(bf16)
    h_lo = (h2 - h_hi.astype(f32)).astype(bf16)
    w = wrt_ref[...]
    w_hi = w.astype(bf16)
    w_lo = (w - w_hi.astype(f32)).astype(bf16)
    logits = (lax.dot_general(w_hi, h_hi, nt, preferred_element_type=f32)
              + lax.dot_general(w_lo, h_hi, nt, preferred_element_type=f32)
              + lax.dot_general(w_hi, h_lo, nt, preferred_element_type=f32)) + br_ref[...]
    e_iota = lax.broadcasted_iota(jnp.int32, (N_EXPERTS, tm), 0)
    work = logits
    vals, hots = [], []
    for _ in range(TOP_K):
        mx = jnp.max(work, axis=0, keepdims=True)
        ix = jnp.min(jnp.where(work == mx, e_iota, N_EXPERTS), axis=0, keepdims=True)
        hot = e_iota == ix
        vals.append(mx)
        hots.append(hot)
        work = jnp.where(hot, -jnp.inf, work)
    exps = [jnp.exp(v - vals[0]) for v in vals]
    denom = exps[0] + exps[1] + exps[2] + exps[3]
    gates = [e / denom for e in exps]
    sel = jnp.zeros((N_EXPERTS, tm), f32)
    for hot in hots:
        sel = jnp.where(hot, 1.0, sel)
    t_row = lax.broadcasted_iota(jnp.int32, (tm, tm), 0)
    t_col = lax.broadcasted_iota(jnp.int32, (tm, tm), 1)
    upper = jnp.where(t_row < t_col, 1.0, 0.0).astype(bf16)
    rank = jnp.dot(sel.astype(bf16), upper, preferred_element_type=f32)
    cnt = jnp.sum(sel, axis=1, keepdims=True)
    run_units = jnp.floor((cnt + (RUN_ALIGN - 1.0)) * (1.0 / RUN_ALIGN))
    run_len = jnp.broadcast_to(run_units * RUN_ALIGN, (N_EXPERTS, LANES))
    e_row = lax.broadcasted_iota(jnp.int32, (N_EXPERTS, N_EXPERTS), 0)
    e_col = lax.broadcasted_iota(jnp.int32, (N_EXPERTS, N_EXPERTS), 1)
    lower = jnp.where(e_col < e_row, 1.0, 0.0).astype(bf16)
    run_start = jnp.dot(lower, jnp.broadcast_to(run_units, (N_EXPERTS, LANES)).astype(bf16),
                        preferred_element_type=f32) * RUN_ALIGN
    pos_full = rank + run_start[:, 0:1]
    pos = [jnp.sum(jnp.where(hot, pos_full, 0.0), axis=0, keepdims=True) for hot in hots]
    for k in range(TOP_K):
        pos_ref[k:k + 1, :] = pos[k].astype(jnp.int32)
    rows = jnp.concatenate(gates + pos + [jnp.zeros((LANES - 2 * TOP_K, tm), f32)], axis=0)
    gate_t_ref[...] = rows.T
    len_ref[0] = run_len
    off_ref[0] = tot_ref[...]
    tot_ref[...] = tot_ref[...] + run_len


def _out_router(x2, y_ret, y_diff, wo_bf16, n2w, wrt, br):
    T = x2.shape[0]
    n_tiles = T // TILE
    tok = lambda w: pl.BlockSpec((TILE, w), lambda i: (i, 0))
    const = lambda s: pl.BlockSpec(s, lambda i: (0, 0))
    per_tile = pl.BlockSpec((1, N_EXPERTS, LANES), lambda i: (i, 0, 0))
    return pl.pallas_call(
        _out_router_kernel,
        out_shape=(jax.ShapeDtypeStruct((T, D_MODEL), jnp.float32),
                   jax.ShapeDtypeStruct((T, D_MODEL), jnp.bfloat16),
                   jax.ShapeDtypeStruct((TOP_K, T), jnp.int32),
                   jax.ShapeDtypeStruct((T, LANES), jnp.float32),
                   jax.ShapeDtypeStruct((n_tiles, N_EXPERTS, LANES), jnp.float32),
                   jax.ShapeDtypeStruct((n_tiles, N_EXPERTS, LANES), jnp.float32),
                   jax.ShapeDtypeStruct((N_EXPERTS, LANES), jnp.float32)),
        grid=(n_tiles,),
        in_specs=[tok(D_MODEL), tok(RET_WIDTH), tok(DIFF_WIDTH), const((D_MODEL, D_MODEL)),
                  const((1, D_MODEL)), const((N_EXPERTS, D_MODEL)), const((N_EXPERTS, 1))],
        out_specs=(tok(D_MODEL), tok(D_MODEL), pl.BlockSpec((TOP_K, TILE), lambda i: (0, i)), tok(LANES),
                   per_tile, per_tile, const((N_EXPERTS, LANES))),
        compiler_params=_params(("arbitrary",)),
        name="out_router",
    )(x2, y_ret, y_diff, wo_bf16, n2w, wrt, br)


def _pack_bf16_pairs(x):
    w = x.shape[1] // 2
    bits = lambda v: lax.bitcast_convert_type(v.astype(jnp.bfloat16).astype(jnp.float32), jnp.uint32)
    return (bits(x[:, :w]) >> 16) | (bits(x[:, w:]) & jnp.uint32(0xFFFF0000))


def _unpack_bf16_pairs(p):
    as_bf16 = lambda bits: lax.bitcast_convert_type(bits, jnp.float32).astype(jnp.bfloat16)
    return as_bf16(p << 16), as_bf16(p & jnp.uint32(0xFFFF0000))


def _run_copies(src_ref, len_ref, dst_ref, tile, make_copy):
    for e in range(N_EXPERTS):
        n = pl.multiple_of(len_ref[tile * N_EXPERTS + e], RUN_ALIGN)
        s = pl.multiple_of(src_ref[tile * N_EXPERTS + e], RUN_ALIGN)
        d = pl.multiple_of(dst_ref[tile * N_EXPERTS + e], RUN_ALIGN)

        @pl.when(n > 0)
        def _():
            make_copy(s, d, n).start()


def _dispatch_kernel(src_ref, len_ref, dst_ref, rows_ref, zlo_ref, zlen_ref, nu_ref,
                     pos_ref, h2_ref, xs_hbm, xbuf_ref, zero_ref, sems, zero_sem):
    i = pl.program_id(0)
    n_tiles = pl.num_programs(0)
    n_buf = xbuf_ref.shape[0]
    cur = i % n_buf
    n_rows, tm = xbuf_ref.shape[1], h2_ref.shape[0]

    @pl.when(i == 0)
    def _():
        zero_ref[...] = jnp.zeros(zero_ref.shape, zero_ref.dtype)

        def pad_copy(e):
            n = pl.multiple_of(zlen_ref[e], RUN_ALIGN)
            lo = pl.multiple_of(zlo_ref[e], RUN_ALIGN)
            return pltpu.make_async_copy(zero_ref.at[pl.ds(0, n)], xs_hbm.at[pl.ds(lo, n)], zero_sem)

        def tail_copy(j):
            return pltpu.make_async_copy(zero_ref, xs_hbm.at[pl.ds(j * MOE_HALF, MOE_HALF)], zero_sem)

        def guarded(copy, op):
            def body(e, c):
                @pl.when(zlen_ref[e] > 0)
                def _():
                    op(copy(e))
                return c
            return body

        lax.fori_loop(0, N_EXPERTS, guarded(pad_copy, lambda cp: cp.start()), 0)
        lax.fori_loop(0, N_EXPERTS, guarded(pad_copy, lambda cp: cp.wait()), 0)
        n_halves = xs_hbm.shape[0] // MOE_HALF
        lax.fori_loop(nu_ref[0], n_halves, lambda j, c: (tail_copy(j).start(), c)[1], 0)
        lax.fori_loop(nu_ref[0], n_halves, lambda j, c: (tail_copy(j).wait(), c)[1], 0)

    p_iota = lax.broadcasted_iota(jnp.int32, (n_rows, tm), 0)
    onehot = jnp.zeros((n_rows, tm), jnp.float32)
    for k in range(TOP_K):
        onehot = jnp.where(p_iota == pos_ref[k:k + 1, :], 1.0, onehot)
    xbuf_ref[cur] = _pack_bf16_pairs(
        jnp.dot(onehot.astype(jnp.bfloat16), h2_ref[...], preferred_element_type=jnp.float32))

    _run_copies(src_ref, len_ref, dst_ref, i,
                lambda s, d, n: pltpu.make_async_copy(xbuf_ref.at[cur, pl.ds(s, n)], xs_hbm.at[pl.ds(d, n)],
                                                      sems.at[cur]))

    def wait_tile(tile, slot):
        rows = pl.multiple_of(rows_ref[tile], RUN_ALIGN)
        pltpu.make_async_copy(xbuf_ref.at[slot, pl.ds(0, rows)], xs_hbm.at[pl.ds(0, rows)], sems.at[slot]).wait()

    oldest = n_buf - 1

    @pl.when(i >= oldest)
    def _():
        wait_tile(i - oldest, (i + 1) % n_buf)

    @pl.when(i == n_tiles - 1)
    def _():
        for back in range(oldest - 1, -1, -1):
            wait_tile(i - back, (i - back) % n_buf)


def _dispatch(run_src, run_len, run_dst, tile_rows, zero_lo, zero_len, n_used, pos, h2, P):
    T = h2.shape[0]
    n_pre = 7
    grid_spec = pltpu.PrefetchScalarGridSpec(
        num_scalar_prefetch=n_pre,
        grid=(T // TILE,),
        in_specs=[pl.BlockSpec((TOP_K, TILE), lambda i, *_: (0, i)),
                  pl.BlockSpec((TILE, D_MODEL), lambda i, *_: (i, 0))],
        out_specs=pl.BlockSpec(memory_space=pl.ANY),
        scratch_shapes=[pltpu.VMEM((DISPATCH_BUFS, TILE_ROWS, D_MODEL // 2), jnp.uint32),
                        pltpu.VMEM((MOE_HALF, D_MODEL // 2), jnp.uint32),
                        pltpu.SemaphoreType.DMA((DISPATCH_BUFS,)),
                        pltpu.SemaphoreType.DMA(())],
    )
    return pl.pallas_call(
        _dispatch_kernel,
        out_shape=jax.ShapeDtypeStruct((P, D_MODEL // 2), jnp.uint32),
        grid_spec=grid_spec,
        compiler_params=_params(("arbitrary",), has_side_effects=True),
        name="dispatch",
    )(run_src, run_len, run_dst, tile_rows, zero_lo, zero_len, n_used, pos, h2)


def _experts_kernel(base_ref, nblk_ref, half_ref, w1_ref, b1_ref, w2_ref, b2_ref, xs_hbm, ys_hbm,
                    w1b_ref, w2b_ref, xbuf_ref, ybuf_ref, xhalf_ref, yhalf_ref, in_sems, out_sems, busy_ref):
    e = pl.program_id(0)
    n = nblk_ref[e]
    has_half = half_ref[e] == 1
    HALF_BUF = 2

    def rows(expert, j):
        return pl.ds(pl.multiple_of(base_ref[expert] + j * MOE_BLOCK, MOE_HALF), MOE_BLOCK)

    def half_rows(first_row):
        return pl.ds(pl.multiple_of(first_row, MOE_HALF), MOE_HALF)

    def in_copy(expert, j, slot):
        return pltpu.make_async_copy(xs_hbm.at[rows(expert, j)], xbuf_ref.at[slot], in_sems.at[slot])

    def out_copy(j, slot):
        return pltpu.make_async_copy(ybuf_ref.at[slot], ys_hbm.at[rows(e, j)], out_sems.at[slot])

    half_row0 = base_ref[e] + n * MOE_BLOCK
    half_in = pltpu.make_async_copy(xs_hbm.at[half_rows(half_row0)], xhalf_ref, in_sems.at[HALF_BUF])

    def half_out(first_row):
        return pltpu.make_async_copy(yhalf_ref, ys_hbm.at[half_rows(first_row)], out_sems.at[HALF_BUF])

    def wait_out(buf, half=False):
        @pl.when(busy_ref[buf] == 1)
        def _():
            (half_out(0) if half else out_copy(0, buf)).wait()
            busy_ref[buf] = 0

    def mlp(x_packed):
        x = jnp.concatenate(_unpack_bf16_pairs(x_packed), axis=1)
        u = jnp.dot(x, w1b_ref[...], preferred_element_type=jnp.float32) + b1_ref[...]
        glu = jnp.minimum(u[:, :D_FF], SWIGLU_LIMIT)
        lin = jnp.clip(u[:, D_FF:], -SWIGLU_LIMIT, SWIGLU_LIMIT)
        act = glu * jax.nn.sigmoid(SWIGLU_ALPHA * glu) * (lin + 1.0)
        return _pack_bf16_pairs(jnp.dot(act.astype(jnp.bfloat16), w2b_ref[...],
                                        preferred_element_type=jnp.float32) + b2_ref[...])

    @pl.when(e == 0)
    def _():
        for buf in range(3):
            busy_ref[buf] = 0

        @pl.when(n > 0)
        def _():
            in_copy(0, 0, 0).start()

    @pl.when(has_half)
    def _():
        half_in.start()

    @pl.when((n > 0) | has_half)
    def _():
        w1b_ref[...] = w1_ref[...].astype(jnp.bfloat16)
        w2b_ref[...] = w2_ref[...].astype(jnp.bfloat16)

    def block(j, carry):
        slot = j % 2

        @pl.when(j + 1 < n)
        def _():
            in_copy(e, j + 1, 1 - slot).start(priority=BLOCK_COPY_PRIORITY)

        in_copy(e, j, slot).wait()
        wait_out(slot)
        ybuf_ref[slot] = mlp(xbuf_ref[slot])
        out_copy(j, slot).start(priority=BLOCK_COPY_PRIORITY)
        busy_ref[slot] = 1
        return carry

    lax.fori_loop(0, n, block, 0)

    @pl.when(has_half)
    def _():
        half_in.wait()
        wait_out(HALF_BUF, half=True)
        yhalf_ref[...] = mlp(xhalf_ref[...])
        half_out(half_row0).start()
        busy_ref[HALF_BUF] = 1

    e_next = jnp.minimum(e + 1, N_EXPERTS - 1)

    @pl.when((e + 1 < N_EXPERTS) & (nblk_ref[e_next] > 0))
    def _():
        in_copy(e_next, 0, 0).start()

    @pl.when(e == N_EXPERTS - 1)
    def _():
        wait_out(0)
        wait_out(1)
        wait_out(HALF_BUF, half=True)
        yhalf_ref[...] = jnp.zeros(yhalf_ref.shape, yhalf_ref.dtype)
        first_unused = (half_row0 + half_ref[e] * MOE_HALF) // MOE_HALF
        n_halves = ys_hbm.shape[0] // MOE_HALF
        lax.fori_loop(first_unused, n_halves, lambda j, c: (half_out(j * MOE_HALF).start(), c)[1], 0)
        lax.fori_loop(first_unused, n_halves, lambda j, c: (half_out(j * MOE_HALF).wait(), c)[1], 0)


def _experts(base, n_blk, n_half, xs, w1, b1, w2, b2):
    P = xs.shape[0]
    expert = lambda e, bs, nb, nh: (e, 0, 0)
    grid_spec = pltpu.PrefetchScalarGridSpec(
        num_scalar_prefetch=3,
        grid=(N_EXPERTS,),
        in_specs=[pl.BlockSpec((None, D_MODEL, 2 * D_FF), expert),
                  pl.BlockSpec((None, 1, 2 * D_FF), expert),
                  pl.BlockSpec((None, D_FF, D_MODEL), expert),
                  pl.BlockSpec((None, 1, D_MODEL), expert),
                  pl.BlockSpec(memory_space=pl.ANY)],
        out_specs=pl.BlockSpec(memory_space=pl.ANY),
        scratch_shapes=[pltpu.VMEM((D_MODEL, 2 * D_FF), jnp.bfloat16),
                        pltpu.VMEM((D_FF, D_MODEL), jnp.bfloat16),
                        pltpu.VMEM((2, MOE_BLOCK, D_MODEL // 2), jnp.uint32),
                        pltpu.VMEM((2, MOE_BLOCK, D_MODEL // 2), jnp.uint32),
                        pltpu.VMEM((MOE_HALF, D_MODEL // 2), jnp.uint32),
                        pltpu.VMEM((MOE_HALF, D_MODEL // 2), jnp.uint32),
                        pltpu.SemaphoreType.DMA((3,)),
                        pltpu.SemaphoreType.DMA((3,)),
                        pltpu.SMEM((3,), jnp.int32)],
    )
    return pl.pallas_call(
        _experts_kernel,
        out_shape=jax.ShapeDtypeStruct((P, D_MODEL // 2), jnp.uint32),
        grid_spec=grid_spec,
        compiler_params=_params(("arbitrary",)),
        name="experts",
    )(base, n_blk, n_half, w1, b1, w2, b2, xs)


def _combine_kernel(src_ref, len_ref, dst_ref, rows_ref, gate_t_ref, x1_ref, ys_hbm, o_ref, ybuf_ref, sems):
    i = pl.program_id(0)
    n_tiles = pl.num_programs(0)
    n_buf = ybuf_ref.shape[0]
    cur = i % n_buf
    n_rows, tm = ybuf_ref.shape[1], x1_ref.shape[0]

    def fetch(tile, slot):
        _run_copies(src_ref, len_ref, dst_ref, tile,
                    lambda s, d, n: pltpu.make_async_copy(ys_hbm.at[pl.ds(d, n)], ybuf_ref.at[slot, pl.ds(s, n)],
                                                          sems.at[slot]))

    @pl.when(i == 0)
    def _():
        ybuf_ref[...] = jnp.zeros(ybuf_ref.shape, ybuf_ref.dtype)
        for tile in range(n_buf - 1):
            fetch(tile, tile)

    ahead = i + n_buf - 1

    @pl.when(ahead < n_tiles)
    def _():
        fetch(ahead, ahead % n_buf)

    g = gate_t_ref[...]
    p_iota = lax.broadcasted_iota(jnp.int32, (tm, n_rows), 1)
    weights = jnp.zeros((tm, n_rows), jnp.float32)
    for k in range(TOP_K):
        pos_k = g[:, TOP_K + k:TOP_K + k + 1].astype(jnp.int32)
        weights = jnp.where(p_iota == pos_k, g[:, k:k + 1], weights)
    weights = weights.astype(jnp.bfloat16)

    rows = pl.multiple_of(rows_ref[i], RUN_ALIGN)
    pltpu.make_async_copy(ys_hbm.at[pl.ds(0, rows)], ybuf_ref.at[cur, pl.ds(0, rows)], sems.at[cur]).wait()
    halves = [jnp.dot(weights, y, preferred_element_type=jnp.float32) for y in _unpack_bf16_pairs(ybuf_ref[cur])]
    o_ref[...] = x1_ref[...] + jnp.concatenate(halves, axis=1)


def _combine(run_src, run_len, run_dst, tile_rows, gate_t, x1, ys):
    T = x1.shape[0]
    tok = lambda w: pl.BlockSpec((TILE, w), lambda i, *_: (i, 0))
    grid_spec = pltpu.PrefetchScalarGridSpec(
        num_scalar_prefetch=4,
        grid=(T // TILE,),
        in_specs=[tok(LANES), tok(D_MODEL), pl.BlockSpec(memory_space=pl.ANY)],
        out_specs=tok(D_MODEL),
        scratch_shapes=[pltpu.VMEM((DISPATCH_BUFS, TILE_ROWS, D_MODEL // 2), jnp.uint32),
                        pltpu.SemaphoreType.DMA((DISPATCH_BUFS,))],
    )
    return pl.pallas_call(
        _combine_kernel,
        out_shape=jax.ShapeDtypeStruct((T, D_MODEL), jnp.float32),
        grid_spec=grid_spec,
        compiler_params=_params(("arbitrary",)),
        name="combine",
    )(run_src, run_len, run_dst, tile_rows, gate_t, x1, ys)


def _rotary_tables(positions):
    pos = positions.astype(jnp.float32)[..., None]
    lane = jnp.arange(LANES)
    half_r = RET_DK // 2
    inv_r = RET_ROPE_THETA ** (-jnp.linspace(0.0, 1.0, half_r, dtype=jnp.float32))
    ret_cs = jnp.cos(pos * inv_r[lane % half_r] - jnp.where(lane < half_r, 0.0, 0.5 * jnp.pi))
    half_d = ROT_DIM // 2
    inv_d = ROPE_THETA ** (-jnp.arange(0, ROT_DIM, 2, dtype=jnp.float32) / ROT_DIM)
    sub = lane % DIFF_DH
    ang_d = jnp.where(sub < ROT_DIM, pos * inv_d[sub % half_d], 0.0)
    diff_cs = jnp.cos(ang_d - jnp.where((sub >= half_d) & (sub < ROT_DIM), 0.5 * jnp.pi, 0.0))
    return ret_cs, diff_cs


def kernel(x, positions, norm1_w, w_in, ret_log_decay_fwd, ret_log_decay_bwd, ret_norm_w, q_norm_w, k_norm_w, lambda_q1, lambda_k1, lambda_q2, lambda_k2, diff_norm_w, w_out, norm2_w, w_router, b_router, w1, b1, w2, b2):
    B, S, D = x.shape
    T = B * S
    f32 = jnp.float32
    bf16 = jnp.bfloat16
    x2 = x.reshape(T, D)

    dup = lambda w: jnp.concatenate([w, w]).reshape(1, LANES).astype(f32)
    proj, rq_r, rk_r, qs, ks = _in_proj(x2, norm1_w[0].reshape(1, D), w_in[0].astype(bf16),
                                        _rotary_tables(positions), dup(q_norm_w[0]), dup(k_norm_w[0]), B, S)

    y_ret = _retention(ret_log_decay_fwd[0].astype(f32), ret_log_decay_bwd[0].astype(f32),
                       rq_r, rk_r, proj, ret_norm_w[0].reshape(1, RET_WIDTH).astype(f32), B, S)

    lam = (jnp.exp(jnp.sum(lambda_q1[0].astype(f32) * lambda_k1[0].astype(f32)))
           - jnp.exp(jnp.sum(lambda_q2[0].astype(f32) * lambda_k2[0].astype(f32))) + LAMBDA_INIT)
    lam_row = jnp.full((1, LANES), lam, f32)
    bound = (SCORE_BOUND_SLACK * DIFF_DH ** 0.5 * LOG2_E
             * jnp.max(jnp.abs(q_norm_w[0].astype(f32))) * jnp.max(jnp.abs(k_norm_w[0].astype(f32)))).reshape(1)
    attn_args = (bound, qs, ks, proj, lam_row, diff_norm_w[0].reshape(1, DIFF_DV).astype(f32), B, S)
    y_diff = lax.cond(bound[0] <= MAX_SAFE_SCORE_BOUND,
                      lambda: _diff_attn(False, *attn_args), lambda: _diff_attn(True, *attn_args))

    x1, h2, pos, gate_t, len_t, off_t, tot_t = _out_router(
        x2, y_ret, y_diff, w_out[0].astype(bf16), norm2_w[0].reshape(1, D),
        w_router[0].T.astype(f32), b_router[0].reshape(N_EXPERTS, 1).astype(f32))

    n_tiles = T // TILE
    run_len = len_t[:, :, 0].astype(jnp.int32)
    total = tot_t[:, 0].astype(jnp.int32)
    padded = ((total + MOE_HALF - 1) // MOE_HALF) * MOE_HALF
    pad_end = jnp.cumsum(padded)
    pad_start = pad_end - padded
    run_dst = pad_start[None, :] + off_t[:, :, 0].astype(jnp.int32)
    run_src = jnp.cumsum(run_len, axis=1) - run_len
    tile_rows = jnp.sum(run_len, axis=1)
    P = T * TOP_K + n_tiles * N_EXPERTS * RUN_ALIGN + N_EXPERTS * MOE_HALF
    n_used = (pad_end[-1:] // MOE_HALF).astype(jnp.int32)
    runs = (run_src.reshape(-1), run_len.reshape(-1), run_dst.reshape(-1), tile_rows)

    xs = _dispatch(*runs, pad_start + total, padded - total, n_used, pos, h2, P)
    ys = _experts(pad_start, padded // MOE_BLOCK, (padded // MOE_HALF) % 2, xs, w1[0],
                  b1[0].reshape(N_EXPERTS, 1, 2 * D_FF), w2[0], b2[0].reshape(N_EXPERTS, 1, D))
    out = _combine(*runs, gate_t, x1, ys)
    return out.reshape(B, S, D)
```

```python
import functools

import jax
import jax.numpy as jnp
from jax import lax
from jax.experimental import pallas as pl
from jax.experimental.pallas import tpu as pltpu

EPS = 1e-6
D_MODEL = 1024
RET_HEADS = 4
RET_DK = 128
RET_WIDTH = 512
RET_ROPE_THETA = 10000.0
DIFF_HEADS = 4
DIFF_DH = 64
DIFF_DV = 128
DIFF_WIDTH = 512
ROPE_THETA = 500000.0
ROT_DIM = DIFF_DH // 4
D_IN_PROJ = 3584
N_EXPERTS = 32
TOP_K = 4
D_FF = 1024
SWIGLU_LIMIT = 7.0
SWIGLU_ALPHA = 1.702
LAMBDA_INIT = 0.8 - 0.6 * 1.0

LOG2_E = 1.4426950408889634
SCORE_BOUND_SLACK = 1.02
MAX_SAFE_SCORE_BOUND = 60.0
LANES = 128
SUBLANES = 8
VMEM_LIMIT = 56 * 1024 * 1024

COL_RQ, COL_RK, COL_RV, COL_RG, COL_DQ, COL_DK, COL_DV = 0, 4, 8, 12, 16, 20, 24
VGV_RV, VGV_RG, VGV_DV = 0, 4, 8

TM_PROJ = 512
RET_CHUNK = 128
RET_UNROLL = 32
TQ_ATTN = 1024
TK_ATTN = 2048
TM_ROUTE = 512
TILE = 256
MOE_BLOCK = 512
MOE_HALF = MOE_BLOCK // 2
RUN_ALIGN = SUBLANES
TILE_ROWS = TOP_K * TILE + N_EXPERTS * RUN_ALIGN
DISPATCH_BUFS = 2


def _params(sem, **kw):
    return pltpu.CompilerParams(dimension_semantics=sem, vmem_limit_bytes=VMEM_LIMIT, **kw)


def _in_proj_kernel(x_ref, nw_ref, w_ref, ret_cs_ref, diff_cs_ref, qw_ref, kw_ref,
                    vgv_ref, rqo_ref, rko_ref, qs_ref, ks_ref):
    ts = x_ref.shape[0]
    x = x_ref[...]
    hn = (x * lax.rsqrt(jnp.mean(x * x, axis=-1, keepdims=True) + EPS) * nw_ref[...]).astype(jnp.bfloat16)

    def proj(col_block):
        c0 = col_block * LANES
        return jnp.dot(hn, w_ref[:, c0:c0 + 4 * LANES], preferred_element_type=jnp.float32)

    dq, dk, rq, rk = proj(COL_DQ), proj(COL_DK), proj(COL_RQ), proj(COL_RK)
    lane = lax.broadcasted_iota(jnp.int32, (ts, LANES), 1)
    lo = lane < DIFF_DH
    ret_cs = ret_cs_ref[...]
    ret_sc = pltpu.roll(ret_cs, RET_DK // 2, 1)
    first_half = lane < RET_DK // 2
    c2 = jnp.where(first_half, ret_cs, ret_sc)
    s2 = jnp.where(first_half, -ret_sc, ret_cs)
    sub = lane % DIFF_DH
    sin_lanes = (sub >= ROT_DIM // 2) & (sub < ROT_DIM)
    diff_cs = diff_cs_ref[...]
    ra = jnp.where(sin_lanes, pltpu.roll(diff_cs, ROT_DIM // 2, 1), diff_cs)
    rp = jnp.where(sin_lanes, diff_cs, 0.0)
    rn = jnp.where(sub < ROT_DIM // 2, -pltpu.roll(diff_cs, LANES - ROT_DIM // 2, 1), 0.0)

    def qk_norm_rot(x, w):
        x2 = x * x
        s_lo = jnp.sum(jnp.where(lo, x2, 0.0), axis=-1, keepdims=True)
        s_hi = jnp.sum(jnp.where(lo, 0.0, x2), axis=-1, keepdims=True)
        ms = jnp.where(lo, s_lo, s_hi) * (1.0 / DIFF_DH)
        xn = x * lax.rsqrt(ms + EPS) * w
        return xn * ra + pltpu.roll(xn, ROT_DIM // 2, 1) * rp + pltpu.roll(xn, LANES - ROT_DIM // 2, 1) * rn

    for h in range(DIFF_HEADS):
        sl = slice(h * LANES, (h + 1) * LANES)
        q = qk_norm_rot(dq[:, sl], qw_ref[...]) * (DIFF_DH ** -0.5 * LOG2_E)
        k = qk_norm_rot(dk[:, sl], kw_ref[...])
        qs_ref[h, 0] = jnp.where(lo, q, 0.0).astype(qs_ref.dtype)
        qs_ref[h, 1] = jnp.where(lo, 0.0, q).astype(qs_ref.dtype)
        ks_ref[:, sl] = k.astype(ks_ref.dtype)
    for h in range(RET_HEADS):
        sl = slice(h * LANES, (h + 1) * LANES)
        q = rq[:, sl]
        k = rk[:, sl]
        rqo_ref[:, sl] = (q * c2 + pltpu.roll(q, RET_DK // 2, 1) * s2).astype(rqo_ref.dtype)
        rko_ref[:, sl] = ((k * c2 + pltpu.roll(k, RET_DK // 2, 1) * s2) * (RET_DK ** -0.5)).astype(rko_ref.dtype)
    for slot, col_block in enumerate((COL_RV, COL_RG, COL_DV)):
        vgv_ref[:, slot * 4 * LANES:(slot + 1) * 4 * LANES] = proj(col_block).astype(vgv_ref.dtype)


def _in_proj(x2, nw, w_bf16, tabs, qw2, kw2, B, S):
    T = B * S
    n_s = S // TM_PROJ
    tok = lambda w: pl.BlockSpec((TM_PROJ, w), lambda i: (i, 0))
    const = lambda s: pl.BlockSpec(s, lambda i: (0, 0))
    tab = pl.BlockSpec((None, TM_PROJ, LANES), lambda i: (i // n_s, i % n_s, 0))
    bf16 = jnp.bfloat16
    return pl.pallas_call(
        _in_proj_kernel,
        out_shape=(jax.ShapeDtypeStruct((T, 3 * 4 * LANES), bf16),
                   jax.ShapeDtypeStruct((T, 4 * LANES), bf16),
                   jax.ShapeDtypeStruct((T, 4 * LANES), bf16),
                   jax.ShapeDtypeStruct((B, DIFF_HEADS, 2, S, LANES), bf16),
                   jax.ShapeDtypeStruct((T, 4 * LANES), bf16)),
        grid=(T // TM_PROJ,),
        in_specs=[tok(D_MODEL), const((1, D_MODEL)), const((D_MODEL, D_IN_PROJ)),
                  tab, tab, const((1, LANES)), const((1, LANES))],
        out_specs=(tok(3 * 4 * LANES), tok(4 * LANES), tok(4 * LANES),
                   pl.BlockSpec((None, DIFF_HEADS, 2, TM_PROJ, LANES), lambda i: (i // n_s, 0, 0, i % n_s, 0)),
                   tok(4 * LANES)),
        compiler_params=_params(("arbitrary",)),
        name="in_proj",
    )(x2, nw, w_bf16, *tabs, qw2, kw2)


def _retention_kernel(ldf_ref, ldb_ref, q_ref, k_ref, v_ref, g_ref, nw_ref, o_ref, sb_ref):
    C = RET_CHUNK
    S = q_ref.shape[0]
    n_chunks = S // C
    h = pl.program_id(1)
    ldf = ldf_ref[h]
    ldb = ldb_ref[h]
    row = lax.broadcasted_iota(jnp.int32, (C, C), 0).astype(jnp.float32)
    colm = lax.broadcasted_iota(jnp.int32, (C, C), 1).astype(jnp.float32)
    dist = row - colm
    decay = jnp.where(dist >= 0, jnp.exp(ldf * jnp.maximum(dist, 0.0)), jnp.exp(ldb * jnp.maximum(-dist, 0.0)))
    idx = lax.broadcasted_iota(jnp.int32, (C, 1), 0).astype(jnp.float32)
    q_dec_f = jnp.exp(ldf * (idx + 1.0))
    k_dec_f = jnp.exp(ldf * (C - 1.0 - idx))
    q_dec_b = jnp.exp(ldb * (C - idx))
    k_dec_b = jnp.exp(ldb * idx)
    chunk_dec_f = jnp.exp(ldf * C)
    chunk_dec_b = jnp.exp(ldb * C)
    f32 = jnp.float32
    bf16 = jnp.bfloat16

    def kv_state(k, v, k_dec):
        kd = (k.astype(f32) * k_dec).astype(bf16)
        return lax.dot_general(kd, v, (((0,), (0,)), ((), ())), preferred_element_type=f32)

    def bwd_step(i, state):
        c = n_chunks - 1 - i
        r0 = pl.multiple_of(c * C, C)
        sb_ref[c] = state
        return state * chunk_dec_b + kv_state(k_ref[pl.ds(r0, C), :], v_ref[pl.ds(r0, C), :], k_dec_b)

    lax.fori_loop(0, n_chunks, bwd_step, jnp.zeros((RET_DK, LANES), f32), unroll=RET_UNROLL)

    def fwd_step(c, state):
        r0 = pl.multiple_of(c * C, C)
        q = q_ref[pl.ds(r0, C), :]
        k = k_ref[pl.ds(r0, C), :]
        v = v_ref[pl.ds(r0, C), :]
        scores = lax.dot_general(q, k, (((1,), (1,)), ((), ())), preferred_element_type=f32) * decay
        y = jnp.dot(scores.astype(bf16), v, preferred_element_type=f32)
        qf = q.astype(f32)
        y += jnp.dot((qf * q_dec_f).astype(bf16), state.astype(bf16), preferred_element_type=f32)
        y += jnp.dot((qf * q_dec_b).astype(bf16), sb_ref[c].astype(bf16), preferred_element_type=f32)
        yn = y * lax.rsqrt(jnp.mean(y * y, axis=-1, keepdims=True) + EPS) * nw_ref[...]
        g = g_ref[pl.ds(r0, C), :].astype(f32)
        o_ref[pl.ds(r0, C), :] = (yn * (g * jax.nn.sigmoid(g))).astype(o_ref.dtype)
        return state * chunk_dec_f + kv_state(k, v, k_dec_f)

    lax.fori_loop(0, n_chunks, fwd_step, jnp.zeros((RET_DK, LANES), f32), unroll=RET_UNROLL)


def _retention(ldf, ldb, rq_r, rk_r, proj, nw, B, S):
    T = B * S
    smem = pl.BlockSpec(memory_space=pltpu.SMEM)
    seq = lambda cb: pl.BlockSpec((S, LANES), lambda b, h: (b, cb + h))
    return pl.pallas_call(
        _retention_kernel,
        out_shape=jax.ShapeDtypeStruct((T, RET_WIDTH), jnp.bfloat16),
        grid=(B, RET_HEADS),
        in_specs=[smem, smem, seq(0), seq(0), seq(VGV_RV), seq(VGV_RG),
                  pl.BlockSpec((1, LANES), lambda b, h: (0, h))],
        out_specs=seq(0),
        scratch_shapes=[pltpu.VMEM((S // RET_CHUNK, RET_DK, LANES), jnp.float32)],
        compiler_params=_params(("arbitrary", "arbitrary")),
        name="retention",
    )(ldf, ldb, rq_r, rk_r, proj, proj, nw)


def _diff_attn_kernel(online_max, bound_ref, q_ref, k_ref, v_ref, lam_ref, nw_ref, o_ref, m_ref, l_ref, acc_ref):
    tq = q_ref.shape[1]
    S = k_ref.shape[0]
    f32 = jnp.float32
    q = q_ref[...].reshape(2 * tq, LANES)
    if online_max:
        m_ref[...] = jnp.full(m_ref.shape, -jnp.inf, f32)
    l_ref[...] = jnp.zeros(l_ref.shape, f32)
    acc_ref[...] = jnp.zeros(acc_ref.shape, f32)
    n_tiles = TK_ATTN // LANES

    def kv_step(j, carry):
        r0 = pl.multiple_of(j * TK_ATTN, TK_ATTN)
        k = k_ref[pl.ds(r0, TK_ATTN), :]
        v = v_ref[pl.ds(r0, TK_ATTN), :]
        s = lax.dot_general(q, k, (((1,), (1,)), ((), ())), preferred_element_type=f32)
        tiles = [s[:, c * LANES:(c + 1) * LANES] for c in range(n_tiles)]
        if online_max:
            part = tiles[0]
            for t in tiles[1:]:
                part = jnp.maximum(part, t)
            m_prev = m_ref[...]
            shift = jnp.maximum(m_prev, jnp.max(part, axis=-1, keepdims=True))
            alpha = jnp.exp2(m_prev - shift)
            m_ref[...] = shift
        else:
            shift = bound_ref[0]
        probs = [jnp.exp2(t - shift) for t in tiles]
        psum = probs[0]
        for p in probs[1:]:
            psum = psum + p
        pv = jnp.dot(jnp.concatenate([p.astype(jnp.bfloat16) for p in probs], axis=1), v,
                     preferred_element_type=f32)
        if online_max:
            l_ref[...] = alpha * l_ref[...] + psum
            acc_ref[...] = alpha * acc_ref[...] + pv
        else:
            l_ref[...] = l_ref[...] + psum
            acc_ref[...] = acc_ref[...] + pv
        return carry

    lax.fori_loop(0, S // TK_ATTN, kv_step, 0)
    o = acc_ref[...] / jnp.sum(l_ref[...], axis=-1, keepdims=True)
    d = o[:tq] - lam_ref[...] * o[tq:]
    dn = d * lax.rsqrt(jnp.mean(d * d, axis=-1, keepdims=True) + EPS) * nw_ref[...]
    o_ref[...] = (dn * (1.0 - LAMBDA_INIT)).astype(o_ref.dtype)


def _diff_attn(online_max, bound, qs, ks, proj, lam, nw, B, S):
    T = B * S
    n_q = S // TQ_ATTN
    one = pl.BlockSpec((1, LANES), lambda b, h, i, bd: (0, 0))
    grid_spec = pltpu.PrefetchScalarGridSpec(
        num_scalar_prefetch=1,
        grid=(B, DIFF_HEADS, n_q),
        in_specs=[pl.BlockSpec((None, None, 2, TQ_ATTN, LANES), lambda b, h, i, bd: (b, h, 0, i, 0)),
                  pl.BlockSpec((S, LANES), lambda b, h, i, bd: (b, h)),
                  pl.BlockSpec((S, LANES), lambda b, h, i, bd: (b, VGV_DV + h)),
                  one, one],
        out_specs=pl.BlockSpec((TQ_ATTN, LANES), lambda b, h, i, bd: (b * n_q + i, h)),
        scratch_shapes=[pltpu.VMEM((2 * TQ_ATTN, LANES), jnp.float32)] * 3,
    )
    return pl.pallas_call(
        functools.partial(_diff_attn_kernel, online_max),
        out_shape=jax.ShapeDtypeStruct((T, DIFF_WIDTH), jnp.bfloat16),
        grid_spec=grid_spec,
        compiler_params=_params(("arbitrary", "arbitrary", "arbitrary")),
        name="diff_attn_online" if online_max else "diff_attn",
    )(bound, qs, ks, proj, lam, nw)


def _out_router_kernel(x_ref, yr_ref, yd_ref, wo_ref, n2_ref, wrt_ref, br_ref,
                       x1_ref, h2_ref, pos_ref, gate_t_ref, len_ref, off_ref, tot_ref):
    tm = x_ref.shape[0]
    f32 = jnp.float32
    bf16 = jnp.bfloat16

    @pl.when(pl.program_id(0) == 0)
    def _():
        tot_ref[...] = jnp.zeros(tot_ref.shape, f32)

    att = jnp.dot(yr_ref[...], wo_ref[:RET_WIDTH, :], preferred_element_type=f32)
    att += jnp.dot(yd_ref[...], wo_ref[RET_WIDTH:, :], preferred_element_type=f32)
    x1 = x_ref[...] + att
    x1_ref[...] = x1
    h2 = x1 * lax.rsqrt(jnp.mean(x1 * x1, axis=-1, keepdims=True) + EPS) * n2_ref[...]
    h2_ref[...] = h2.astype(h2_ref.dtype)
    nt = (((1,), (1,)), ((), ()))
    h_hi = h2.astype(bf16)
    h_lo = (h2 - h_hi.astype(f32)).astype(bf16)
    w = wrt_ref[...]
    w_hi = w.astype(bf16)
    w_lo = (w - w_hi.astype(f32)).astype(bf16)
    logits = (lax.dot_general(w_hi, h_hi, nt, preferred_element_type=f32)
              + lax.dot_general(w_lo, h_hi, nt, preferred_element_type=f32)
              + lax.dot_general(w_hi, h_lo, nt, preferred_element_type=f32)) + br_ref[...]
    e_iota = lax.broadcasted_iota(jnp.int32, (N_EXPERTS, tm), 0)
    work = logits
    vals, hots = [], []
    for _ in range(TOP_K):
        mx = jnp.max(work, axis=0, keepdims=True)
        ix = jnp.min(jnp.where(work == mx, e_iota, N_EXPERTS), axis=0, keepdims=True)
        hot = e_iota == ix
        vals.append(mx)
        hots.append(hot)
        work = jnp.where(hot, -jnp.inf, work)
    exps = [jnp.exp(v - vals[0]) for v in vals]
    denom = exps[0] + exps[1] + exps[2] + exps[3]
    gates = [e / denom for e in exps]
    masks = [jnp.where(hot, 1.0, 0.0) for hot in hots]
    sel = masks[0] + masks[1] + masks[2] + masks[3]
    t_row = lax.broadcasted_iota(jnp.int32, (TILE, TILE), 0)
    t_col = lax.broadcasted_iota(jnp.int32, (TILE, TILE), 1)
    upper = jnp.where(t_row < t_col, 1.0, 0.0).astype(bf16)
    e_row = lax.broadcasted_iota(jnp.int32, (N_EXPERTS, N_EXPERTS), 0)
    e_col = lax.broadcasted_iota(jnp.int32, (N_EXPERTS, N_EXPERTS), 1)
    lower = jnp.where(e_col < e_row, 1.0, 0.0).astype(bf16)
    pos_parts = []
    for t in range(tm // TILE):
        cols = slice(t * TILE, (t + 1) * TILE)
        sel_t = sel[:, cols]
        rank = jnp.dot(sel_t.astype(bf16), upper, preferred_element_type=f32)
        cnt = jnp.sum(sel_t, axis=1, keepdims=True)
        run_units = jnp.floor((cnt + (RUN_ALIGN - 1.0)) * (1.0 / RUN_ALIGN))
        run_len = jnp.broadcast_to(run_units * RUN_ALIGN, (N_EXPERTS, LANES))
        run_start = jnp.dot(lower, jnp.broadcast_to(run_units, (N_EXPERTS, LANES)).astype(bf16),
                            preferred_element_type=f32) * RUN_ALIGN
        pos_full = rank + run_start[:, 0:1]
        pos_parts.append([jnp.sum(mask[:, cols] * pos_full, axis=0, keepdims=True) for mask in masks])
        len_ref[t] = run_len
        off_ref[t] = tot_ref[...]
        tot_ref[...] = tot_ref[...] + run_len
    pos = [jnp.concatenate([part[k] for part in pos_parts], axis=1) for k in range(TOP_K)]
    for k in range(TOP_K):
        pos_ref[k:k + 1, :] = pos[k].astype(jnp.int32)
    rows = jnp.concatenate(gates + pos + [jnp.zeros((LANES - 2 * TOP_K, tm), f32)], axis=0)
    gate_t_ref[...] = rows.T


def _out_router(x2, y_ret, y_diff, wo_bf16, n2w, wrt, br):
    T = x2.shape[0]
    n_tiles = T // TILE
    tiles_per_step = TM_ROUTE // TILE
    tok = lambda w: pl.BlockSpec((TM_ROUTE, w), lambda i: (i, 0))
    const = lambda s: pl.BlockSpec(s, lambda i: (0, 0))
    per_tile = pl.BlockSpec((tiles_per_step, N_EXPERTS, LANES), lambda i: (i, 0, 0))
    return pl.pallas_call(
        _out_router_kernel,
        out_shape=(jax.ShapeDtypeStruct((T, D_MODEL), jnp.float32),
                   jax.ShapeDtypeStruct((T, D_MODEL), jnp.bfloat16),
                   jax.ShapeDtypeStruct((TOP_K, T), jnp.int32),
                   jax.ShapeDtypeStruct((T, LANES), jnp.float32),
                   jax.ShapeDtypeStruct((n_tiles, N_EXPERTS, LANES), jnp.float32),
                   jax.ShapeDtypeStruct((n_tiles, N_EXPERTS, LANES), jnp.float32),
                   jax.ShapeDtypeStruct((N_EXPERTS, LANES), jnp.float32)),
        grid=(T // TM_ROUTE,),
        in_specs=[tok(D_MODEL), tok(RET_WIDTH), tok(DIFF_WIDTH), const((D_MODEL, D_MODEL)),
                  const((1, D_MODEL)), const((N_EXPERTS, D_MODEL)), const((N_EXPERTS, 1))],
        out_specs=(tok(D_MODEL), tok(D_MODEL), pl.BlockSpec((TOP_K, TM_ROUTE), lambda i: (0, i)), tok(LANES),
                   per_tile, per_tile, const((N_EXPERTS, LANES))),
        compiler_params=_params(("arbitrary",)),
        name="out_router",
    )(x2, y_ret, y_diff, wo_bf16, n2w, wrt, br)


def _pack_bf16_pairs(x):
    w = x.shape[1] // 2
    bits = lambda v: lax.bitcast_convert_type(v.astype(jnp.bfloat16).astype(jnp.float32), jnp.uint32)
    return (bits(x[:, :w]) >> 16) | (bits(x[:, w:]) & jnp.uint32(0xFFFF0000))


def _unpack_bf16_pairs(p):
    as_bf16 = lambda bits: lax.bitcast_convert_type(bits, jnp.float32).astype(jnp.bfloat16)
    return as_bf16(p << 16), as_bf16(p & jnp.uint32(0xFFFF0000))


def _run_copies(src_ref, len_ref, dst_ref, tile, make_copy):
    for e in range(N_EXPERTS):
        n = pl.multiple_of(len_ref[tile * N_EXPERTS + e], RUN_ALIGN)
        s = pl.multiple_of(src_ref[tile * N_EXPERTS + e], RUN_ALIGN)
        d = pl.multiple_of(dst_ref[tile * N_EXPERTS + e], RUN_ALIGN)

        @pl.when(n > 0)
        def _():
            make_copy(s, d, n).start()


def _dispatch_kernel(src_ref, len_ref, dst_ref, rows_ref, zlo_ref, zlen_ref, nu_ref,
                     pos_ref, h2_ref, xs_hbm, xbuf_ref, zero_ref, sems, zero_sem):
    i = pl.program_id(0)
    n_tiles = pl.num_programs(0)
    n_buf = xbuf_ref.shape[0]
    cur = i % n_buf
    n_rows, tm = xbuf_ref.shape[1], h2_ref.shape[0]

    @pl.when(i == 0)
    def _():
        zero_ref[...] = jnp.zeros(zero_ref.shape, zero_ref.dtype)

        def pad_copy(e):
            n = pl.multiple_of(zlen_ref[e], RUN_ALIGN)
            lo = pl.multiple_of(zlo_ref[e], RUN_ALIGN)
            return pltpu.make_async_copy(zero_ref.at[pl.ds(0, n)], xs_hbm.at[pl.ds(lo, n)], zero_sem)

        def tail_copy(j):
            return pltpu.make_async_copy(zero_ref, xs_hbm.at[pl.ds(j * MOE_HALF, MOE_HALF)], zero_sem)

        def guarded(copy, op):
            def body(e, c):
                @pl.when(zlen_ref[e] > 0)
                def _():
                    op(copy(e))
                return c
            return body

        lax.fori_loop(0, N_EXPERTS, guarded(pad_copy, lambda cp: cp.start()), 0)
        lax.fori_loop(0, N_EXPERTS, guarded(pad_copy, lambda cp: cp.wait()), 0)
        n_halves = xs_hbm.shape[0] // MOE_HALF
        lax.fori_loop(nu_ref[0], n_halves, lambda j, c: (tail_copy(j).start(), c)[1], 0)
        lax.fori_loop(nu_ref[0], n_halves, lambda j, c: (tail_copy(j).wait(), c)[1], 0)

    p_iota = lax.broadcasted_iota(jnp.int32, (n_rows, tm), 0)
    onehot = jnp.zeros((n_rows, tm), jnp.float32)
    for k in range(TOP_K):
        onehot = jnp.where(p_iota == pos_ref[k:k + 1, :], 1.0, onehot)
    xbuf_ref[cur] = _pack_bf16_pairs(
        jnp.dot(onehot.astype(jnp.bfloat16), h2_ref[...], preferred_element_type=jnp.float32))

    _run_copies(src_ref, len_ref, dst_ref, i,
                lambda s, d, n: pltpu.make_async_copy(xbuf_ref.at[cur, pl.ds(s, n)], xs_hbm.at[pl.ds(d, n)],
                                                      sems.at[cur]))

    def wait_tile(tile, slot):
        rows = pl.multiple_of(rows_ref[tile], RUN_ALIGN)
        pltpu.make_async_copy(xbuf_ref.at[slot, pl.ds(0, rows)], xs_hbm.at[pl.ds(0, rows)], sems.at[slot]).wait()

    oldest = n_buf - 1

    @pl.when(i >= oldest)
    def _():
        wait_tile(i - oldest, (i + 1) % n_buf)

    @pl.when(i == n_tiles - 1)
    def _():
        for back in range(oldest - 1, -1, -1):
            wait_tile(i - back, (i - back) % n_buf)


def _dispatch(run_src, run_len, run_dst, tile_rows, zero_lo, zero_len, n_used, pos, h2, P):
    T = h2.shape[0]
    n_pre = 7
    grid_spec = pltpu.PrefetchScalarGridSpec(
        num_scalar_prefetch=n_pre,
        grid=(T // TILE,),
        in_specs=[pl.BlockSpec((TOP_K, TILE), lambda i, *_: (0, i)),
                  pl.BlockSpec((TILE, D_MODEL), lambda i, *_: (i, 0))],
        out_specs=pl.BlockSpec(memory_space=pl.ANY),
        scratch_shapes=[pltpu.VMEM((DISPATCH_BUFS, TILE_ROWS, D_MODEL // 2), jnp.uint32),
                        pltpu.VMEM((MOE_HALF, D_MODEL // 2), jnp.uint32),
                        pltpu.SemaphoreType.DMA((DISPATCH_BUFS,)),
                        pltpu.SemaphoreType.DMA(())],
    )
    return pl.pallas_call(
        _dispatch_kernel,
        out_shape=jax.ShapeDtypeStruct((P, D_MODEL // 2), jnp.uint32),
        grid_spec=grid_spec,
        compiler_params=_params(("arbitrary",), has_side_effects=True),
        name="dispatch",
    )(run_src, run_len, run_dst, tile_rows, zero_lo, zero_len, n_used, pos, h2)


def _experts_kernel(base_ref, nblk_ref, half_ref, w1_ref, b1_ref, w2_ref, b2_ref, xs_hbm, ys_hbm,
                    w1b_ref, w2b_ref, xbuf_ref, ybuf_ref, xhalf_ref, yhalf_ref, in_sems, out_sems, busy_ref):
    e = pl.program_id(0)
    n = nblk_ref[e]
    has_half = half_ref[e] == 1
    HALF_BUF = 2

    def rows(expert, j):
        return pl.ds(pl.multiple_of(base_ref[expert] + j * MOE_BLOCK, MOE_HALF), MOE_BLOCK)

    def half_rows(first_row):
        return pl.ds(pl.multiple_of(first_row, MOE_HALF), MOE_HALF)

    def in_copy(expert, j, slot):
        return pltpu.make_async_copy(xs_hbm.at[rows(expert, j)], xbuf_ref.at[slot], in_sems.at[slot])

    def out_copy(j, slot):
        return pltpu.make_async_copy(ybuf_ref.at[slot], ys_hbm.at[rows(e, j)], out_sems.at[slot])

    half_row0 = base_ref[e] + n * MOE_BLOCK
    half_in = pltpu.make_async_copy(xs_hbm.at[half_rows(half_row0)], xhalf_ref, in_sems.at[HALF_BUF])

    def half_out(first_row):
        return pltpu.make_async_copy(yhalf_ref, ys_hbm.at[half_rows(first_row)], out_sems.at[HALF_BUF])

    def wait_out(buf, half=False):
        @pl.when(busy_ref[buf] == 1)
        def _():
            (half_out(0) if half else out_copy(0, buf)).wait()
            busy_ref[buf] = 0

    def mlp(x_packed):
        x = jnp.concatenate(_unpack_bf16_pairs(x_packed), axis=1)
        u = jnp.dot(x, w1b_ref[...], preferred_element_type=jnp.float32) + b1_ref[...]
        glu = jnp.minimum(u[:, :D_FF], SWIGLU_LIMIT)
        lin = jnp.clip(u[:, D_FF:], -SWIGLU_LIMIT, SWIGLU_LIMIT)
        act = glu * jax.nn.sigmoid(SWIGLU_ALPHA * glu) * (lin + 1.0)
        return _pack_bf16_pairs(jnp.dot(act.astype(jnp.bfloat16), w2b_ref[...],
                                        preferred_element_type=jnp.float32) + b2_ref[...])

    @pl.when(e == 0)
    def _():
        for buf in range(3):
            busy_ref[buf] = 0

        @pl.when(n > 0)
        def _():
            in_copy(0, 0, 0).start()

    @pl.when(has_half)
    def _():
        half_in.start()

    @pl.when((n > 0) | has_half)
    def _():
        w1b_ref[...] = w1_ref[...].astype(jnp.bfloat16)
        w2b_ref[...] = w2_ref[...].astype(jnp.bfloat16)

    def block(j, carry):
        slot = j % 2

        @pl.when(j + 1 < n)
        def _():
            in_copy(e, j + 1, 1 - slot).start()

        in_copy(e, j, slot).wait()
        wait_out(slot)
        ybuf_ref[slot] = mlp(xbuf_ref[slot])
        out_copy(j, slot).start()
        busy_ref[slot] = 1
        return carry

    lax.fori_loop(0, n, block, 0)

    @pl.when(has_half)
    def _():
        half_in.wait()
        wait_out(HALF_BUF, half=True)
        yhalf_ref[...] = mlp(xhalf_ref[...])
        half_out(half_row0).start()
        busy_ref[HALF_BUF] = 1

    e_next = jnp.minimum(e + 1, N_EXPERTS - 1)

    @pl.when((e + 1 < N_EXPERTS) & (nblk_ref[e_next] > 0))
    def _():
        in_copy(e_next, 0, 0).start()

    @pl.when(e == N_EXPERTS - 1)
    def _():
        wait_out(0)
        wait_out(1)
        wait_out(HALF_BUF, half=True)
        yhalf_ref[...] = jnp.zeros(yhalf_ref.shape, yhalf_ref.dtype)
        first_unused = (half_row0 + half_ref[e] * MOE_HALF) // MOE_HALF
        n_halves = ys_hbm.shape[0] // MOE_HALF
        lax.fori_loop(first_unused, n_halves, lambda j, c: (half_out(j * MOE_HALF).start(), c)[1], 0)
        lax.fori_loop(first_unused, n_halves, lambda j, c: (half_out(j * MOE_HALF).wait(), c)[1], 0)


def _experts(base, n_blk, n_half, xs, w1, b1, w2, b2):
    P = xs.shape[0]
    expert = lambda e, bs, nb, nh: (e, 0, 0)
    grid_spec = pltpu.PrefetchScalarGridSpec(
        num_scalar_prefetch=3,
        grid=(N_EXPERTS,),
        in_specs=[pl.BlockSpec((None, D_MODEL, 2 * D_FF), expert),
                  pl.BlockSpec((None, 1, 2 * D_FF), expert),
                  pl.BlockSpec((None, D_FF, D_MODEL), expert),
                  pl.BlockSpec((None, 1, D_MODEL), expert),
                  pl.BlockSpec(memory_space=pl.ANY)],
        out_specs=pl.BlockSpec(memory_space=pl.ANY),
        scratch_shapes=[pltpu.VMEM((D_MODEL, 2 * D_FF), jnp.bfloat16),
                        pltpu.VMEM((D_FF, D_MODEL), jnp.bfloat16),
                        pltpu.VMEM((2, MOE_BLOCK, D_MODEL // 2), jnp.uint32),
                        pltpu.VMEM((2, MOE_BLOCK, D_MODEL // 2), jnp.uint32),
                        pltpu.VMEM((MOE_HALF, D_MODEL // 2), jnp.uint32),
                        pltpu.VMEM((MOE_HALF, D_MODEL // 2), jnp.uint32),
                        pltpu.SemaphoreType.DMA((3,)),
                        pltpu.SemaphoreType.DMA((3,)),
                        pltpu.SMEM((3,), jnp.int32)],
    )
    return pl.pallas_call(
        _experts_kernel,
        out_shape=jax.ShapeDtypeStruct((P, D_MODEL // 2), jnp.uint32),
        grid_spec=grid_spec,
        compiler_params=_params(("arbitrary",)),
        name="experts",
    )(base, n_blk, n_half, w1, b1, w2, b2, xs)


def _combine_kernel(src_ref, len_ref, dst_ref, rows_ref, gate_t_ref, x1_ref, ys_hbm, o_ref, ybuf_ref, sems):
    i = pl.program_id(0)
    n_tiles = pl.num_programs(0)
    n_buf = ybuf_ref.shape[0]
    cur = i % n_buf
    n_rows, tm = ybuf_ref.shape[1], x1_ref.shape[0]

    def fetch(tile, slot):
        _run_copies(src_ref, len_ref, dst_ref, tile,
                    lambda s, d, n: pltpu.make_async_copy(ys_hbm.at[pl.ds(d, n)], ybuf_ref.at[slot, pl.ds(s, n)],
                                                          sems.at[slot]))

    @pl.when(i == 0)
    def _():
        ybuf_ref[...] = jnp.zeros(ybuf_ref.shape, ybuf_ref.dtype)
        for tile in range(n_buf - 1):
            fetch(tile, tile)

    ahead = i + n_buf - 1

    @pl.when(ahead < n_tiles)
    def _():
        fetch(ahead, ahead % n_buf)

    g = gate_t_ref[...]
    p_iota = lax.broadcasted_iota(jnp.int32, (tm, n_rows), 1)
    weights = jnp.zeros((tm, n_rows), jnp.float32)
    for k in range(TOP_K):
        pos_k = g[:, TOP_K + k:TOP_K + k + 1].astype(jnp.int32)
        weights = jnp.where(p_iota == pos_k, g[:, k:k + 1], weights)
    weights = weights.astype(jnp.bfloat16)

    rows = pl.multiple_of(rows_ref[i], RUN_ALIGN)
    pltpu.make_async_copy(ys_hbm.at[pl.ds(0, rows)], ybuf_ref.at[cur, pl.ds(0, rows)], sems.at[cur]).wait()
    halves = [jnp.dot(weights, y, preferred_element_type=jnp.float32) for y in _unpack_bf16_pairs(ybuf_ref[cur])]
    o_ref[...] = x1_ref[...] + jnp.concatenate(halves, axis=1)


def _combine(run_src, run_len, run_dst, tile_rows, gate_t, x1, ys):
    T = x1.shape[0]
    tok = lambda w: pl.BlockSpec((TILE, w), lambda i, *_: (i, 0))
    grid_spec = pltpu.PrefetchScalarGridSpec(
        num_scalar_prefetch=4,
        grid=(T // TILE,),
        in_specs=[tok(LANES), tok(D_MODEL), pl.BlockSpec(memory_space=pl.ANY)],
        out_specs=tok(D_MODEL),
        scratch_shapes=[pltpu.VMEM((DISPATCH_BUFS, TILE_ROWS, D_MODEL // 2), jnp.uint32),
                        pltpu.SemaphoreType.DMA((DISPATCH_BUFS,))],
    )
    return pl.pallas_call(
        _combine_kernel,
        out_shape=jax.ShapeDtypeStruct((T, D_MODEL), jnp.float32),
        grid_spec=grid_spec,
        compiler_params=_params(("arbitrary",)),
        name="combine",
    )(run_src, run_len, run_dst, tile_rows, gate_t, x1, ys)


def _rotary_tables(positions):
    pos = positions.astype(jnp.float32)[..., None]
    lane = jnp.arange(LANES)
    half_r = RET_DK // 2
    inv_r = RET_ROPE_THETA ** (-jnp.linspace(0.0, 1.0, half_r, dtype=jnp.float32))
    ret_cs = jnp.cos(pos * inv_r[lane % half_r] - jnp.where(lane < half_r, 0.0, 0.5 * jnp.pi))
    half_d = ROT_DIM // 2
    inv_d = ROPE_THETA ** (-jnp.arange(0, ROT_DIM, 2, dtype=jnp.float32) / ROT_DIM)
    sub = lane % DIFF_DH
    ang_d = jnp.where(sub < ROT_DIM, pos * inv_d[sub % half_d], 0.0)
    diff_cs = jnp.cos(ang_d - jnp.where((sub >= half_d) & (sub < ROT_DIM), 0.5 * jnp.pi, 0.0))
    return ret_cs, diff_cs


def kernel(x, positions, norm1_w, w_in, ret_log_decay_fwd, ret_log_decay_bwd, ret_norm_w, q_norm_w, k_norm_w, lambda_q1, lambda_k1, lambda_q2, lambda_k2, diff_norm_w, w_out, norm2_w, w_router, b_router, w1, b1, w2, b2):
    B, S, D = x.shape
    T = B * S
    f32 = jnp.float32
    bf16 = jnp.bfloat16
    x2 = x.reshape(T, D)

    dup = lambda w: jnp.concatenate([w, w]).reshape(1, LANES).astype(f32)
    proj, rq_r, rk_r, qs, ks = _in_proj(x2, norm1_w[0].reshape(1, D), w_in[0].astype(bf16),
                                        _rotary_tables(positions), dup(q_norm_w[0]), dup(k_norm_w[0]), B, S)

    y_ret = _retention(ret_log_decay_fwd[0].astype(f32), ret_log_decay_bwd[0].astype(f32),
                       rq_r, rk_r, proj, ret_norm_w[0].reshape(1, RET_WIDTH).astype(f32), B, S)

    lam = (jnp.exp(jnp.sum(lambda_q1[0].astype(f32) * lambda_k1[0].astype(f32)))
           - jnp.exp(jnp.sum(lambda_q2[0].astype(f32) * lambda_k2[0].astype(f32))) + LAMBDA_INIT)
    lam_row = jnp.full((1, LANES), lam, f32)
    bound = (SCORE_BOUND_SLACK * DIFF_DH ** 0.5 * LOG2_E
             * jnp.max(jnp.abs(q_norm_w[0].astype(f32))) * jnp.max(jnp.abs(k_norm_w[0].astype(f32)))).reshape(1)
    attn_args = (bound, qs, ks, proj, lam_row, diff_norm_w[0].reshape(1, DIFF_DV).astype(f32), B, S)
    y_diff = lax.cond(bound[0] <= MAX_SAFE_SCORE_BOUND,
                      lambda: _diff_attn(False, *attn_args), lambda: _diff_attn(True, *attn_args))

    x1, h2, pos, gate_t, len_t, off_t, tot_t = _out_router(
        x2, y_ret, y_diff, w_out[0].astype(bf16), norm2_w[0].reshape(1, D),
        w_router[0].T.astype(f32), b_router[0].reshape(N_EXPERTS, 1).astype(f32))

    n_tiles = T // TILE
    run_len = len_t[:, :, 0].astype(jnp.int32)
    total = tot_t[:, 0].astype(jnp.int32)
    padded = ((total + MOE_HALF - 1) // MOE_HALF) * MOE_HALF
    pad_end = jnp.cumsum(padded)
    pad_start = pad_end - padded
    run_dst = pad_start[None, :] + off_t[:, :, 0].astype(jnp.int32)
    run_src = jnp.cumsum(run_len, axis=1) - run_len
    tile_rows = jnp.sum(run_len, axis=1)
    P = T * TOP_K + n_tiles * N_EXPERTS * RUN_ALIGN + N_EXPERTS * MOE_HALF
    n_used = (pad_end[-1:] // MOE_HALF).astype(jnp.int32)
    runs = (run_src.reshape(-1), run_len.reshape(-1), run_dst.reshape(-1), tile_rows)

    xs = _dispatch(*runs, pad_start + total, padded - total, n_used, pos, h2, P)
    ys = _experts(pad_start, padded // MOE_BLOCK, (padded // MOE_HALF) % 2, xs, w1[0],
                  b1[0].reshape(N_EXPERTS, 1, 2 * D_FF), w2[0], b2[0].reshape(N_EXPERTS, 1, D))
    out = _combine(*runs, gate_t, x1, ys)
    return out.reshape(B, S, D)
```

```python
import functools

import jax
import jax.numpy as jnp
from jax import lax
from jax.experimental import pallas as pl
from jax.experimental.pallas import tpu as pltpu

EPS = 1e-6
D_MODEL = 1024
RET_HEADS = 4
RET_DK = 128
RET_WIDTH = 512
RET_ROPE_THETA = 10000.0
DIFF_HEADS = 4
DIFF_DH = 64
DIFF_DV = 128
DIFF_WIDTH = 512
ROPE_THETA = 500000.0
ROT_DIM = DIFF_DH // 4
D_IN_PROJ = 3584
N_EXPERTS = 32
TOP_K = 4
D_FF = 1024
SWIGLU_LIMIT = 7.0
SWIGLU_ALPHA = 1.702
LAMBDA_INIT = 0.8 - 0.6 * 1.0

LOG2_E = 1.4426950408889634
SCORE_BOUND_SLACK = 1.02
MAX_SAFE_SCORE_BOUND = 60.0
LANES = 128
SUBLANES = 8
VMEM_LIMIT = 56 * 1024 * 1024

COL_RQ, COL_RK, COL_RV, COL_RG, COL_DQ, COL_DK, COL_DV = 0, 4, 8, 12, 16, 20, 24
VGV_RV, VGV_RG, VGV_DV = 0, 4, 8

TM_PROJ = 512
RET_CHUNK = 128
RET_UNROLL = 32
TQ_ATTN = 1024
TK_ATTN = 2048
TM_ROUTE = 512
TILE = 256
MOE_BLOCK = 512
MOE_HALF = MOE_BLOCK // 2
RUN_ALIGN = SUBLANES
TILE_ROWS = TOP_K * TILE + N_EXPERTS * RUN_ALIGN
DISPATCH_BUFS = 4


def _params(sem, **kw):
    return pltpu.CompilerParams(dimension_semantics=sem, vmem_limit_bytes=VMEM_LIMIT, **kw)


def _in_proj_kernel(x_ref, nw_ref, w_ref, ret_cs_ref, diff_cs_ref, qw_ref, kw_ref,
                    vgv_ref, rqo_ref, rko_ref, qs_ref, ks_ref):
    ts = x_ref.shape[0]
    x = x_ref[...]
    hn = (x * lax.rsqrt(jnp.mean(x * x, axis=-1, keepdims=True) + EPS) * nw_ref[...]).astype(jnp.bfloat16)

    def proj(col_block):
        c0 = col_block * LANES
        return jnp.dot(hn, w_ref[:, c0:c0 + 4 * LANES], preferred_element_type=jnp.float32)

    dq, dk, rq, rk = proj(COL_DQ), proj(COL_DK), proj(COL_RQ), proj(COL_RK)
    lane = lax.broadcasted_iota(jnp.int32, (ts, LANES), 1)
    lo = lane < DIFF_DH
    ret_cs = ret_cs_ref[...]
    ret_sc = pltpu.roll(ret_cs, RET_DK // 2, 1)
    first_half = lane < RET_DK // 2
    c2 = jnp.where(first_half, ret_cs, ret_sc)
    s2 = jnp.where(first_half, -ret_sc, ret_cs)
    sub = lane % DIFF_DH
    sin_lanes = (sub >= ROT_DIM // 2) & (sub < ROT_DIM)
    diff_cs = diff_cs_ref[...]
    ra = jnp.where(sin_lanes, pltpu.roll(diff_cs, ROT_DIM // 2, 1), diff_cs)
    rp = jnp.where(sin_lanes, diff_cs, 0.0)
    rn = jnp.where(sub < ROT_DIM // 2, -pltpu.roll(diff_cs, LANES - ROT_DIM // 2, 1), 0.0)

    def qk_norm_rot(x, w):
        x2 = x * x
        s_lo = jnp.sum(jnp.where(lo, x2, 0.0), axis=-1, keepdims=True)
        s_hi = jnp.sum(jnp.where(lo, 0.0, x2), axis=-1, keepdims=True)
        ms = jnp.where(lo, s_lo, s_hi) * (1.0 / DIFF_DH)
        xn = x * lax.rsqrt(ms + EPS) * w
        return xn * ra + pltpu.roll(xn, ROT_DIM // 2, 1) * rp + pltpu.roll(xn, LANES - ROT_DIM // 2, 1) * rn

    for h in range(DIFF_HEADS):
        sl = slice(h * LANES, (h + 1) * LANES)
        q = qk_norm_rot(dq[:, sl], qw_ref[...]) * (DIFF_DH ** -0.5 * LOG2_E)
        k = qk_norm_rot(dk[:, sl], kw_ref[...])
        qs_ref[h, 0] = jnp.where(lo, q, 0.0).astype(qs_ref.dtype)
        qs_ref[h, 1] = jnp.where(lo, 0.0, q).astype(qs_ref.dtype)
        ks_ref[:, sl] = k.astype(ks_ref.dtype)
    for h in range(RET_HEADS):
        sl = slice(h * LANES, (h + 1) * LANES)
        q = rq[:, sl]
        k = rk[:, sl]
        rqo_ref[:, sl] = (q * c2 + pltpu.roll(q, RET_DK // 2, 1) * s2).astype(rqo_ref.dtype)
        rko_ref[:, sl] = ((k * c2 + pltpu.roll(k, RET_DK // 2, 1) * s2) * (RET_DK ** -0.5)).astype(rko_ref.dtype)
    for slot, col_block in enumerate((COL_RV, COL_RG, COL_DV)):
        vgv_ref[:, slot * 4 * LANES:(slot + 1) * 4 * LANES] = proj(col_block).astype(vgv_ref.dtype)


def _in_proj(x2, nw, w_bf16, tabs, qw2, kw2, B, S):
    T = B * S
    n_s = S // TM_PROJ
    tok = lambda w: pl.BlockSpec((TM_PROJ, w), lambda i: (i, 0))
    const = lambda s: pl.BlockSpec(s, lambda i: (0, 0))
    tab = pl.BlockSpec((None, TM_PROJ, LANES), lambda i: (i // n_s, i % n_s, 0))
    bf16 = jnp.bfloat16
    return pl.pallas_call(
        _in_proj_kernel,
        out_shape=(jax.ShapeDtypeStruct((T, 3 * 4 * LANES), bf16),
                   jax.ShapeDtypeStruct((T, 4 * LANES), bf16),
                   jax.ShapeDtypeStruct((T, 4 * LANES), bf16),
                   jax.ShapeDtypeStruct((B, DIFF_HEADS, 2, S, LANES), bf16),
                   jax.ShapeDtypeStruct((T, 4 * LANES), bf16)),
        grid=(T // TM_PROJ,),
        in_specs=[tok(D_MODEL), const((1, D_MODEL)), const((D_MODEL, D_IN_PROJ)),
                  tab, tab, const((1, LANES)), const((1, LANES))],
        out_specs=(tok(3 * 4 * LANES), tok(4 * LANES), tok(4 * LANES),
                   pl.BlockSpec((None, DIFF_HEADS, 2, TM_PROJ, LANES), lambda i: (i // n_s, 0, 0, i % n_s, 0)),
                   tok(4 * LANES)),
        compiler_params=_params(("arbitrary",)),
        name="in_proj",
    )(x2, nw, w_bf16, *tabs, qw2, kw2)


def _retention_kernel(ldf_ref, ldb_ref, q_ref, k_ref, v_ref, g_ref, nw_ref, o_ref, sb_ref):
    C = RET_CHUNK
    S = q_ref.shape[0]
    n_chunks = S // C
    h = pl.program_id(1)
    ldf = ldf_ref[h]
    ldb = ldb_ref[h]
    row = lax.broadcasted_iota(jnp.int32, (C, C), 0).astype(jnp.float32)
    colm = lax.broadcasted_iota(jnp.int32, (C, C), 1).astype(jnp.float32)
    dist = row - colm
    decay = jnp.where(dist >= 0, jnp.exp(ldf * jnp.maximum(dist, 0.0)), jnp.exp(ldb * jnp.maximum(-dist, 0.0)))
    idx = lax.broadcasted_iota(jnp.int32, (C, 1), 0).astype(jnp.float32)
    q_dec_f = jnp.exp(ldf * (idx + 1.0))
    k_dec_f = jnp.exp(ldf * (C - 1.0 - idx))
    q_dec_b = jnp.exp(ldb * (C - idx))
    k_dec_b = jnp.exp(ldb * idx)
    chunk_dec_f = jnp.exp(ldf * C)
    chunk_dec_b = jnp.exp(ldb * C)
    f32 = jnp.float32
    bf16 = jnp.bfloat16

    def kv_state(k, v, k_dec):
        kd = (k.astype(f32) * k_dec).astype(bf16)
        return lax.dot_general(kd, v, (((0,), (0,)), ((), ())), preferred_element_type=f32)

    def bwd_step(i, state):
        c = n_chunks - 1 - i
        r0 = pl.multiple_of(c * C, C)
        sb_ref[c] = state
        return state * chunk_dec_b + kv_state(k_ref[pl.ds(r0, C), :], v_ref[pl.ds(r0, C), :], k_dec_b)

    lax.fori_loop(0, n_chunks, bwd_step, jnp.zeros((RET_DK, LANES), f32), unroll=RET_UNROLL)

    def fwd_step(c, state):
        r0 = pl.multiple_of(c * C, C)
        q = q_ref[pl.ds(r0, C), :]
        k = k_ref[pl.ds(r0, C), :]
        v = v_ref[pl.ds(r0, C), :]
        scores = lax.dot_general(q, k, (((1,), (1,)), ((), ())), preferred_element_type=f32) * decay
        y = jnp.dot(scores.astype(bf16), v, preferred_element_type=f32)
        qf = q.astype(f32)
        y += jnp.dot((qf * q_dec_f).astype(bf16), state.astype(bf16), preferred_element_type=f32)
        y += jnp.dot((qf * q_dec_b).astype(bf16), sb_ref[c].astype(bf16), preferred_element_type=f32)
        yn = y * lax.rsqrt(jnp.mean(y * y, axis=-1, keepdims=True) + EPS) * nw_ref[...]
        g = g_ref[pl.ds(r0, C), :].astype(f32)
        o_ref[pl.ds(r0, C), :] = (yn * (g * jax.nn.sigmoid(g))).astype(o_ref.dtype)
        return state * chunk_dec_f + kv_state(k, v, k_dec_f)

    lax.fori_loop(0, n_chunks, fwd_step, jnp.zeros((RET_DK, LANES), f32), unroll=RET_UNROLL)


def _retention(ldf, ldb, rq_r, rk_r, proj, nw, B, S):
    T = B * S
    smem = pl.BlockSpec(memory_space=pltpu.SMEM)
    seq = lambda cb: pl.BlockSpec((S, LANES), lambda b, h: (b, cb + h))
    return pl.pallas_call(
        _retention_kernel,
        out_shape=jax.ShapeDtypeStruct((T, RET_WIDTH), jnp.bfloat16),
        grid=(B, RET_HEADS),
        in_specs=[smem, smem, seq(0), seq(0), seq(VGV_RV), seq(VGV_RG),
                  pl.BlockSpec((1, LANES), lambda b, h: (0, h))],
        out_specs=seq(0),
        scratch_shapes=[pltpu.VMEM((S // RET_CHUNK, RET_DK, LANES), jnp.float32)],
        compiler_params=_params(("arbitrary", "arbitrary")),
        name="retention",
    )(ldf, ldb, rq_r, rk_r, proj, proj, nw)


def _diff_attn_kernel(online_max, bound_ref, q_ref, k_ref, v_ref, lam_ref, nw_ref, o_ref, m_ref, l_ref, acc_ref):
    tq = q_ref.shape[1]
    S = k_ref.shape[0]
    f32 = jnp.float32
    q = q_ref[...].reshape(2 * tq, LANES)
    if online_max:
        m_ref[...] = jnp.full(m_ref.shape, -jnp.inf, f32)
    l_ref[...] = jnp.zeros(l_ref.shape, f32)
    acc_ref[...] = jnp.zeros(acc_ref.shape, f32)
    n_tiles = TK_ATTN // LANES

    def kv_step(j, carry):
        r0 = pl.multiple_of(j * TK_ATTN, TK_ATTN)
        k = k_ref[pl.ds(r0, TK_ATTN), :]
        v = v_ref[pl.ds(r0, TK_ATTN), :]
        s = lax.dot_general(q, k, (((1,), (1,)), ((), ())), preferred_element_type=f32)
        tiles = [s[:, c * LANES:(c + 1) * LANES] for c in range(n_tiles)]
        if online_max:
            part = tiles[0]
            for t in tiles[1:]:
                part = jnp.maximum(part, t)
            m_prev = m_ref[...]
            shift = jnp.maximum(m_prev, jnp.max(part, axis=-1, keepdims=True))
            alpha = jnp.exp2(m_prev - shift)
            m_ref[...] = shift
        else:
            shift = bound_ref[0]
        probs = [jnp.exp2(t - shift) for t in tiles]
        psum = probs[0]
        for p in probs[1:]:
            psum = psum + p
        pv = jnp.dot(jnp.concatenate([p.astype(jnp.bfloat16) for p in probs], axis=1), v,
                     preferred_element_type=f32)
        if online_max:
            l_ref[...] = alpha * l_ref[...] + psum
            acc_ref[...] = alpha * acc_ref[...] + pv
        else:
            l_ref[...] = l_ref[...] + psum
            acc_ref[...] = acc_ref[...] + pv
        return carry

    lax.fori_loop(0, S // TK_ATTN, kv_step, 0)
    o = acc_ref[...] / jnp.sum(l_ref[...], axis=-1, keepdims=True)
    d = o[:tq] - lam_ref[...] * o[tq:]
    dn = d * lax.rsqrt(jnp.mean(d * d, axis=-1, keepdims=True) + EPS) * nw_ref[...]
    o_ref[...] = (dn * (1.0 - LAMBDA_INIT)).astype(o_ref.dtype)


def _diff_attn(online_max, bound, qs, ks, proj, lam, nw, B, S):
    T = B * S
    n_q = S // TQ_ATTN
    one = pl.BlockSpec((1, LANES), lambda b, h, i, bd: (0, 0))
    grid_spec = pltpu.PrefetchScalarGridSpec(
        num_scalar_prefetch=1,
        grid=(B, DIFF_HEADS, n_q),
        in_specs=[pl.BlockSpec((None, None, 2, TQ_ATTN, LANES), lambda b, h, i, bd: (b, h, 0, i, 0)),
                  pl.BlockSpec((S, LANES), lambda b, h, i, bd: (b, h)),
                  pl.BlockSpec((S, LANES), lambda b, h, i, bd: (b, VGV_DV + h)),
                  one, one],
        out_specs=pl.BlockSpec((TQ_ATTN, LANES), lambda b, h, i, bd: (b * n_q + i, h)),
        scratch_shapes=[pltpu.VMEM((2 * TQ_ATTN, LANES), jnp.float32)] * 3,
    )
    return pl.pallas_call(
        functools.partial(_diff_attn_kernel, online_max),
        out_shape=jax.ShapeDtypeStruct((T, DIFF_WIDTH), jnp.bfloat16),
        grid_spec=grid_spec,
        compiler_params=_params(("arbitrary", "arbitrary", "arbitrary")),
        name="diff_attn_online" if online_max else "diff_attn",
    )(bound, qs, ks, proj, lam, nw)


def _out_router_kernel(x_ref, yr_ref, yd_ref, wo_ref, n2_ref, wrt_ref, br_ref,
                       x1_ref, h2_ref, pos_ref, gate_t_ref, len_ref, off_ref, tot_ref):
    tm = x_ref.shape[0]
    f32 = jnp.float32
    bf16 = jnp.bfloat16

    @pl.when(pl.program_id(0) == 0)
    def _():
        tot_ref[...] = jnp.zeros(tot_ref.shape, f32)

    att = jnp.dot(yr_ref[...], wo_ref[:RET_WIDTH, :], preferred_element_type=f32)
    att += jnp.dot(yd_ref[...], wo_ref[RET_WIDTH:, :], preferred_element_type=f32)
    x1 = x_ref[...] + att
    x1_ref[...] = x1
    h2 = x1 * lax.rsqrt(jnp.mean(x1 * x1, axis=-1, keepdims=True) + EPS) * n2_ref[...]
    h2_ref[...] = h2.astype(h2_ref.dtype)
    nt = (((1,), (1,)), ((), ()))
    h_hi = h2.astype(bf16)
    h_lo = (h2 - h_hi.astype(f32)).astype(bf16)
    w = wrt_ref[...]
    w_hi = w.astype(bf16)
    w_lo = (w - w_hi.astype(f32)).astype(bf16)
    logits = (lax.dot_general(w_hi, h_hi, nt, preferred_element_type=f32)
              + lax.dot_general(w_lo, h_hi, nt, preferred_element_type=f32)
              + lax.dot_general(w_hi, h_lo, nt, preferred_element_type=f32)) + br_ref[...]
    e_iota = lax.broadcasted_iota(jnp.int32, (N_EXPERTS, tm), 0)
    work = logits
    vals, hots = [], []
    for _ in range(TOP_K):
        mx = jnp.max(work, axis=0, keepdims=True)
        ix = jnp.min(jnp.where(work == mx, e_iota, N_EXPERTS), axis=0, keepdims=True)
        hot = e_iota == ix
        vals.append(mx)
        hots.append(hot)
        work = jnp.where(hot, -jnp.inf, work)
    exps = [jnp.exp(v - vals[0]) for v in vals]
    denom = exps[0] + exps[1] + exps[2] + exps[3]
    gates = [e / denom for e in exps]
    masks = [jnp.where(hot, 1.0, 0.0) for hot in hots]
    sel = masks[0] + masks[1] + masks[2] + masks[3]
    t_row = lax.broadcasted_iota(jnp.int32, (TILE, TILE), 0)
    t_col = lax.broadcasted_iota(jnp.int32, (TILE, TILE), 1)
    upper = jnp.where(t_row < t_col, 1.0, 0.0).astype(bf16)
    e_row = lax.broadcasted_iota(jnp.int32, (N_EXPERTS, N_EXPERTS), 0)
    e_col = lax.broadcasted_iota(jnp.int32, (N_EXPERTS, N_EXPERTS), 1)
    lower = jnp.where(e_col < e_row, 1.0, 0.0).astype(bf16)
    pos_parts = []
    for t in range(tm // TILE):
        cols = slice(t * TILE, (t + 1) * TILE)
        sel_t = sel[:, cols]
        rank = jnp.dot(sel_t.astype(bf16), upper, preferred_element_type=f32)
        cnt = jnp.sum(sel_t, axis=1, keepdims=True)
        run_units = jnp.floor((cnt + (RUN_ALIGN - 1.0)) * (1.0 / RUN_ALIGN))
        run_len = jnp.broadcast_to(run_units * RUN_ALIGN, (N_EXPERTS, LANES))
        run_start = jnp.dot(lower, jnp.broadcast_to(run_units, (N_EXPERTS, LANES)).astype(bf16),
                            preferred_element_type=f32) * RUN_ALIGN
        pos_full = rank + run_start[:, 0:1]
        pos_parts.append([jnp.sum(mask[:, cols] * pos_full, axis=0, keepdims=True) for mask in masks])
        len_ref[t] = run_len
        off_ref[t] = tot_ref[...]
        tot_ref[...] = tot_ref[...] + run_len
    pos = [jnp.concatenate([part[k] for part in pos_parts], axis=1) for k in range(TOP_K)]
    for k in range(TOP_K):
        pos_ref[k:k + 1, :] = pos[k].astype(jnp.int32)
    rows = jnp.concatenate(gates + pos + [jnp.zeros((LANES - 2 * TOP_K, tm), f32)], axis=0)
    gate_t_ref[...] = rows.T


def _out_router(x2, y_ret, y_diff, wo_bf16, n2w, wrt, br):
    T = x2.shape[0]
    n_tiles = T // TILE
    tiles_per_step = TM_ROUTE // TILE
    tok = lambda w: pl.BlockSpec((TM_ROUTE, w), lambda i: (i, 0))
    const = lambda s: pl.BlockSpec(s, lambda i: (0, 0))
    per_tile = pl.BlockSpec((tiles_per_step, N_EXPERTS, LANES), lambda i: (i, 0, 0))
    return pl.pallas_call(
        _out_router_kernel,
        out_shape=(jax.ShapeDtypeStruct((T, D_MODEL), jnp.float32),
                   jax.ShapeDtypeStruct((T, D_MODEL), jnp.bfloat16),
                   jax.ShapeDtypeStruct((TOP_K, T), jnp.int32),
                   jax.ShapeDtypeStruct((T, LANES), jnp.float32),
                   jax.ShapeDtypeStruct((n_tiles, N_EXPERTS, LANES), jnp.float32),
                   jax.ShapeDtypeStruct((n_tiles, N_EXPERTS, LANES), jnp.float32),
                   jax.ShapeDtypeStruct((N_EXPERTS, LANES), jnp.float32)),
        grid=(T // TM_ROUTE,),
        in_specs=[tok(D_MODEL), tok(RET_WIDTH), tok(DIFF_WIDTH), const((D_MODEL, D_MODEL)),
                  const((1, D_MODEL)), const((N_EXPERTS, D_MODEL)), const((N_EXPERTS, 1))],
        out_specs=(tok(D_MODEL), tok(D_MODEL), pl.BlockSpec((TOP_K, TM_ROUTE), lambda i: (0, i)), tok(LANES),
                   per_tile, per_tile, const((N_EXPERTS, LANES))),
        compiler_params=_params(("arbitrary",)),
        name="out_router",
    )(x2, y_ret, y_diff, wo_bf16, n2w, wrt, br)


def _pack_bf16_pairs(x):
    w = x.shape[1] // 2
    bits = lambda v: lax.bitcast_convert_type(v.astype(jnp.bfloat16).astype(jnp.float32), jnp.uint32)
    return (bits(x[:, :w]) >> 16) | (bits(x[:, w:]) & jnp.uint32(0xFFFF0000))


def _unpack_bf16_pairs(p):
    as_bf16 = lambda bits: lax.bitcast_convert_type(bits, jnp.float32).astype(jnp.bfloat16)
    return as_bf16(p << 16), as_bf16(p & jnp.uint32(0xFFFF0000))


def _run_copies(src_ref, len_ref, dst_ref, tile, make_copy):
    for e in range(N_EXPERTS):
        n = pl.multiple_of(len_ref[tile * N_EXPERTS + e], RUN_ALIGN)
        s = pl.multiple_of(src_ref[tile * N_EXPERTS + e], RUN_ALIGN)
        d = pl.multiple_of(dst_ref[tile * N_EXPERTS + e], RUN_ALIGN)

        @pl.when(n > 0)
        def _():
            make_copy(s, d, n).start()


def _dispatch_kernel(src_ref, len_ref, dst_ref, rows_ref, zlo_ref, zlen_ref, nu_ref,
                     pos_ref, h2_ref, xs_hbm, xbuf_ref, zero_ref, sems, zero_sem):
    i = pl.program_id(0)
    n_tiles = pl.num_programs(0)
    n_buf = xbuf_ref.shape[0]
    cur = i % n_buf
    n_rows, tm = xbuf_ref.shape[1], h2_ref.shape[0]

    @pl.when(i == 0)
    def _():
        zero_ref[...] = jnp.zeros(zero_ref.shape, zero_ref.dtype)

        def pad_copy(e):
            n = pl.multiple_of(zlen_ref[e], RUN_ALIGN)
            lo = pl.multiple_of(zlo_ref[e], RUN_ALIGN)
            return pltpu.make_async_copy(zero_ref.at[pl.ds(0, n)], xs_hbm.at[pl.ds(lo, n)], zero_sem)

        def tail_copy(j):
            return pltpu.make_async_copy(zero_ref, xs_hbm.at[pl.ds(j * MOE_HALF, MOE_HALF)], zero_sem)

        def guarded(copy, op):
            def body(e, c):
                @pl.when(zlen_ref[e] > 0)
                def _():
                    op(copy(e))
                return c
            return body

        lax.fori_loop(0, N_EXPERTS, guarded(pad_copy, lambda cp: cp.start()), 0)
        lax.fori_loop(0, N_EXPERTS, guarded(pad_copy, lambda cp: cp.wait()), 0)
        n_halves = xs_hbm.shape[0] // MOE_HALF
        lax.fori_loop(nu_ref[0], n_halves, lambda j, c: (tail_copy(j).start(), c)[1], 0)
        lax.fori_loop(nu_ref[0], n_halves, lambda j, c: (tail_copy(j).wait(), c)[1], 0)

    p_iota = lax.broadcasted_iota(jnp.int32, (n_rows, tm), 0)
    onehot = jnp.zeros((n_rows, tm), jnp.float32)
    for k in range(TOP_K):
        onehot = jnp.where(p_iota == pos_ref[k:k + 1, :], 1.0, onehot)
    xbuf_ref[cur] = _pack_bf16_pairs(
        jnp.dot(onehot.astype(jnp.bfloat16), h2_ref[...], preferred_element_type=jnp.float32))

    _run_copies(src_ref, len_ref, dst_ref, i,
                lambda s, d, n: pltpu.make_async_copy(xbuf_ref.at[cur, pl.ds(s, n)], xs_hbm.at[pl.ds(d, n)],
                                                      sems.at[cur]))

    def wait_tile(tile, slot):
        rows = pl.multiple_of(rows_ref[tile], RUN_ALIGN)
        pltpu.make_async_copy(xbuf_ref.at[slot, pl.ds(0, rows)], xs_hbm.at[pl.ds(0, rows)], sems.at[slot]).wait()

    oldest = n_buf - 1

    @pl.when(i >= oldest)
    def _():
        wait_tile(i - oldest, (i + 1) % n_buf)

    @pl.when(i == n_tiles - 1)
    def _():
        for back in range(oldest - 1, -1, -1):
            wait_tile(i - back, (i - back) % n_buf)


def _dispatch(run_src, run_len, run_dst, tile_rows, zero_lo, zero_len, n_used, pos, h2, P):
    T = h2.shape[0]
    n_pre = 7
    grid_spec = pltpu.PrefetchScalarGridSpec(
        num_scalar_prefetch=n_pre,
        grid=(T // TILE,),
        in_specs=[pl.BlockSpec((TOP_K, TILE), lambda i, *_: (0, i)),
                  pl.BlockSpec((TILE, D_MODEL), lambda i, *_: (i, 0))],
        out_specs=pl.BlockSpec(memory_space=pl.ANY),
        scratch_shapes=[pltpu.VMEM((DISPATCH_BUFS, TILE_ROWS, D_MODEL // 2), jnp.uint32),
                        pltpu.VMEM((MOE_HALF, D_MODEL // 2), jnp.uint32),
                        pltpu.SemaphoreType.DMA((DISPATCH_BUFS,)),
                        pltpu.SemaphoreType.DMA(())],
    )
    return pl.pallas_call(
        _dispatch_kernel,
        out_shape=jax.ShapeDtypeStruct((P, D_MODEL // 2), jnp.uint32),
        grid_spec=grid_spec,
        compiler_params=_params(("arbitrary",), has_side_effects=True),
        name="dispatch",
    )(run_src, run_len, run_dst, tile_rows, zero_lo, zero_len, n_used, pos, h2)


def _experts_kernel(base_ref, nblk_ref, half_ref, w1_ref, b1_ref, w2_ref, b2_ref, xs_hbm, ys_hbm,
                    w1b_ref, w2b_ref, xbuf_ref, ybuf_ref, xhalf_ref, yhalf_ref, in_sems, out_sems, busy_ref):
    e = pl.program_id(0)
    n = nblk_ref[e]
    has_half = half_ref[e] == 1
    HALF_BUF = 2

    def rows(expert, j):
        return pl.ds(pl.multiple_of(base_ref[expert] + j * MOE_BLOCK, MOE_HALF), MOE_BLOCK)

    def half_rows(first_row):
        return pl.ds(pl.multiple_of(first_row, MOE_HALF), MOE_HALF)

    def in_copy(expert, j, slot):
        return pltpu.make_async_copy(xs_hbm.at[rows(expert, j)], xbuf_ref.at[slot], in_sems.at[slot])

    def out_copy(j, slot):
        return pltpu.make_async_copy(ybuf_ref.at[slot], ys_hbm.at[rows(e, j)], out_sems.at[slot])

    half_row0 = base_ref[e] + n * MOE_BLOCK
    half_in = pltpu.make_async_copy(xs_hbm.at[half_rows(half_row0)], xhalf_ref, in_sems.at[HALF_BUF])

    def half_out(first_row):
        return pltpu.make_async_copy(yhalf_ref, ys_hbm.at[half_rows(first_row)], out_sems.at[HALF_BUF])

    def wait_out(buf, half=False):
        @pl.when(busy_ref[buf] == 1)
        def _():
            (half_out(0) if half else out_copy(0, buf)).wait()
            busy_ref[buf] = 0

    def mlp(x_packed):
        x = jnp.concatenate(_unpack_bf16_pairs(x_packed), axis=1)
        u = jnp.dot(x, w1b_ref[...], preferred_element_type=jnp.float32) + b1_ref[...]
        glu = jnp.minimum(u[:, :D_FF], SWIGLU_LIMIT)
        lin = jnp.clip(u[:, D_FF:], -SWIGLU_LIMIT, SWIGLU_LIMIT)
        act = glu * jax.nn.sigmoid(SWIGLU_ALPHA * glu) * (lin + 1.0)
        return _pack_bf16_pairs(jnp.dot(act.astype(jnp.bfloat16), w2b_ref[...],
                                        preferred_element_type=jnp.float32) + b2_ref[...])

    @pl.when(e == 0)
    def _():
        for buf in range(3):
            busy_ref[buf] = 0

        @pl.when(n > 0)
        def _():
            in_copy(0, 0, 0).start()

    @pl.when(has_half)
    def _():
        half_in.start()

    @pl.when((n > 0) | has_half)
    def _():
        w1b_ref[...] = w1_ref[...].astype(jnp.bfloat16)
        w2b_ref[...] = w2_ref[...].astype(jnp.bfloat16)

    def block(j, carry):
        slot = j % 2

        @pl.when(j + 1 < n)
        def _():
            in_copy(e, j + 1, 1 - slot).start()

        in_copy(e, j, slot).wait()
        wait_out(slot)
        ybuf_ref[slot] = mlp(xbuf_ref[slot])
        out_copy(j, slot).start()
        busy_ref[slot] = 1
        return carry

    lax.fori_loop(0, n, block, 0)

    @pl.when(has_half)
    def _():
        half_in.wait()
        wait_out(HALF_BUF, half=True)
        yhalf_ref[...] = mlp(xhalf_ref[...])
        half_out(half_row0).start()
        busy_ref[HALF_BUF] = 1

    e_next = jnp.minimum(e + 1, N_EXPERTS - 1)

    @pl.when((e + 1 < N_EXPERTS) & (nblk_ref[e_next] > 0))
    def _():
        in_copy(e_next, 0, 0).start()

    @pl.when(e == N_EXPERTS - 1)
    def _():
        wait_out(0)
        wait_out(1)
        wait_out(HALF_BUF, half=True)
        yhalf_ref[...] = jnp.zeros(yhalf_ref.shape, yhalf_ref.dtype)
        first_unused = (half_row0 + half_ref[e] * MOE_HALF) // MOE_HALF
        n_halves = ys_hbm.shape[0] // MOE_HALF
        lax.fori_loop(first_unused, n_halves, lambda j, c: (half_out(j * MOE_HALF).start(), c)[1], 0)
        lax.fori_loop(first_unused, n_halves, lambda j, c: (half_out(j * MOE_HALF).wait(), c)[1], 0)


def _experts(base, n_blk, n_half, xs, w1, b1, w2, b2):
    P = xs.shape[0]
    expert = lambda e, bs, nb, nh: (e, 0, 0)
    grid_spec = pltpu.PrefetchScalarGridSpec(
        num_scalar_prefetch=3,
        grid=(N_EXPERTS,),
        in_specs=[pl.BlockSpec((None, D_MODEL, 2 * D_FF), expert),
                  pl.BlockSpec((None, 1, 2 * D_FF), expert),
                  pl.BlockSpec((None, D_FF, D_MODEL), expert),
                  pl.BlockSpec((None, 1, D_MODEL), expert),
                  pl.BlockSpec(memory_space=pl.ANY)],
        out_specs=pl.BlockSpec(memory_space=pl.ANY),
        scratch_shapes=[pltpu.VMEM((D_MODEL, 2 * D_FF), jnp.bfloat16),
                        pltpu.VMEM((D_FF, D_MODEL), jnp.bfloat16),
                        pltpu.VMEM((2, MOE_BLOCK, D_MODEL // 2), jnp.uint32),
                        pltpu.VMEM((2, MOE_BLOCK, D_MODEL // 2), jnp.uint32),
                        pltpu.VMEM((MOE_HALF, D_MODEL // 2), jnp.uint32),
                        pltpu.VMEM((MOE_HALF, D_MODEL // 2), jnp.uint32),
                        pltpu.SemaphoreType.DMA((3,)),
                        pltpu.SemaphoreType.DMA((3,)),
                        pltpu.SMEM((3,), jnp.int32)],
    )
    return pl.pallas_call(
        _experts_kernel,
        out_shape=jax.ShapeDtypeStruct((P, D_MODEL // 2), jnp.uint32),
        grid_spec=grid_spec,
        compiler_params=_params(("arbitrary",)),
        name="experts",
    )(base, n_blk, n_half, w1, b1, w2, b2, xs)


def _combine_kernel(src_ref, len_ref, dst_ref, rows_ref, gate_t_ref, x1_ref, ys_hbm, o_ref, ybuf_ref, sems):
    i = pl.program_id(0)
    n_tiles = pl.num_programs(0)
    n_buf = ybuf_ref.shape[0]
    cur = i % n_buf
    n_rows, tm = ybuf_ref.shape[1], x1_ref.shape[0]

    def fetch(tile, slot):
        _run_copies(src_ref, len_ref, dst_ref, tile,
                    lambda s, d, n: pltpu.make_async_copy(ys_hbm.at[pl.ds(d, n)], ybuf_ref.at[slot, pl.ds(s, n)],
                                                          sems.at[slot]))

    @pl.when(i == 0)
    def _():
        ybuf_ref[...] = jnp.zeros(ybuf_ref.shape, ybuf_ref.dtype)
        for tile in range(n_buf - 1):
            fetch(tile, tile)

    ahead = i + n_buf - 1

    @pl.when(ahead < n_tiles)
    def _():
        fetch(ahead, ahead % n_buf)

    g = gate_t_ref[...]
    p_iota = lax.broadcasted_iota(jnp.int32, (tm, n_rows), 1)
    weights = jnp.zeros((tm, n_rows), jnp.float32)
    for k in range(TOP_K):
        pos_k = g[:, TOP_K + k:TOP_K + k + 1].astype(jnp.int32)
        weights = jnp.where(p_iota == pos_k, g[:, k:k + 1], weights)
    weights = weights.astype(jnp.bfloat16)

    rows = pl.multiple_of(rows_ref[i], RUN_ALIGN)
    pltpu.make_async_copy(ys_hbm.at[pl.ds(0, rows)], ybuf_ref.at[cur, pl.ds(0, rows)], sems.at[cur]).wait()
    halves = [jnp.dot(weights, y, preferred_element_type=jnp.float32) for y in _unpack_bf16_pairs(ybuf_ref[cur])]
    o_ref[...] = x1_ref[...] + jnp.concatenate(halves, axis=1)


def _combine(run_src, run_len, run_dst, tile_rows, gate_t, x1, ys):
    T = x1.shape[0]
    tok = lambda w: pl.BlockSpec((TILE, w), lambda i, *_: (i, 0))
    grid_spec = pltpu.PrefetchScalarGridSpec(
        num_scalar_prefetch=4,
        grid=(T // TILE,),
        in_specs=[tok(LANES), tok(D_MODEL), pl.BlockSpec(memory_space=pl.ANY)],
        out_specs=tok(D_MODEL),
        scratch_shapes=[pltpu.VMEM((DISPATCH_BUFS, TILE_ROWS, D_MODEL // 2), jnp.uint32),
                        pltpu.SemaphoreType.DMA((DISPATCH_BUFS,))],
    )
    return pl.pallas_call(
        _combine_kernel,
        out_shape=jax.ShapeDtypeStruct((T, D_MODEL), jnp.float32),
        grid_spec=grid_spec,
        compiler_params=_params(("arbitrary",)),
        name="combine",
    )(run_src, run_len, run_dst, tile_rows, gate_t, x1, ys)


def _rotary_tables(positions):
    pos = positions.astype(jnp.float32)[..., None]
    lane = jnp.arange(LANES)
    half_r = RET_DK // 2
    inv_r = RET_ROPE_THETA ** (-jnp.linspace(0.0, 1.0, half_r, dtype=jnp.float32))
    ret_cs = jnp.cos(pos * inv_r[lane % half_r] - jnp.where(lane < half_r, 0.0, 0.5 * jnp.pi))
    half_d = ROT_DIM // 2
    inv_d = ROPE_THETA ** (-jnp.arange(0, ROT_DIM, 2, dtype=jnp.float32) / ROT_DIM)
    sub = lane % DIFF_DH
    ang_d = jnp.where(sub < ROT_DIM, pos * inv_d[sub % half_d], 0.0)
    diff_cs = jnp.cos(ang_d - jnp.where((sub >= half_d) & (sub < ROT_DIM), 0.5 * jnp.pi, 0.0))
    return ret_cs, diff_cs


def kernel(x, positions, norm1_w, w_in, ret_log_decay_fwd, ret_log_decay_bwd, ret_norm_w, q_norm_w, k_norm_w, lambda_q1, lambda_k1, lambda_q2, lambda_k2, diff_norm_w, w_out, norm2_w, w_router, b_router, w1, b1, w2, b2):
    B, S, D = x.shape
    T = B * S
    f32 = jnp.float32
    bf16 = jnp.bfloat16
    x2 = x.reshape(T, D)

    dup = lambda w: jnp.concatenate([w, w]).reshape(1, LANES).astype(f32)
    proj, rq_r, rk_r, qs, ks = _in_proj(x2, norm1_w[0].reshape(1, D), w_in[0].astype(bf16),
                                        _rotary_tables(positions), dup(q_norm_w[0]), dup(k_norm_w[0]), B, S)

    y_ret = _retention(ret_log_decay_fwd[0].astype(f32), ret_log_decay_bwd[0].astype(f32),
                       rq_r, rk_r, proj, ret_norm_w[0].reshape(1, RET_WIDTH).astype(f32), B, S)

    lam = (jnp.exp(jnp.sum(lambda_q1[0].astype(f32) * lambda_k1[0].astype(f32)))
           - jnp.exp(jnp.sum(lambda_q2[0].astype(f32) * lambda_k2[0].astype(f32))) + LAMBDA_INIT)
    lam_row = jnp.full((1, LANES), lam, f32)
    bound = (SCORE_BOUND_SLACK * DIFF_DH ** 0.5 * LOG2_E
             * jnp.max(jnp.abs(q_norm_w[0].astype(f32))) * jnp.max(jnp.abs(k_norm_w[0].astype(f32)))).reshape(1)
    attn_args = (bound, qs, ks, proj, lam_row, diff_norm_w[0].reshape(1, DIFF_DV).astype(f32), B, S)
    y_diff = lax.cond(bound[0] <= MAX_SAFE_SCORE_BOUND,
                      lambda: _diff_attn(False, *attn_args), lambda: _diff_attn(True, *attn_args))

    x1, h2, pos, gate_t, len_t, off_t, tot_t = _out_router(
        x2, y_ret, y_diff, w_out[0].astype(bf16), norm2_w[0].reshape(1, D),
        w_router[0].T.astype(f32), b_router[0].reshape(N_EXPERTS, 1).astype(f32))

    n_tiles = T // TILE
    run_len = len_t[:, :, 0].astype(jnp.int32)
    total = tot_t[:, 0].astype(jnp.int32)
    padded = ((total + MOE_HALF - 1) // MOE_HALF) * MOE_HALF
    pad_end = jnp.cumsum(padded)
    pad_start = pad_end - padded
    run_dst = pad_start[None, :] + off_t[:, :, 0].astype(jnp.int32)
    run_src = jnp.cumsum(run_len, axis=1) - run_len
    tile_rows = jnp.sum(run_len, axis=1)
    P = T * TOP_K + n_tiles * N_EXPERTS * RUN_ALIGN + N_EXPERTS * MOE_HALF
    n_used = (pad_end[-1:] // MOE_HALF).astype(jnp.int32)
    runs = (run_src.reshape(-1), run_len.reshape(-1), run_dst.reshape(-1), tile_rows)

    xs = _dispatch(*runs, pad_start + total, padded - total, n_used, pos, h2, P)
    ys = _experts(pad_start, padded // MOE_BLOCK, (padded // MOE_HALF) % 2, xs, w1[0],
                  b1[0].reshape(N_EXPERTS, 1, 2 * D_FF), w2[0], b2[0].reshape(N_EXPERTS, 1, D))
    out = _combine(*runs, gate_t, x1, ys)
    return out.reshape(B, S, D)
```

```python
import functools

import jax
import jax.numpy as jnp
from jax import lax
from jax.experimental import pallas as pl
from jax.experimental.pallas import tpu as pltpu

EPS = 1e-6
D_MODEL = 1024
RET_HEADS = 4
RET_DK = 128
RET_WIDTH = 512
RET_ROPE_THETA = 10000.0
DIFF_HEADS = 4
DIFF_DH = 64
DIFF_DV = 128
DIFF_WIDTH = 512
ROPE_THETA = 500000.0
ROT_DIM = DIFF_DH // 4
D_IN_PROJ = 3584
N_EXPERTS = 32
TOP_K = 4
D_FF = 1024
SWIGLU_LIMIT = 7.0
SWIGLU_ALPHA = 1.702
LAMBDA_INIT = 0.8 - 0.6 * 1.0

LOG2_E = 1.4426950408889634
SCORE_BOUND_SLACK = 1.02
MAX_SAFE_SCORE_BOUND = 60.0
LANES = 128
SUBLANES = 8
VMEM_LIMIT = 56 * 1024 * 1024

COL_RQ, COL_RK, COL_RV, COL_RG, COL_DQ, COL_DK, COL_DV = 0, 4, 8, 12, 16, 20, 24
VGV_RV, VGV_RG, VGV_DV = 0, 4, 8

TM_PROJ = 512
RET_CHUNK = 128
RET_UNROLL = 32
TQ_ATTN = 1024
TK_ATTN = 4096
TK_ATTN_ONLINE = 2048
TM_ROUTE = 512
TILE = 256
MOE_BLOCK = 512
MOE_HALF = MOE_BLOCK // 2
RUN_ALIGN = SUBLANES
TILE_ROWS = TOP_K * TILE + N_EXPERTS * RUN_ALIGN
DISPATCH_BUFS = 2


def _params(sem, **kw):
    return pltpu.CompilerParams(dimension_semantics=sem, vmem_limit_bytes=VMEM_LIMIT, **kw)


def _in_proj_kernel(x_ref, nw_ref, w_ref, ret_cs_ref, diff_cs_ref, qw_ref, kw_ref,
                    vgv_ref, rqo_ref, rko_ref, qs_ref, ks_ref):
    ts = x_ref.shape[0]
    x = x_ref[...]
    hn = (x * lax.rsqrt(jnp.mean(x * x, axis=-1, keepdims=True) + EPS) * nw_ref[...]).astype(jnp.bfloat16)

    def proj(col_block):
        c0 = col_block * LANES
        return jnp.dot(hn, w_ref[:, c0:c0 + 4 * LANES], preferred_element_type=jnp.float32)

    dq, dk, rq, rk = proj(COL_DQ), proj(COL_DK), proj(COL_RQ), proj(COL_RK)
    lane = lax.broadcasted_iota(jnp.int32, (ts, LANES), 1)
    lo = lane < DIFF_DH
    ret_cs = ret_cs_ref[...]
    ret_sc = pltpu.roll(ret_cs, RET_DK // 2, 1)
    first_half = lane < RET_DK // 2
    c2 = jnp.where(first_half, ret_cs, ret_sc)
    s2 = jnp.where(first_half, -ret_sc, ret_cs)
    sub = lane % DIFF_DH
    sin_lanes = (sub >= ROT_DIM // 2) & (sub < ROT_DIM)
    cs_rows = jnp.concatenate([diff_cs_ref[...], jnp.ones((LANES - ROT_DIM, ts), jnp.float32)], axis=0)
    cs_first = cs_rows.T
    diff_cs = jnp.where((lane >= DIFF_DH) & (lane < DIFF_DH + ROT_DIM), pltpu.roll(cs_first, DIFF_DH, 1), cs_first)
    ra = jnp.where(sin_lanes, pltpu.roll(diff_cs, ROT_DIM // 2, 1), diff_cs)
    rp = jnp.where(sin_lanes, diff_cs, 0.0)
    rn = jnp.where(sub < ROT_DIM // 2, -pltpu.roll(diff_cs, LANES - ROT_DIM // 2, 1), 0.0)

    def qk_norm_rot(x, w):
        x2 = x * x
        s_lo = jnp.sum(jnp.where(lo, x2, 0.0), axis=-1, keepdims=True)
        s_hi = jnp.sum(jnp.where(lo, 0.0, x2), axis=-1, keepdims=True)
        ms = jnp.where(lo, s_lo, s_hi) * (1.0 / DIFF_DH)
        xn = x * lax.rsqrt(ms + EPS) * w
        return xn * ra + pltpu.roll(xn, ROT_DIM // 2, 1) * rp + pltpu.roll(xn, LANES - ROT_DIM // 2, 1) * rn

    for h in range(DIFF_HEADS):
        sl = slice(h * LANES, (h + 1) * LANES)
        q = qk_norm_rot(dq[:, sl], qw_ref[...]) * (DIFF_DH ** -0.5 * LOG2_E)
        k = qk_norm_rot(dk[:, sl], kw_ref[...])
        qs_ref[h, 0] = jnp.where(lo, q, 0.0).astype(qs_ref.dtype)
        qs_ref[h, 1] = jnp.where(lo, 0.0, q).astype(qs_ref.dtype)
        ks_ref[:, sl] = k.astype(ks_ref.dtype)
    for h in range(RET_HEADS):
        sl = slice(h * LANES, (h + 1) * LANES)
        q = rq[:, sl]
        k = rk[:, sl]
        rqo_ref[:, sl] = (q * c2 + pltpu.roll(q, RET_DK // 2, 1) * s2).astype(rqo_ref.dtype)
        rko_ref[:, sl] = ((k * c2 + pltpu.roll(k, RET_DK // 2, 1) * s2) * (RET_DK ** -0.5)).astype(rko_ref.dtype)
    for slot, col_block in enumerate((COL_RV, COL_RG, COL_DV)):
        vgv_ref[:, slot * 4 * LANES:(slot + 1) * 4 * LANES] = proj(col_block).astype(vgv_ref.dtype)


def _in_proj(x2, nw, w_bf16, tabs, qw2, kw2, B, S):
    T = B * S
    n_s = S // TM_PROJ
    tok = lambda w: pl.BlockSpec((TM_PROJ, w), lambda i: (i, 0))
    const = lambda s: pl.BlockSpec(s, lambda i: (0, 0))
    tab = pl.BlockSpec((None, TM_PROJ, LANES), lambda i: (i // n_s, i % n_s, 0))
    bf16 = jnp.bfloat16
    return pl.pallas_call(
        _in_proj_kernel,
        out_shape=(jax.ShapeDtypeStruct((T, 3 * 4 * LANES), bf16),
                   jax.ShapeDtypeStruct((T, 4 * LANES), bf16),
                   jax.ShapeDtypeStruct((T, 4 * LANES), bf16),
                   jax.ShapeDtypeStruct((B, DIFF_HEADS, 2, S, LANES), bf16),
                   jax.ShapeDtypeStruct((T, 4 * LANES), bf16)),
        grid=(T // TM_PROJ,),
        in_specs=[tok(D_MODEL), const((1, D_MODEL)), const((D_MODEL, D_IN_PROJ)),
                  tab, pl.BlockSpec((None, ROT_DIM, TM_PROJ), lambda i: (i // n_s, 0, i % n_s)),
                  const((1, LANES)), const((1, LANES))],
        out_specs=(tok(3 * 4 * LANES), tok(4 * LANES), tok(4 * LANES),
                   pl.BlockSpec((None, DIFF_HEADS, 2, TM_PROJ, LANES), lambda i: (i // n_s, 0, 0, i % n_s, 0)),
                   tok(4 * LANES)),
        compiler_params=_params(("arbitrary",)),
        name="in_proj",
    )(x2, nw, w_bf16, *tabs, qw2, kw2)


def _retention_kernel(ldf_ref, ldb_ref, q_ref, k_ref, v_ref, g_ref, nw_ref, o_ref, sb_ref):
    C = RET_CHUNK
    S = q_ref.shape[0]
    n_chunks = S // C
    h = pl.program_id(1)
    ldf = ldf_ref[h]
    ldb = ldb_ref[h]
    row = lax.broadcasted_iota(jnp.int32, (C, C), 0).astype(jnp.float32)
    colm = lax.broadcasted_iota(jnp.int32, (C, C), 1).astype(jnp.float32)
    dist = row - colm
    decay = jnp.where(dist >= 0, jnp.exp(ldf * jnp.maximum(dist, 0.0)), jnp.exp(ldb * jnp.maximum(-dist, 0.0)))
    idx = lax.broadcasted_iota(jnp.int32, (C, 1), 0).astype(jnp.float32)
    q_dec_f = jnp.exp(ldf * (idx + 1.0))
    k_dec_f = jnp.exp(ldf * (C - 1.0 - idx))
    q_dec_b = jnp.exp(ldb * (C - idx))
    k_dec_b = jnp.exp(ldb * idx)
    chunk_dec_f = jnp.exp(ldf * C)
    chunk_dec_b = jnp.exp(ldb * C)
    f32 = jnp.float32
    bf16 = jnp.bfloat16

    def kv_state(k, v, k_dec):
        kd = (k.astype(f32) * k_dec).astype(bf16)
        return lax.dot_general(kd, v, (((0,), (0,)), ((), ())), preferred_element_type=f32)

    def bwd_step(i, state):
        c = n_chunks - 1 - i
        r0 = pl.multiple_of(c * C, C)
        sb_ref[c] = state
        return state * chunk_dec_b + kv_state(k_ref[pl.ds(r0, C), :], v_ref[pl.ds(r0, C), :], k_dec_b)

    lax.fori_loop(0, n_chunks, bwd_step, jnp.zeros((RET_DK, LANES), f32), unroll=RET_UNROLL)

    def fwd_step(c, state):
        r0 = pl.multiple_of(c * C, C)
        q = q_ref[pl.ds(r0, C), :]
        k = k_ref[pl.ds(r0, C), :]
        v = v_ref[pl.ds(r0, C), :]
        scores = lax.dot_general(q, k, (((1,), (1,)), ((), ())), preferred_element_type=f32) * decay
        y = jnp.dot(scores.astype(bf16), v, preferred_element_type=f32)
        qf = q.astype(f32)
        y += jnp.dot((qf * q_dec_f).astype(bf16), state.astype(bf16), preferred_element_type=f32)
        y += jnp.dot((qf * q_dec_b).astype(bf16), sb_ref[c].astype(bf16), preferred_element_type=f32)
        yn = y * lax.rsqrt(jnp.mean(y * y, axis=-1, keepdims=True) + EPS) * nw_ref[...]
        g = g_ref[pl.ds(r0, C), :].astype(f32)
        o_ref[pl.ds(r0, C), :] = (yn * (g * jax.nn.sigmoid(g))).astype(o_ref.dtype)
        return state * chunk_dec_f + kv_state(k, v, k_dec_f)

    lax.fori_loop(0, n_chunks, fwd_step, jnp.zeros((RET_DK, LANES), f32), unroll=RET_UNROLL)


def _retention(ldf, ldb, rq_r, rk_r, proj, nw, B, S):
    T = B * S
    smem = pl.BlockSpec(memory_space=pltpu.SMEM)
    seq = lambda cb: pl.BlockSpec((S, LANES), lambda b, h: (b, cb + h))
    return pl.pallas_call(
        _retention_kernel,
        out_shape=jax.ShapeDtypeStruct((T, RET_WIDTH), jnp.bfloat16),
        grid=(B, RET_HEADS),
        in_specs=[smem, smem, seq(0), seq(0), seq(VGV_RV), seq(VGV_RG),
                  pl.BlockSpec((1, LANES), lambda b, h: (0, h))],
        out_specs=seq(0),
        scratch_shapes=[pltpu.VMEM((S // RET_CHUNK, RET_DK, LANES), jnp.float32)],
        compiler_params=_params(("arbitrary", "arbitrary")),
        name="retention",
    )(ldf, ldb, rq_r, rk_r, proj, proj, nw)


def _diff_attn_kernel(online_max, bound_ref, q_ref, k_ref, v_ref, lam_ref, nw_ref, o_ref, m_ref, l_ref, acc_ref):
    tq = q_ref.shape[1]
    S = k_ref.shape[0]
    f32 = jnp.float32
    q = q_ref[...].reshape(2 * tq, LANES)
    if online_max:
        m_ref[...] = jnp.full(m_ref.shape, -jnp.inf, f32)
    l_ref[...] = jnp.zeros(l_ref.shape, f32)
    acc_ref[...] = jnp.zeros(acc_ref.shape, f32)
    tk = TK_ATTN_ONLINE if online_max else TK_ATTN
    n_tiles = tk // LANES

    def kv_step(j, carry):
        r0 = pl.multiple_of(j * tk, tk)
        k = k_ref[pl.ds(r0, tk), :]
        v = v_ref[pl.ds(r0, tk), :]
        s = lax.dot_general(q, k, (((1,), (1,)), ((), ())), preferred_element_type=f32)
        tiles = [s[:, c * LANES:(c + 1) * LANES] for c in range(n_tiles)]
        if online_max:
            part = tiles[0]
            for t in tiles[1:]:
                part = jnp.maximum(part, t)
            m_prev = m_ref[...]
            shift = jnp.maximum(m_prev, jnp.max(part, axis=-1, keepdims=True))
            alpha = jnp.exp2(m_prev - shift)
            m_ref[...] = shift
        else:
            shift = bound_ref[0]
        probs = [jnp.exp2(t - shift) for t in tiles]
        psum = probs[0]
        for p in probs[1:]:
            psum = psum + p
        pv = jnp.dot(jnp.concatenate([p.astype(jnp.bfloat16) for p in probs], axis=1), v,
                     preferred_element_type=f32)
        if online_max:
            l_ref[...] = alpha * l_ref[...] + psum
            acc_ref[...] = alpha * acc_ref[...] + pv
        else:
            l_ref[...] = l_ref[...] + psum
            acc_ref[...] = acc_ref[...] + pv
        return carry

    lax.fori_loop(0, S // tk, kv_step, 0)
    o = acc_ref[...] / jnp.sum(l_ref[...], axis=-1, keepdims=True)
    d = o[:tq] - lam_ref[...] * o[tq:]
    dn = d * lax.rsqrt(jnp.mean(d * d, axis=-1, keepdims=True) + EPS) * nw_ref[...]
    o_ref[...] = (dn * (1.0 - LAMBDA_INIT)).astype(o_ref.dtype)


def _diff_attn(online_max, bound, qs, ks, proj, lam, nw, B, S):
    T = B * S
    n_q = S // TQ_ATTN
    one = pl.BlockSpec((1, LANES), lambda b, h, i, bd: (0, 0))
    grid_spec = pltpu.PrefetchScalarGridSpec(
        num_scalar_prefetch=1,
        grid=(B, DIFF_HEADS, n_q),
        in_specs=[pl.BlockSpec((None, None, 2, TQ_ATTN, LANES), lambda b, h, i, bd: (b, h, 0, i, 0)),
                  pl.BlockSpec((S, LANES), lambda b, h, i, bd: (b, h)),
                  pl.BlockSpec((S, LANES), lambda b, h, i, bd: (b, VGV_DV + h)),
                  one, one],
        out_specs=pl.BlockSpec((TQ_ATTN, LANES), lambda b, h, i, bd: (b * n_q + i, h)),
        scratch_shapes=[pltpu.VMEM((2 * TQ_ATTN, LANES), jnp.float32)] * 3,
    )
    return pl.pallas_call(
        functools.partial(_diff_attn_kernel, online_max),
        out_shape=jax.ShapeDtypeStruct((T, DIFF_WIDTH), jnp.bfloat16),
        grid_spec=grid_spec,
        compiler_params=_params(("arbitrary", "arbitrary", "arbitrary")),
        name="diff_attn_online" if online_max else "diff_attn",
    )(bound, qs, ks, proj, lam, nw)


def _out_router_kernel(x_ref, yr_ref, yd_ref, wo_ref, n2_ref, wrt_ref, br_ref,
                       x1_ref, h2_ref, pos_ref, gate_t_ref, len_ref, off_ref, tot_ref):
    tm = x_ref.shape[0]
    f32 = jnp.float32
    bf16 = jnp.bfloat16

    @pl.when(pl.program_id(0) == 0)
    def _():
        tot_ref[...] = jnp.zeros(tot_ref.shape, f32)

    att = jnp.dot(yr_ref[...], wo_ref[:RET_WIDTH, :], preferred_element_type=f32)
    att += jnp.dot(yd_ref[...], wo_ref[RET_WIDTH:, :], preferred_element_type=f32)
    x1 = x_ref[...] + att
    x1_ref[...] = x1
    h2 = x1 * lax.rsqrt(jnp.mean(x1 * x1, axis=-1, keepdims=True) + EPS) * n2_ref[...]
    h2_ref[...] = h2.astype(h2_ref.dtype)
    nt = (((1,), (1,)), ((), ()))
    h_hi = h2.astype(bf16)
    h_lo = (h2 - h_hi.astype(f32)).astype(bf16)
    w = wrt_ref[...]
    w_hi = w.astype(bf16)
    w_lo = (w - w_hi.astype(f32)).astype(bf16)
    with_h_hi = lax.dot_general(jnp.concatenate([w_hi, w_lo], axis=0), h_hi, nt, preferred_element_type=f32)
    logits = (with_h_hi[:N_EXPERTS] + with_h_hi[N_EXPERTS:]
              + lax.dot_general(w_hi, h_lo, nt, preferred_element_type=f32)) + br_ref[...]
    e_iota = lax.broadcasted_iota(jnp.int32, (N_EXPERTS, tm), 0)
    work = logits
    vals, hots = [], []
    for _ in range(TOP_K):
        mx = jnp.max(work, axis=0, keepdims=True)
        ix = jnp.min(jnp.where(work == mx, e_iota, N_EXPERTS), axis=0, keepdims=True)
        hot = e_iota == ix
        vals.append(mx)
        hots.append(hot)
        work = jnp.where(hot, -jnp.inf, work)
    exps = [jnp.exp(v - vals[0]) for v in vals]
    denom = exps[0] + exps[1] + exps[2] + exps[3]
    gates = [e / denom for e in exps]
    masks = [jnp.where(hot, 1.0, 0.0) for hot in hots]
    sel = masks[0] + masks[1] + masks[2] + masks[3]
    t_row = lax.broadcasted_iota(jnp.int32, (TILE, TILE), 0)
    t_col = lax.broadcasted_iota(jnp.int32, (TILE, TILE), 1)
    upper = jnp.where(t_row < t_col, 1.0, 0.0).astype(bf16)
    e_row = lax.broadcasted_iota(jnp.int32, (N_EXPERTS, N_EXPERTS), 0)
    e_col = lax.broadcasted_iota(jnp.int32, (N_EXPERTS, N_EXPERTS), 1)
    lower = jnp.where(e_col < e_row, 1.0, 0.0).astype(bf16)
    pos_parts = []
    for t in range(tm // TILE):
        cols = slice(t * TILE, (t + 1) * TILE)
        sel_t = sel[:, cols]
        rank = jnp.dot(sel_t.astype(bf16), upper, preferred_element_type=f32)
        cnt = jnp.sum(sel_t, axis=1, keepdims=True)
        run_units = jnp.floor((cnt + (RUN_ALIGN - 1.0)) * (1.0 / RUN_ALIGN))
        run_len = jnp.broadcast_to(run_units * RUN_ALIGN, (N_EXPERTS, LANES))
        run_start = jnp.dot(lower, jnp.broadcast_to(run_units, (N_EXPERTS, LANES)).astype(bf16),
                            preferred_element_type=f32) * RUN_ALIGN
        pos_full = rank + run_start[:, 0:1]
        pos_parts.append([jnp.sum(mask[:, cols] * pos_full, axis=0, keepdims=True) for mask in masks])
        len_ref[t] = run_len
        off_ref[t] = tot_ref[...]
        tot_ref[...] = tot_ref[...] + run_len
    pos = [jnp.concatenate([part[k] for part in pos_parts], axis=1) for k in range(TOP_K)]
    for k in range(TOP_K):
        pos_ref[k:k + 1, :] = pos[k].astype(jnp.int32)
    rows = jnp.concatenate(gates + pos + [jnp.zeros((LANES - 2 * TOP_K, tm), f32)], axis=0)
    gate_t_ref[...] = rows.T


def _out_router(x2, y_ret, y_diff, wo_bf16, n2w, wrt, br):
    T = x2.shape[0]
    n_tiles = T // TILE
    tiles_per_step = TM_ROUTE // TILE
    tok = lambda w: pl.BlockSpec((TM_ROUTE, w), lambda i: (i, 0))
    const = lambda s: pl.BlockSpec(s, lambda i: (0, 0))
    per_tile = pl.BlockSpec((tiles_per_step, N_EXPERTS, LANES), lambda i: (i, 0, 0))
    return pl.pallas_call(
        _out_router_kernel,
        out_shape=(jax.ShapeDtypeStruct((T, D_MODEL), jnp.float32),
                   jax.ShapeDtypeStruct((T, D_MODEL), jnp.bfloat16),
                   jax.ShapeDtypeStruct((TOP_K, T), jnp.int32),
                   jax.ShapeDtypeStruct((T, LANES), jnp.float32),
                   jax.ShapeDtypeStruct((n_tiles, N_EXPERTS, LANES), jnp.float32),
                   jax.ShapeDtypeStruct((n_tiles, N_EXPERTS, LANES), jnp.float32),
                   jax.ShapeDtypeStruct((N_EXPERTS, LANES), jnp.float32)),
        grid=(T // TM_ROUTE,),
        in_specs=[tok(D_MODEL), tok(RET_WIDTH), tok(DIFF_WIDTH), const((D_MODEL, D_MODEL)),
                  const((1, D_MODEL)), const((N_EXPERTS, D_MODEL)), const((N_EXPERTS, 1))],
        out_specs=(tok(D_MODEL), tok(D_MODEL), pl.BlockSpec((TOP_K, TM_ROUTE), lambda i: (0, i)), tok(LANES),
                   per_tile, per_tile, const((N_EXPERTS, LANES))),
        compiler_params=_params(("arbitrary",)),
        name="out_router",
    )(x2, y_ret, y_diff, wo_bf16, n2w, wrt, br)


def _pack_bf16_pairs(x):
    w = x.shape[1] // 2
    bits = lambda v: lax.bitcast_convert_type(v.astype(jnp.bfloat16).astype(jnp.float32), jnp.uint32)
    return (bits(x[:, :w]) >> 16) | (bits(x[:, w:]) & jnp.uint32(0xFFFF0000))


def _unpack_bf16_pairs(p):
    as_bf16 = lambda bits: lax.bitcast_convert_type(bits, jnp.float32).astype(jnp.bfloat16)
    return as_bf16(p << 16), as_bf16(p & jnp.uint32(0xFFFF0000))


def _run_copies(src_ref, len_ref, dst_ref, tile, make_copy):
    for e in range(N_EXPERTS):
        n = pl.multiple_of(len_ref[tile * N_EXPERTS + e], RUN_ALIGN)
        s = pl.multiple_of(src_ref[tile * N_EXPERTS + e], RUN_ALIGN)
        d = pl.multiple_of(dst_ref[tile * N_EXPERTS + e], RUN_ALIGN)

        @pl.when(n > 0)
        def _():
            make_copy(s, d, n).start()


def _dispatch_kernel(src_ref, len_ref, dst_ref, rows_ref, zlo_ref, zlen_ref, nu_ref,
                     pos_ref, h2_ref, xs_hbm, xbuf_ref, zero_ref, sems, zero_sem):
    i = pl.program_id(0)
    n_tiles = pl.num_programs(0)
    n_buf = xbuf_ref.shape[0]
    cur = i % n_buf
    n_rows, tm = xbuf_ref.shape[1], h2_ref.shape[0]

    @pl.when(i == 0)
    def _():
        zero_ref[...] = jnp.zeros(zero_ref.shape, zero_ref.dtype)

        def pad_copy(e):
            n = pl.multiple_of(zlen_ref[e], RUN_ALIGN)
            lo = pl.multiple_of(zlo_ref[e], RUN_ALIGN)
            return pltpu.make_async_copy(zero_ref.at[pl.ds(0, n)], xs_hbm.at[pl.ds(lo, n)], zero_sem)

        def tail_copy(j):
            return pltpu.make_async_copy(zero_ref, xs_hbm.at[pl.ds(j * MOE_HALF, MOE_HALF)], zero_sem)

        def guarded(copy, op):
            def body(e, c):
                @pl.when(zlen_ref[e] > 0)
                def _():
                    op(copy(e))
                return c
            return body

        lax.fori_loop(0, N_EXPERTS, guarded(pad_copy, lambda cp: cp.start()), 0)
        lax.fori_loop(0, N_EXPERTS, guarded(pad_copy, lambda cp: cp.wait()), 0)
        n_halves = xs_hbm.shape[0] // MOE_HALF
        lax.fori_loop(nu_ref[0], n_halves, lambda j, c: (tail_copy(j).start(), c)[1], 0)
        lax.fori_loop(nu_ref[0], n_halves, lambda j, c: (tail_copy(j).wait(), c)[1], 0)

    p_iota = lax.broadcasted_iota(jnp.int32, (n_rows, tm), 0)
    onehot = jnp.zeros((n_rows, tm), jnp.float32)
    for k in range(TOP_K):
        onehot = jnp.where(p_iota == pos_ref[k:k + 1, :], 1.0, onehot)
    xbuf_ref[cur] = _pack_bf16_pairs(
        jnp.dot(onehot.astype(jnp.bfloat16), h2_ref[...], preferred_element_type=jnp.float32))

    _run_copies(src_ref, len_ref, dst_ref, i,
                lambda s, d, n: pltpu.make_async_copy(xbuf_ref.at[cur, pl.ds(s, n)], xs_hbm.at[pl.ds(d, n)],
                                                      sems.at[cur]))

    def wait_tile(tile, slot):
        rows = pl.multiple_of(rows_ref[tile], RUN_ALIGN)
        pltpu.make_async_copy(xbuf_ref.at[slot, pl.ds(0, rows)], xs_hbm.at[pl.ds(0, rows)], sems.at[slot]).wait()

    oldest = n_buf - 1

    @pl.when(i >= oldest)
    def _():
        wait_tile(i - oldest, (i + 1) % n_buf)

    @pl.when(i == n_tiles - 1)
    def _():
        for back in range(oldest - 1, -1, -1):
            wait_tile(i - back, (i - back) % n_buf)


def _dispatch(run_src, run_len, run_dst, tile_rows, zero_lo, zero_len, n_used, pos, h2, P):
    T = h2.shape[0]
    n_pre = 7
    grid_spec = pltpu.PrefetchScalarGridSpec(
        num_scalar_prefetch=n_pre,
        grid=(T // TILE,),
        in_specs=[pl.BlockSpec((TOP_K, TILE), lambda i, *_: (0, i)),
                  pl.BlockSpec((TILE, D_MODEL), lambda i, *_: (i, 0))],
        out_specs=pl.BlockSpec(memory_space=pl.ANY),
        scratch_shapes=[pltpu.VMEM((DISPATCH_BUFS, TILE_ROWS, D_MODEL // 2), jnp.uint32),
                        pltpu.VMEM((MOE_HALF, D_MODEL // 2), jnp.uint32),
                        pltpu.SemaphoreType.DMA((DISPATCH_BUFS,)),
                        pltpu.SemaphoreType.DMA(())],
    )
    return pl.pallas_call(
        _dispatch_kernel,
        out_shape=jax.ShapeDtypeStruct((P, D_MODEL // 2), jnp.uint32),
        grid_spec=grid_spec,
        compiler_params=_params(("arbitrary",), has_side_effects=True),
        name="dispatch",
    )(run_src, run_len, run_dst, tile_rows, zero_lo, zero_len, n_used, pos, h2)


def _experts_kernel(base_ref, nblk_ref, half_ref, w1_ref, b1_ref, w2_ref, b2_ref, xs_hbm, ys_hbm,
                    w1b_ref, w2b_ref, xbuf_ref, ybuf_ref, xhalf_ref, yhalf_ref, in_sems, out_sems, busy_ref):
    e = pl.program_id(0)
    n = nblk_ref[e]
    has_half = half_ref[e] == 1
    HALF_BUF = 2

    def rows(expert, j):
        return pl.ds(pl.multiple_of(base_ref[expert] + j * MOE_BLOCK, MOE_HALF), MOE_BLOCK)

    def half_rows(first_row):
        return pl.ds(pl.multiple_of(first_row, MOE_HALF), MOE_HALF)

    def in_copy(expert, j, slot):
        return pltpu.make_async_copy(xs_hbm.at[rows(expert, j)], xbuf_ref.at[slot], in_sems.at[slot])

    def out_copy(j, slot):
        return pltpu.make_async_copy(ybuf_ref.at[slot], ys_hbm.at[rows(e, j)], out_sems.at[slot])

    half_row0 = base_ref[e] + n * MOE_BLOCK
    half_in = pltpu.make_async_copy(xs_hbm.at[half_rows(half_row0)], xhalf_ref, in_sems.at[HALF_BUF])

    def half_out(first_row):
        return pltpu.make_async_copy(yhalf_ref, ys_hbm.at[half_rows(first_row)], out_sems.at[HALF_BUF])

    def wait_out(buf, half=False):
        @pl.when(busy_ref[buf] == 1)
        def _():
            (half_out(0) if half else out_copy(0, buf)).wait()
            busy_ref[buf] = 0

    def mlp(x_packed):
        x = jnp.concatenate(_unpack_bf16_pairs(x_packed), axis=1)
        u = jnp.dot(x, w1b_ref[...], preferred_element_type=jnp.float32) + b1_ref[...]
        glu = jnp.minimum(u[:, :D_FF], SWIGLU_LIMIT)
        lin = jnp.clip(u[:, D_FF:], -SWIGLU_LIMIT, SWIGLU_LIMIT)
        act = glu * jax.nn.sigmoid(SWIGLU_ALPHA * glu) * (lin + 1.0)
        return _pack_bf16_pairs(jnp.dot(act.astype(jnp.bfloat16), w2b_ref[...],
                                        preferred_element_type=jnp.float32) + b2_ref[...])

    @pl.when(e == 0)
    def _():
        for buf in range(3):
            busy_ref[buf] = 0

        @pl.when(n > 0)
        def _():
            in_copy(0, 0, 0).start()

    @pl.when(has_half)
    def _():
        half_in.start()

    @pl.when((n > 0) | has_half)
    def _():
        w1b_ref[...] = w1_ref[...].astype(jnp.bfloat16)
        w2b_ref[...] = w2_ref[...].astype(jnp.bfloat16)

    def block(j, carry):
        slot = j % 2

        @pl.when(j + 1 < n)
        def _():
            in_copy(e, j + 1, 1 - slot).start()

        in_copy(e, j, slot).wait()
        wait_out(slot)
        ybuf_ref[slot] = mlp(xbuf_ref[slot])
        out_copy(j, slot).start()
        busy_ref[slot] = 1
        return carry

    lax.fori_loop(0, n, block, 0)

    @pl.when(has_half)
    def _():
        half_in.wait()
        wait_out(HALF_BUF, half=True)
        yhalf_ref[...] = mlp(xhalf_ref[...])
        half_out(half_row0).start()
        busy_ref[HALF_BUF] = 1

    e_next = jnp.minimum(e + 1, N_EXPERTS - 1)

    @pl.when((e + 1 < N_EXPERTS) & (nblk_ref[e_next] > 0))
    def _():
        in_copy(e_next, 0, 0).start()

    @pl.when(e == N_EXPERTS - 1)
    def _():
        wait_out(0)
        wait_out(1)
        wait_out(HALF_BUF, half=True)
        yhalf_ref[...] = jnp.zeros(yhalf_ref.shape, yhalf_ref.dtype)
        first_unused = (half_row0 + half_ref[e] * MOE_HALF) // MOE_HALF
        n_halves = ys_hbm.shape[0] // MOE_HALF
        lax.fori_loop(first_unused, n_halves, lambda j, c: (half_out(j * MOE_HALF).start(), c)[1], 0)
        lax.fori_loop(first_unused, n_halves, lambda j, c: (half_out(j * MOE_HALF).wait(), c)[1], 0)


def _experts(base, n_blk, n_half, xs, w1, b1, w2, b2):
    P = xs.shape[0]
    expert = lambda e, bs, nb, nh: (e, 0, 0)
    grid_spec = pltpu.PrefetchScalarGridSpec(
        num_scalar_prefetch=3,
        grid=(N_EXPERTS,),
        in_specs=[pl.BlockSpec((None, D_MODEL, 2 * D_FF), expert),
                  pl.BlockSpec((None, 1, 2 * D_FF), expert),
                  pl.BlockSpec((None, D_FF, D_MODEL), expert),
                  pl.BlockSpec((None, 1, D_MODEL), expert),
                  pl.BlockSpec(memory_space=pl.ANY)],
        out_specs=pl.BlockSpec(memory_space=pl.ANY),
        scratch_shapes=[pltpu.VMEM((D_MODEL, 2 * D_FF), jnp.bfloat16),
                        pltpu.VMEM((D_FF, D_MODEL), jnp.bfloat16),
                        pltpu.VMEM((2, MOE_BLOCK, D_MODEL // 2), jnp.uint32),
                        pltpu.VMEM((2, MOE_BLOCK, D_MODEL // 2), jnp.uint32),
                        pltpu.VMEM((MOE_HALF, D_MODEL // 2), jnp.uint32),
                        pltpu.VMEM((MOE_HALF, D_MODEL // 2), jnp.uint32),
                        pltpu.SemaphoreType.DMA((3,)),
                        pltpu.SemaphoreType.DMA((3,)),
                        pltpu.SMEM((3,), jnp.int32)],
    )
    return pl.pallas_call(
        _experts_kernel,
        out_shape=jax.ShapeDtypeStruct((P, D_MODEL // 2), jnp.uint32),
        grid_spec=grid_spec,
        compiler_params=_params(("arbitrary",)),
        name="experts",
    )(base, n_blk, n_half, w1, b1, w2, b2, xs)


def _combine_kernel(src_ref, len_ref, dst_ref, rows_ref, gate_t_ref, x1_ref, ys_hbm, o_ref, ybuf_ref, sems):
    i = pl.program_id(0)
    n_tiles = pl.num_programs(0)
    n_buf = ybuf_ref.shape[0]
    cur = i % n_buf
    n_rows, tm = ybuf_ref.shape[1], x1_ref.shape[0]

    def fetch(tile, slot):
        _run_copies(src_ref, len_ref, dst_ref, tile,
                    lambda s, d, n: pltpu.make_async_copy(ys_hbm.at[pl.ds(d, n)], ybuf_ref.at[slot, pl.ds(s, n)],
                                                          sems.at[slot]))

    @pl.when(i == 0)
    def _():
        ybuf_ref[...] = jnp.zeros(ybuf_ref.shape, ybuf_ref.dtype)
        for tile in range(n_buf - 1):
            fetch(tile, tile)

    ahead = i + n_buf - 1

    @pl.when(ahead < n_tiles)
    def _():
        fetch(ahead, ahead % n_buf)

    g = gate_t_ref[...]
    p_iota = lax.broadcasted_iota(jnp.int32, (tm, n_rows), 1)
    weights = jnp.zeros((tm, n_rows), jnp.float32)
    for k in range(TOP_K):
        pos_k = g[:, TOP_K + k:TOP_K + k + 1].astype(jnp.int32)
        weights = jnp.where(p_iota == pos_k, g[:, k:k + 1], weights)
    weights = weights.astype(jnp.bfloat16)

    rows = pl.multiple_of(rows_ref[i], RUN_ALIGN)
    pltpu.make_async_copy(ys_hbm.at[pl.ds(0, rows)], ybuf_ref.at[cur, pl.ds(0, rows)], sems.at[cur]).wait()
    halves = [jnp.dot(weights, y, preferred_element_type=jnp.float32) for y in _unpack_bf16_pairs(ybuf_ref[cur])]
    o_ref[...] = x1_ref[...] + jnp.concatenate(halves, axis=1)


def _combine(run_src, run_len, run_dst, tile_rows, gate_t, x1, ys):
    T = x1.shape[0]
    tok = lambda w: pl.BlockSpec((TILE, w), lambda i, *_: (i, 0))
    grid_spec = pltpu.PrefetchScalarGridSpec(
        num_scalar_prefetch=4,
        grid=(T // TILE,),
        in_specs=[tok(LANES), tok(D_MODEL), pl.BlockSpec(memory_space=pl.ANY)],
        out_specs=tok(D_MODEL),
        scratch_shapes=[pltpu.VMEM((DISPATCH_BUFS, TILE_ROWS, D_MODEL // 2), jnp.uint32),
                        pltpu.SemaphoreType.DMA((DISPATCH_BUFS,))],
    )
    return pl.pallas_call(
        _combine_kernel,
        out_shape=jax.ShapeDtypeStruct((T, D_MODEL), jnp.float32),
        grid_spec=grid_spec,
        compiler_params=_params(("arbitrary",)),
        name="combine",
    )(run_src, run_len, run_dst, tile_rows, gate_t, x1, ys)


def _rotary_tables(positions):
    pos = positions.astype(jnp.float32)[..., None]
    lane = jnp.arange(LANES)
    half_r = RET_DK // 2
    inv_r = RET_ROPE_THETA ** (-jnp.linspace(0.0, 1.0, half_r, dtype=jnp.float32))
    ret_cs = jnp.cos(pos * inv_r[lane % half_r] - jnp.where(lane < half_r, 0.0, 0.5 * jnp.pi))
    half_d = ROT_DIM // 2
    inv_d = ROPE_THETA ** (-jnp.arange(0, ROT_DIM, 2, dtype=jnp.float32) / ROT_DIM)
    row = jnp.arange(ROT_DIM)
    ang_d = positions.astype(jnp.float32)[:, None, :] * inv_d[row % half_d][None, :, None]
    diff_cs = jnp.cos(ang_d - jnp.where(row >= half_d, 0.5 * jnp.pi, 0.0)[None, :, None])
    return ret_cs, diff_cs


def kernel(x, positions, norm1_w, w_in, ret_log_decay_fwd, ret_log_decay_bwd, ret_norm_w, q_norm_w, k_norm_w, lambda_q1, lambda_k1, lambda_q2, lambda_k2, diff_norm_w, w_out, norm2_w, w_router, b_router, w1, b1, w2, b2):
    B, S, D = x.shape
    T = B * S
    f32 = jnp.float32
    bf16 = jnp.bfloat16
    x2 = x.reshape(T, D)

    dup = lambda w: jnp.concatenate([w, w]).reshape(1, LANES).astype(f32)
    proj, rq_r, rk_r, qs, ks = _in_proj(x2, norm1_w[0].reshape(1, D), w_in[0].astype(bf16),
                                        _rotary_tables(positions), dup(q_norm_w[0]), dup(k_norm_w[0]), B, S)

    y_ret = _retention(ret_log_decay_fwd[0].astype(f32), ret_log_decay_bwd[0].astype(f32),
                       rq_r, rk_r, proj, ret_norm_w[0].reshape(1, RET_WIDTH).astype(f32), B, S)

    lam = (jnp.exp(jnp.sum(lambda_q1[0].astype(f32) * lambda_k1[0].astype(f32)))
           - jnp.exp(jnp.sum(lambda_q2[0].astype(f32) * lambda_k2[0].astype(f32))) + LAMBDA_INIT)
    lam_row = jnp.full((1, LANES), lam, f32)
    bound = (SCORE_BOUND_SLACK * DIFF_DH ** 0.5 * LOG2_E
             * jnp.max(jnp.abs(q_norm_w[0].astype(f32))) * jnp.max(jnp.abs(k_norm_w[0].astype(f32)))).reshape(1)
    attn_args = (bound, qs, ks, proj, lam_row, diff_norm_w[0].reshape(1, DIFF_DV).astype(f32), B, S)
    y_diff = lax.cond(bound[0] <= MAX_SAFE_SCORE_BOUND,
                      lambda: _diff_attn(False, *attn_args), lambda: _diff_attn(True, *attn_args))

    x1, h2, pos, gate_t, len_t, off_t, tot_t = _out_router(
        x2, y_ret, y_diff, w_out[0].astype(bf16), norm2_w[0].reshape(1, D),
        w_router[0].T.astype(f32), b_router[0].reshape(N_EXPERTS, 1).astype(f32))

    n_tiles = T // TILE
    run_len = len_t[:, :, 0].astype(jnp.int32)
    total = tot_t[:, 0].astype(jnp.int32)
    padded = ((total + MOE_HALF - 1) // MOE_HALF) * MOE_HALF
    pad_end = jnp.cumsum(padded)
    pad_start = pad_end - padded
    run_dst = pad_start[None, :] + off_t[:, :, 0].astype(jnp.int32)
    run_src = jnp.cumsum(run_len, axis=1) - run_len
    tile_rows = jnp.sum(run_len, axis=1)
    P = T * TOP_K + n_tiles * N_EXPERTS * RUN_ALIGN + N_EXPERTS * MOE_HALF
    n_used = (pad_end[-1:] // MOE_HALF).astype(jnp.int32)
    runs = (run_src.reshape(-1), run_len.reshape(-1), run_dst.reshape(-1), tile_rows)

    xs = _dispatch(*runs, pad_start + total, padded - total, n_used, pos, h2, P)
    ys = _experts(pad_start, padded // MOE_BLOCK, (padded // MOE_HALF) % 2, xs, w1[0],
                  b1[0].reshape(N_EXPERTS, 1, 2 * D_FF), w2[0], b2[0].reshape(N_EXPERTS, 1, D))
    out = _combine(*runs, gate_t, x1, ys)
    return out.reshape(B, S, D)
```

```python
import functools

import jax
import jax.numpy as jnp
from jax import lax
from jax.experimental import pallas as pl
from jax.experimental.pallas import tpu as pltpu

EPS = 1e-6
D_MODEL = 1024
RET_HEADS = 4
RET_DK = 128
RET_WIDTH = 512
RET_ROPE_THETA = 10000.0
DIFF_HEADS = 4
DIFF_DH = 64
DIFF_DV = 128
DIFF_WIDTH = 512
ROPE_THETA = 500000.0
ROT_DIM = DIFF_DH // 4
D_IN_PROJ = 3584
N_EXPERTS = 32
TOP_K = 4
D_FF = 1024
SWIGLU_LIMIT = 7.0
SWIGLU_ALPHA = 1.702
LAMBDA_INIT = 0.8 - 0.6 * 1.0

LOG2_E = 1.4426950408889634
SCORE_BOUND_SLACK = 1.02
MAX_SAFE_SCORE_BOUND = 60.0
LANES = 128
SUBLANES = 8
VMEM_LIMIT = 56 * 1024 * 1024

COL_RQ, COL_RK, COL_RV, COL_RG, COL_DQ, COL_DK, COL_DV = 0, 4, 8, 12, 16, 20, 24
VGV_RV, VGV_RG, VGV_DV = 0, 4, 8

TM_PROJ = 512
RET_CHUNK = 128
RET_UNROLL = 32
TQ_ATTN = 1024
TK_ATTN = 4096
TK_ATTN_ONLINE = 2048
TM_ROUTE = 512
TILE = 256
MOE_BLOCK = 512
MOE_HALF = MOE_BLOCK // 2
RUN_ALIGN = SUBLANES
TILE_ROWS = TOP_K * TILE + N_EXPERTS * RUN_ALIGN
DISPATCH_BUFS = 2


def _params(sem, **kw):
    return pltpu.CompilerParams(dimension_semantics=sem, vmem_limit_bytes=VMEM_LIMIT, **kw)


def _in_proj_kernel(x_ref, nw_ref, w_ref, ret_cs_ref, diff_cs_ref, qw_ref, kw_ref,
                    vgv_ref, rqo_ref, rko_ref, qs_ref, ks_ref):
    ts = x_ref.shape[0]
    x = x_ref[...]
    hn = (x * lax.rsqrt(jnp.mean(x * x, axis=-1, keepdims=True) + EPS) * nw_ref[...]).astype(jnp.bfloat16)

    def proj(col_block):
        c0 = col_block * LANES
        return jnp.dot(hn, w_ref[:, c0:c0 + 4 * LANES], preferred_element_type=jnp.float32)

    dq, dk, rq, rk = proj(COL_DQ), proj(COL_DK), proj(COL_RQ), proj(COL_RK)
    lane = lax.broadcasted_iota(jnp.int32, (ts, LANES), 1)
    lo = lane < DIFF_DH
    ret_cs = ret_cs_ref[...]
    ret_sc = pltpu.roll(ret_cs, RET_DK // 2, 1)
    first_half = lane < RET_DK // 2
    c2 = jnp.where(first_half, ret_cs, ret_sc)
    s2 = jnp.where(first_half, -ret_sc, ret_cs)
    sub = lane % DIFF_DH
    sin_lanes = (sub >= ROT_DIM // 2) & (sub < ROT_DIM)
    cs_rows = jnp.concatenate([diff_cs_ref[...], jnp.ones((LANES - ROT_DIM, ts), jnp.float32)], axis=0)
    cs_first = cs_rows.T
    diff_cs = jnp.where((lane >= DIFF_DH) & (lane < DIFF_DH + ROT_DIM), pltpu.roll(cs_first, DIFF_DH, 1), cs_first)
    ra = jnp.where(sin_lanes, pltpu.roll(diff_cs, ROT_DIM // 2, 1), diff_cs)
    rp = jnp.where(sin_lanes, diff_cs, 0.0)
    rn = jnp.where(sub < ROT_DIM // 2, -pltpu.roll(diff_cs, LANES - ROT_DIM // 2, 1), 0.0)

    def qk_norm_rot(x, w):
        x2 = x * x
        s_lo = jnp.sum(jnp.where(lo, x2, 0.0), axis=-1, keepdims=True)
        s_hi = jnp.sum(jnp.where(lo, 0.0, x2), axis=-1, keepdims=True)
        ms = jnp.where(lo, s_lo, s_hi) * (1.0 / DIFF_DH)
        xn = x * lax.rsqrt(ms + EPS) * w
        return xn * ra + pltpu.roll(xn, ROT_DIM // 2, 1) * rp + pltpu.roll(xn, LANES - ROT_DIM // 2, 1) * rn

    for h in range(DIFF_HEADS):
        sl = slice(h * LANES, (h + 1) * LANES)
        q = qk_norm_rot(dq[:, sl], qw_ref[...]) * (DIFF_DH ** -0.5 * LOG2_E)
        k = qk_norm_rot(dk[:, sl], kw_ref[...])
        qs_ref[h, 0] = jnp.where(lo, q, 0.0).astype(qs_ref.dtype)
        qs_ref[h, 1] = jnp.where(lo, 0.0, q).astype(qs_ref.dtype)
        ks_ref[:, sl] = k.astype(ks_ref.dtype)
    for h in range(RET_HEADS):
        sl = slice(h * LANES, (h + 1) * LANES)
        q = rq[:, sl]
        k = rk[:, sl]
        rqo_ref[:, sl] = (q * c2 + pltpu.roll(q, RET_DK // 2, 1) * s2).astype(rqo_ref.dtype)
        rko_ref[:, sl] = ((k * c2 + pltpu.roll(k, RET_DK // 2, 1) * s2) * (RET_DK ** -0.5)).astype(rko_ref.dtype)
    for slot, col_block in enumerate((COL_RV, COL_RG, COL_DV)):
        vgv_ref[:, slot * 4 * LANES:(slot + 1) * 4 * LANES] = proj(col_block).astype(vgv_ref.dtype)


def _in_proj(x2, nw, w_bf16, tabs, qw2, kw2, B, S):
    T = B * S
    n_s = S // TM_PROJ
    tok = lambda w: pl.BlockSpec((TM_PROJ, w), lambda i: (i, 0))
    const = lambda s: pl.BlockSpec(s, lambda i: (0, 0))
    tab = pl.BlockSpec((None, TM_PROJ, LANES), lambda i: (i // n_s, i % n_s, 0))
    bf16 = jnp.bfloat16
    return pl.pallas_call(
        _in_proj_kernel,
        out_shape=(jax.ShapeDtypeStruct((T, 3 * 4 * LANES), bf16),
                   jax.ShapeDtypeStruct((T, 4 * LANES), bf16),
                   jax.ShapeDtypeStruct((T, 4 * LANES), bf16),
                   jax.ShapeDtypeStruct((B, DIFF_HEADS, 2, S, LANES), bf16),
                   jax.ShapeDtypeStruct((T, 4 * LANES), bf16)),
        grid=(T // TM_PROJ,),
        in_specs=[tok(D_MODEL), const((1, D_MODEL)), const((D_MODEL, D_IN_PROJ)),
                  tab, pl.BlockSpec((None, ROT_DIM, TM_PROJ), lambda i: (i // n_s, 0, i % n_s)),
                  const((1, LANES)), const((1, LANES))],
        out_specs=(tok(3 * 4 * LANES), tok(4 * LANES), tok(4 * LANES),
                   pl.BlockSpec((None, DIFF_HEADS, 2, TM_PROJ, LANES), lambda i: (i // n_s, 0, 0, i % n_s, 0)),
                   tok(4 * LANES)),
        compiler_params=_params(("arbitrary",)),
        name="in_proj",
    )(x2, nw, w_bf16, *tabs, qw2, kw2)


def _retention_kernel(ldf_ref, ldb_ref, q_ref, k_ref, v_ref, g_ref, nw_ref, o_ref, sb_ref):
    C = RET_CHUNK
    S = q_ref.shape[0]
    n_chunks = S // C
    h = pl.program_id(1)
    ldf = ldf_ref[h]
    ldb = ldb_ref[h]
    row = lax.broadcasted_iota(jnp.int32, (C, C), 0).astype(jnp.float32)
    colm = lax.broadcasted_iota(jnp.int32, (C, C), 1).astype(jnp.float32)
    dist = row - colm
    decay = jnp.where(dist >= 0, jnp.exp(ldf * jnp.maximum(dist, 0.0)), jnp.exp(ldb * jnp.maximum(-dist, 0.0)))
    idx = lax.broadcasted_iota(jnp.int32, (C, 1), 0).astype(jnp.float32)
    q_dec_f = jnp.exp(ldf * (idx + 1.0))
    k_dec_f = jnp.exp(ldf * (C - 1.0 - idx))
    q_dec_b = jnp.exp(ldb * (C - idx))
    k_dec_b = jnp.exp(ldb * idx)
    chunk_dec_f = jnp.exp(ldf * C)
    chunk_dec_b = jnp.exp(ldb * C)
    f32 = jnp.float32
    bf16 = jnp.bfloat16

    def kv_state(k, v, k_dec):
        kd = (k.astype(f32) * k_dec).astype(bf16)
        return lax.dot_general(kd, v, (((0,), (0,)), ((), ())), preferred_element_type=f32)

    def bwd_step(i, state):
        c = n_chunks - 1 - i
        r0 = pl.multiple_of(c * C, C)
        sb_ref[c] = state
        return state * chunk_dec_b + kv_state(k_ref[pl.ds(r0, C), :], v_ref[pl.ds(r0, C), :], k_dec_b)

    lax.fori_loop(0, n_chunks, bwd_step, jnp.zeros((RET_DK, LANES), f32), unroll=RET_UNROLL)

    def fwd_step(c, state):
        r0 = pl.multiple_of(c * C, C)
        q = q_ref[pl.ds(r0, C), :]
        k = k_ref[pl.ds(r0, C), :]
        v = v_ref[pl.ds(r0, C), :]
        scores = lax.dot_general(q, k, (((1,), (1,)), ((), ())), preferred_element_type=f32) * decay
        y = jnp.dot(scores.astype(bf16), v, preferred_element_type=f32)
        qf = q.astype(f32)
        y += jnp.dot((qf * q_dec_f).astype(bf16), state.astype(bf16), preferred_element_type=f32)
        y += jnp.dot((qf * q_dec_b).astype(bf16), sb_ref[c].astype(bf16), preferred_element_type=f32)
        yn = y * lax.rsqrt(jnp.mean(y * y, axis=-1, keepdims=True) + EPS) * nw_ref[...]
        g = g_ref[pl.ds(r0, C), :].astype(f32)
        o_ref[pl.ds(r0, C), :] = (yn * (g * jax.nn.sigmoid(g))).astype(o_ref.dtype)
        return state * chunk_dec_f + kv_state(k, v, k_dec_f)

    lax.fori_loop(0, n_chunks, fwd_step, jnp.zeros((RET_DK, LANES), f32), unroll=RET_UNROLL)


def _retention(ldf, ldb, rq_r, rk_r, proj, nw, B, S):
    T = B * S
    smem = pl.BlockSpec(memory_space=pltpu.SMEM)
    seq = lambda cb: pl.BlockSpec((S, LANES), lambda b, h: (b, cb + h))
    return pl.pallas_call(
        _retention_kernel,
        out_shape=jax.ShapeDtypeStruct((T, RET_WIDTH), jnp.bfloat16),
        grid=(B, RET_HEADS),
        in_specs=[smem, smem, seq(0), seq(0), seq(VGV_RV), seq(VGV_RG),
                  pl.BlockSpec((1, LANES), lambda b, h: (0, h))],
        out_specs=seq(0),
        scratch_shapes=[pltpu.VMEM((S // RET_CHUNK, RET_DK, LANES), jnp.float32)],
        compiler_params=_params(("arbitrary", "arbitrary")),
        name="retention",
    )(ldf, ldb, rq_r, rk_r, proj, proj, nw)


def _diff_attn_kernel(online_max, bound_ref, q_ref, k_ref, v_ref, lam_ref, nw_ref, o_ref, m_ref, l_ref, acc_ref):
    tq = q_ref.shape[1]
    S = k_ref.shape[0]
    f32 = jnp.float32
    q = q_ref[...].reshape(2 * tq, LANES)
    if online_max:
        m_ref[...] = jnp.full(m_ref.shape, -jnp.inf, f32)
    l_ref[...] = jnp.zeros(l_ref.shape, f32)
    acc_ref[...] = jnp.zeros(acc_ref.shape, f32)
    tk = TK_ATTN_ONLINE if online_max else TK_ATTN
    n_tiles = tk // LANES

    def kv_step(j, carry):
        r0 = pl.multiple_of(j * tk, tk)
        k = k_ref[pl.ds(r0, tk), :]
        v = v_ref[pl.ds(r0, tk), :]
        s = lax.dot_general(q, k, (((1,), (1,)), ((), ())), preferred_element_type=f32)
        tiles = [s[:, c * LANES:(c + 1) * LANES] for c in range(n_tiles)]
        if online_max:
            part = tiles[0]
            for t in tiles[1:]:
                part = jnp.maximum(part, t)
            m_prev = m_ref[...]
            shift = jnp.maximum(m_prev, jnp.max(part, axis=-1, keepdims=True))
            alpha = jnp.exp2(m_prev - shift)
            m_ref[...] = shift
        else:
            shift = bound_ref[0]
        probs = [jnp.exp2(t - shift) for t in tiles]
        psum = probs[0]
        for p in probs[1:]:
            psum = psum + p
        pv = jnp.dot(jnp.concatenate([p.astype(jnp.bfloat16) for p in probs], axis=1), v,
                     preferred_element_type=f32)
        if online_max:
            l_ref[...] = alpha * l_ref[...] + psum
            acc_ref[...] = alpha * acc_ref[...] + pv
        else:
            l_ref[...] = l_ref[...] + psum
            acc_ref[...] = acc_ref[...] + pv
        return carry

    lax.fori_loop(0, S // tk, kv_step, 0)
    o = acc_ref[...] / jnp.sum(l_ref[...], axis=-1, keepdims=True)
    d = o[:tq] - lam_ref[...] * o[tq:]
    dn = d * lax.rsqrt(jnp.mean(d * d, axis=-1, keepdims=True) + EPS) * nw_ref[...]
    o_ref[...] = (dn * (1.0 - LAMBDA_INIT)).astype(o_ref.dtype)


def _diff_attn(online_max, bound, qs, ks, proj, lam, nw, B, S):
    T = B * S
    n_q = S // TQ_ATTN
    one = pl.BlockSpec((1, LANES), lambda b, h, i, bd: (0, 0))
    grid_spec = pltpu.PrefetchScalarGridSpec(
        num_scalar_prefetch=1,
        grid=(B, DIFF_HEADS, n_q),
        in_specs=[pl.BlockSpec((None, None, 2, TQ_ATTN, LANES), lambda b, h, i, bd: (b, h, 0, i, 0)),
                  pl.BlockSpec((S, LANES), lambda b, h, i, bd: (b, h)),
                  pl.BlockSpec((S, LANES), lambda b, h, i, bd: (b, VGV_DV + h)),
                  one, one],
        out_specs=pl.BlockSpec((TQ_ATTN, LANES), lambda b, h, i, bd: (b * n_q + i, h)),
        scratch_shapes=[pltpu.VMEM((2 * TQ_ATTN, LANES), jnp.float32)] * 3,
    )
    return pl.pallas_call(
        functools.partial(_diff_attn_kernel, online_max),
        out_shape=jax.ShapeDtypeStruct((T, DIFF_WIDTH), jnp.bfloat16),
        grid_spec=grid_spec,
        compiler_params=_params(("arbitrary", "arbitrary", "arbitrary")),
        name="diff_attn_online" if online_max else "diff_attn",
    )(bound, qs, ks, proj, lam, nw)


def _out_router_kernel(x_ref, yr_ref, yd_ref, wo_ref, n2_ref, wrt_ref, br_ref,
                       x1_ref, h2_ref, pos_ref, gate_t_ref, len_ref, off_ref, tot_ref):
    tm = x_ref.shape[0]
    f32 = jnp.float32
    bf16 = jnp.bfloat16

    @pl.when(pl.program_id(0) == 0)
    def _():
        tot_ref[...] = jnp.zeros(tot_ref.shape, f32)

    att = jnp.dot(yr_ref[...], wo_ref[:RET_WIDTH, :], preferred_element_type=f32)
    att += jnp.dot(yd_ref[...], wo_ref[RET_WIDTH:, :], preferred_element_type=f32)
    x1 = x_ref[...] + att
    x1_ref[...] = x1
    h2 = x1 * lax.rsqrt(jnp.mean(x1 * x1, axis=-1, keepdims=True) + EPS) * n2_ref[...]
    h2_ref[...] = h2.astype(h2_ref.dtype)
    nt = (((1,), (1,)), ((), ()))
    h_hi = h2.astype(bf16)
    h_lo = (h2 - h_hi.astype(f32)).astype(bf16)
    w = wrt_ref[...]
    w_hi = w.astype(bf16)
    w_lo = (w - w_hi.astype(f32)).astype(bf16)
    with_h_hi = lax.dot_general(jnp.concatenate([w_hi, w_lo], axis=0), h_hi, nt, preferred_element_type=f32)
    logits = (with_h_hi[:N_EXPERTS] + with_h_hi[N_EXPERTS:]
              + lax.dot_general(w_hi, h_lo, nt, preferred_element_type=f32)) + br_ref[...]
    e_iota = lax.broadcasted_iota(jnp.int32, (N_EXPERTS, tm), 0)
    work = logits
    vals, hots = [], []
    for _ in range(TOP_K):
        mx = jnp.max(work, axis=0, keepdims=True)
        ix = jnp.min(jnp.where(work == mx, e_iota, N_EXPERTS), axis=0, keepdims=True)
        hot = e_iota == ix
        vals.append(mx)
        hots.append(hot)
        work = jnp.where(hot, -jnp.inf, work)
    exps = [jnp.exp(v - vals[0]) for v in vals]
    denom = exps[0] + exps[1] + exps[2] + exps[3]
    gates = [e / denom for e in exps]
    masks = [jnp.where(hot, 1.0, 0.0) for hot in hots]
    sel = masks[0] + masks[1] + masks[2] + masks[3]
    t_row = lax.broadcasted_iota(jnp.int32, (TILE, TILE), 0)
    t_col = lax.broadcasted_iota(jnp.int32, (TILE, TILE), 1)
    upper = jnp.where(t_row < t_col, 1.0, 0.0).astype(bf16)
    e_row = lax.broadcasted_iota(jnp.int32, (N_EXPERTS, N_EXPERTS), 0)
    e_col = lax.broadcasted_iota(jnp.int32, (N_EXPERTS, N_EXPERTS), 1)
    lower = jnp.where(e_col < e_row, 1.0, 0.0).astype(bf16)
    pos_parts = []
    for t in range(tm // TILE):
        cols = slice(t * TILE, (t + 1) * TILE)
        sel_t = sel[:, cols]
        rank = jnp.dot(sel_t.astype(bf16), upper, preferred_element_type=f32)
        cnt = jnp.sum(sel_t, axis=1, keepdims=True)
        run_units = jnp.floor((cnt + (RUN_ALIGN - 1.0)) * (1.0 / RUN_ALIGN))
        run_len = jnp.broadcast_to(run_units * RUN_ALIGN, (N_EXPERTS, LANES))
        run_start = jnp.dot(lower, jnp.broadcast_to(run_units, (N_EXPERTS, LANES)).astype(bf16),
                            preferred_element_type=f32) * RUN_ALIGN
        pos_full = rank + run_start[:, 0:1]
        pos_parts.append([jnp.sum(mask[:, cols] * pos_full, axis=0, keepdims=True) for mask in masks])
        len_ref[t] = run_len
        off_ref[t] = tot_ref[...]
        tot_ref[...] = tot_ref[...] + run_len
    pos = [jnp.concatenate([part[k] for part in pos_parts], axis=1) for k in range(TOP_K)]
    for k in range(TOP_K):
        pos_ref[k:k + 1, :] = pos[k].astype(jnp.int32)
    rows = jnp.concatenate(gates + pos + [jnp.zeros((LANES - 2 * TOP_K, tm), f32)], axis=0)
    gate_t_ref[...] = rows.T


def _out_router(x2, y_ret, y_diff, wo_bf16, n2w, wrt, br):
    T = x2.shape[0]
    n_tiles = T // TILE
    tiles_per_step = TM_ROUTE // TILE
    tok = lambda w: pl.BlockSpec((TM_ROUTE, w), lambda i: (i, 0))
    const = lambda s: pl.BlockSpec(s, lambda i: (0, 0))
    per_tile = pl.BlockSpec((tiles_per_step, N_EXPERTS, LANES), lambda i: (i, 0, 0))
    return pl.pallas_call(
        _out_router_kernel,
        out_shape=(jax.ShapeDtypeStruct((T, D_MODEL), jnp.float32),
                   jax.ShapeDtypeStruct((T, D_MODEL), jnp.bfloat16),
                   jax.ShapeDtypeStruct((TOP_K, T), jnp.int32),
                   jax.ShapeDtypeStruct((T, LANES), jnp.float32),
                   jax.ShapeDtypeStruct((n_tiles, N_EXPERTS, LANES), jnp.float32),
                   jax.ShapeDtypeStruct((n_tiles, N_EXPERTS, LANES), jnp.float32),
                   jax.ShapeDtypeStruct((N_EXPERTS, LANES), jnp.float32)),
        grid=(T // TM_ROUTE,),
        in_specs=[tok(D_MODEL), tok(RET_WIDTH), tok(DIFF_WIDTH), const((D_MODEL, D_MODEL)),
                  const((1, D_MODEL)), const((N_EXPERTS, D_MODEL)), const((N_EXPERTS, 1))],
        out_specs=(tok(D_MODEL), tok(D_MODEL), pl.BlockSpec((TOP_K, TM_ROUTE), lambda i: (0, i)), tok(LANES),
                   per_tile, per_tile, const((N_EXPERTS, LANES))),
        compiler_params=_params(("arbitrary",)),
        name="out_router",
    )(x2, y_ret, y_diff, wo_bf16, n2w, wrt, br)


def _pack_bf16_pairs(x, is_bf16_valued=False):
    w = x.shape[1] // 2
    if not is_bf16_valued:
        x = x.astype(jnp.bfloat16).astype(jnp.float32)
    bits = lax.bitcast_convert_type(x, jnp.uint32)
    return (bits[:, :w] >> 16) | bits[:, w:]


def _unpack_bf16_pairs(p):
    as_bf16 = lambda bits: lax.bitcast_convert_type(bits, jnp.float32).astype(jnp.bfloat16)
    return as_bf16(p << 16), as_bf16(p & jnp.uint32(0xFFFF0000))


def _run_copies(src_ref, len_ref, dst_ref, tile, make_copy):
    for e in range(N_EXPERTS):
        n = pl.multiple_of(len_ref[tile * N_EXPERTS + e], RUN_ALIGN)
        s = pl.multiple_of(src_ref[tile * N_EXPERTS + e], RUN_ALIGN)
        d = pl.multiple_of(dst_ref[tile * N_EXPERTS + e], RUN_ALIGN)

        @pl.when(n > 0)
        def _():
            make_copy(s, d, n).start()


def _dispatch_kernel(src_ref, len_ref, dst_ref, rows_ref, zlo_ref, zlen_ref, nu_ref,
                     pos_ref, h2_ref, xs_hbm, xbuf_ref, zero_ref, sems, zero_sem):
    i = pl.program_id(0)
    n_tiles = pl.num_programs(0)
    n_buf = xbuf_ref.shape[0]
    cur = i % n_buf
    n_rows, tm = xbuf_ref.shape[1], h2_ref.shape[0]

    @pl.when(i == 0)
    def _():
        zero_ref[...] = jnp.zeros(zero_ref.shape, zero_ref.dtype)

        def pad_copy(e):
            n = pl.multiple_of(zlen_ref[e], RUN_ALIGN)
            lo = pl.multiple_of(zlo_ref[e], RUN_ALIGN)
            return pltpu.make_async_copy(zero_ref.at[pl.ds(0, n)], xs_hbm.at[pl.ds(lo, n)], zero_sem)

        def tail_copy(j):
            return pltpu.make_async_copy(zero_ref, xs_hbm.at[pl.ds(j * MOE_HALF, MOE_HALF)], zero_sem)

        def guarded(copy, op):
            def body(e, c):
                @pl.when(zlen_ref[e] > 0)
                def _():
                    op(copy(e))
                return c
            return body

        lax.fori_loop(0, N_EXPERTS, guarded(pad_copy, lambda cp: cp.start()), 0)
        lax.fori_loop(0, N_EXPERTS, guarded(pad_copy, lambda cp: cp.wait()), 0)
        n_halves = xs_hbm.shape[0] // MOE_HALF
        lax.fori_loop(nu_ref[0], n_halves, lambda j, c: (tail_copy(j).start(), c)[1], 0)
        lax.fori_loop(nu_ref[0], n_halves, lambda j, c: (tail_copy(j).wait(), c)[1], 0)

    p_iota = lax.broadcasted_iota(jnp.int32, (n_rows, tm), 0)
    onehot = jnp.zeros((n_rows, tm), jnp.float32)
    for k in range(TOP_K):
        onehot = jnp.where(p_iota == pos_ref[k:k + 1, :], 1.0, onehot)
    xbuf_ref[cur] = _pack_bf16_pairs(
        jnp.dot(onehot.astype(jnp.bfloat16), h2_ref[...], preferred_element_type=jnp.float32), is_bf16_valued=True)

    _run_copies(src_ref, len_ref, dst_ref, i,
                lambda s, d, n: pltpu.make_async_copy(xbuf_ref.at[cur, pl.ds(s, n)], xs_hbm.at[pl.ds(d, n)],
                                                      sems.at[cur]))

    def wait_tile(tile, slot):
        rows = pl.multiple_of(rows_ref[tile], RUN_ALIGN)
        pltpu.make_async_copy(xbuf_ref.at[slot, pl.ds(0, rows)], xs_hbm.at[pl.ds(0, rows)], sems.at[slot]).wait()

    oldest = n_buf - 1

    @pl.when(i >= oldest)
    def _():
        wait_tile(i - oldest, (i + 1) % n_buf)

    @pl.when(i == n_tiles - 1)
    def _():
        for back in range(oldest - 1, -1, -1):
            wait_tile(i - back, (i - back) % n_buf)


def _dispatch(run_src, run_len, run_dst, tile_rows, zero_lo, zero_len, n_used, pos, h2, P):
    T = h2.shape[0]
    n_pre = 7
    grid_spec = pltpu.PrefetchScalarGridSpec(
        num_scalar_prefetch=n_pre,
        grid=(T // TILE,),
        in_specs=[pl.BlockSpec((TOP_K, TILE), lambda i, *_: (0, i)),
                  pl.BlockSpec((TILE, D_MODEL), lambda i, *_: (i, 0))],
        out_specs=pl.BlockSpec(memory_space=pl.ANY),
        scratch_shapes=[pltpu.VMEM((DISPATCH_BUFS, TILE_ROWS, D_MODEL // 2), jnp.uint32),
                        pltpu.VMEM((MOE_HALF, D_MODEL // 2), jnp.uint32),
                        pltpu.SemaphoreType.DMA((DISPATCH_BUFS,)),
                        pltpu.SemaphoreType.DMA(())],
    )
    return pl.pallas_call(
        _dispatch_kernel,
        out_shape=jax.ShapeDtypeStruct((P, D_MODEL // 2), jnp.uint32),
        grid_spec=grid_spec,
        compiler_params=_params(("arbitrary",), has_side_effects=True),
        name="dispatch",
    )(run_src, run_len, run_dst, tile_rows, zero_lo, zero_len, n_used, pos, h2)


def _experts_kernel(base_ref, nblk_ref, half_ref, w1_ref, b1_ref, w2_ref, b2_ref, xs_hbm, ys_hbm,
                    w1b_ref, w2b_ref, xbuf_ref, ybuf_ref, xhalf_ref, yhalf_ref, in_sems, out_sems, busy_ref):
    e = pl.program_id(0)
    n = nblk_ref[e]
    has_half = half_ref[e] == 1
    HALF_BUF = 2

    def rows(expert, j):
        return pl.ds(pl.multiple_of(base_ref[expert] + j * MOE_BLOCK, MOE_HALF), MOE_BLOCK)

    def half_rows(first_row):
        return pl.ds(pl.multiple_of(first_row, MOE_HALF), MOE_HALF)

    def in_copy(expert, j, slot):
        return pltpu.make_async_copy(xs_hbm.at[rows(expert, j)], xbuf_ref.at[slot], in_sems.at[slot])

    def out_copy(j, slot):
        return pltpu.make_async_copy(ybuf_ref.at[slot], ys_hbm.at[rows(e, j)], out_sems.at[slot])

    half_row0 = base_ref[e] + n * MOE_BLOCK
    half_in = pltpu.make_async_copy(xs_hbm.at[half_rows(half_row0)], xhalf_ref, in_sems.at[HALF_BUF])

    def half_out(first_row):
        return pltpu.make_async_copy(yhalf_ref, ys_hbm.at[half_rows(first_row)], out_sems.at[HALF_BUF])

    def wait_out(buf, half=False):
        @pl.when(busy_ref[buf] == 1)
        def _():
            (half_out(0) if half else out_copy(0, buf)).wait()
            busy_ref[buf] = 0

    def mlp(x_packed):
        x = jnp.concatenate(_unpack_bf16_pairs(x_packed), axis=1)
        u = jnp.dot(x, w1b_ref[...], preferred_element_type=jnp.float32) + b1_ref[...]
        glu = jnp.minimum(u[:, :D_FF], SWIGLU_LIMIT)
        lin = jnp.clip(u[:, D_FF:], -SWIGLU_LIMIT, SWIGLU_LIMIT)
        act = glu * jax.nn.sigmoid(SWIGLU_ALPHA * glu) * (lin + 1.0)
        return _pack_bf16_pairs(jnp.dot(act.astype(jnp.bfloat16), w2b_ref[...],
                                        preferred_element_type=jnp.float32) + b2_ref[...])

    @pl.when(e == 0)
    def _():
        for buf in range(3):
            busy_ref[buf] = 0

        @pl.when(n > 0)
        def _():
            in_copy(0, 0, 0).start()

    @pl.when(has_half)
    def _():
        half_in.start()

    @pl.when((n > 0) | has_half)
    def _():
        w1b_ref[...] = w1_ref[...].astype(jnp.bfloat16)
        w2b_ref[...] = w2_ref[...].astype(jnp.bfloat16)

    def block(j, carry):
        slot = j % 2

        @pl.when(j + 1 < n)
        def _():
            in_copy(e, j + 1, 1 - slot).start()

        in_copy(e, j, slot).wait()
        wait_out(slot)
        ybuf_ref[slot] = mlp(xbuf_ref[slot])
        out_copy(j, slot).start()
        busy_ref[slot] = 1
        return carry

    lax.fori_loop(0, n, block, 0)

    @pl.when(has_half)
    def _():
        half_in.wait()
        wait_out(HALF_BUF, half=True)
        yhalf_ref[...] = mlp(xhalf_ref[...])
        half_out(half_row0).start()
        busy_ref[HALF_BUF] = 1

    e_next = jnp.minimum(e + 1, N_EXPERTS - 1)

    @pl.when((e + 1 < N_EXPERTS) & (nblk_ref[e_next] > 0))
    def _():
        in_copy(e_next, 0, 0).start()

    @pl.when(e == N_EXPERTS - 1)
    def _():
        wait_out(0)
        wait_out(1)
        wait_out(HALF_BUF, half=True)
        yhalf_ref[...] = jnp.zeros(yhalf_ref.shape, yhalf_ref.dtype)
        first_unused = (half_row0 + half_ref[e] * MOE_HALF) // MOE_HALF
        n_halves = ys_hbm.shape[0] // MOE_HALF
        lax.fori_loop(first_unused, n_halves, lambda j, c: (half_out(j * MOE_HALF).start(), c)[1], 0)
        lax.fori_loop(first_unused, n_halves, lambda j, c: (half_out(j * MOE_HALF).wait(), c)[1], 0)


def _experts(base, n_blk, n_half, xs, w1, b1, w2, b2):
    P = xs.shape[0]
    expert = lambda e, bs, nb, nh: (e, 0, 0)
    grid_spec = pltpu.PrefetchScalarGridSpec(
        num_scalar_prefetch=3,
        grid=(N_EXPERTS,),
        in_specs=[pl.BlockSpec((None, D_MODEL, 2 * D_FF), expert),
                  pl.BlockSpec((None, 1, 2 * D_FF), expert),
                  pl.BlockSpec((None, D_FF, D_MODEL), expert),
                  pl.BlockSpec((None, 1, D_MODEL), expert),
                  pl.BlockSpec(memory_space=pl.ANY)],
        out_specs=pl.BlockSpec(memory_space=pl.ANY),
        scratch_shapes=[pltpu.VMEM((D_MODEL, 2 * D_FF), jnp.bfloat16),
                        pltpu.VMEM((D_FF, D_MODEL), jnp.bfloat16),
                        pltpu.VMEM((2, MOE_BLOCK, D_MODEL // 2), jnp.uint32),
                        pltpu.VMEM((2, MOE_BLOCK, D_MODEL // 2), jnp.uint32),
                        pltpu.VMEM((MOE_HALF, D_MODEL // 2), jnp.uint32),
                        pltpu.VMEM((MOE_HALF, D_MODEL // 2), jnp.uint32),
                        pltpu.SemaphoreType.DMA((3,)),
                        pltpu.SemaphoreType.DMA((3,)),
                        pltpu.SMEM((3,), jnp.int32)],
    )
    return pl.pallas_call(
        _experts_kernel,
        out_shape=jax.ShapeDtypeStruct((P, D_MODEL // 2), jnp.uint32),
        grid_spec=grid_spec,
        compiler_params=_params(("arbitrary",)),
        name="experts",
    )(base, n_blk, n_half, w1, b1, w2, b2, xs)


def _combine_kernel(src_ref, len_ref, dst_ref, rows_ref, gate_t_ref, x1_ref, ys_hbm, o_ref, ybuf_ref, sems):
    i = pl.program_id(0)
    n_tiles = pl.num_programs(0)
    n_buf = ybuf_ref.shape[0]
    cur = i % n_buf
    n_rows, tm = ybuf_ref.shape[1], x1_ref.shape[0]

    def fetch(tile, slot):
        _run_copies(src_ref, len_ref, dst_ref, tile,
                    lambda s, d, n: pltpu.make_async_copy(ys_hbm.at[pl.ds(d, n)], ybuf_ref.at[slot, pl.ds(s, n)],
                                                          sems.at[slot]))

    @pl.when(i == 0)
    def _():
        ybuf_ref[...] = jnp.zeros(ybuf_ref.shape, ybuf_ref.dtype)
        for tile in range(n_buf - 1):
            fetch(tile, tile)

    ahead = i + n_buf - 1

    @pl.when(ahead < n_tiles)
    def _():
        fetch(ahead, ahead % n_buf)

    g = gate_t_ref[...]
    p_iota = lax.broadcasted_iota(jnp.int32, (tm, n_rows), 1)
    weights = jnp.zeros((tm, n_rows), jnp.float32)
    for k in range(TOP_K):
        pos_k = g[:, TOP_K + k:TOP_K + k + 1].astype(jnp.int32)
        weights = jnp.where(p_iota == pos_k, g[:, k:k + 1], weights)
    weights = weights.astype(jnp.bfloat16)

    rows = pl.multiple_of(rows_ref[i], RUN_ALIGN)
    pltpu.make_async_copy(ys_hbm.at[pl.ds(0, rows)], ybuf_ref.at[cur, pl.ds(0, rows)], sems.at[cur]).wait()
    halves = [jnp.dot(weights, y, preferred_element_type=jnp.float32) for y in _unpack_bf16_pairs(ybuf_ref[cur])]
    o_ref[...] = x1_ref[...] + jnp.concatenate(halves, axis=1)


def _combine(run_src, run_len, run_dst, tile_rows, gate_t, x1, ys):
    T = x1.shape[0]
    tok = lambda w: pl.BlockSpec((TILE, w), lambda i, *_: (i, 0))
    grid_spec = pltpu.PrefetchScalarGridSpec(
        num_scalar_prefetch=4,
        grid=(T // TILE,),
        in_specs=[tok(LANES), tok(D_MODEL), pl.BlockSpec(memory_space=pl.ANY)],
        out_specs=tok(D_MODEL),
        scratch_shapes=[pltpu.VMEM((DISPATCH_BUFS, TILE_ROWS, D_MODEL // 2), jnp.uint32),
                        pltpu.SemaphoreType.DMA((DISPATCH_BUFS,))],
    )
    return pl.pallas_call(
        _combine_kernel,
        out_shape=jax.ShapeDtypeStruct((T, D_MODEL), jnp.float32),
        grid_spec=grid_spec,
        compiler_params=_params(("arbitrary",)),
        name="combine",
    )(run_src, run_len, run_dst, tile_rows, gate_t, x1, ys)


def _rotary_tables(positions):
    pos = positions.astype(jnp.float32)[..., None]
    lane = jnp.arange(LANES)
    half_r = RET_DK // 2
    inv_r = RET_ROPE_THETA ** (-jnp.linspace(0.0, 1.0, half_r, dtype=jnp.float32))
    ret_cs = jnp.cos(pos * inv_r[lane % half_r] - jnp.where(lane < half_r, 0.0, 0.5 * jnp.pi))
    half_d = ROT_DIM // 2
    inv_d = ROPE_THETA ** (-jnp.arange(0, ROT_DIM, 2, dtype=jnp.float32) / ROT_DIM)
    row = jnp.arange(ROT_DIM)
    ang_d = positions.astype(jnp.float32)[:, None, :] * inv_d[row % half_d][None, :, None]
    diff_cs = jnp.cos(ang_d - jnp.where(row >= half_d, 0.5 * jnp.pi, 0.0)[None, :, None])
    return ret_cs, diff_cs


def kernel(x, positions, norm1_w, w_in, ret_log_decay_fwd, ret_log_decay_bwd, ret_norm_w, q_norm_w, k_norm_w, lambda_q1, lambda_k1, lambda_q2, lambda_k2, diff_norm_w, w_out, norm2_w, w_router, b_router, w1, b1, w2, b2):
    B, S, D = x.shape
    T = B * S
    f32 = jnp.float32
    bf16 = jnp.bfloat16
    x2 = x.reshape(T, D)

    dup = lambda w: jnp.concatenate([w, w]).reshape(1, LANES).astype(f32)
    proj, rq_r, rk_r, qs, ks = _in_proj(x2, norm1_w[0].reshape(1, D), w_in[0].astype(bf16),
                                        _rotary_tables(positions), dup(q_norm_w[0]), dup(k_norm_w[0]), B, S)

    y_ret = _retention(ret_log_decay_fwd[0].astype(f32), ret_log_decay_bwd[0].astype(f32),
                       rq_r, rk_r, proj, ret_norm_w[0].reshape(1, RET_WIDTH).astype(f32), B, S)

    lam = (jnp.exp(jnp.sum(lambda_q1[0].astype(f32) * lambda_k1[0].astype(f32)))
           - jnp.exp(jnp.sum(lambda_q2[0].astype(f32) * lambda_k2[0].astype(f32))) + LAMBDA_INIT)
    lam_row = jnp.full((1, LANES), lam, f32)
    bound = (SCORE_BOUND_SLACK * DIFF_DH ** 0.5 * LOG2_E
             * jnp.max(jnp.abs(q_norm_w[0].astype(f32))) * jnp.max(jnp.abs(k_norm_w[0].astype(f32)))).reshape(1)
    attn_args = (bound, qs, ks, proj, lam_row, diff_norm_w[0].reshape(1, DIFF_DV).astype(f32), B, S)
    y_diff = lax.cond(bound[0] <= MAX_SAFE_SCORE_BOUND,
                      lambda: _diff_attn(False, *attn_args), lambda: _diff_attn(True, *attn_args))

    x1, h2, pos, gate_t, len_t, off_t, tot_t = _out_router(
        x2, y_ret, y_diff, w_out[0].astype(bf16), norm2_w[0].reshape(1, D),
        w_router[0].T.astype(f32), b_router[0].reshape(N_EXPERTS, 1).astype(f32))

    n_tiles = T // TILE
    run_len = len_t[:, :, 0].astype(jnp.int32)
    total = tot_t[:, 0].astype(jnp.int32)
    padded = ((total + MOE_HALF - 1) // MOE_HALF) * MOE_HALF
    pad_end = jnp.cumsum(padded)
    pad_start = pad_end - padded
    run_dst = pad_start[None, :] + off_t[:, :, 0].astype(jnp.int32)
    run_src = jnp.cumsum(run_len, axis=1) - run_len
    tile_rows = jnp.sum(run_len, axis=1)
    P = T * TOP_K + n_tiles * N_EXPERTS * RUN_ALIGN + N_EXPERTS * MOE_HALF
    n_used = (pad_end[-1:] // MOE_HALF).astype(jnp.int32)
    runs = (run_src.reshape(-1), run_len.reshape(-1), run_dst.reshape(-1), tile_rows)

    xs = _dispatch(*runs, pad_start + total, padded - total, n_used, pos, h2, P)
    ys = _experts(pad_start, padded // MOE_BLOCK, (padded // MOE_HALF) % 2, xs, w1[0],
                  b1[0].reshape(N_EXPERTS, 1, 2 * D_FF), w2[0], b2[0].reshape(N_EXPERTS, 1, D))
    out = _combine(*runs, gate_t, x1, ys)
    return out.reshape(B, S, D)
```

```python
import functools

import jax
import jax.numpy as jnp
from jax import lax
from jax.experimental import pallas as pl
from jax.experimental.pallas import tpu as pltpu

EPS = 1e-6
D_MODEL = 1024
RET_HEADS = 4
RET_DK = 128
RET_WIDTH = 512
RET_ROPE_THETA = 10000.0
DIFF_HEADS = 4
DIFF_DH = 64
DIFF_DV = 128
DIFF_WIDTH = 512
ROPE_THETA = 500000.0
ROT_DIM = DIFF_DH // 4
D_IN_PROJ = 3584
N_EXPERTS = 32
TOP_K = 4
D_FF = 1024
SWIGLU_LIMIT = 7.0
SWIGLU_ALPHA = 1.702
LAMBDA_INIT = 0.8 - 0.6 * 1.0

LOG2_E = 1.4426950408889634
SCORE_BOUND_SLACK = 1.02
MAX_SAFE_SCORE_BOUND = 60.0
LANES = 128
SUBLANES = 8
VMEM_LIMIT = 56 * 1024 * 1024

COL_RQ, COL_RK, COL_RV, COL_RG, COL_DQ, COL_DK, COL_DV = 0, 4, 8, 12, 16, 20, 24
VGV_RV, VGV_RG, VGV_DV = 0, 4, 8

TM_PROJ = 512
RET_CHUNK = 128
RET_UNROLL = 32
TQ_ATTN = 1024
TK_ATTN = 4096
TK_ATTN_ONLINE = 2048
TM_ROUTE = 512
TILE = 256
MOE_BLOCK = 512
MOE_HALF = MOE_BLOCK // 2
RUN_ALIGN = SUBLANES
TILE_ROWS = TOP_K * TILE + N_EXPERTS * RUN_ALIGN
DISPATCH_BUFS = 2


def _params(sem, **kw):
    return pltpu.CompilerParams(dimension_semantics=sem, vmem_limit_bytes=VMEM_LIMIT, **kw)


def _in_proj_kernel(x_ref, nw_ref, w_ref, ret_cs_ref, diff_cs_ref, qw_ref, kw_ref,
                    vgv_ref, rqo_ref, rko_ref, qs_ref, ks_ref):
    ts = x_ref.shape[0]
    x = x_ref[...]
    hn = (x * lax.rsqrt(jnp.mean(x * x, axis=-1, keepdims=True) + EPS) * nw_ref[...]).astype(jnp.bfloat16)

    def proj(col_block):
        c0 = col_block * LANES
        return jnp.dot(hn, w_ref[:, c0:c0 + 4 * LANES], preferred_element_type=jnp.float32)

    dq, dk, rq, rk = proj(COL_DQ), proj(COL_DK), proj(COL_RQ), proj(COL_RK)
    lane = lax.broadcasted_iota(jnp.int32, (ts, LANES), 1)
    lo = lane < DIFF_DH
    ret_cs = ret_cs_ref[...]
    ret_sc = pltpu.roll(ret_cs, RET_DK // 2, 1)
    first_half = lane < RET_DK // 2
    c2 = jnp.where(first_half, ret_cs, ret_sc)
    s2 = jnp.where(first_half, -ret_sc, ret_cs)
    sub = lane % DIFF_DH
    sin_lanes = (sub >= ROT_DIM // 2) & (sub < ROT_DIM)
    cs_rows = jnp.concatenate([diff_cs_ref[...], jnp.ones((LANES - ROT_DIM, ts), jnp.float32)], axis=0)
    cs_first = cs_rows.T
    diff_cs = jnp.where((lane >= DIFF_DH) & (lane < DIFF_DH + ROT_DIM), pltpu.roll(cs_first, DIFF_DH, 1), cs_first)
    ra = jnp.where(sin_lanes, pltpu.roll(diff_cs, ROT_DIM // 2, 1), diff_cs)
    rp = jnp.where(sin_lanes, diff_cs, 0.0)
    rn = jnp.where(sub < ROT_DIM // 2, -pltpu.roll(diff_cs, LANES - ROT_DIM // 2, 1), 0.0)

    def qk_norm_rot(x, w):
        x2 = x * x
        s_lo = jnp.sum(jnp.where(lo, x2, 0.0), axis=-1, keepdims=True)
        s_hi = jnp.sum(jnp.where(lo, 0.0, x2), axis=-1, keepdims=True)
        ms = jnp.where(lo, s_lo, s_hi) * (1.0 / DIFF_DH)
        xn = x * lax.rsqrt(ms + EPS) * w
        return xn * ra + pltpu.roll(xn, ROT_DIM // 2, 1) * rp + pltpu.roll(xn, LANES - ROT_DIM // 2, 1) * rn

    for h in range(DIFF_HEADS):
        sl = slice(h * LANES, (h + 1) * LANES)
        q = qk_norm_rot(dq[:, sl], qw_ref[...]) * (DIFF_DH ** -0.5 * LOG2_E)
        k = qk_norm_rot(dk[:, sl], kw_ref[...])
        qs_ref[h, 0] = jnp.where(lo, q, 0.0).astype(qs_ref.dtype)
        qs_ref[h, 1] = jnp.where(lo, 0.0, q).astype(qs_ref.dtype)
        ks_ref[:, sl] = k.astype(ks_ref.dtype)
    for h in range(RET_HEADS):
        sl = slice(h * LANES, (h + 1) * LANES)
        q = rq[:, sl]
        k = rk[:, sl]
        rqo_ref[:, sl] = (q * c2 + pltpu.roll(q, RET_DK // 2, 1) * s2).astype(rqo_ref.dtype)
        rko_ref[:, sl] = ((k * c2 + pltpu.roll(k, RET_DK // 2, 1) * s2) * (RET_DK ** -0.5)).astype(rko_ref.dtype)
    for slot, col_block in enumerate((COL_RV, COL_RG, COL_DV)):
        vgv_ref[:, slot * 4 * LANES:(slot + 1) * 4 * LANES] = proj(col_block).astype(vgv_ref.dtype)


def _in_proj(x2, nw, w_bf16, tabs, qw2, kw2, B, S):
    T = B * S
    n_s = S // TM_PROJ
    tok = lambda w: pl.BlockSpec((TM_PROJ, w), lambda i: (i, 0))
    const = lambda s: pl.BlockSpec(s, lambda i: (0, 0))
    tab = pl.BlockSpec((None, TM_PROJ, LANES), lambda i: (i // n_s, i % n_s, 0))
    bf16 = jnp.bfloat16
    return pl.pallas_call(
        _in_proj_kernel,
        out_shape=(jax.ShapeDtypeStruct((T, 3 * 4 * LANES), bf16),
                   jax.ShapeDtypeStruct((T, 4 * LANES), bf16),
                   jax.ShapeDtypeStruct((T, 4 * LANES), bf16),
                   jax.ShapeDtypeStruct((B, DIFF_HEADS, 2, S, LANES), bf16),
                   jax.ShapeDtypeStruct((T, 4 * LANES), bf16)),
        grid=(T // TM_PROJ,),
        in_specs=[tok(D_MODEL), const((1, D_MODEL)), const((D_MODEL, D_IN_PROJ)),
                  tab, pl.BlockSpec((None, ROT_DIM, TM_PROJ), lambda i: (i // n_s, 0, i % n_s)),
                  const((1, LANES)), const((1, LANES))],
        out_specs=(tok(3 * 4 * LANES), tok(4 * LANES), tok(4 * LANES),
                   pl.BlockSpec((None, DIFF_HEADS, 2, TM_PROJ, LANES), lambda i: (i // n_s, 0, 0, i % n_s, 0)),
                   tok(4 * LANES)),
        compiler_params=_params(("arbitrary",)),
        name="in_proj",
    )(x2, nw, w_bf16, *tabs, qw2, kw2)


def _retention_kernel(ldf_ref, ldb_ref, q_ref, k_ref, v_ref, g_ref, nw_ref, o_ref, sb_ref):
    C = RET_CHUNK
    S = q_ref.shape[0]
    n_chunks = S // C
    h = pl.program_id(1)
    ldf = ldf_ref[h]
    ldb = ldb_ref[h]
    row = lax.broadcasted_iota(jnp.int32, (C, C), 0).astype(jnp.float32)
    colm = lax.broadcasted_iota(jnp.int32, (C, C), 1).astype(jnp.float32)
    dist = row - colm
    decay = jnp.where(dist >= 0, jnp.exp(ldf * jnp.maximum(dist, 0.0)), jnp.exp(ldb * jnp.maximum(-dist, 0.0)))
    idx = lax.broadcasted_iota(jnp.int32, (C, 1), 0).astype(jnp.float32)
    q_dec_f = jnp.exp(ldf * (idx + 1.0))
    k_dec_f = jnp.exp(ldf * (C - 1.0 - idx))
    q_dec_b = jnp.exp(ldb * (C - idx))
    k_dec_b = jnp.exp(ldb * idx)
    chunk_dec_f = jnp.exp(ldf * C)
    chunk_dec_b = jnp.exp(ldb * C)
    f32 = jnp.float32
    bf16 = jnp.bfloat16

    def kv_state(k, v, k_dec):
        kd = (k.astype(f32) * k_dec).astype(bf16)
        return lax.dot_general(kd, v, (((0,), (0,)), ((), ())), preferred_element_type=f32)

    def bwd_step(i, state):
        c = n_chunks - 1 - i
        r0 = pl.multiple_of(c * C, C)
        sb_ref[c] = state
        return state * chunk_dec_b + kv_state(k_ref[pl.ds(r0, C), :], v_ref[pl.ds(r0, C), :], k_dec_b)

    lax.fori_loop(0, n_chunks, bwd_step, jnp.zeros((RET_DK, LANES), f32), unroll=RET_UNROLL)

    def fwd_step(c, state):
        r0 = pl.multiple_of(c * C, C)
        q = q_ref[pl.ds(r0, C), :]
        k = k_ref[pl.ds(r0, C), :]
        v = v_ref[pl.ds(r0, C), :]
        scores = lax.dot_general(q, k, (((1,), (1,)), ((), ())), preferred_element_type=f32) * decay
        y = jnp.dot(scores.astype(bf16), v, preferred_element_type=f32)
        qf = q.astype(f32)
        y += jnp.dot((qf * q_dec_f).astype(bf16), state.astype(bf16), preferred_element_type=f32)
        y += jnp.dot((qf * q_dec_b).astype(bf16), sb_ref[c].astype(bf16), preferred_element_type=f32)
        yn = y * lax.rsqrt(jnp.mean(y * y, axis=-1, keepdims=True) + EPS) * nw_ref[...]
        g = g_ref[pl.ds(r0, C), :].astype(f32)
        o_ref[pl.ds(r0, C), :] = (yn * (g * jax.nn.sigmoid(g))).astype(o_ref.dtype)
        return state * chunk_dec_f + kv_state(k, v, k_dec_f)

    lax.fori_loop(0, n_chunks, fwd_step, jnp.zeros((RET_DK, LANES), f32), unroll=RET_UNROLL)


def _retention(ldf, ldb, rq_r, rk_r, proj, nw, B, S):
    T = B * S
    smem = pl.BlockSpec(memory_space=pltpu.SMEM)
    seq = lambda cb: pl.BlockSpec((S, LANES), lambda b, h: (b, cb + h))
    return pl.pallas_call(
        _retention_kernel,
        out_shape=jax.ShapeDtypeStruct((T, RET_WIDTH), jnp.bfloat16),
        grid=(B, RET_HEADS),
        in_specs=[smem, smem, seq(0), seq(0), seq(VGV_RV), seq(VGV_RG),
                  pl.BlockSpec((1, LANES), lambda b, h: (0, h))],
        out_specs=seq(0),
        scratch_shapes=[pltpu.VMEM((S // RET_CHUNK, RET_DK, LANES), jnp.float32)],
        compiler_params=_params(("arbitrary", "arbitrary")),
        name="retention",
    )(ldf, ldb, rq_r, rk_r, proj, proj, nw)


def _diff_attn_kernel(online_max, bound_ref, q_ref, k_ref, v_ref, lam_ref, nw_ref, o_ref, m_ref, l_ref, acc_ref):
    tq = q_ref.shape[1]
    S = k_ref.shape[0]
    f32 = jnp.float32
    q = q_ref[...].reshape(2 * tq, LANES)
    if online_max:
        m_ref[...] = jnp.full(m_ref.shape, -jnp.inf, f32)
    l_ref[...] = jnp.zeros(l_ref.shape, f32)
    acc_ref[...] = jnp.zeros(acc_ref.shape, f32)
    tk = TK_ATTN_ONLINE if online_max else TK_ATTN
    n_tiles = tk // LANES

    def kv_step(j, carry):
        r0 = pl.multiple_of(j * tk, tk)
        k = k_ref[pl.ds(r0, tk), :]
        v = v_ref[pl.ds(r0, tk), :]
        s = lax.dot_general(q, k, (((1,), (1,)), ((), ())), preferred_element_type=f32)
        tiles = [s[:, c * LANES:(c + 1) * LANES] for c in range(n_tiles)]
        if online_max:
            part = tiles[0]
            for t in tiles[1:]:
                part = jnp.maximum(part, t)
            m_prev = m_ref[...]
            shift = jnp.maximum(m_prev, jnp.max(part, axis=-1, keepdims=True))
            alpha = jnp.exp2(m_prev - shift)
            m_ref[...] = shift
        else:
            shift = bound_ref[0]
        probs = [jnp.exp2(t - shift) for t in tiles]
        psum = probs[0]
        for p in probs[1:]:
            psum = psum + p
        pv = jnp.dot(jnp.concatenate([p.astype(jnp.bfloat16) for p in probs], axis=1), v,
                     preferred_element_type=f32)
        if online_max:
            l_ref[...] = alpha * l_ref[...] + psum
            acc_ref[...] = alpha * acc_ref[...] + pv
        else:
            l_ref[...] = l_ref[...] + psum
            acc_ref[...] = acc_ref[...] + pv
        return carry

    lax.fori_loop(0, S // tk, kv_step, 0)
    o = acc_ref[...] / jnp.sum(l_ref[...], axis=-1, keepdims=True)
    d = o[:tq] - lam_ref[...] * o[tq:]
    dn = d * lax.rsqrt(jnp.mean(d * d, axis=-1, keepdims=True) + EPS) * nw_ref[...]
    o_ref[...] = (dn * (1.0 - LAMBDA_INIT)).astype(o_ref.dtype)


def _diff_attn(online_max, bound, qs, ks, proj, lam, nw, B, S):
    T = B * S
    n_q = S // TQ_ATTN
    one = pl.BlockSpec((1, LANES), lambda b, h, i, bd: (0, 0))
    grid_spec = pltpu.PrefetchScalarGridSpec(
        num_scalar_prefetch=1,
        grid=(B, DIFF_HEADS, n_q),
        in_specs=[pl.BlockSpec((None, None, 2, TQ_ATTN, LANES), lambda b, h, i, bd: (b, h, 0, i, 0)),
                  pl.BlockSpec((S, LANES), lambda b, h, i, bd: (b, h)),
                  pl.BlockSpec((S, LANES), lambda b, h, i, bd: (b, VGV_DV + h)),
                  one, one],
        out_specs=pl.BlockSpec((TQ_ATTN, LANES), lambda b, h, i, bd: (b * n_q + i, h)),
        scratch_shapes=[pltpu.VMEM((2 * TQ_ATTN, LANES), jnp.float32)] * 3,
    )
    return pl.pallas_call(
        functools.partial(_diff_attn_kernel, online_max),
        out_shape=jax.ShapeDtypeStruct((T, DIFF_WIDTH), jnp.bfloat16),
        grid_spec=grid_spec,
        compiler_params=_params(("arbitrary", "arbitrary", "arbitrary")),
        name="diff_attn_online" if online_max else "diff_attn",
    )(bound, qs, ks, proj, lam, nw)


def _out_router_kernel(x_ref, yr_ref, yd_ref, wo_ref, n2_ref, wrt_ref, br_ref,
                       x1_ref, h2_ref, pos_ref, gate_t_ref, len_ref, off_ref, tot_ref):
    tm = x_ref.shape[0]
    f32 = jnp.float32
    bf16 = jnp.bfloat16

    @pl.when(pl.program_id(0) == 0)
    def _():
        tot_ref[...] = jnp.zeros(tot_ref.shape, f32)

    att = jnp.dot(yr_ref[...], wo_ref[:RET_WIDTH, :], preferred_element_type=f32)
    att += jnp.dot(yd_ref[...], wo_ref[RET_WIDTH:, :], preferred_element_type=f32)
    x1 = x_ref[...] + att
    x1_ref[...] = x1
    h2 = x1 * lax.rsqrt(jnp.mean(x1 * x1, axis=-1, keepdims=True) + EPS) * n2_ref[...]
    h2_ref[...] = h2.astype(h2_ref.dtype)
    nt = (((1,), (1,)), ((), ()))
    h_hi = h2.astype(bf16)
    h_lo = (h2 - h_hi.astype(f32)).astype(bf16)
    w = wrt_ref[...]
    w_hi = w.astype(bf16)
    w_lo = (w - w_hi.astype(f32)).astype(bf16)
    with_h_hi = lax.dot_general(jnp.concatenate([w_hi, w_lo], axis=0), h_hi, nt, preferred_element_type=f32)
    logits = (with_h_hi[:N_EXPERTS] + with_h_hi[N_EXPERTS:]
              + lax.dot_general(w_hi, h_lo, nt, preferred_element_type=f32)) + br_ref[...]
    e_iota = lax.broadcasted_iota(jnp.int32, (N_EXPERTS, tm), 0)
    work = logits
    vals, hots = [], []
    for _ in range(TOP_K):
        mx = jnp.max(work, axis=0, keepdims=True)
        ix = jnp.min(jnp.where(work == mx, e_iota, N_EXPERTS), axis=0, keepdims=True)
        hot = e_iota == ix
        vals.append(mx)
        hots.append(hot)
        work = jnp.where(hot, -jnp.inf, work)
    exps = [jnp.exp(v - vals[0]) for v in vals]
    denom = exps[0] + exps[1] + exps[2] + exps[3]
    gates = [e / denom for e in exps]
    masks = [jnp.where(hot, 1.0, 0.0) for hot in hots]
    sel = masks[0] + masks[1] + masks[2] + masks[3]
    t_row = lax.broadcasted_iota(jnp.int32, (TILE, TILE), 0)
    t_col = lax.broadcasted_iota(jnp.int32, (TILE, TILE), 1)
    upper = jnp.where(t_row < t_col, 1.0, 0.0).astype(bf16)
    e_row = lax.broadcasted_iota(jnp.int32, (N_EXPERTS, N_EXPERTS), 0)
    e_col = lax.broadcasted_iota(jnp.int32, (N_EXPERTS, N_EXPERTS), 1)
    lower = jnp.where(e_col < e_row, 1.0, 0.0).astype(bf16)
    pos_parts = []
    for t in range(tm // TILE):
        cols = slice(t * TILE, (t + 1) * TILE)
        sel_t = sel[:, cols]
        rank = jnp.dot(sel_t.astype(bf16), upper, preferred_element_type=f32)
        cnt = jnp.sum(sel_t, axis=1, keepdims=True)
        run_units = jnp.floor((cnt + (RUN_ALIGN - 1.0)) * (1.0 / RUN_ALIGN))
        run_len = jnp.broadcast_to(run_units * RUN_ALIGN, (N_EXPERTS, LANES))
        run_start = jnp.dot(lower, jnp.broadcast_to(run_units, (N_EXPERTS, LANES)).astype(bf16),
                            preferred_element_type=f32) * RUN_ALIGN
        pos_full = rank + run_start[:, 0:1]
        pos_parts.append([jnp.sum(mask[:, cols] * pos_full, axis=0, keepdims=True) for mask in masks])
        len_ref[t] = run_len
        off_ref[t] = tot_ref[...]
        tot_ref[...] = tot_ref[...] + run_len
    pos = [jnp.concatenate([part[k] for part in pos_parts], axis=1) for k in range(TOP_K)]
    for k in range(TOP_K):
        pos_ref[k:k + 1, :] = pos[k].astype(jnp.int32)
    rows = jnp.concatenate(gates + pos + [jnp.zeros((LANES - 2 * TOP_K, tm), f32)], axis=0)
    gate_t_ref[...] = rows.T


def _out_router(x2, y_ret, y_diff, wo_bf16, n2w, wrt, br):
    T = x2.shape[0]
    n_tiles = T // TILE
    tiles_per_step = TM_ROUTE // TILE
    tok = lambda w: pl.BlockSpec((TM_ROUTE, w), lambda i: (i, 0))
    const = lambda s: pl.BlockSpec(s, lambda i: (0, 0))
    per_tile = pl.BlockSpec((tiles_per_step, N_EXPERTS, LANES), lambda i: (i, 0, 0))
    return pl.pallas_call(
        _out_router_kernel,
        out_shape=(jax.ShapeDtypeStruct((T, D_MODEL), jnp.float32),
                   jax.ShapeDtypeStruct((T, D_MODEL), jnp.bfloat16),
                   jax.ShapeDtypeStruct((TOP_K, T), jnp.int32),
                   jax.ShapeDtypeStruct((T, LANES), jnp.float32),
                   jax.ShapeDtypeStruct((n_tiles, N_EXPERTS, LANES), jnp.float32),
                   jax.ShapeDtypeStruct((n_tiles, N_EXPERTS, LANES), jnp.float32),
                   jax.ShapeDtypeStruct((N_EXPERTS, LANES), jnp.float32)),
        grid=(T // TM_ROUTE,),
        in_specs=[tok(D_MODEL), tok(RET_WIDTH), tok(DIFF_WIDTH), const((D_MODEL, D_MODEL)),
                  const((1, D_MODEL)), const((N_EXPERTS, D_MODEL)), const((N_EXPERTS, 1))],
        out_specs=(tok(D_MODEL), tok(D_MODEL), pl.BlockSpec((TOP_K, TM_ROUTE), lambda i: (0, i)), tok(LANES),
                   per_tile, per_tile, const((N_EXPERTS, LANES))),
        compiler_params=_params(("arbitrary",)),
        name="out_router",
    )(x2, y_ret, y_diff, wo_bf16, n2w, wrt, br)


def _pack_bf16_pairs(x, is_bf16_valued=False):
    w = x.shape[1] // 2
    if not is_bf16_valued:
        x = x.astype(jnp.bfloat16).astype(jnp.float32)
    bits = lax.bitcast_convert_type(x, jnp.uint32)
    return (bits[:, :w] >> 16) | bits[:, w:]


def _unpack_bf16_pairs(p):
    as_bf16 = lambda bits: lax.bitcast_convert_type(bits, jnp.float32).astype(jnp.bfloat16)
    return as_bf16(p << 16), as_bf16(p & jnp.uint32(0xFFFF0000))


def _run_copies(src_ref, len_ref, dst_ref, tile, make_copy):
    for e in range(N_EXPERTS):
        n = pl.multiple_of(len_ref[tile * N_EXPERTS + e], RUN_ALIGN)
        s = pl.multiple_of(src_ref[tile * N_EXPERTS + e], RUN_ALIGN)
        d = pl.multiple_of(dst_ref[tile * N_EXPERTS + e], RUN_ALIGN)

        @pl.when(n > 0)
        def _():
            make_copy(s, d, n).start(priority=e % 2)


def _dispatch_kernel(src_ref, len_ref, dst_ref, rows_ref, zlo_ref, zlen_ref, nu_ref,
                     pos_ref, h2_ref, xs_hbm, xbuf_ref, zero_ref, sems, zero_sem):
    i = pl.program_id(0)
    n_tiles = pl.num_programs(0)
    n_buf = xbuf_ref.shape[0]
    cur = i % n_buf
    n_rows, tm = xbuf_ref.shape[1], h2_ref.shape[0]

    @pl.when(i == 0)
    def _():
        zero_ref[...] = jnp.zeros(zero_ref.shape, zero_ref.dtype)

        def pad_copy(e):
            n = pl.multiple_of(zlen_ref[e], RUN_ALIGN)
            lo = pl.multiple_of(zlo_ref[e], RUN_ALIGN)
            return pltpu.make_async_copy(zero_ref.at[pl.ds(0, n)], xs_hbm.at[pl.ds(lo, n)], zero_sem)

        def tail_copy(j):
            return pltpu.make_async_copy(zero_ref, xs_hbm.at[pl.ds(j * MOE_HALF, MOE_HALF)], zero_sem)

        def guarded(copy, op):
            def body(e, c):
                @pl.when(zlen_ref[e] > 0)
                def _():
                    op(copy(e))
                return c
            return body

        lax.fori_loop(0, N_EXPERTS, guarded(pad_copy, lambda cp: cp.start()), 0)
        lax.fori_loop(0, N_EXPERTS, guarded(pad_copy, lambda cp: cp.wait()), 0)
        n_halves = xs_hbm.shape[0] // MOE_HALF
        lax.fori_loop(nu_ref[0], n_halves, lambda j, c: (tail_copy(j).start(), c)[1], 0)
        lax.fori_loop(nu_ref[0], n_halves, lambda j, c: (tail_copy(j).wait(), c)[1], 0)

    p_iota = lax.broadcasted_iota(jnp.int32, (n_rows, tm), 0)
    onehot = jnp.zeros((n_rows, tm), jnp.float32)
    for k in range(TOP_K):
        onehot = jnp.where(p_iota == pos_ref[k:k + 1, :], 1.0, onehot)
    xbuf_ref[cur] = _pack_bf16_pairs(
        jnp.dot(onehot.astype(jnp.bfloat16), h2_ref[...], preferred_element_type=jnp.float32), is_bf16_valued=True)

    _run_copies(src_ref, len_ref, dst_ref, i,
                lambda s, d, n: pltpu.make_async_copy(xbuf_ref.at[cur, pl.ds(s, n)], xs_hbm.at[pl.ds(d, n)],
                                                      sems.at[cur]))

    def wait_tile(tile, slot):
        rows = pl.multiple_of(rows_ref[tile], RUN_ALIGN)
        pltpu.make_async_copy(xbuf_ref.at[slot, pl.ds(0, rows)], xs_hbm.at[pl.ds(0, rows)], sems.at[slot]).wait()

    oldest = n_buf - 1

    @pl.when(i >= oldest)
    def _():
        wait_tile(i - oldest, (i + 1) % n_buf)

    @pl.when(i == n_tiles - 1)
    def _():
        for back in range(oldest - 1, -1, -1):
            wait_tile(i - back, (i - back) % n_buf)


def _dispatch(run_src, run_len, run_dst, tile_rows, zero_lo, zero_len, n_used, pos, h2, P):
    T = h2.shape[0]
    n_pre = 7
    grid_spec = pltpu.PrefetchScalarGridSpec(
        num_scalar_prefetch=n_pre,
        grid=(T // TILE,),
        in_specs=[pl.BlockSpec((TOP_K, TILE), lambda i, *_: (0, i)),
                  pl.BlockSpec((TILE, D_MODEL), lambda i, *_: (i, 0))],
        out_specs=pl.BlockSpec(memory_space=pl.ANY),
        scratch_shapes=[pltpu.VMEM((DISPATCH_BUFS, TILE_ROWS, D_MODEL // 2), jnp.uint32),
                        pltpu.VMEM((MOE_HALF, D_MODEL // 2), jnp.uint32),
                        pltpu.SemaphoreType.DMA((DISPATCH_BUFS,)),
                        pltpu.SemaphoreType.DMA(())],
    )
    return pl.pallas_call(
        _dispatch_kernel,
        out_shape=jax.ShapeDtypeStruct((P, D_MODEL // 2), jnp.uint32),
        grid_spec=grid_spec,
        compiler_params=_params(("arbitrary",), has_side_effects=True),
        name="dispatch",
    )(run_src, run_len, run_dst, tile_rows, zero_lo, zero_len, n_used, pos, h2)


def _experts_kernel(base_ref, nblk_ref, half_ref, w1_ref, b1_ref, w2_ref, b2_ref, xs_hbm, ys_hbm,
                    w1b_ref, w2b_ref, xbuf_ref, ybuf_ref, xhalf_ref, yhalf_ref, in_sems, out_sems, busy_ref):
    e = pl.program_id(0)
    n = nblk_ref[e]
    has_half = half_ref[e] == 1
    HALF_BUF = 2

    def rows(expert, j):
        return pl.ds(pl.multiple_of(base_ref[expert] + j * MOE_BLOCK, MOE_HALF), MOE_BLOCK)

    def half_rows(first_row):
        return pl.ds(pl.multiple_of(first_row, MOE_HALF), MOE_HALF)

    def in_copy(expert, j, slot):
        return pltpu.make_async_copy(xs_hbm.at[rows(expert, j)], xbuf_ref.at[slot], in_sems.at[slot])

    def out_copy(j, slot):
        return pltpu.make_async_copy(ybuf_ref.at[slot], ys_hbm.at[rows(e, j)], out_sems.at[slot])

    half_row0 = base_ref[e] + n * MOE_BLOCK
    half_in = pltpu.make_async_copy(xs_hbm.at[half_rows(half_row0)], xhalf_ref, in_sems.at[HALF_BUF])

    def half_out(first_row):
        return pltpu.make_async_copy(yhalf_ref, ys_hbm.at[half_rows(first_row)], out_sems.at[HALF_BUF])

    def wait_out(buf, half=False):
        @pl.when(busy_ref[buf] == 1)
        def _():
            (half_out(0) if half else out_copy(0, buf)).wait()
            busy_ref[buf] = 0

    def mlp(x_packed):
        x = jnp.concatenate(_unpack_bf16_pairs(x_packed), axis=1)
        u = jnp.dot(x, w1b_ref[...], preferred_element_type=jnp.float32) + b1_ref[...]
        glu = jnp.minimum(u[:, :D_FF], SWIGLU_LIMIT)
        lin = jnp.clip(u[:, D_FF:], -SWIGLU_LIMIT, SWIGLU_LIMIT)
        act = glu * jax.nn.sigmoid(SWIGLU_ALPHA * glu) * (lin + 1.0)
        return _pack_bf16_pairs(jnp.dot(act.astype(jnp.bfloat16), w2b_ref[...],
                                        preferred_element_type=jnp.float32) + b2_ref[...])

    @pl.when(e == 0)
    def _():
        for buf in range(3):
            busy_ref[buf] = 0

        @pl.when(n > 0)
        def _():
            in_copy(0, 0, 0).start()

    @pl.when(has_half)
    def _():
        half_in.start()

    @pl.when((n > 0) | has_half)
    def _():
        w1b_ref[...] = w1_ref[...].astype(jnp.bfloat16)
        w2b_ref[...] = w2_ref[...].astype(jnp.bfloat16)

    def block(j, carry):
        slot = j % 2

        @pl.when(j + 1 < n)
        def _():
            in_copy(e, j + 1, 1 - slot).start()

        in_copy(e, j, slot).wait()
        wait_out(slot)
        ybuf_ref[slot] = mlp(xbuf_ref[slot])
        out_copy(j, slot).start()
        busy_ref[slot] = 1
        return carry

    lax.fori_loop(0, n, block, 0)

    @pl.when(has_half)
    def _():
        half_in.wait()
        wait_out(HALF_BUF, half=True)
        yhalf_ref[...] = mlp(xhalf_ref[...])
        half_out(half_row0).start()
        busy_ref[HALF_BUF] = 1

    e_next = jnp.minimum(e + 1, N_EXPERTS - 1)

    @pl.when((e + 1 < N_EXPERTS) & (nblk_ref[e_next] > 0))
    def _():
        in_copy(e_next, 0, 0).start()

    @pl.when(e == N_EXPERTS - 1)
    def _():
        wait_out(0)
        wait_out(1)
        wait_out(HALF_BUF, half=True)
        yhalf_ref[...] = jnp.zeros(yhalf_ref.shape, yhalf_ref.dtype)
        first_unused = (half_row0 + half_ref[e] * MOE_HALF) // MOE_HALF
        n_halves = ys_hbm.shape[0] // MOE_HALF
        lax.fori_loop(first_unused, n_halves, lambda j, c: (half_out(j * MOE_HALF).start(), c)[1], 0)
        lax.fori_loop(first_unused, n_halves, lambda j, c: (half_out(j * MOE_HALF).wait(), c)[1], 0)


def _experts(base, n_blk, n_half, xs, w1, b1, w2, b2):
    P = xs.shape[0]
    expert = lambda e, bs, nb, nh: (e, 0, 0)
    grid_spec = pltpu.PrefetchScalarGridSpec(
        num_scalar_prefetch=3,
        grid=(N_EXPERTS,),
        in_specs=[pl.BlockSpec((None, D_MODEL, 2 * D_FF), expert),
                  pl.BlockSpec((None, 1, 2 * D_FF), expert),
                  pl.BlockSpec((None, D_FF, D_MODEL), expert),
                  pl.BlockSpec((None, 1, D_MODEL), expert),
                  pl.BlockSpec(memory_space=pl.ANY)],
        out_specs=pl.BlockSpec(memory_space=pl.ANY),
        scratch_shapes=[pltpu.VMEM((D_MODEL, 2 * D_FF), jnp.bfloat16),
                        pltpu.VMEM((D_FF, D_MODEL), jnp.bfloat16),
                        pltpu.VMEM((2, MOE_BLOCK, D_MODEL // 2), jnp.uint32),
                        pltpu.VMEM((2, MOE_BLOCK, D_MODEL // 2), jnp.uint32),
                        pltpu.VMEM((MOE_HALF, D_MODEL // 2), jnp.uint32),
                        pltpu.VMEM((MOE_HALF, D_MODEL // 2), jnp.uint32),
                        pltpu.SemaphoreType.DMA((3,)),
                        pltpu.SemaphoreType.DMA((3,)),
                        pltpu.SMEM((3,), jnp.int32)],
    )
    return pl.pallas_call(
        _experts_kernel,
        out_shape=jax.ShapeDtypeStruct((P, D_MODEL // 2), jnp.uint32),
        grid_spec=grid_spec,
        compiler_params=_params(("arbitrary",)),
        name="experts",
    )(base, n_blk, n_half, w1, b1, w2, b2, xs)


def _combine_kernel(src_ref, len_ref, dst_ref, rows_ref, gate_t_ref, x1_ref, ys_hbm, o_ref, ybuf_ref, sems):
    i = pl.program_id(0)
    n_tiles = pl.num_programs(0)
    n_buf = ybuf_ref.shape[0]
    cur = i % n_buf
    n_rows, tm = ybuf_ref.shape[1], x1_ref.shape[0]

    def fetch(tile, slot):
        _run_copies(src_ref, len_ref, dst_ref, tile,
                    lambda s, d, n: pltpu.make_async_copy(ys_hbm.at[pl.ds(d, n)], ybuf_ref.at[slot, pl.ds(s, n)],
                                                          sems.at[slot]))

    @pl.when(i == 0)
    def _():
        ybuf_ref[...] = jnp.zeros(ybuf_ref.shape, ybuf_ref.dtype)
        for tile in range(n_buf - 1):
            fetch(tile, tile)

    ahead = i + n_buf - 1

    @pl.when(ahead < n_tiles)
    def _():
        fetch(ahead, ahead % n_buf)

    g = gate_t_ref[...]
    p_iota = lax.broadcasted_iota(jnp.int32, (tm, n_rows), 1)
    weights = jnp.zeros((tm, n_rows), jnp.float32)
    for k in range(TOP_K):
        pos_k = g[:, TOP_K + k:TOP_K + k + 1].astype(jnp.int32)
        weights = jnp.where(p_iota == pos_k, g[:, k:k + 1], weights)
    weights = weights.astype(jnp.bfloat16)

    rows = pl.multiple_of(rows_ref[i], RUN_ALIGN)
    pltpu.make_async_copy(ys_hbm.at[pl.ds(0, rows)], ybuf_ref.at[cur, pl.ds(0, rows)], sems.at[cur]).wait()
    halves = [jnp.dot(weights, y, preferred_element_type=jnp.float32) for y in _unpack_bf16_pairs(ybuf_ref[cur])]
    o_ref[...] = x1_ref[...] + jnp.concatenate(halves, axis=1)


def _combine(run_src, run_len, run_dst, tile_rows, gate_t, x1, ys):
    T = x1.shape[0]
    tok = lambda w: pl.BlockSpec((TILE, w), lambda i, *_: (i, 0))
    grid_spec = pltpu.PrefetchScalarGridSpec(
        num_scalar_prefetch=4,
        grid=(T // TILE,),
        in_specs=[tok(LANES), tok(D_MODEL), pl.BlockSpec(memory_space=pl.ANY)],
        out_specs=tok(D_MODEL),
        scratch_shapes=[pltpu.VMEM((DISPATCH_BUFS, TILE_ROWS, D_MODEL // 2), jnp.uint32),
                        pltpu.SemaphoreType.DMA((DISPATCH_BUFS,))],
    )
    return pl.pallas_call(
        _combine_kernel,
        out_shape=jax.ShapeDtypeStruct((T, D_MODEL), jnp.float32),
        grid_spec=grid_spec,
        compiler_params=_params(("arbitrary",)),
        name="combine",
    )(run_src, run_len, run_dst, tile_rows, gate_t, x1, ys)


def _rotary_tables(positions):
    pos = positions.astype(jnp.float32)[..., None]
    lane = jnp.arange(LANES)
    half_r = RET_DK // 2
    inv_r = RET_ROPE_THETA ** (-jnp.linspace(0.0, 1.0, half_r, dtype=jnp.float32))
    ret_cs = jnp.cos(pos * inv_r[lane % half_r] - jnp.where(lane < half_r, 0.0, 0.5 * jnp.pi))
    half_d = ROT_DIM // 2
    inv_d = ROPE_THETA ** (-jnp.arange(0, ROT_DIM, 2, dtype=jnp.float32) / ROT_DIM)
    row = jnp.arange(ROT_DIM)
    ang_d = positions.astype(jnp.float32)[:, None, :] * inv_d[row % half_d][None, :, None]
    diff_cs = jnp.cos(ang_d - jnp.where(row >= half_d, 0.5 * jnp.pi, 0.0)[None, :, None])
    return ret_cs, diff_cs


def kernel(x, positions, norm1_w, w_in, ret_log_decay_fwd, ret_log_decay_bwd, ret_norm_w, q_norm_w, k_norm_w, lambda_q1, lambda_k1, lambda_q2, lambda_k2, diff_norm_w, w_out, norm2_w, w_router, b_router, w1, b1, w2, b2):
    B, S, D = x.shape
    T = B * S
    f32 = jnp.float32
    bf16 = jnp.bfloat16
    x2 = x.reshape(T, D)

    dup = lambda w: jnp.concatenate([w, w]).reshape(1, LANES).astype(f32)
    proj, rq_r, rk_r, qs, ks = _in_proj(x2, norm1_w[0].reshape(1, D), w_in[0].astype(bf16),
                                        _rotary_tables(positions), dup(q_norm_w[0]), dup(k_norm_w[0]), B, S)

    y_ret = _retention(ret_log_decay_fwd[0].astype(f32), ret_log_decay_bwd[0].astype(f32),
                       rq_r, rk_r, proj, ret_norm_w[0].reshape(1, RET_WIDTH).astype(f32), B, S)

    lam = (jnp.exp(jnp.sum(lambda_q1[0].astype(f32) * lambda_k1[0].astype(f32)))
           - jnp.exp(jnp.sum(lambda_q2[0].astype(f32) * lambda_k2[0].astype(f32))) + LAMBDA_INIT)
    lam_row = jnp.full((1, LANES), lam, f32)
    bound = (SCORE_BOUND_SLACK * DIFF_DH ** 0.5 * LOG2_E
             * jnp.max(jnp.abs(q_norm_w[0].astype(f32))) * jnp.max(jnp.abs(k_norm_w[0].astype(f32)))).reshape(1)
    attn_args = (bound, qs, ks, proj, lam_row, diff_norm_w[0].reshape(1, DIFF_DV).astype(f32), B, S)
    y_diff = lax.cond(bound[0] <= MAX_SAFE_SCORE_BOUND,
                      lambda: _diff_attn(False, *attn_args), lambda: _diff_attn(True, *attn_args))

    x1, h2, pos, gate_t, len_t, off_t, tot_t = _out_router(
        x2, y_ret, y_diff, w_out[0].astype(bf16), norm2_w[0].reshape(1, D),
        w_router[0].T.astype(f32), b_router[0].reshape(N_EXPERTS, 1).astype(f32))

    n_tiles = T // TILE
    run_len = len_t[:, :, 0].astype(jnp.int32)
    total = tot_t[:, 0].astype(jnp.int32)
    padded = ((total + MOE_HALF - 1) // MOE_HALF) * MOE_HALF
    pad_end = jnp.cumsum(padded)
    pad_start = pad_end - padded
    run_dst = pad_start[None, :] + off_t[:, :, 0].astype(jnp.int32)
    run_src = jnp.cumsum(run_len, axis=1) - run_len
    tile_rows = jnp.sum(run_len, axis=1)
    P = T * TOP_K + n_tiles * N_EXPERTS * RUN_ALIGN + N_EXPERTS * MOE_HALF
    n_used = (pad_end[-1:] // MOE_HALF).astype(jnp.int32)
    runs = (run_src.reshape(-1), run_len.reshape(-1), run_dst.reshape(-1), tile_rows)

    xs = _dispatch(*runs, pad_start + total, padded - total, n_used, pos, h2, P)
    ys = _experts(pad_start, padded // MOE_BLOCK, (padded // MOE_HALF) % 2, xs, w1[0],
                  b1[0].reshape(N_EXPERTS, 1, 2 * D_FF), w2[0], b2[0].reshape(N_EXPERTS, 1, D))
    out = _combine(*runs, gate_t, x1, ys)
    return out.reshape(B, S, D)
```
